```python
import math, functools
import jax
import jax.numpy as jnp
from jax import lax
import numpy as np

D_MODEL = 1024
BATCH = 8
SEQ = 2048
DEPTH = 1
DEC_BATCH = 32
DEC_SEQ = 8
PAST_LEN = 8192
PAGE_SIZE = 128

HEAD_DIM = 64
V_DIM = 2 * HEAD_DIM
N_HEADS = D_MODEL // V_DIM
QK_WIDTH = N_HEADS * 2 * HEAD_DIM
ATTN_WIDTH = N_HEADS * V_DIM
ATTN_SCALE = HEAD_DIM ** -0.5
Q_BLOCK = 128
D_CONV = D_MODEL
CONV_WIDTH = 3
IN_SPLITS = (QK_WIDTH, 2 * QK_WIDTH, 2 * QK_WIDTH + ATTN_WIDTH,
             2 * QK_WIDTH + ATTN_WIDTH + D_CONV, 2 * QK_WIDTH + ATTN_WIDTH + 2 * D_CONV,
             2 * QK_WIDTH + ATTN_WIDTH + 3 * D_CONV, 2 * QK_WIDTH + ATTN_WIDTH + 3 * D_CONV + D_MODEL)
IN_WIDTH = 2 * QK_WIDTH + ATTN_WIDTH + 3 * D_CONV + 2 * D_MODEL
N_EXPERTS = 32
TOP_K = 4
D_FF = D_MODEL
SWIGLU_LIMIT = 7.0
SWIGLU_ALPHA = 1.702
MOE_BLOCK = 256
NORM_EPS = 1e-6

kernel_name = 'hybrid_diffattn_shortconv_moe_adaln_step'


def _rms(x, g):
    xf = x.astype(jnp.float32)
    y = xf * lax.rsqrt(jnp.mean(xf * xf, axis=-1, keepdims=True) + NORM_EPS)
    return (y * g.astype(jnp.float32)).astype(x.dtype)


def _lambda(p, lambda_init):
    f32 = jnp.float32
    e1 = jnp.exp(jnp.sum(p['lambda_q1'].astype(f32) * p['lambda_k1'].astype(f32)))
    e2 = jnp.exp(jnp.sum(p['lambda_q2'].astype(f32) * p['lambda_k2'].astype(f32)))
    return e1 - e2 + lambda_init


def _diff_probs(s, lam):
    prob = jax.nn.softmax(s, axis=-1)
    return prob[:, :, 0] - lam * prob[:, :, 1]


def _attn_prompt(q, k, v, lam):
    B, S = q.shape[0], q.shape[1]
    kpos = jnp.arange(S)

    def block(i):
        qb = lax.dynamic_slice_in_dim(q, i * Q_BLOCK, Q_BLOCK, axis=1)
        s = jnp.einsum('bqhmd,bkhmd->bhmqk', qb, k, preferred_element_type=jnp.float32) * ATTN_SCALE
        qpos = i * Q_BLOCK + jnp.arange(Q_BLOCK)
        s = jnp.where(kpos[None, :] <= qpos[:, None], s, -jnp.inf)
        pd = _diff_probs(s, lam)
        return jnp.einsum('bhqk,bkhe->bqhe', pd.astype(v.dtype), v)

    o = lax.map(block, jnp.arange(S // Q_BLOCK))
    return jnp.moveaxis(o, 0, 1).reshape(B, S, N_HEADS, V_DIM)


def _attn_sample(q, k_new, v_new, lam, cache_k, cache_v, page_table, layer):
    DB, T = q.shape[0], q.shape[1]
    past = page_table.shape[1] * PAGE_SIZE
    k_past = cache_k[layer, page_table].reshape(DB, past, N_HEADS, 2, HEAD_DIM)
    v_past = cache_v[layer, page_table].reshape(DB, past, N_HEADS, V_DIM)
    s_past = jnp.einsum('bqhmd,bkhmd->bhmqk', q, k_past, preferred_element_type=jnp.float32) * ATTN_SCALE
    s_new = jnp.einsum('bqhmd,bkhmd->bhmqk', q, k_new, preferred_element_type=jnp.float32) * ATTN_SCALE
    causal = jnp.arange(T)[None, :] <= jnp.arange(T)[:, None]
    s_new = jnp.where(causal, s_new, -jnp.inf)
    pd = _diff_probs(jnp.concatenate([s_past, s_new], axis=-1), lam).astype(v_new.dtype)
    return (jnp.einsum('bhqk,bkhe->bqhe', pd[..., :past], v_past)
            + jnp.einsum('bhqk,bkhe->bqhe', pd[..., past:], v_new))


def _short_conv(u, state, w_conv):
    T = u.shape[1]
    u_pad = jnp.concatenate([state, u], axis=1)
    y = sum(w_conv[j] * u_pad[:, j:j + T] for j in range(CONV_WIDTH))
    return y, u_pad[:, T:]


def _expert(xb, w_gu, b_gu, w_dn, b_dn):
    gu = xb @ w_gu + b_gu
    gate = jnp.minimum(gu[..., ::2], SWIGLU_LIMIT)
    up = jnp.clip(gu[..., 1::2], -SWIGLU_LIMIT, SWIGLU_LIMIT)
    glu = gate * jax.nn.sigmoid(SWIGLU_ALPHA * gate)
    return ((up + 1) * glu) @ w_dn + b_dn


def _moe(h, w_router, b_router, w_gu, b_gu, w_dn, b_dn):
    T, D = h.shape
    logits = (h @ w_router + b_router).astype(jnp.float32)
    top_logits, top_idx = lax.top_k(logits, TOP_K)
    gates = jax.nn.softmax(top_logits, axis=-1)
    A = T * TOP_K
    flat_e = top_idx.reshape(A).astype(jnp.int32)
    flat_tok = jnp.arange(A, dtype=jnp.int32) // TOP_K
    order = jnp.argsort(flat_e)
    sorted_e = flat_e[order]
    counts = jnp.bincount(flat_e, length=N_EXPERTS).astype(jnp.int32)
    padded = (counts + MOE_BLOCK - 1) // MOE_BLOCK * MOE_BLOCK
    pad_end = jnp.cumsum(padded)
    pad_start = pad_end - padded
    start = jnp.cumsum(counts) - counts
    dest = pad_start[sorted_e] + jnp.arange(A, dtype=jnp.int32) - start[sorted_e]
    n_blocks = -(-A // MOE_BLOCK) + N_EXPERTS
    rows = n_blocks * MOE_BLOCK
    tok_buf = jnp.full((rows,), T, jnp.int32).at[dest].set(flat_tok[order])
    block_e = jnp.minimum(jnp.searchsorted(pad_end, jnp.arange(n_blocks, dtype=jnp.int32) * MOE_BLOCK, side='right'),
                          N_EXPERTS - 1)
    h_pad = jnp.concatenate([h, jnp.zeros((1, D), h.dtype)], axis=0)
    x_blocks = h_pad[tok_buf].reshape(n_blocks, MOE_BLOCK, D)

    def run_block(args):
        xb, e = args
        return _expert(xb, w_gu[e], b_gu[e], w_dn[e], b_dn[e])

    y_buf = lax.map(run_block, (x_blocks, block_e)).reshape(rows, D)
    slot = jnp.zeros((A,), jnp.int32).at[order].set(dest)
    y = y_buf[slot].reshape(T, TOP_K, D)
    return jnp.einsum('tkd,tk->td', y, gates.astype(y.dtype))


def _layer(x, c, conv_state, attend, p, lambda_init):
    ada = (jax.nn.silu(c) @ p['w_ada'] + p['b_ada'])[:, None, :]
    shift1, scale1, gate1, shift2, scale2, gate2 = jnp.split(ada, 6, axis=-1)
    B, T, _ = x.shape
    h = _rms(x, p['g_norm_mix']) * (1 + scale1) + shift1
    q, k, v, cb, cc, cx, ga, gb = jnp.split(h @ p['w_in'], IN_SPLITS, axis=-1)
    q = _rms(q.reshape(B, T, N_HEADS, 2, HEAD_DIM), p['g_q'])
    k = _rms(k.reshape(B, T, N_HEADS, 2, HEAD_DIM), p['g_k'])
    v = v.reshape(B, T, N_HEADS, V_DIM)
    lam = _lambda(p, lambda_init)
    o = attend(q, k, v, lam)
    attn = (_rms(o, p['g_subln']) * (1.0 - lambda_init)).reshape(B, T, ATTN_WIDTH)
    yc, new_conv = _short_conv(cc * cx, conv_state, p['w_conv'])
    mix = jax.nn.sigmoid(ga) * attn + jax.nn.sigmoid(gb) * (cb * yc)
    x = x + gate1 * (mix @ p['w_o'])
    h2 = _rms(x, p['g_norm_ffn']) * (1 + scale2) + shift2
    ffn = _moe(h2.reshape(B * T, D_MODEL), p['w_router'], p['b_router'], p['w_gate_up'],
               p['b_gate_up'], p['w_down'], p['b_down']).reshape(B, T, D_MODEL)
    x = x + gate2 * ffn
    return x, k, v, new_conv


def setup_inputs(seed: int = 0) -> dict:
    key = jax.random.key(seed)
    ks = jax.random.split(key, 32)
    n_pages = PAST_LEN // PAGE_SIZE
    n_used = DEC_BATCH * n_pages
    n_pool = (5 * n_used) // 4

    def nrm(k, shape, scale=1.0):
        return scale * jax.random.normal(k, shape, jnp.float32)

    def gain(k, shape):
        return 1.0 + 0.01 * jax.random.normal(k, shape, jnp.float32)

    page_table = jax.random.permutation(ks[7], n_pool)[:n_used].reshape(DEC_BATCH, n_pages).astype(jnp.int32)
    return {
        'x_prompt': nrm(ks[0], (BATCH, SEQ, D_MODEL)),
        'x_sample': nrm(ks[1], (DEC_BATCH, DEC_SEQ, D_MODEL)),
        'c_prompt': nrm(ks[2], (BATCH, D_MODEL)),
        'c_sample': nrm(ks[3], (DEC_BATCH, D_MODEL)),
        'cache_k': nrm(ks[4], (DEPTH, n_pool, PAGE_SIZE, N_HEADS, 2, HEAD_DIM)),
        'cache_v': nrm(ks[5], (DEPTH, n_pool, PAGE_SIZE, N_HEADS, V_DIM)),
        'state_conv': nrm(ks[6], (DEPTH, DEC_BATCH, CONV_WIDTH - 1, D_CONV)),
        'page_table': page_table,
        'w_ada': nrm(ks[8], (DEPTH, D_MODEL, 6 * D_MODEL), 0.5 * D_MODEL ** -0.5),
        'b_ada': nrm(ks[9], (DEPTH, 6 * D_MODEL), 0.02),
        'g_norm_mix': gain(ks[10], (DEPTH, D_MODEL)),
        'w_in': nrm(ks[11], (DEPTH, D_MODEL, IN_WIDTH), D_MODEL ** -0.5),
        'g_q': gain(ks[12], (DEPTH, HEAD_DIM)),
        'g_k': gain(ks[13], (DEPTH, HEAD_DIM)),
        'lambda_q1': nrm(ks[14], (DEPTH, HEAD_DIM), 0.1),
        'lambda_k1': nrm(ks[15], (DEPTH, HEAD_DIM), 0.1),
        'lambda_q2': nrm(ks[16], (DEPTH, HEAD_DIM), 0.1),
        'lambda_k2': nrm(ks[17], (DEPTH, HEAD_DIM), 0.1),
        'g_subln': gain(ks[18], (DEPTH, V_DIM)),
        'w_conv': nrm(ks[19], (DEPTH, CONV_WIDTH, D_CONV), CONV_WIDTH ** -0.5),
        'w_o': nrm(ks[20], (DEPTH, D_MODEL, D_MODEL), D_MODEL ** -0.5),
        'g_norm_ffn': gain(ks[21], (DEPTH, D_MODEL)),
        'w_router': nrm(ks[22], (DEPTH, D_MODEL, N_EXPERTS), D_MODEL ** -0.5),
        'b_router': nrm(ks[23], (DEPTH, N_EXPERTS), 0.01),
        'w_gate_up': nrm(ks[24], (DEPTH, N_EXPERTS, D_MODEL, 2 * D_FF), D_MODEL ** -0.5),
        'b_gate_up': nrm(ks[25], (DEPTH, N_EXPERTS, 2 * D_FF), 0.01),
        'w_down': nrm(ks[26], (DEPTH, N_EXPERTS, D_FF, D_MODEL), D_FF ** -0.5),
        'b_down': nrm(ks[27], (DEPTH, N_EXPERTS, D_MODEL), 0.01),
    }


def reference(x_prompt, x_sample, c_prompt, c_sample, cache_k, cache_v, state_conv, page_table,
              w_ada, b_ada, g_norm_mix, w_in, g_q, g_k, lambda_q1, lambda_k1, lambda_q2, lambda_k2,
              g_subln, w_conv, w_o, g_norm_ffn, w_router, b_router, w_gate_up, b_gate_up, w_down, b_down):
    y_p, y_s = x_prompt, x_sample
    k_p, v_p, cv_p, k_s, v_s, cv_s = [], [], [], [], [], []
    for l in range(DEPTH):
        p = dict(w_ada=w_ada[l], b_ada=b_ada[l], g_norm_mix=g_norm_mix[l], w_in=w_in[l], g_q=g_q[l], g_k=g_k[l],
                 lambda_q1=lambda_q1[l], lambda_k1=lambda_k1[l], lambda_q2=lambda_q2[l], lambda_k2=lambda_k2[l],
                 g_subln=g_subln[l], w_conv=w_conv[l], w_o=w_o[l], g_norm_ffn=g_norm_ffn[l],
                 w_router=w_router[l], b_router=b_router[l], w_gate_up=w_gate_up[l], b_gate_up=b_gate_up[l],
                 w_down=w_down[l], b_down=b_down[l])
        lambda_init = 0.8 - 0.6 * math.exp(-0.3 * l)
        zero_state = jnp.zeros((y_p.shape[0], CONV_WIDTH - 1, D_CONV), y_p.dtype)
        y_p, kp, vp, cvp = _layer(y_p, c_prompt, zero_state, _attn_prompt, p, lambda_init)
        attend_s = functools.partial(_attn_sample, cache_k=cache_k, cache_v=cache_v,
                                     page_table=page_table, layer=l)
        y_s, ksm, vsm, cvs = _layer(y_s, c_sample, state_conv[l], attend_s, p, lambda_init)
        k_p.append(kp); v_p.append(vp); cv_p.append(cvp)
        k_s.append(ksm); v_s.append(vsm); cv_s.append(cvs)
    return (y_p, y_s, jnp.stack(k_p), jnp.stack(v_p), jnp.stack(cv_p), jnp.stack(k_s), jnp.stack(v_s), jnp.stack(cv_s))
```

```python
import functools
import math

import jax
import jax.numpy as jnp
from jax import lax
from jax.experimental import pallas as pl
from jax.experimental.pallas import tpu as pltpu

F32 = jnp.float32
BF16 = jnp.bfloat16

D_MODEL = 1024
HEAD_DIM = 64
V_DIM = 2 * HEAD_DIM
N_HEADS = D_MODEL // V_DIM
ATTN_SCALE = HEAD_DIM ** -0.5
CONV_WIDTH = 3
PAGE_SIZE = 128
N_EXPERTS = 32
TOP_K = 4
SWIGLU_LIMIT = 7.0
SWIGLU_ALPHA = 1.702
NORM_EPS = 1e-6
LAMBDA_INIT = 0.8 - 0.6 * math.exp(-0.3 * 0)

VMEM_LIMIT_BYTES = 48 * 1024 * 1024
LANES = 128
SUBLANES = 8

ROW_TILE_PROMPT = 512
ATTN_BLOCK = 512
MOE_BLOCK = 256
ADA_COL_TILE = 1536
NEG_BIG = -1e30


def _params(*sem):
    return pltpu.CompilerParams(dimension_semantics=sem, vmem_limit_bytes=VMEM_LIMIT_BYTES)


def _rms_mod(x, g, scale, shift):
    ms = jnp.mean(x * x, axis=-1, keepdims=True)
    return (x * lax.rsqrt(ms + NORM_EPS) * g) * (1.0 + scale) + shift


def _ada_kernel(c_ref, w_ref, b_ref, o_ref):
    c = c_ref[...]
    s = (c * jax.nn.sigmoid(c)).astype(BF16)
    o_ref[...] = jnp.dot(s, w_ref[...].astype(BF16), preferred_element_type=F32) + b_ref[...]


def _ada(c_all, w_ada, b_ada):
    n = c_all.shape[0]
    width = w_ada.shape[1]
    return pl.pallas_call(
        _ada_kernel,
        out_shape=jax.ShapeDtypeStruct((n, width), F32),
        grid=(width // ADA_COL_TILE,),
        in_specs=[pl.BlockSpec((n, D_MODEL), lambda j: (0, 0)),
                  pl.BlockSpec((D_MODEL, ADA_COL_TILE), lambda j: (0, j)),
                  pl.BlockSpec((1, ADA_COL_TILE), lambda j: (0, j))],
        out_specs=pl.BlockSpec((n, ADA_COL_TILE), lambda j: (0, j)),
        compiler_params=_params("arbitrary"),
        name="ada",
    )(c_all, w_ada, b_ada)


class _Rows:
    def __init__(self, n_tokens, rows_per_batch):
        if rows_per_batch >= ROW_TILE_PROMPT:
            self.tm = ROW_TILE_PROMPT
            self.tiles_per_batch = rows_per_batch // self.tm
            self.mod_rows = 1
        else:
            self.tm = n_tokens
            self.tiles_per_batch = None
            self.mod_rows = n_tokens
        self.n_tokens = n_tokens
        self.rows_per_batch = rows_per_batch
        self.n_tiles = n_tokens // self.tm

    def mod_array(self, m):
        if self.tiles_per_batch is not None:
            return m[:, None, :]
        return jnp.repeat(m, self.rows_per_batch, axis=0)[None]

    def mod_spec(self):
        if self.tiles_per_batch is not None:
            tpb = self.tiles_per_batch
            return pl.BlockSpec((1, 1, D_MODEL), lambda i, *_: (i // tpb, 0, 0))
        return pl.BlockSpec((1, self.mod_rows, D_MODEL), lambda i, *_: (0, 0, 0))

    def row_spec(self, width=D_MODEL):
        return pl.BlockSpec((self.tm, width), lambda i, *_: (i, 0))


def _qk_kernel(x_ref, g_ref, sc_ref, sh_ref, w_ref, gqk_ref, p_ref, qkb_ref, qkf_ref, h_ref):
    j = pl.program_id(1)

    @pl.when(j == 0)
    def _():
        h_ref[...] = _rms_mod(x_ref[...], g_ref[...], sc_ref[0], sh_ref[0]).astype(BF16)

    z = jnp.dot(h_ref[...], w_ref[...], preferred_element_type=F32)
    ss = jnp.dot((z * z).astype(BF16), p_ref[...], preferred_element_type=F32)
    zn = z * lax.rsqrt(ss * (1.0 / HEAD_DIM) + NORM_EPS) * gqk_ref[0]
    qkf_ref[0] = zn
    scale = jnp.where(j == 0, ATTN_SCALE, 1.0)
    qkb_ref[...] = (zn * scale).astype(BF16)


def _qk_proj(rows, x, g, scale, shift, w_in_b, gqk, pmat):
    t = rows.n_tokens
    return pl.pallas_call(
        _qk_kernel,
        out_shape=(jax.ShapeDtypeStruct((t, 2 * D_MODEL), BF16),
                   jax.ShapeDtypeStruct((2, t, D_MODEL), F32)),
        grid=(rows.n_tiles, 2),
        in_specs=[rows.row_spec(),
                  pl.BlockSpec((1, D_MODEL), lambda i, j: (0, 0)),
                  rows.mod_spec(), rows.mod_spec(),
                  pl.BlockSpec((D_MODEL, D_MODEL), lambda i, j: (0, j)),
                  pl.BlockSpec((1, 1, D_MODEL), lambda i, j: (j, 0, 0)),
                  pl.BlockSpec((D_MODEL, D_MODEL), lambda i, j: (0, 0))],
        out_specs=(pl.BlockSpec((rows.tm, D_MODEL), lambda i, j: (i, j)),
                   pl.BlockSpec((1, rows.tm, D_MODEL), lambda i, j: (j, i, 0))),
        scratch_shapes=[pltpu.VMEM((rows.tm, D_MODEL), BF16)],
        compiler_params=_params("parallel", "arbitrary"),
        name="qk_proj",
    )(x, g, scale, shift, w_in_b, gqk, pmat)


def _v_kernel(x_ref, g_ref, sc_ref, sh_ref, w_ref, vf_ref, vb_ref):
    h = _rms_mod(x_ref[...], g_ref[...], sc_ref[0], sh_ref[0]).astype(BF16)
    z = jnp.dot(h, w_ref[...], preferred_element_type=F32)
    vf_ref[...] = z
    vb_ref[...] = z.astype(BF16)


def _v_proj(rows, x, g, scale, shift, w_in_b):
    t = rows.n_tokens
    return pl.pallas_call(
        _v_kernel,
        out_shape=(jax.ShapeDtypeStruct((t, D_MODEL), F32),
                   jax.ShapeDtypeStruct((t, D_MODEL), BF16)),
        grid=(rows.n_tiles,),
        in_specs=[rows.row_spec(),
                  pl.BlockSpec((1, D_MODEL), lambda i: (0, 0)),
                  rows.mod_spec(), rows.mod_spec(),
                  pl.BlockSpec((D_MODEL, D_MODEL), lambda i: (0, 2))],
        out_specs=(rows.row_spec(), rows.row_spec()),
        compiler_params=_params("parallel"),
        name="v_proj",
    )(x, g, scale, shift, w_in_b)


def _conv_kernel(*refs, tiles_per_batch, rows_per_batch, tail_rows):
    if tiles_per_batch is None:
        (x_ref, g_ref, sc_ref, sh_ref, w_ref, wc_ref, s1_ref, s2_ref,
         sga_ref, cvp_ref, tail_ref, h_ref, a_ref, b_ref, carry_ref) = refs
    else:
        (x_ref, g_ref, sc_ref, sh_ref, w_ref, wc_ref,
         sga_ref, cvp_ref, tail_ref, h_ref, a_ref, b_ref, carry_ref) = refs
    i = pl.program_id(0)
    j = pl.program_id(1)

    @pl.when(j == 0)
    def _():
        h_ref[...] = _rms_mod(x_ref[...], g_ref[...], sc_ref[0], sh_ref[0]).astype(BF16)

    z = jnp.dot(h_ref[...], w_ref[...], preferred_element_type=F32)

    @pl.when(j == 0)
    def _():
        a_ref[...] = z

    @pl.when(j == 1)
    def _():
        b_ref[...] = z

    @pl.when(j == 2)
    def _():
        u = b_ref[...] * z
        tm = u.shape[0]
        row = lax.broadcasted_iota(jnp.int32, (tm, 1), 0)
        r1 = pltpu.roll(u, 1, 0)
        r2 = pltpu.roll(u, 2, 0)
        if tiles_per_batch is None:
            t = row & (rows_per_batch - 1)
            u1 = jnp.where(t >= 1, r1, s1_ref[...])
            u2 = jnp.where(t >= 2, r2, s2_ref[...])
        else:
            first = (i % tiles_per_batch) == 0
            c = jnp.where(first, 0.0, carry_ref[...])
            u1 = jnp.where(row == 0, c[7:8], r1)
            u2 = jnp.where(row == 0, c[6:7], jnp.where(row == 1, c[7:8], r2))
            carry_ref[...] = u[tm - SUBLANES:]
        wc = wc_ref[...]
        yc = wc[0:1] * u2 + wc[1:2] * u1 + wc[2:3] * u
        a_ref[...] = a_ref[...] * yc
        tail_ref[0] = u[tm - tail_rows:]

    @pl.when(j == 3)
    def _():
        sga_ref[...] = jax.nn.sigmoid(z).astype(sga_ref.dtype)

    @pl.when(j == 4)
    def _():
        cvp_ref[...] = (jax.nn.sigmoid(z) * a_ref[...]).astype(cvp_ref.dtype)


def _conv_proj(rows, x, g, scale, shift, w_in_b, w_conv, state_rows, out_dtype):
    t = rows.n_tokens
    sample_mode = rows.tiles_per_batch is None
    tail_rows = rows.tm if sample_mode else SUBLANES
    n_tail_blocks = 1 if sample_mode else t // rows.rows_per_batch
    in_specs = [rows.row_spec(),
                pl.BlockSpec((1, D_MODEL), lambda i, j: (0, 0)),
                rows.mod_spec(), rows.mod_spec(),
                pl.BlockSpec((D_MODEL, D_MODEL), lambda i, j: (0, 3 + j)),
                pl.BlockSpec((CONV_WIDTH, D_MODEL), lambda i, j: (0, 0))]
    args = [x, g, scale, shift, w_in_b, w_conv]
    if sample_mode:
        in_specs += [rows.row_spec(), rows.row_spec()]
        args += list(state_rows)
        tail_spec = pl.BlockSpec((1, tail_rows, D_MODEL), lambda i, j: (0, 0, 0))
    else:
        tpb = rows.tiles_per_batch
        tail_spec = pl.BlockSpec((1, tail_rows, D_MODEL), lambda i, j: (i // tpb, 0, 0))
    kern = functools.partial(_conv_kernel, tiles_per_batch=rows.tiles_per_batch,
                             rows_per_batch=rows.rows_per_batch, tail_rows=tail_rows)
    return pl.pallas_call(
        kern,
        out_shape=(jax.ShapeDtypeStruct((t, D_MODEL), out_dtype),
                   jax.ShapeDtypeStruct((t, D_MODEL), out_dtype),
                   jax.ShapeDtypeStruct((n_tail_blocks, tail_rows, D_MODEL), F32)),
        grid=(rows.n_tiles, 5),
        in_specs=in_specs,
        out_specs=(rows.row_spec(), rows.row_spec(), tail_spec),
        scratch_shapes=[pltpu.VMEM((rows.tm, D_MODEL), BF16),
                        pltpu.VMEM((rows.tm, D_MODEL), F32),
                        pltpu.VMEM((rows.tm, D_MODEL), F32),
                        pltpu.VMEM((SUBLANES, D_MODEL), F32)],
        compiler_params=_params("arbitrary", "arbitrary"),
        name="conv_proj",
    )(*args)


def _lambda_value(lq1, lk1, lq2, lk2):
    e1 = jnp.exp(jnp.sum(lq1 * lk1, axis=-1, keepdims=True))
    e2 = jnp.exp(jnp.sum(lq2 * lk2, axis=-1, keepdims=True))
    return e1 - e2 + LAMBDA_INIT


def _subln_mix(o, g_sub, sga, cvp):
    ms = jnp.mean(o * o, axis=-1, keepdims=True)
    attn = (o * lax.rsqrt(ms + NORM_EPS) * g_sub) * (1.0 - LAMBDA_INIT)
    return sga * attn + cvp


def _online_update(s, v, m_ref, l_ref, acc_ref):
    m_prev = m_ref[...]
    m_new = jnp.maximum(m_prev, jnp.max(s, axis=-1, keepdims=True))
    alpha = jnp.exp(m_prev - m_new)
    p = jnp.exp(s - m_new)
    l_ref[...] = alpha * l_ref[...] + jnp.sum(p, axis=-1, keepdims=True)
    acc_ref[...] = alpha * acc_ref[...] + jnp.dot(p.astype(BF16), v, preferred_element_type=F32)
    m_ref[...] = m_new


_NT = (((1,), (1,)), ((), ()))


def _attn_prompt_kernel(q_ref, k_ref, v_ref, sga_ref, cvp_ref, gs_ref, lq1_ref, lk1_ref, lq2_ref, lk2_ref,
                        o_ref, m0_ref, l0_ref, a0_ref, m1_ref, l1_ref, a1_ref):
    qi = pl.program_id(2)
    ki = pl.program_id(3)

    @pl.when(ki == 0)
    def _():
        for m_ref, l_ref, a_ref in ((m0_ref, l0_ref, a0_ref), (m1_ref, l1_ref, a1_ref)):
            m_ref[...] = jnp.full(m_ref.shape, -jnp.inf, F32)
            l_ref[...] = jnp.zeros(l_ref.shape, F32)
            a_ref[...] = jnp.zeros(a_ref.shape, F32)

    def step(masked):
        q = q_ref[...]
        k = k_ref[...]
        v = v_ref[...]
        lane = lax.broadcasted_iota(jnp.int32, (1, V_DIM), 1)
        zero = jnp.zeros((), BF16)
        qs = (jnp.where(lane < HEAD_DIM, q, zero), jnp.where(lane >= HEAD_DIM, q, zero))
        states = ((m0_ref, l0_ref, a0_ref), (m1_ref, l1_ref, a1_ref))
        for qm, (m_ref, l_ref, a_ref) in zip(qs, states):
            s = lax.dot_general(qm, k, _NT, preferred_element_type=F32)
            if masked:
                r = lax.broadcasted_iota(jnp.int32, s.shape, 0)
                c = lax.broadcasted_iota(jnp.int32, s.shape, 1)
                s = jnp.where(c <= r, s, -jnp.inf)
            _online_update(s, v, m_ref, l_ref, a_ref)

    @pl.when(ki < qi)
    def _():
        step(False)

    @pl.when(ki == qi)
    def _():
        step(True)
        lam = _lambda_value(lq1_ref[...], lk1_ref[...], lq2_ref[...], lk2_ref[...])
        o = a0_ref[...] / l0_ref[...] - lam * (a1_ref[...] / l1_ref[...])
        mix = _subln_mix(o, gs_ref[...], sga_ref[...].astype(F32), cvp_ref[...].astype(F32))
        o_ref[...] = mix.astype(o_ref.dtype)


def _attn_prompt(qk_b, v_b, sga, cvp, g_sub, lams, batch, seq):
    nb = seq // ATTN_BLOCK
    tq = ATTN_BLOCK
    q_spec = pl.BlockSpec((tq, V_DIM), lambda b, h, qi, ki: (b * nb + qi, h))
    k_spec = pl.BlockSpec((tq, V_DIM), lambda b, h, qi, ki: (b * nb + jnp.minimum(ki, qi), N_HEADS + h))
    v_spec = pl.BlockSpec((tq, V_DIM), lambda b, h, qi, ki: (b * nb + jnp.minimum(ki, qi), h))
    vec64 = pl.BlockSpec((1, HEAD_DIM), lambda b, h, qi, ki: (0, 0))
    return pl.pallas_call(
        _attn_prompt_kernel,
        out_shape=jax.ShapeDtypeStruct((batch * seq, D_MODEL), BF16),
        grid=(batch, N_HEADS, nb, nb),
        in_specs=[q_spec, k_spec, v_spec, q_spec, q_spec,
                  pl.BlockSpec((1, V_DIM), lambda b, h, qi, ki: (0, 0)),
                  vec64, vec64, vec64, vec64],
        out_specs=q_spec,
        scratch_shapes=[pltpu.VMEM((tq, 1), F32), pltpu.VMEM((tq, 1), F32), pltpu.VMEM((tq, V_DIM), F32),
                        pltpu.VMEM((tq, 1), F32), pltpu.VMEM((tq, 1), F32), pltpu.VMEM((tq, V_DIM), F32)],
        compiler_params=_params("parallel", "parallel", "parallel", "arbitrary"),
        name="attn_prompt",
    )(qk_b, qk_b, v_b, sga, cvp, g_sub, *lams)


def _attn_sample_kernel(pt_ref, q_ref, kn_ref, vn_ref, kc_ref, vc_ref, sga_ref, cvp_ref, gs_ref,
                        lq1_ref, lk1_ref, lq2_ref, lk2_ref, o_ref, qbd_ref, m_ref, l_ref, acc_ref,
                        *, n_new):
    p = pl.program_id(1)
    n_rows = N_HEADS * 2 * n_new

    @pl.when(p == 0)
    def _():
        q = q_ref[...] * ATTN_SCALE
        qrep = jnp.concatenate([q] * (N_HEADS * 2), axis=0)
        r = lax.broadcasted_iota(jnp.int32, qrep.shape, 0)
        c = lax.broadcasted_iota(jnp.int32, qrep.shape, 1)
        qbd_ref[...] = jnp.where(c // HEAD_DIM == r // n_new, qrep, 0.0).astype(BF16)
        m_ref[...] = jnp.full(m_ref.shape, -jnp.inf, F32)
        l_ref[...] = jnp.zeros(l_ref.shape, F32)
        acc_ref[...] = jnp.zeros(acc_ref.shape, F32)

    s = lax.dot_general(qbd_ref[...], kc_ref[...].astype(BF16), _NT, preferred_element_type=F32)
    _online_update(s, vc_ref[...].astype(BF16), m_ref, l_ref, acc_ref)

    @pl.when(p == pl.num_programs(1) - 1)
    def _():
        s_new = lax.dot_general(qbd_ref[...], kn_ref[...].astype(BF16), _NT, preferred_element_type=F32)
        r = lax.broadcasted_iota(jnp.int32, s_new.shape, 0)
        c = lax.broadcasted_iota(jnp.int32, s_new.shape, 1)
        s_new = jnp.where(c <= (r & (n_new - 1)), s_new, -jnp.inf)
        _online_update(s_new, vn_ref[...].astype(BF16), m_ref, l_ref, acc_ref)
        lam = _lambda_value(lq1_ref[...], lk1_ref[...], lq2_ref[...], lk2_ref[...])
        acc = acc_ref[...] / l_ref[...]
        for h in range(N_HEADS):
            cols = slice(h * V_DIM, (h + 1) * V_DIM)
            r0 = h * 2 * n_new
            o = acc[r0:r0 + n_new, cols] - lam * acc[r0 + n_new:r0 + 2 * n_new, cols]
            o_ref[:, cols] = _subln_mix(o, gs_ref[...], sga_ref[:, cols], cvp_ref[:, cols])


def _attn_sample(qkf, vf, cache_k, cache_v, page_table, sga, cvp, g_sub, lams, n_seq, n_new):
    n_pages = page_table.shape[1]
    width = N_HEADS * V_DIM
    kc = cache_k.reshape(cache_k.shape[0], PAGE_SIZE, width)
    vc = cache_v.reshape(cache_v.shape[0], PAGE_SIZE, width)
    pt = page_table.reshape(-1)
    n_rows = N_HEADS * 2 * n_new
    row = pl.BlockSpec((n_new, width), lambda b, p, pt: (b, 0))
    page = pl.BlockSpec((None, PAGE_SIZE, width), lambda b, p, pt: (pt[b * n_pages + p], 0, 0))
    vec64 = pl.BlockSpec((1, HEAD_DIM), lambda b, p, pt: (0, 0))
    grid_spec = pltpu.PrefetchScalarGridSpec(
        num_scalar_prefetch=1,
        grid=(n_seq, n_pages),
        in_specs=[pl.BlockSpec((None, n_new, width), lambda b, p, pt: (0, b, 0)),
                  pl.BlockSpec((None, n_new, width), lambda b, p, pt: (1, b, 0)),
                  row, page, page, row, row,
                  pl.BlockSpec((1, V_DIM), lambda b, p, pt: (0, 0)),
                  vec64, vec64, vec64, vec64],
        out_specs=row,
        scratch_shapes=[pltpu.VMEM((n_rows, width), BF16),
                        pltpu.VMEM((n_rows, 1), F32), pltpu.VMEM((n_rows, 1), F32),
                        pltpu.VMEM((n_rows, width), F32)])
    return pl.pallas_call(
        functools.partial(_attn_sample_kernel, n_new=n_new),
        out_shape=jax.ShapeDtypeStruct((n_seq * n_new, width), F32),
        grid_spec=grid_spec,
        compiler_params=_params("parallel", "arbitrary"),
        name="attn_sample",
    )(pt, qkf, qkf, vf, kc, vc, sga, cvp, g_sub, *lams)


def _out_kernel(mix_ref, x_ref, wo_ref, g1_ref, g_ref, sc_ref, sh_ref, wrh_ref, wrl_ref, br_ref,
                xm_ref, h2_ref, idx_ref, gt_ref):
    y = jnp.dot(mix_ref[...].astype(BF16), wo_ref[...], preferred_element_type=F32)
    xm = x_ref[...] + g1_ref[0] * y
    xm_ref[...] = xm
    h2 = _rms_mod(xm, g_ref[...], sc_ref[0], sh_ref[0])
    hi = h2.astype(BF16)
    h2_ref[...] = hi
    lo = (h2 - hi.astype(F32)).astype(BF16)
    logits = (jnp.dot(hi, wrh_ref[...], preferred_element_type=F32)
              + jnp.dot(lo, wrh_ref[...], preferred_element_type=F32)
              + jnp.dot(hi, wrl_ref[...], preferred_element_type=F32)) + br_ref[...]
    lane = lax.broadcasted_iota(jnp.int32, logits.shape, 1).astype(F32)
    vals, idxs = [], []
    for _ in range(TOP_K):
        m = jnp.max(logits, axis=-1, keepdims=True)
        ix = jnp.min(jnp.where(logits == m, lane, float(LANES)), axis=-1, keepdims=True)
        logits = jnp.where(lane == ix, -jnp.inf, logits)
        vals.append(m)
        idxs.append(ix)
    es = [jnp.exp(v - vals[0]) for v in vals]
    denom = es[0] + es[1] + es[2] + es[3]
    idx_out = jnp.zeros(logits.shape, F32)
    gt_out = jnp.zeros(logits.shape, F32)
    for k in range(TOP_K):
        idx_out = jnp.where(lane == float(k), idxs[k], idx_out)
        gt_out = jnp.where(lane == float(k), es[k] / denom, gt_out)
    idx_ref[...] = idx_out.astype(jnp.int32)
    gt_ref[...] = gt_out


def _out_proj(rows, mix, x, w_o_b, gate1, g, scale, shift, wr_hi, wr_lo, b_r):
    t = rows.n_tokens
    full = lambda shape: pl.BlockSpec(shape, lambda i: (0,) * len(shape))
    return pl.pallas_call(
        _out_kernel,
        out_shape=(jax.ShapeDtypeStruct((t, D_MODEL), F32),
                   jax.ShapeDtypeStruct((t, D_MODEL), BF16),
                   jax.ShapeDtypeStruct((t, LANES), jnp.int32),
                   jax.ShapeDtypeStruct((t, LANES), F32)),
        grid=(rows.n_tiles,),
        in_specs=[rows.row_spec(), rows.row_spec(), full((D_MODEL, D_MODEL)),
                  rows.mod_spec(), full((1, D_MODEL)), rows.mod_spec(), rows.mod_spec(),
                  full((D_MODEL, LANES)), full((D_MODEL, LANES)), full((1, LANES))],
        out_specs=(rows.row_spec(), rows.row_spec(), rows.row_spec(LANES), rows.row_spec(LANES)),
        compiler_params=_params("parallel"),
        name="out_proj_router",
    )(mix, x, w_o_b, gate1, g, scale, shift, wr_hi, wr_lo, b_r)


def _expert_kernel(be_ref, nu_ref, x_ref, wg_ref, wu_ref, wd_ref, bg_ref, bu_ref, bd_ref, y_ref):
    i = pl.program_id(0)

    @pl.when(i < nu_ref[0])
    def _():
        x = x_ref[...]
        g = jnp.dot(x, wg_ref[...], preferred_element_type=F32) + bg_ref[...]
        u = jnp.dot(x, wu_ref[...], preferred_element_type=F32) + bu_ref[...]
        gate = jnp.minimum(g, SWIGLU_LIMIT)
        up = jnp.clip(u, -SWIGLU_LIMIT, SWIGLU_LIMIT)
        glu = gate * jax.nn.sigmoid(SWIGLU_ALPHA * gate)
        a = ((up + 1.0) * glu).astype(BF16)
        y = jnp.dot(a, wd_ref[...], preferred_element_type=F32) + bd_ref[...]
        y_ref[...] = y.astype(y_ref.dtype)

    @pl.when(i >= nu_ref[0])
    def _():
        y_ref[...] = jnp.zeros(y_ref.shape, y_ref.dtype)


def _experts(x_sorted, block_e, n_used, w_g, w_u, w_d, b_g, b_u, b_d):
    rows = x_sorted.shape[0]
    n_blocks = rows // MOE_BLOCK
    wspec = pl.BlockSpec((None, D_MODEL, D_MODEL), lambda i, be, nu: (be[i], 0, 0))
    bspec = pl.BlockSpec((None, 1, D_MODEL), lambda i, be, nu: (be[i], 0, 0))
    xspec = pl.BlockSpec((MOE_BLOCK, D_MODEL), lambda i, be, nu: (i, 0))
    grid_spec = pltpu.PrefetchScalarGridSpec(
        num_scalar_prefetch=2, grid=(n_blocks,),
        in_specs=[xspec, wspec, wspec, wspec, bspec, bspec, bspec],
        out_specs=xspec)
    return pl.pallas_call(
        _expert_kernel,
        out_shape=jax.ShapeDtypeStruct((rows, D_MODEL), BF16),
        grid_spec=grid_spec,
        compiler_params=_params("arbitrary"),
        name="experts",
    )(block_e, n_used, x_sorted, w_g, w_u, w_d, b_g, b_u, b_d)


def _combine_kernel(xm_ref, yg_ref, gt_ref, g2_ref, o_ref):
    gt = gt_ref[...]
    acc = jnp.zeros(xm_ref.shape, F32)
    for k in range(TOP_K):
        acc = acc + gt[:, k:k + 1] * yg_ref[:, k * D_MODEL:(k + 1) * D_MODEL].astype(F32)
    o_ref[...] = xm_ref[...] + g2_ref[0] * acc


def _combine(rows, xm, y_gathered, gates, gate2):
    return pl.pallas_call(
        _combine_kernel,
        out_shape=jax.ShapeDtypeStruct((rows.n_tokens, D_MODEL), F32),
        grid=(rows.n_tiles,),
        in_specs=[rows.row_spec(), rows.row_spec(TOP_K * D_MODEL), rows.row_spec(LANES), rows.mod_spec()],
        out_specs=rows.row_spec(),
        compiler_params=_params("parallel"),
        name="moe_combine",
    )(xm, y_gathered, gates, gate2)


def _route(top_idx):
    t = top_idx.shape[0]
    a = t * TOP_K
    flat_e = top_idx.reshape(a)
    flat_tok = jnp.arange(a, dtype=jnp.int32) // TOP_K
    order = jnp.argsort(flat_e)
    sorted_e = flat_e[order]
    counts = jnp.bincount(flat_e, length=N_EXPERTS).astype(jnp.int32)
    padded = (counts + MOE_BLOCK - 1) // MOE_BLOCK * MOE_BLOCK
    pad_end = jnp.cumsum(padded)
    pad_start = pad_end - padded
    start = jnp.cumsum(counts) - counts
    dest = pad_start[sorted_e] + jnp.arange(a, dtype=jnp.int32) - start[sorted_e]
    n_blocks = -(-a // MOE_BLOCK) + N_EXPERTS
    tok_buf = jnp.full((n_blocks * MOE_BLOCK,), t, jnp.int32).at[dest].set(flat_tok[order])
    block_e = jnp.minimum(
        jnp.searchsorted(pad_end, jnp.arange(n_blocks, dtype=jnp.int32) * MOE_BLOCK, side='right'),
        N_EXPERTS - 1).astype(jnp.int32)
    slot = jnp.zeros((a,), jnp.int32).at[order].set(dest)
    n_used = (pad_end[-1:] // MOE_BLOCK).astype(jnp.int32)
    return tok_buf, block_e, slot, n_used


def _moe(rows, xm, h2, idx, gates, gate2, moe_w):
    t = rows.n_tokens
    tok_buf, block_e, slot, n_used = _route(idx[:, :TOP_K])
    h_pad = jnp.concatenate([h2, jnp.zeros((1, D_MODEL), h2.dtype)], axis=0)
    x_sorted = h_pad[tok_buf]
    y_buf = _experts(x_sorted, block_e, n_used, *moe_w)
    y_g = y_buf[slot].reshape(t, TOP_K * D_MODEL)
    return _combine(rows, xm, y_g, gates, gate2)


def _group(rows, x, ada, state_rows, attend, shared):
    (g_mix, w_in_b, gqk, pmat, w_conv, w_o_b, g_ffn, wr_hi, wr_lo, b_r, moe_w, mid_dtype) = shared
    shift1, scale1, gate1, shift2, scale2, gate2 = [rows.mod_array(m) for m in jnp.split(ada, 6, axis=-1)]
    qk_b, qk_f = _qk_proj(rows, x, g_mix, scale1, shift1, w_in_b, gqk, pmat)
    v_f, v_b = _v_proj(rows, x, g_mix, scale1, shift1, w_in_b)
    sga, cvp, tail = _conv_proj(rows, x, g_mix, scale1, shift1, w_in_b, w_conv, state_rows, mid_dtype)
    mix = attend(qk_b, qk_f, v_f, v_b, sga, cvp)
    xm, h2, idx, gates = _out_proj(rows, mix, x, w_o_b, gate1, g_ffn, scale2, shift2, wr_hi, wr_lo, b_r)
    y = _moe(rows, xm, h2, idx, gates, gate2, moe_w)
    return y, qk_f[1], v_f, tail


def kernel(x_prompt, x_sample, c_prompt, c_sample, cache_k, cache_v, state_conv, page_table, w_ada, b_ada, g_norm_mix, w_in, g_q, g_k, lambda_q1, lambda_k1, lambda_q2, lambda_k2, g_subln, w_conv, w_o, g_norm_ffn, w_router, b_router, w_gate_up, b_gate_up, w_down, b_down):
    assert w_in.shape[0] == 1, "single-layer stack"
    batch, seq, _ = x_prompt.shape
    n_seq, n_new, _ = x_sample.shape
    tp, ts = batch * seq, n_seq * n_new

    ada = _ada(jnp.concatenate([c_prompt, c_sample], axis=0), w_ada[0], b_ada[0][None])

    w_in_b = w_in[0].astype(BF16)
    w_o_b = w_o[0].astype(BF16)
    gqk = jnp.stack([jnp.tile(g_q[0], D_MODEL // HEAD_DIM), jnp.tile(g_k[0], D_MODEL // HEAD_DIM)])[:, None, :]
    blk = jnp.arange(D_MODEL, dtype=jnp.int32) // HEAD_DIM
    pmat = (blk[:, None] == blk[None, :]).astype(BF16)
    wr = jnp.pad(w_router[0], ((0, 0), (0, LANES - N_EXPERTS)))
    wr_hi = wr.astype(BF16)
    wr_lo = (wr - wr_hi.astype(F32)).astype(BF16)
    b_r = jnp.pad(b_router[0], (0, LANES - N_EXPERTS), constant_values=NEG_BIG)[None]
    wgu = w_gate_up[0]
    moe_w = (wgu[:, :, 0::2].astype(BF16), wgu[:, :, 1::2].astype(BF16), w_down[0].astype(BF16),
             b_gate_up[0][:, None, 0::2], b_gate_up[0][:, None, 1::2], b_down[0][:, None, :])
    g_mix = g_norm_mix[0][None]
    g_ffn = g_norm_ffn[0][None]
    g_sub = g_subln[0][None]
    lams = (lambda_q1[0][None], lambda_k1[0][None], lambda_q2[0][None], lambda_k2[0][None])

    def shared(mid_dtype):
        return (g_mix, w_in_b, gqk, pmat, w_conv[0], w_o_b, g_ffn, wr_hi, wr_lo, b_r, moe_w, mid_dtype)

    rows_p = _Rows(tp, seq)

    def attend_p(qk_b, qk_f, v_f, v_b, sga, cvp):
        return _attn_prompt(qk_b, v_b, sga, cvp, g_sub, lams, batch, seq)

    y_p, k_p, v_p, tail_p = _group(rows_p, x_prompt.reshape(tp, D_MODEL), ada[:batch], None, attend_p, shared(BF16))

    rows_s = _Rows(ts, n_new)
    st = state_conv[0]
    zeros = jnp.zeros((n_seq, n_new - 2, D_MODEL), F32)
    s1 = jnp.concatenate([st[:, 1:2], jnp.zeros((n_seq, n_new - 1, D_MODEL), F32)], axis=1).reshape(ts, D_MODEL)
    s2 = jnp.concatenate([st, zeros], axis=1).reshape(ts, D_MODEL)

    def attend_s(qk_b, qk_f, v_f, v_b, sga, cvp):
        return _attn_sample(qk_f, v_f, cache_k[0], cache_v[0], page_table, sga, cvp, g_sub, lams, n_seq, n_new)

    y_s, k_s, v_s, tail_s = _group(rows_s, x_sample.reshape(ts, D_MODEL), ada[batch:], (s1, s2), attend_s, shared(F32))

    tail_s = tail_s.reshape(n_seq, n_new, D_MODEL)
    return (y_p.reshape(batch, seq, D_MODEL),
            y_s.reshape(n_seq, n_new, D_MODEL),
            k_p.reshape(1, batch, seq, N_HEADS, 2, HEAD_DIM),
            v_p.reshape(1, batch, seq, N_HEADS, V_DIM),
            tail_p[:, SUBLANES - (CONV_WIDTH - 1):][None],
            k_s.reshape(1, n_seq, n_new, N_HEADS, 2, HEAD_DIM),
            v_s.reshape(1, n_seq, n_new, N_HEADS, V_DIM),
            tail_s[:, n_new - (CONV_WIDTH - 1):][None])
```

```python
import functools
import math

import jax
import jax.numpy as jnp
from jax import lax
from jax.experimental import pallas as pl
from jax.experimental.pallas import tpu as pltpu

F32 = jnp.float32
BF16 = jnp.bfloat16

D_MODEL = 1024
HEAD_DIM = 64
V_DIM = 2 * HEAD_DIM
N_HEADS = D_MODEL // V_DIM
ATTN_SCALE = HEAD_DIM ** -0.5
CONV_WIDTH = 3
PAGE_SIZE = 128
N_EXPERTS = 32
TOP_K = 4
SWIGLU_LIMIT = 7.0
SWIGLU_ALPHA = 1.702
NORM_EPS = 1e-6
LAMBDA_INIT = 0.8 - 0.6 * math.exp(-0.3 * 0)

VMEM_LIMIT_BYTES = 48 * 1024 * 1024
LANES = 128
SUBLANES = 8

ROW_TILE_PROMPT = 512
ATTN_BLOCK = 512
MOE_BLOCK = 256
ADA_COL_TILE = 1536
SAMPLE_PAGES_PER_STEP = 4
NEG_BIG = -1e30

_NT = (((1,), (1,)), ((), ()))


def _params(*sem):
    return pltpu.CompilerParams(dimension_semantics=sem, vmem_limit_bytes=VMEM_LIMIT_BYTES)


def _rms_mod(x, g, scale, shift):
    ms = jnp.mean(x * x, axis=-1, keepdims=True)
    return (x * lax.rsqrt(ms + NORM_EPS) * g) * (1.0 + scale) + shift


def _ada_kernel(c_ref, w_ref, b_ref, o_ref):
    c = c_ref[...]
    s = (c * jax.nn.sigmoid(c)).astype(BF16)
    o_ref[...] = jnp.dot(s, w_ref[...].astype(BF16), preferred_element_type=F32) + b_ref[...]


def _ada(c_all, w_ada, b_ada):
    n = c_all.shape[0]
    width = w_ada.shape[1]
    return pl.pallas_call(
        _ada_kernel,
        out_shape=jax.ShapeDtypeStruct((n, width), F32),
        grid=(width // ADA_COL_TILE,),
        in_specs=[pl.BlockSpec((n, D_MODEL), lambda j: (0, 0)),
                  pl.BlockSpec((D_MODEL, ADA_COL_TILE), lambda j: (0, j)),
                  pl.BlockSpec((1, ADA_COL_TILE), lambda j: (0, j))],
        out_specs=pl.BlockSpec((n, ADA_COL_TILE), lambda j: (0, j)),
        compiler_params=_params("arbitrary"),
        name="ada",
    )(c_all, w_ada, b_ada)


class _Rows:
    def __init__(self, n_tokens, rows_per_batch):
        if rows_per_batch >= ROW_TILE_PROMPT:
            self.tm = ROW_TILE_PROMPT
            self.tiles_per_batch = rows_per_batch // self.tm
            self.mod_rows = 1
        else:
            self.tm = n_tokens
            self.tiles_per_batch = None
            self.mod_rows = n_tokens
        self.n_tokens = n_tokens
        self.rows_per_batch = rows_per_batch
        self.n_tiles = n_tokens // self.tm

    def mod_array(self, m):
        if self.tiles_per_batch is not None:
            return m[:, None, :]
        return jnp.repeat(m, self.rows_per_batch, axis=0)[None]

    def mod_spec(self):
        if self.tiles_per_batch is not None:
            tpb = self.tiles_per_batch
            return pl.BlockSpec((1, 1, D_MODEL), lambda i, *_: (i // tpb, 0, 0))
        return pl.BlockSpec((1, self.mod_rows, D_MODEL), lambda i, *_: (0, 0, 0))

    def row_spec(self, width=D_MODEL):
        return pl.BlockSpec((self.tm, width), lambda i, *_: (i, 0))


def _qk_kernel(x_ref, g_ref, sc_ref, sh_ref, w_ref, gqk_ref, p_ref, qkb_ref, qkf_ref, h_ref):
    j = pl.program_id(1)

    @pl.when(j == 0)
    def _():
        h_ref[...] = _rms_mod(x_ref[...], g_ref[...], sc_ref[0], sh_ref[0]).astype(BF16)

    z = jnp.dot(h_ref[...], w_ref[...], preferred_element_type=F32)
    ss = jnp.dot((z * z).astype(BF16), p_ref[...], preferred_element_type=F32)
    zn = z * lax.rsqrt(ss * (1.0 / HEAD_DIM) + NORM_EPS) * gqk_ref[0]
    qkf_ref[0] = zn
    scale = jnp.where(j == 0, ATTN_SCALE, 1.0)
    qkb_ref[...] = (zn * scale).astype(BF16)


def _qk_proj(rows, x, g, scale, shift, w_in_b, gqk, pmat):
    t = rows.n_tokens
    return pl.pallas_call(
        _qk_kernel,
        out_shape=(jax.ShapeDtypeStruct((t, 2 * D_MODEL), BF16),
                   jax.ShapeDtypeStruct((2, t, D_MODEL), F32)),
        grid=(rows.n_tiles, 2),
        in_specs=[rows.row_spec(),
                  pl.BlockSpec((1, D_MODEL), lambda i, j: (0, 0)),
                  rows.mod_spec(), rows.mod_spec(),
                  pl.BlockSpec((D_MODEL, D_MODEL), lambda i, j: (0, j)),
                  pl.BlockSpec((1, 1, D_MODEL), lambda i, j: (j, 0, 0)),
                  pl.BlockSpec((D_MODEL, D_MODEL), lambda i, j: (0, 0))],
        out_specs=(pl.BlockSpec((rows.tm, D_MODEL), lambda i, j: (i, j)),
                   pl.BlockSpec((1, rows.tm, D_MODEL), lambda i, j: (j, i, 0))),
        scratch_shapes=[pltpu.VMEM((rows.tm, D_MODEL), BF16)],
        compiler_params=_params("parallel", "arbitrary"),
        name="qk_proj",
    )(x, g, scale, shift, w_in_b, gqk, pmat)


def _q_kernel(x_ref, g_ref, sc_ref, sh_ref, w_ref, gq_ref, p_ref, qb_ref):
    h = _rms_mod(x_ref[...], g_ref[...], sc_ref[0], sh_ref[0]).astype(BF16)
    z = jnp.dot(h, w_ref[...], preferred_element_type=F32)
    ss = jnp.dot((z * z).astype(BF16), p_ref[...], preferred_element_type=F32)
    zn = z * lax.rsqrt(ss * (1.0 / HEAD_DIM) + NORM_EPS) * gq_ref[0]
    qb_ref[...] = (zn * ATTN_SCALE).astype(BF16)


def _q_proj(rows, x, g, scale, shift, w_in_b, gqk, pmat):
    return pl.pallas_call(
        _q_kernel,
        out_shape=jax.ShapeDtypeStruct((rows.n_tokens, D_MODEL), BF16),
        grid=(rows.n_tiles,),
        in_specs=[rows.row_spec(),
                  pl.BlockSpec((1, D_MODEL), lambda i: (0, 0)),
                  rows.mod_spec(), rows.mod_spec(),
                  pl.BlockSpec((D_MODEL, D_MODEL), lambda i: (0, 0)),
                  pl.BlockSpec((1, 1, D_MODEL), lambda i: (0, 0, 0)),
                  pl.BlockSpec((D_MODEL, D_MODEL), lambda i: (0, 0))],
        out_specs=rows.row_spec(),
        compiler_params=_params("parallel"),
        name="q_proj",
    )(x, g, scale, shift, w_in_b, gqk, pmat)


def _kt_kernel(x_ref, g_ref, sc_ref, sh_ref, wt_ref, gk_ref, kf_ref, kb_ref):
    h = _rms_mod(x_ref[...], g_ref[...], sc_ref[0], sh_ref[0]).astype(BF16)
    zt = lax.dot_general(wt_ref[...], h, _NT, preferred_element_type=F32)
    tm = zt.shape[1]
    z3 = zt.reshape(D_MODEL // HEAD_DIM, HEAD_DIM, tm)
    ss = jnp.sum(z3 * z3, axis=1, keepdims=True)
    g3 = gk_ref[...].reshape(D_MODEL // HEAD_DIM, HEAD_DIM, 1)
    zn = (z3 * lax.rsqrt(ss * (1.0 / HEAD_DIM) + NORM_EPS) * g3).reshape(D_MODEL, tm)
    kf_ref[...] = zn
    kb_ref[...] = zn.astype(BF16)


def _kt_proj(rows, x, g, scale, shift, w_kt, gk_col, batch, seq):
    tpb = rows.tiles_per_batch
    out_spec = pl.BlockSpec((None, D_MODEL, rows.tm), lambda i: (i // tpb, 0, i % tpb))
    return pl.pallas_call(
        _kt_kernel,
        out_shape=(jax.ShapeDtypeStruct((batch, D_MODEL, seq), F32),
                   jax.ShapeDtypeStruct((batch, D_MODEL, seq), BF16)),
        grid=(rows.n_tiles,),
        in_specs=[rows.row_spec(),
                  pl.BlockSpec((1, D_MODEL), lambda i: (0, 0)),
                  rows.mod_spec(), rows.mod_spec(),
                  pl.BlockSpec((D_MODEL, D_MODEL), lambda i: (0, 0)),
                  pl.BlockSpec((D_MODEL, 1), lambda i: (0, 0))],
        out_specs=(out_spec, out_spec),
        compiler_params=_params("parallel"),
        name="kt_proj",
    )(x, g, scale, shift, w_kt, gk_col)


def _v_kernel(x_ref, g_ref, sc_ref, sh_ref, w_ref, vf_ref, vb_ref):
    h = _rms_mod(x_ref[...], g_ref[...], sc_ref[0], sh_ref[0]).astype(BF16)
    z = jnp.dot(h, w_ref[...], preferred_element_type=F32)
    vf_ref[...] = z
    vb_ref[...] = z.astype(BF16)


def _v_proj(rows, x, g, scale, shift, w_in_b):
    t = rows.n_tokens
    return pl.pallas_call(
        _v_kernel,
        out_shape=(jax.ShapeDtypeStruct((t, D_MODEL), F32),
                   jax.ShapeDtypeStruct((t, D_MODEL), BF16)),
        grid=(rows.n_tiles,),
        in_specs=[rows.row_spec(),
                  pl.BlockSpec((1, D_MODEL), lambda i: (0, 0)),
                  rows.mod_spec(), rows.mod_spec(),
                  pl.BlockSpec((D_MODEL, D_MODEL), lambda i: (0, 2))],
        out_specs=(rows.row_spec(), rows.row_spec()),
        compiler_params=_params("parallel"),
        name="v_proj",
    )(x, g, scale, shift, w_in_b)


def _conv_kernel(*refs, tiles_per_batch, rows_per_batch, tail_rows):
    if tiles_per_batch is None:
        (x_ref, g_ref, sc_ref, sh_ref, w_ref, wc_ref, s1_ref, s2_ref,
         sga_ref, cvp_ref, tail_ref, h_ref, a_ref, b_ref, carry_ref) = refs
    else:
        (x_ref, g_ref, sc_ref, sh_ref, w_ref, wc_ref,
         sga_ref, cvp_ref, tail_ref, h_ref, a_ref, b_ref, carry_ref) = refs
    i = pl.program_id(0)
    j = pl.program_id(1)

    @pl.when(j == 0)
    def _():
        h_ref[...] = _rms_mod(x_ref[...], g_ref[...], sc_ref[0], sh_ref[0]).astype(BF16)

    z = jnp.dot(h_ref[...], w_ref[...], preferred_element_type=F32)

    @pl.when(j == 0)
    def _():
        a_ref[...] = z

    @pl.when(j == 1)
    def _():
        b_ref[...] = z

    @pl.when(j == 2)
    def _():
        u = b_ref[...] * z
        tm = u.shape[0]
        row = lax.broadcasted_iota(jnp.int32, (tm, 1), 0)
        r1 = pltpu.roll(u, 1, 0)
        r2 = pltpu.roll(u, 2, 0)
        if tiles_per_batch is None:
            t = row & (rows_per_batch - 1)
            u1 = jnp.where(t >= 1, r1, s1_ref[...])
            u2 = jnp.where(t >= 2, r2, s2_ref[...])
        else:
            first = (i % tiles_per_batch) == 0
            c = jnp.where(first, 0.0, carry_ref[...])
            u1 = jnp.where(row == 0, c[7:8], r1)
            u2 = jnp.where(row == 0, c[6:7], jnp.where(row == 1, c[7:8], r2))
            carry_ref[...] = u[tm - SUBLANES:]
        wc = wc_ref[...]
        yc = wc[0:1] * u2 + wc[1:2] * u1 + wc[2:3] * u
        a_ref[...] = a_ref[...] * yc
        tail_ref[0] = u[tm - tail_rows:]

    @pl.when(j == 3)
    def _():
        sga_ref[...] = jax.nn.sigmoid(z).astype(sga_ref.dtype)

    @pl.when(j == 4)
    def _():
        cvp_ref[...] = (jax.nn.sigmoid(z) * a_ref[...]).astype(cvp_ref.dtype)


def _conv_proj(rows, x, g, scale, shift, w_in_b, w_conv, state_rows, out_dtype):
    t = rows.n_tokens
    sample_mode = rows.tiles_per_batch is None
    tail_rows = rows.tm if sample_mode else SUBLANES
    n_tail_blocks = 1 if sample_mode else t // rows.rows_per_batch
    in_specs = [rows.row_spec(),
                pl.BlockSpec((1, D_MODEL), lambda i, j: (0, 0)),
                rows.mod_spec(), rows.mod_spec(),
                pl.BlockSpec((D_MODEL, D_MODEL), lambda i, j: (0, 3 + j)),
                pl.BlockSpec((CONV_WIDTH, D_MODEL), lambda i, j: (0, 0))]
    args = [x, g, scale, shift, w_in_b, w_conv]
    if sample_mode:
        in_specs += [rows.row_spec(), rows.row_spec()]
        args += list(state_rows)
        tail_spec = pl.BlockSpec((1, tail_rows, D_MODEL), lambda i, j: (0, 0, 0))
    else:
        tpb = rows.tiles_per_batch
        tail_spec = pl.BlockSpec((1, tail_rows, D_MODEL), lambda i, j: (i // tpb, 0, 0))
    kern = functools.partial(_conv_kernel, tiles_per_batch=rows.tiles_per_batch,
                             rows_per_batch=rows.rows_per_batch, tail_rows=tail_rows)
    return pl.pallas_call(
        kern,
        out_shape=(jax.ShapeDtypeStruct((t, D_MODEL), out_dtype),
                   jax.ShapeDtypeStruct((t, D_MODEL), out_dtype),
                   jax.ShapeDtypeStruct((n_tail_blocks, tail_rows, D_MODEL), F32)),
        grid=(rows.n_tiles, 5),
        in_specs=in_specs,
        out_specs=(rows.row_spec(), rows.row_spec(), tail_spec),
        scratch_shapes=[pltpu.VMEM((rows.tm, D_MODEL), BF16),
                        pltpu.VMEM((rows.tm, D_MODEL), F32),
                        pltpu.VMEM((rows.tm, D_MODEL), F32),
                        pltpu.VMEM((SUBLANES, D_MODEL), F32)],
        compiler_params=_params("arbitrary", "arbitrary"),
        name="conv_proj",
    )(*args)


def _lambda_value(lq1, lk1, lq2, lk2):
    e1 = jnp.exp(jnp.sum(lq1 * lk1, axis=-1, keepdims=True))
    e2 = jnp.exp(jnp.sum(lq2 * lk2, axis=-1, keepdims=True))
    return e1 - e2 + LAMBDA_INIT


def _subln_mix(o, g_sub, sga, cvp):
    ms = jnp.mean(o * o, axis=-1, keepdims=True)
    attn = (o * lax.rsqrt(ms + NORM_EPS) * g_sub) * (1.0 - LAMBDA_INIT)
    return sga * attn + cvp


def _flash_update(s, v, m_ref, l_ref, acc_ref):
    m_prev = m_ref[...]
    m_next = jnp.maximum(m_prev, jnp.max(s, axis=1, keepdims=True))
    alpha = jnp.exp(m_prev - m_next)
    p = jnp.exp(s - jnp.concatenate([m_next] * (s.shape[1] // LANES), axis=1))
    l_ref[...] = alpha * l_ref[...] + jnp.sum(p, axis=1, keepdims=True)
    acc_ref[...] = alpha * acc_ref[...] + jnp.dot(p.astype(BF16), v, preferred_element_type=F32)
    m_ref[...] = m_next


def _attn_prompt_kernel(q_ref, kt_ref, v_ref, sga_ref, cvp_ref, gs_ref, lq1_ref, lk1_ref, lq2_ref, lk2_ref,
                        o_ref, m0_ref, l0_ref, a0_ref, m1_ref, l1_ref, a1_ref):
    qi = pl.program_id(2)
    ki = pl.program_id(3)

    @pl.when(ki == 0)
    def _():
        for m_ref, l_ref, a_ref in ((m0_ref, l0_ref, a0_ref), (m1_ref, l1_ref, a1_ref)):
            m_ref[...] = jnp.full(m_ref.shape, -jnp.inf, F32)
            l_ref[...] = jnp.zeros(l_ref.shape, F32)
            a_ref[...] = jnp.zeros(a_ref.shape, F32)

    def step(masked):
        q = q_ref[...]
        kt = kt_ref[...]
        v = v_ref[...]
        lane = lax.broadcasted_iota(jnp.int32, (1, V_DIM), 1)
        zero = jnp.zeros((), BF16)
        qs = (jnp.where(lane < HEAD_DIM, q, zero), jnp.where(lane >= HEAD_DIM, q, zero))
        states = ((m0_ref, l0_ref, a0_ref), (m1_ref, l1_ref, a1_ref))
        for qm, (m_ref, l_ref, a_ref) in zip(qs, states):
            s = jnp.dot(qm, kt, preferred_element_type=F32)
            if masked:
                r = lax.broadcasted_iota(jnp.int32, s.shape, 0)
                c = lax.broadcasted_iota(jnp.int32, s.shape, 1)
                s = jnp.where(c <= r, s, -jnp.inf)
            _flash_update(s, v, m_ref, l_ref, a_ref)

    @pl.when(ki < qi)
    def _():
        step(False)

    @pl.when(ki == qi)
    def _():
        step(True)
        lam = _lambda_value(lq1_ref[...], lk1_ref[...], lq2_ref[...], lk2_ref[...])
        o = a0_ref[...] / l0_ref[...] - lam * (a1_ref[...] / l1_ref[...])
        mix = _subln_mix(o, gs_ref[...], sga_ref[...].astype(F32), cvp_ref[...].astype(F32))
        o_ref[...] = mix.astype(o_ref.dtype)


def _attn_prompt(q_b, kt_b, v_b, sga, cvp, g_sub, lams, batch, seq):
    nb = seq // ATTN_BLOCK
    tq = ATTN_BLOCK
    q_spec = pl.BlockSpec((tq, V_DIM), lambda b, h, qi, ki: (b * nb + qi, h))
    k_spec = pl.BlockSpec((None, V_DIM, tq), lambda b, h, qi, ki: (b, h, jnp.minimum(ki, qi)))
    v_spec = pl.BlockSpec((tq, V_DIM), lambda b, h, qi, ki: (b * nb + jnp.minimum(ki, qi), h))
    vec64 = pl.BlockSpec((1, HEAD_DIM), lambda b, h, qi, ki: (0, 0))
    return pl.pallas_call(
        _attn_prompt_kernel,
        out_shape=jax.ShapeDtypeStruct((batch * seq, D_MODEL), BF16),
        grid=(batch, N_HEADS, nb, nb),
        in_specs=[q_spec, k_spec, v_spec, q_spec, q_spec,
                  pl.BlockSpec((1, V_DIM), lambda b, h, qi, ki: (0, 0)),
                  vec64, vec64, vec64, vec64],
        out_specs=q_spec,
        scratch_shapes=[pltpu.VMEM((tq, V_DIM), F32)] * 6,
        compiler_params=_params("parallel", "parallel", "parallel", "arbitrary"),
        name="attn_prompt",
    )(q_b, kt_b, v_b, sga, cvp, g_sub, *lams)


def _attn_sample_kernel(pt_ref, q_ref, kn_ref, vn_ref, *rest, n_new, n_group):
    kc_refs = rest[:n_group]
    vc_refs = rest[n_group:2 * n_group]
    (sga_ref, cvp_ref, gs_ref, lq1_ref, lk1_ref, lq2_ref, lk2_ref,
     o_ref, qbd_ref, m_ref, l_ref, acc_ref) = rest[2 * n_group:]
    p = pl.program_id(1)
    rows_per_head = 2 * n_new

    @pl.when(p == 0)
    def _():
        q = q_ref[...] * ATTN_SCALE
        qrep = jnp.concatenate([q] * (N_HEADS * 2), axis=0)
        r = lax.broadcasted_iota(jnp.int32, qrep.shape, 0)
        c = lax.broadcasted_iota(jnp.int32, qrep.shape, 1)
        qbd_ref[...] = jnp.where(c // HEAD_DIM == r // n_new, qrep, 0.0).astype(BF16)
        m_ref[...] = jnp.full(m_ref.shape, -jnp.inf, F32)
        l_ref[...] = jnp.zeros(l_ref.shape, F32)
        acc_ref[...] = jnp.zeros(acc_ref.shape, F32)

    def update(s, head_pv):
        m_prev = m_ref[...]
        m_next = jnp.maximum(m_prev, jnp.max(s, axis=1, keepdims=True))
        alpha = jnp.exp(m_prev - m_next)
        pr = jnp.exp(s - m_next[:, :s.shape[1]])
        l_ref[...] = alpha * l_ref[...] + jnp.sum(pr, axis=1, keepdims=True)
        pb = pr.astype(BF16)
        pv = jnp.concatenate(
            [head_pv(h, pb[h * rows_per_head:(h + 1) * rows_per_head]) for h in range(N_HEADS)], axis=0)
        acc_ref[...] = alpha * acc_ref[...] + pv
        m_ref[...] = m_next

    for g in range(n_group):
        s = jnp.dot(qbd_ref[...], kc_refs[g][...].astype(BF16), preferred_element_type=F32)
        update(s, lambda h, ph, g=g: jnp.dot(ph, vc_refs[g][:, h, :].astype(BF16), preferred_element_type=F32))

    @pl.when(p == pl.num_programs(1) - 1)
    def _():
        s_new = lax.dot_general(qbd_ref[...], kn_ref[...].astype(BF16), _NT, preferred_element_type=F32)
        r = lax.broadcasted_iota(jnp.int32, s_new.shape, 0)
        c = lax.broadcasted_iota(jnp.int32, s_new.shape, 1)
        s_new = jnp.where(c <= (r & (n_new - 1)), s_new, -jnp.inf)
        update(s_new, lambda h, ph: jnp.dot(
            ph.astype(F32), vn_ref[:, h * V_DIM:(h + 1) * V_DIM].astype(BF16).astype(F32),
            preferred_element_type=F32))
        lam = _lambda_value(lq1_ref[...], lk1_ref[...], lq2_ref[...], lk2_ref[...])
        acc = acc_ref[...] / l_ref[...]
        for h in range(N_HEADS):
            cols = slice(h * V_DIM, (h + 1) * V_DIM)
            r0 = h * rows_per_head
            o = acc[r0:r0 + n_new] - lam * acc[r0 + n_new:r0 + rows_per_head]
            o_ref[:, cols] = _subln_mix(o, gs_ref[...], sga_ref[:, cols], cvp_ref[:, cols])


def _attn_sample(qkf, vf, cache_k, cache_v, page_table, sga, cvp, g_sub, lams, n_seq, n_new):
    n_pages = page_table.shape[1]
    n_pool = cache_k.shape[0]
    width = N_HEADS * V_DIM
    grp = SAMPLE_PAGES_PER_STEP
    kc = jnp.transpose(cache_k, (0, 2, 3, 4, 1)).reshape(n_pool, width, PAGE_SIZE)
    pt = page_table.reshape(-1)
    n_rows = N_HEADS * 2 * n_new
    row = pl.BlockSpec((n_new, width), lambda b, p, pt: (b, 0))

    def page_index(g):
        return lambda b, p, pt: (pt[b * n_pages + p * grp + g], 0, 0)

    def page_index4(g):
        return lambda b, p, pt: (pt[b * n_pages + p * grp + g], 0, 0, 0)

    k_pages = [pl.BlockSpec((None, width, PAGE_SIZE), page_index(g)) for g in range(grp)]
    v_pages = [pl.BlockSpec((None, PAGE_SIZE, N_HEADS, V_DIM), page_index4(g)) for g in range(grp)]
    vec64 = pl.BlockSpec((1, HEAD_DIM), lambda b, p, pt: (0, 0))
    grid_spec = pltpu.PrefetchScalarGridSpec(
        num_scalar_prefetch=1,
        grid=(n_seq, n_pages // grp),
        in_specs=[pl.BlockSpec((None, n_new, width), lambda b, p, pt: (0, b, 0)),
                  pl.BlockSpec((None, n_new, width), lambda b, p, pt: (1, b, 0)),
                  row, *k_pages, *v_pages, row, row,
                  pl.BlockSpec((1, V_DIM), lambda b, p, pt: (0, 0)),
                  vec64, vec64, vec64, vec64],
        out_specs=row,
        scratch_shapes=[pltpu.VMEM((n_rows, width), BF16),
                        pltpu.VMEM((n_rows, V_DIM), F32), pltpu.VMEM((n_rows, V_DIM), F32),
                        pltpu.VMEM((n_rows, V_DIM), F32)])
    return pl.pallas_call(
        functools.partial(_attn_sample_kernel, n_new=n_new, n_group=grp),
        out_shape=jax.ShapeDtypeStruct((n_seq * n_new, width), F32),
        grid_spec=grid_spec,
        compiler_params=_params("parallel", "arbitrary"),
        name="attn_sample",
    )(pt, qkf, qkf, vf, *([kc] * grp), *([cache_v] * grp), sga, cvp, g_sub, *lams)


def _out_kernel(mix_ref, x_ref, wo_ref, g1_ref, g_ref, sc_ref, sh_ref, wrh_ref, wrl_ref, br_ref,
                xm_ref, h2_ref, idx_ref, gt_ref):
    y = jnp.dot(mix_ref[...].astype(BF16), wo_ref[...], preferred_element_type=F32)
    xm = x_ref[...] + g1_ref[0] * y
    xm_ref[...] = xm
    h2 = _rms_mod(xm, g_ref[...], sc_ref[0], sh_ref[0])
    hi = h2.astype(BF16)
    h2_ref[...] = hi
    lo = (h2 - hi.astype(F32)).astype(BF16)
    logits = (jnp.dot(hi, wrh_ref[...], preferred_element_type=F32)
              + jnp.dot(lo, wrh_ref[...], preferred_element_type=F32)
              + jnp.dot(hi, wrl_ref[...], preferred_element_type=F32)) + br_ref[...]
    lane = lax.broadcasted_iota(jnp.int32, logits.shape, 1).astype(F32)
    vals, idxs = [], []
    for _ in range(TOP_K):
        m = jnp.max(logits, axis=-1, keepdims=True)
        ix = jnp.min(jnp.where(logits == m, lane, float(LANES)), axis=-1, keepdims=True)
        logits = jnp.where(lane == ix, -jnp.inf, logits)
        vals.append(m)
        idxs.append(ix)
    es = [jnp.exp(v - vals[0]) for v in vals]
    denom = es[0] + es[1] + es[2] + es[3]
    idx_out = jnp.zeros(logits.shape, F32)
    gt_out = jnp.zeros(logits.shape, F32)
    for k in range(TOP_K):
        idx_out = jnp.where(lane == float(k), idxs[k], idx_out)
        gt_out = jnp.where(lane == float(k), es[k] / denom, gt_out)
    idx_ref[...] = idx_out.astype(jnp.int32)
    gt_ref[...] = gt_out


def _out_proj(rows, mix, x, w_o_b, gate1, g, scale, shift, wr_hi, wr_lo, b_r):
    t = rows.n_tokens
    full = lambda shape: pl.BlockSpec(shape, lambda i: (0,) * len(shape))
    return pl.pallas_call(
        _out_kernel,
        out_shape=(jax.ShapeDtypeStruct((t, D_MODEL), F32),
                   jax.ShapeDtypeStruct((t, D_MODEL), BF16),
                   jax.ShapeDtypeStruct((t, LANES), jnp.int32),
                   jax.ShapeDtypeStruct((t, LANES), F32)),
        grid=(rows.n_tiles,),
        in_specs=[rows.row_spec(), rows.row_spec(), full((D_MODEL, D_MODEL)),
                  rows.mod_spec(), full((1, D_MODEL)), rows.mod_spec(), rows.mod_spec(),
                  full((D_MODEL, LANES)), full((D_MODEL, LANES)), full((1, LANES))],
        out_specs=(rows.row_spec(), rows.row_spec(), rows.row_spec(LANES), rows.row_spec(LANES)),
        compiler_params=_params("parallel"),
        name="out_proj_router",
    )(mix, x, w_o_b, gate1, g, scale, shift, wr_hi, wr_lo, b_r)


def _expert_kernel(be_ref, nu_ref, x_ref, wgu_ref, wd_ref, bgu_ref, bd_ref, y_ref):
    i = pl.program_id(0)

    @pl.when(i < nu_ref[0])
    def _():
        gu = jnp.dot(x_ref[...], wgu_ref[...], preferred_element_type=F32) + bgu_ref[...]
        lane = lax.broadcasted_iota(jnp.int32, (1, gu.shape[1]), 1)
        gate = jnp.minimum(gu, SWIGLU_LIMIT)
        glu = gate * jax.nn.sigmoid(SWIGLU_ALPHA * gate)
        up1 = jnp.clip(gu, -SWIGLU_LIMIT, SWIGLU_LIMIT) + 1.0
        t = jnp.where((lane & 1) == 0, glu, up1)
        a = (t * pltpu.roll(t, gu.shape[1] - 1, 1)).astype(BF16)
        y = jnp.dot(a, wd_ref[...], preferred_element_type=F32) + bd_ref[...]
        y_ref[...] = y.astype(y_ref.dtype)

    @pl.when(i >= nu_ref[0])
    def _():
        y_ref[...] = jnp.zeros(y_ref.shape, y_ref.dtype)


def _experts(x_sorted, block_e, n_used, w_gu, w_d2, b_gu, b_d):
    rows = x_sorted.shape[0]
    n_blocks = rows // MOE_BLOCK
    d_gu = w_gu.shape[2]
    by_expert = lambda i, be, nu: (be[i], 0, 0)
    xspec = pl.BlockSpec((MOE_BLOCK, D_MODEL), lambda i, be, nu: (i, 0))
    grid_spec = pltpu.PrefetchScalarGridSpec(
        num_scalar_prefetch=2, grid=(n_blocks,),
        in_specs=[xspec,
                  pl.BlockSpec((None, D_MODEL, d_gu), by_expert),
                  pl.BlockSpec((None, d_gu, D_MODEL), by_expert),
                  pl.BlockSpec((None, 1, d_gu), by_expert),
                  pl.BlockSpec((None, 1, D_MODEL), by_expert)],
        out_specs=xspec)
    return pl.pallas_call(
        _expert_kernel,
        out_shape=jax.ShapeDtypeStruct((rows, D_MODEL), BF16),
        grid_spec=grid_spec,
        compiler_params=_params("arbitrary"),
        name="experts",
    )(block_e, n_used, x_sorted, w_gu, w_d2, b_gu, b_d)


def _combine_kernel(xm_ref, yg_ref, gt_ref, g2_ref, o_ref):
    gt = gt_ref[...]
    acc = jnp.zeros(xm_ref.shape, F32)
    for k in range(TOP_K):
        acc = acc + gt[:, k:k + 1] * yg_ref[:, k * D_MODEL:(k + 1) * D_MODEL].astype(F32)
    o_ref[...] = xm_ref[...] + g2_ref[0] * acc


def _combine(rows, xm, y_gathered, gates, gate2):
    return pl.pallas_call(
        _combine_kernel,
        out_shape=jax.ShapeDtypeStruct((rows.n_tokens, D_MODEL), F32),
        grid=(rows.n_tiles,),
        in_specs=[rows.row_spec(), rows.row_spec(TOP_K * D_MODEL), rows.row_spec(LANES), rows.mod_spec()],
        out_specs=rows.row_spec(),
        compiler_params=_params("parallel"),
        name="moe_combine",
    )(xm, y_gathered, gates, gate2)


def _route(top_idx):
    t = top_idx.shape[0]
    a = t * TOP_K
    flat_e = top_idx.reshape(a)
    flat_tok = jnp.arange(a, dtype=jnp.int32) // TOP_K
    order = jnp.argsort(flat_e)
    sorted_e = flat_e[order]
    counts = jnp.bincount(flat_e, length=N_EXPERTS).astype(jnp.int32)
    padded = (counts + MOE_BLOCK - 1) // MOE_BLOCK * MOE_BLOCK
    pad_end = jnp.cumsum(padded)
    pad_start = pad_end - padded
    start = jnp.cumsum(counts) - counts
    dest = pad_start[sorted_e] + jnp.arange(a, dtype=jnp.int32) - start[sorted_e]
    n_blocks = -(-a // MOE_BLOCK) + N_EXPERTS
    tok_buf = jnp.full((n_blocks * MOE_BLOCK,), t, jnp.int32).at[dest].set(flat_tok[order])
    block_e = jnp.minimum(
        jnp.searchsorted(pad_end, jnp.arange(n_blocks, dtype=jnp.int32) * MOE_BLOCK, side='right'),
        N_EXPERTS - 1).astype(jnp.int32)
    slot = jnp.zeros((a,), jnp.int32).at[order].set(dest)
    n_used = (pad_end[-1:] // MOE_BLOCK).astype(jnp.int32)
    return tok_buf, block_e, slot, n_used


def _moe(rows, xm, h2, idx, gates, gate2, moe_w):
    t = rows.n_tokens
    tok_buf, block_e, slot, n_used = _route(idx[:, :TOP_K])
    h_pad = jnp.concatenate([h2, jnp.zeros((1, D_MODEL), h2.dtype)], axis=0)
    x_sorted = h_pad[tok_buf]
    y_buf = _experts(x_sorted, block_e, n_used, *moe_w)
    y_g = y_buf[slot].reshape(t, TOP_K * D_MODEL)
    return _combine(rows, xm, y_g, gates, gate2)


def _group(rows, x, ada, state_rows, qk_and_attend, shared):
    (g_mix, w_in_b, w_conv, w_o_b, g_ffn, wr_hi, wr_lo, b_r, moe_w, mid_dtype) = shared
    shift1, scale1, gate1, shift2, scale2, gate2 = [rows.mod_array(m) for m in jnp.split(ada, 6, axis=-1)]
    v_f, v_b = _v_proj(rows, x, g_mix, scale1, shift1, w_in_b)
    sga, cvp, tail = _conv_proj(rows, x, g_mix, scale1, shift1, w_in_b, w_conv, state_rows, mid_dtype)
    mix, k_out = qk_and_attend(x, scale1, shift1, v_f, v_b, sga, cvp)
    xm, h2, idx, gates = _out_proj(rows, mix, x, w_o_b, gate1, g_ffn, scale2, shift2, wr_hi, wr_lo, b_r)
    y = _moe(rows, xm, h2, idx, gates, gate2, moe_w)
    return y, k_out, v_f, tail


def kernel(x_prompt, x_sample, c_prompt, c_sample, cache_k, cache_v, state_conv, page_table, w_ada, b_ada, g_norm_mix, w_in, g_q, g_k, lambda_q1, lambda_k1, lambda_q2, lambda_k2, g_subln, w_conv, w_o, g_norm_ffn, w_router, b_router, w_gate_up, b_gate_up, w_down, b_down):
    assert w_in.shape[0] == 1, "single-layer stack"
    batch, seq, _ = x_prompt.shape
    n_seq, n_new, _ = x_sample.shape
    tp, ts = batch * seq, n_seq * n_new
    n_chunks = D_MODEL // HEAD_DIM

    ada = _ada(jnp.concatenate([c_prompt, c_sample], axis=0), w_ada[0], b_ada[0][None])

    w_in_b = w_in[0].astype(BF16)
    w_kt = w_in[0][:, D_MODEL:2 * D_MODEL].T.astype(BF16)
    w_o_b = w_o[0].astype(BF16)
    gqk = jnp.stack([jnp.tile(g_q[0], n_chunks), jnp.tile(g_k[0], n_chunks)])[:, None, :]
    gk_col = jnp.tile(g_k[0], n_chunks)[:, None]
    blk = jnp.arange(D_MODEL, dtype=jnp.int32) // HEAD_DIM
    pmat = (blk[:, None] == blk[None, :]).astype(BF16)
    wr = jnp.pad(w_router[0], ((0, 0), (0, LANES - N_EXPERTS)))
    wr_hi = wr.astype(BF16)
    wr_lo = (wr - wr_hi.astype(F32)).astype(BF16)
    b_r = jnp.pad(b_router[0], (0, LANES - N_EXPERTS), constant_values=NEG_BIG)[None]
    w_dn = w_down[0].astype(BF16)
    w_d2 = jnp.stack([w_dn, jnp.zeros_like(w_dn)], axis=2).reshape(N_EXPERTS, 2 * w_dn.shape[1], D_MODEL)
    moe_w = (w_gate_up[0].astype(BF16), w_d2, b_gate_up[0][:, None, :], b_down[0][:, None, :])
    g_mix = g_norm_mix[0][None]
    g_ffn = g_norm_ffn[0][None]
    g_sub = g_subln[0][None]
    lams = (lambda_q1[0][None], lambda_k1[0][None], lambda_q2[0][None], lambda_k2[0][None])

    def shared(mid_dtype):
        return (g_mix, w_in_b, w_conv[0], w_o_b, g_ffn, wr_hi, wr_lo, b_r, moe_w, mid_dtype)

    rows_p = _Rows(tp, seq)

    def attend_p(x, scale1, shift1, v_f, v_b, sga, cvp):
        q_b = _q_proj(rows_p, x, g_mix, scale1, shift1, w_in_b, gqk, pmat)
        kt_f, kt_b = _kt_proj(rows_p, x, g_mix, scale1, shift1, w_kt, gk_col, batch, seq)
        mix = _attn_prompt(q_b, kt_b, v_b, sga, cvp, g_sub, lams, batch, seq)
        return mix, kt_f

    y_p, kt_p, v_p, tail_p = _group(rows_p, x_prompt.reshape(tp, D_MODEL), ada[:batch], None, attend_p, shared(BF16))

    rows_s = _Rows(ts, n_new)
    st = state_conv[0]
    zeros = jnp.zeros((n_seq, n_new - 2, D_MODEL), F32)
    s1 = jnp.concatenate([st[:, 1:2], jnp.zeros((n_seq, n_new - 1, D_MODEL), F32)], axis=1).reshape(ts, D_MODEL)
    s2 = jnp.concatenate([st, zeros], axis=1).reshape(ts, D_MODEL)

    def attend_s(x, scale1, shift1, v_f, v_b, sga, cvp):
        _, qk_f = _qk_proj(rows_s, x, g_mix, scale1, shift1, w_in_b, gqk, pmat)
        mix = _attn_sample(qk_f, v_f, cache_k[0], cache_v[0], page_table, sga, cvp, g_sub, lams, n_seq, n_new)
        return mix, qk_f[1]

    y_s, k_s, v_s, tail_s = _group(rows_s, x_sample.reshape(ts, D_MODEL), ada[batch:], (s1, s2), attend_s, shared(F32))

    tail_s = tail_s.reshape(n_seq, n_new, D_MODEL)
    k_p = kt_p.reshape(1, batch, N_HEADS, 2, HEAD_DIM, seq).transpose(0, 1, 5, 2, 3, 4)
    return (y_p.reshape(batch, seq, D_MODEL),
            y_s.reshape(n_seq, n_new, D_MODEL),
            k_p,
            v_p.reshape(1, batch, seq, N_HEADS, V_DIM),
            tail_p[:, SUBLANES - (CONV_WIDTH - 1):][None],
            k_s.reshape(1, n_seq, n_new, N_HEADS, 2, HEAD_DIM),
            v_s.reshape(1, n_seq, n_new, N_HEADS, V_DIM),
            tail_s[:, n_new - (CONV_WIDTH - 1):][None])
```

```python
import functools
import math

import jax
import jax.numpy as jnp
from jax import lax
from jax.experimental import pallas as pl
from jax.experimental.pallas import tpu as pltpu

F32 = jnp.float32
BF16 = jnp.bfloat16

D_MODEL = 1024
HEAD_DIM = 64
V_DIM = 2 * HEAD_DIM
N_HEADS = D_MODEL // V_DIM
ATTN_SCALE = HEAD_DIM ** -0.5
CONV_WIDTH = 3
PAGE_SIZE = 128
N_EXPERTS = 32
TOP_K = 4
SWIGLU_LIMIT = 7.0
SWIGLU_ALPHA = 1.702
NORM_EPS = 1e-6
LAMBDA_INIT = 0.8 - 0.6 * math.exp(-0.3 * 0)

VMEM_LIMIT_BYTES = 48 * 1024 * 1024
LANES = 128
PACKED = D_MODEL // 2
SUBLANES = 8

ROW_TILE_PROMPT = 512
ATTN_BLOCK = 512
MOE_BLOCK = 256
ADA_COL_TILE = 1536
SAMPLE_PAGES_PER_STEP = 4
NEG_BIG = -1e30

_NT = (((1,), (1,)), ((), ()))


def _params(*sem):
    return pltpu.CompilerParams(dimension_semantics=sem, vmem_limit_bytes=VMEM_LIMIT_BYTES)


def _rms_mod(x, g, scale, shift):
    ms = jnp.mean(x * x, axis=-1, keepdims=True)
    return (x * lax.rsqrt(ms + NORM_EPS) * g) * (1.0 + scale) + shift


def _ada_kernel(c_ref, w_ref, b_ref, o_ref):
    c = c_ref[...]
    s = (c * jax.nn.sigmoid(c)).astype(BF16)
    o_ref[...] = jnp.dot(s, w_ref[...].astype(BF16), preferred_element_type=F32) + b_ref[...]


def _ada(c_all, w_ada, b_ada):
    n = c_all.shape[0]
    width = w_ada.shape[1]
    return pl.pallas_call(
        _ada_kernel,
        out_shape=jax.ShapeDtypeStruct((n, width), F32),
        grid=(width // ADA_COL_TILE,),
        in_specs=[pl.BlockSpec((n, D_MODEL), lambda j: (0, 0)),
                  pl.BlockSpec((D_MODEL, ADA_COL_TILE), lambda j: (0, j)),
                  pl.BlockSpec((1, ADA_COL_TILE), lambda j: (0, j))],
        out_specs=pl.BlockSpec((n, ADA_COL_TILE), lambda j: (0, j)),
        compiler_params=_params("arbitrary"),
        name="ada",
    )(c_all, w_ada, b_ada)


class _Rows:
    def __init__(self, n_tokens, rows_per_batch):
        if rows_per_batch >= ROW_TILE_PROMPT:
            self.tm = ROW_TILE_PROMPT
            self.tiles_per_batch = rows_per_batch // self.tm
            self.mod_rows = 1
        else:
            self.tm = n_tokens
            self.tiles_per_batch = None
            self.mod_rows = n_tokens
        self.n_tokens = n_tokens
        self.rows_per_batch = rows_per_batch
        self.n_tiles = n_tokens // self.tm

    def mod_array(self, m):
        if self.tiles_per_batch is not None:
            return m[:, None, :]
        return jnp.repeat(m, self.rows_per_batch, axis=0)[None]

    def mod_spec(self):
        if self.tiles_per_batch is not None:
            tpb = self.tiles_per_batch
            return pl.BlockSpec((1, 1, D_MODEL), lambda i, *_: (i // tpb, 0, 0))
        return pl.BlockSpec((1, self.mod_rows, D_MODEL), lambda i, *_: (0, 0, 0))

    def row_spec(self, width=D_MODEL):
        return pl.BlockSpec((self.tm, width), lambda i, *_: (i, 0))


def _qk_kernel(x_ref, g_ref, sc_ref, sh_ref, w_ref, gqk_ref, p_ref, qkb_ref, qkf_ref, h_ref):
    j = pl.program_id(1)

    @pl.when(j == 0)
    def _():
        h_ref[...] = _rms_mod(x_ref[...], g_ref[...], sc_ref[0], sh_ref[0]).astype(BF16)

    z = jnp.dot(h_ref[...], w_ref[...], preferred_element_type=F32)
    ss = jnp.dot((z * z).astype(BF16), p_ref[...], preferred_element_type=F32)
    zn = z * lax.rsqrt(ss * (1.0 / HEAD_DIM) + NORM_EPS) * gqk_ref[0]
    qkf_ref[0] = zn
    scale = jnp.where(j == 0, ATTN_SCALE, 1.0)
    qkb_ref[...] = (zn * scale).astype(BF16)


def _qk_proj(rows, x, g, scale, shift, w_in_b, gqk, pmat):
    t = rows.n_tokens
    return pl.pallas_call(
        _qk_kernel,
        out_shape=(jax.ShapeDtypeStruct((t, 2 * D_MODEL), BF16),
                   jax.ShapeDtypeStruct((2, t, D_MODEL), F32)),
        grid=(rows.n_tiles, 2),
        in_specs=[rows.row_spec(),
                  pl.BlockSpec((1, D_MODEL), lambda i, j: (0, 0)),
                  rows.mod_spec(), rows.mod_spec(),
                  pl.BlockSpec((D_MODEL, D_MODEL), lambda i, j: (0, j)),
                  pl.BlockSpec((1, 1, D_MODEL), lambda i, j: (j, 0, 0)),
                  pl.BlockSpec((D_MODEL, D_MODEL), lambda i, j: (0, 0))],
        out_specs=(pl.BlockSpec((rows.tm, D_MODEL), lambda i, j: (i, j)),
                   pl.BlockSpec((1, rows.tm, D_MODEL), lambda i, j: (j, i, 0))),
        scratch_shapes=[pltpu.VMEM((rows.tm, D_MODEL), BF16)],
        compiler_params=_params("parallel", "arbitrary"),
        name="qk_proj",
    )(x, g, scale, shift, w_in_b, gqk, pmat)


def _q_kernel(x_ref, g_ref, sc_ref, sh_ref, w_ref, gq_ref, p_ref, qb_ref):
    h = _rms_mod(x_ref[...], g_ref[...], sc_ref[0], sh_ref[0]).astype(BF16)
    z = jnp.dot(h, w_ref[...], preferred_element_type=F32)
    ss = jnp.dot((z * z).astype(BF16), p_ref[...], preferred_element_type=F32)
    zn = z * lax.rsqrt(ss * (1.0 / HEAD_DIM) + NORM_EPS) * gq_ref[0]
    qb_ref[...] = (zn * ATTN_SCALE).astype(BF16)


def _q_proj(rows, x, g, scale, shift, w_in_b, gqk, pmat):
    return pl.pallas_call(
        _q_kernel,
        out_shape=jax.ShapeDtypeStruct((rows.n_tokens, D_MODEL), BF16),
        grid=(rows.n_tiles,),
        in_specs=[rows.row_spec(),
                  pl.BlockSpec((1, D_MODEL), lambda i: (0, 0)),
                  rows.mod_spec(), rows.mod_spec(),
                  pl.BlockSpec((D_MODEL, D_MODEL), lambda i: (0, 0)),
                  pl.BlockSpec((1, 1, D_MODEL), lambda i: (0, 0, 0)),
                  pl.BlockSpec((D_MODEL, D_MODEL), lambda i: (0, 0))],
        out_specs=rows.row_spec(),
        compiler_params=_params("parallel"),
        name="q_proj",
    )(x, g, scale, shift, w_in_b, gqk, pmat)


def _kt_kernel(x_ref, g_ref, sc_ref, sh_ref, wt_ref, gk_ref, kf_ref, kb_ref):
    h = _rms_mod(x_ref[...], g_ref[...], sc_ref[0], sh_ref[0]).astype(BF16)
    zt = lax.dot_general(wt_ref[...], h, _NT, preferred_element_type=F32)
    tm = zt.shape[1]
    z3 = zt.reshape(D_MODEL // HEAD_DIM, HEAD_DIM, tm)
    ss = jnp.sum(z3 * z3, axis=1, keepdims=True)
    g3 = gk_ref[...].reshape(D_MODEL // HEAD_DIM, HEAD_DIM, 1)
    zn = (z3 * lax.rsqrt(ss * (1.0 / HEAD_DIM) + NORM_EPS) * g3).reshape(D_MODEL, tm)
    kf_ref[...] = zn
    kb_ref[...] = zn.astype(BF16)


def _kt_proj(rows, x, g, scale, shift, w_kt, gk_col, batch, seq):
    tpb = rows.tiles_per_batch
    out_spec = pl.BlockSpec((None, D_MODEL, rows.tm), lambda i: (i // tpb, 0, i % tpb))
    return pl.pallas_call(
        _kt_kernel,
        out_shape=(jax.ShapeDtypeStruct((batch, D_MODEL, seq), F32),
                   jax.ShapeDtypeStruct((batch, D_MODEL, seq), BF16)),
        grid=(rows.n_tiles,),
        in_specs=[rows.row_spec(),
                  pl.BlockSpec((1, D_MODEL), lambda i: (0, 0)),
                  rows.mod_spec(), rows.mod_spec(),
                  pl.BlockSpec((D_MODEL, D_MODEL), lambda i: (0, 0)),
                  pl.BlockSpec((D_MODEL, 1), lambda i: (0, 0))],
        out_specs=(out_spec, out_spec),
        compiler_params=_params("parallel"),
        name="kt_proj",
    )(x, g, scale, shift, w_kt, gk_col)


def _v_kernel(x_ref, g_ref, sc_ref, sh_ref, w_ref, vf_ref, vb_ref):
    h = _rms_mod(x_ref[...], g_ref[...], sc_ref[0], sh_ref[0]).astype(BF16)
    z = jnp.dot(h, w_ref[...], preferred_element_type=F32)
    vf_ref[...] = z
    vb_ref[...] = z.astype(BF16)


def _v_proj(rows, x, g, scale, shift, w_in_b):
    t = rows.n_tokens
    return pl.pallas_call(
        _v_kernel,
        out_shape=(jax.ShapeDtypeStruct((t, D_MODEL), F32),
                   jax.ShapeDtypeStruct((t, D_MODEL), BF16)),
        grid=(rows.n_tiles,),
        in_specs=[rows.row_spec(),
                  pl.BlockSpec((1, D_MODEL), lambda i: (0, 0)),
                  rows.mod_spec(), rows.mod_spec(),
                  pl.BlockSpec((D_MODEL, D_MODEL), lambda i: (0, 2))],
        out_specs=(rows.row_spec(), rows.row_spec()),
        compiler_params=_params("parallel"),
        name="v_proj",
    )(x, g, scale, shift, w_in_b)


def _conv_kernel(*refs, tiles_per_batch, rows_per_batch, tail_rows):
    if tiles_per_batch is None:
        (x_ref, g_ref, sc_ref, sh_ref, w_ref, wc_ref, s1_ref, s2_ref,
         sga_ref, cvp_ref, tail_ref, h_ref, a_ref, b_ref, carry_ref) = refs
    else:
        (x_ref, g_ref, sc_ref, sh_ref, w_ref, wc_ref,
         sga_ref, cvp_ref, tail_ref, h_ref, a_ref, b_ref, carry_ref) = refs
    i = pl.program_id(0)
    j = pl.program_id(1)

    @pl.when(j == 0)
    def _():
        h_ref[...] = _rms_mod(x_ref[...], g_ref[...], sc_ref[0], sh_ref[0]).astype(BF16)

    z = jnp.dot(h_ref[...], w_ref[...], preferred_element_type=F32)

    @pl.when(j == 0)
    def _():
        a_ref[...] = z

    @pl.when(j == 1)
    def _():
        b_ref[...] = z

    @pl.when(j == 2)
    def _():
        u = b_ref[...] * z
        tm = u.shape[0]
        row = lax.broadcasted_iota(jnp.int32, (tm, 1), 0)
        r1 = pltpu.roll(u, 1, 0)
        r2 = pltpu.roll(u, 2, 0)
        if tiles_per_batch is None:
            t = row & (rows_per_batch - 1)
            u1 = jnp.where(t >= 1, r1, s1_ref[...])
            u2 = jnp.where(t >= 2, r2, s2_ref[...])
        else:
            first = (i % tiles_per_batch) == 0
            c = jnp.where(first, 0.0, carry_ref[...])
            u1 = jnp.where(row == 0, c[7:8], r1)
            u2 = jnp.where(row == 0, c[6:7], jnp.where(row == 1, c[7:8], r2))
            carry_ref[...] = u[tm - SUBLANES:]
        wc = wc_ref[...]
        yc = wc[0:1] * u2 + wc[1:2] * u1 + wc[2:3] * u
        a_ref[...] = a_ref[...] * yc
        tail_ref[0] = u[tm - tail_rows:]

    @pl.when(j == 3)
    def _():
        sga_ref[...] = jax.nn.sigmoid(z).astype(sga_ref.dtype)

    @pl.when(j == 4)
    def _():
        cvp_ref[...] = (jax.nn.sigmoid(z) * a_ref[...]).astype(cvp_ref.dtype)


def _conv_proj(rows, x, g, scale, shift, w_in_b, w_conv, state_rows, out_dtype):
    t = rows.n_tokens
    sample_mode = rows.tiles_per_batch is None
    tail_rows = rows.tm if sample_mode else SUBLANES
    n_tail_blocks = 1 if sample_mode else t // rows.rows_per_batch
    in_specs = [rows.row_spec(),
                pl.BlockSpec((1, D_MODEL), lambda i, j: (0, 0)),
                rows.mod_spec(), rows.mod_spec(),
                pl.BlockSpec((D_MODEL, D_MODEL), lambda i, j: (0, 3 + j)),
                pl.BlockSpec((CONV_WIDTH, D_MODEL), lambda i, j: (0, 0))]
    args = [x, g, scale, shift, w_in_b, w_conv]
    if sample_mode:
        in_specs += [rows.row_spec(), rows.row_spec()]
        args += list(state_rows)
        tail_spec = pl.BlockSpec((1, tail_rows, D_MODEL), lambda i, j: (0, 0, 0))
    else:
        tpb = rows.tiles_per_batch
        tail_spec = pl.BlockSpec((1, tail_rows, D_MODEL), lambda i, j: (i // tpb, 0, 0))
    kern = functools.partial(_conv_kernel, tiles_per_batch=rows.tiles_per_batch,
                             rows_per_batch=rows.rows_per_batch, tail_rows=tail_rows)
    return pl.pallas_call(
        kern,
        out_shape=(jax.ShapeDtypeStruct((t, D_MODEL), out_dtype),
                   jax.ShapeDtypeStruct((t, D_MODEL), out_dtype),
                   jax.ShapeDtypeStruct((n_tail_blocks, tail_rows, D_MODEL), F32)),
        grid=(rows.n_tiles, 5),
        in_specs=in_specs,
        out_specs=(rows.row_spec(), rows.row_spec(), tail_spec),
        scratch_shapes=[pltpu.VMEM((rows.tm, D_MODEL), BF16),
                        pltpu.VMEM((rows.tm, D_MODEL), F32),
                        pltpu.VMEM((rows.tm, D_MODEL), F32),
                        pltpu.VMEM((SUBLANES, D_MODEL), F32)],
        compiler_params=_params("arbitrary", "arbitrary"),
        name="conv_proj",
    )(*args)


def _lambda_value(lq1, lk1, lq2, lk2):
    e1 = jnp.exp(jnp.sum(lq1 * lk1, axis=-1, keepdims=True))
    e2 = jnp.exp(jnp.sum(lq2 * lk2, axis=-1, keepdims=True))
    return e1 - e2 + LAMBDA_INIT


def _subln_mix(o, g_sub, sga, cvp):
    ms = jnp.mean(o * o, axis=-1, keepdims=True)
    attn = (o * lax.rsqrt(ms + NORM_EPS) * g_sub) * (1.0 - LAMBDA_INIT)
    return sga * attn + cvp


def _flash_update(s, v, m_ref, l_ref, acc_ref):
    m_prev = m_ref[...]
    m_next = jnp.maximum(m_prev, jnp.max(s, axis=1, keepdims=True))
    alpha = jnp.exp(m_prev - m_next)
    p = jnp.exp(s - jnp.concatenate([m_next] * (s.shape[1] // LANES), axis=1))
    l_ref[...] = alpha * l_ref[...] + jnp.sum(p, axis=1, keepdims=True)
    acc_ref[...] = alpha * acc_ref[...] + jnp.dot(p.astype(BF16), v, preferred_element_type=F32)
    m_ref[...] = m_next


def _attn_prompt_kernel(q_ref, kt_ref, v_ref, sga_ref, cvp_ref, gs_ref, lq1_ref, lk1_ref, lq2_ref, lk2_ref,
                        o_ref, m0_ref, l0_ref, a0_ref, m1_ref, l1_ref, a1_ref):
    qi = pl.program_id(2)
    ki = pl.program_id(3)

    @pl.when(ki == 0)
    def _():
        for m_ref, l_ref, a_ref in ((m0_ref, l0_ref, a0_ref), (m1_ref, l1_ref, a1_ref)):
            m_ref[...] = jnp.full(m_ref.shape, -jnp.inf, F32)
            l_ref[...] = jnp.zeros(l_ref.shape, F32)
            a_ref[...] = jnp.zeros(a_ref.shape, F32)

    def step(masked):
        q = q_ref[...]
        kt = kt_ref[...]
        v = v_ref[...]
        lane = lax.broadcasted_iota(jnp.int32, (1, V_DIM), 1)
        zero = jnp.zeros((), BF16)
        qs = (jnp.where(lane < HEAD_DIM, q, zero), jnp.where(lane >= HEAD_DIM, q, zero))
        states = ((m0_ref, l0_ref, a0_ref), (m1_ref, l1_ref, a1_ref))
        for qm, (m_ref, l_ref, a_ref) in zip(qs, states):
            s = jnp.dot(qm, kt, preferred_element_type=F32)
            if masked:
                r = lax.broadcasted_iota(jnp.int32, s.shape, 0)
                c = lax.broadcasted_iota(jnp.int32, s.shape, 1)
                s = jnp.where(c <= r, s, -jnp.inf)
            _flash_update(s, v, m_ref, l_ref, a_ref)

    @pl.when(ki < qi)
    def _():
        step(False)

    @pl.when(ki == qi)
    def _():
        step(True)
        lam = _lambda_value(lq1_ref[...], lk1_ref[...], lq2_ref[...], lk2_ref[...])
        o = a0_ref[...] / l0_ref[...] - lam * (a1_ref[...] / l1_ref[...])
        mix = _subln_mix(o, gs_ref[...], sga_ref[...].astype(F32), cvp_ref[...].astype(F32))
        o_ref[...] = mix.astype(o_ref.dtype)


def _attn_prompt(q_b, kt_b, v_b, sga, cvp, g_sub, lams, batch, seq):
    nb = seq // ATTN_BLOCK
    tq = ATTN_BLOCK
    q_spec = pl.BlockSpec((tq, V_DIM), lambda b, h, qi, ki: (b * nb + qi, h))
    k_spec = pl.BlockSpec((None, V_DIM, tq), lambda b, h, qi, ki: (b, h, jnp.minimum(ki, qi)))
    v_spec = pl.BlockSpec((tq, V_DIM), lambda b, h, qi, ki: (b * nb + jnp.minimum(ki, qi), h))
    vec64 = pl.BlockSpec((1, HEAD_DIM), lambda b, h, qi, ki: (0, 0))
    return pl.pallas_call(
        _attn_prompt_kernel,
        out_shape=jax.ShapeDtypeStruct((batch * seq, D_MODEL), BF16),
        grid=(batch, N_HEADS, nb, nb),
        in_specs=[q_spec, k_spec, v_spec, q_spec, q_spec,
                  pl.BlockSpec((1, V_DIM), lambda b, h, qi, ki: (0, 0)),
                  vec64, vec64, vec64, vec64],
        out_specs=q_spec,
        scratch_shapes=[pltpu.VMEM((tq, V_DIM), F32)] * 6,
        compiler_params=_params("parallel", "parallel", "parallel", "arbitrary"),
        name="attn_prompt",
    )(q_b, kt_b, v_b, sga, cvp, g_sub, *lams)


def _attn_sample_kernel(pt_ref, q_ref, kn_ref, vn_ref, *rest, n_new, n_group):
    kc_refs = rest[:n_group]
    vc_refs = rest[n_group:2 * n_group]
    (sga_ref, cvp_ref, gs_ref, lq1_ref, lk1_ref, lq2_ref, lk2_ref,
     o_ref, qbd_ref, m_ref, l_ref, acc_ref) = rest[2 * n_group:]
    p = pl.program_id(1)
    rows_per_head = 2 * n_new

    @pl.when(p == 0)
    def _():
        q = q_ref[...] * ATTN_SCALE
        qrep = jnp.concatenate([q] * (N_HEADS * 2), axis=0)
        r = lax.broadcasted_iota(jnp.int32, qrep.shape, 0)
        c = lax.broadcasted_iota(jnp.int32, qrep.shape, 1)
        qbd_ref[...] = jnp.where(c // HEAD_DIM == r // n_new, qrep, 0.0).astype(BF16)
        m_ref[...] = jnp.full(m_ref.shape, -jnp.inf, F32)
        l_ref[...] = jnp.zeros(l_ref.shape, F32)
        acc_ref[...] = jnp.zeros(acc_ref.shape, F32)

    def update(s, head_pv):
        m_prev = m_ref[...]
        m_next = jnp.maximum(m_prev, jnp.max(s, axis=1, keepdims=True))
        alpha = jnp.exp(m_prev - m_next)
        pr = jnp.exp(s - m_next[:, :s.shape[1]])
        l_ref[...] = alpha * l_ref[...] + jnp.sum(pr, axis=1, keepdims=True)
        pb = pr.astype(BF16)
        pv = jnp.concatenate(
            [head_pv(h, pb[h * rows_per_head:(h + 1) * rows_per_head]) for h in range(N_HEADS)], axis=0)
        acc_ref[...] = alpha * acc_ref[...] + pv
        m_ref[...] = m_next

    for g in range(n_group):
        s = jnp.dot(qbd_ref[...], kc_refs[g][...].astype(BF16), preferred_element_type=F32)
        update(s, lambda h, ph, g=g: jnp.dot(ph, vc_refs[g][:, h, :].astype(BF16), preferred_element_type=F32))

    @pl.when(p == pl.num_programs(1) - 1)
    def _():
        s_new = lax.dot_general(qbd_ref[...], kn_ref[...].astype(BF16), _NT, preferred_element_type=F32)
        r = lax.broadcasted_iota(jnp.int32, s_new.shape, 0)
        c = lax.broadcasted_iota(jnp.int32, s_new.shape, 1)
        s_new = jnp.where(c <= (r & (n_new - 1)), s_new, -jnp.inf)
        update(s_new, lambda h, ph: jnp.dot(
            ph.astype(F32), vn_ref[:, h * V_DIM:(h + 1) * V_DIM].astype(BF16).astype(F32),
            preferred_element_type=F32))
        lam = _lambda_value(lq1_ref[...], lk1_ref[...], lq2_ref[...], lk2_ref[...])
        acc = acc_ref[...] / l_ref[...]
        for h in range(N_HEADS):
            cols = slice(h * V_DIM, (h + 1) * V_DIM)
            r0 = h * rows_per_head
            o = acc[r0:r0 + n_new] - lam * acc[r0 + n_new:r0 + rows_per_head]
            o_ref[:, cols] = _subln_mix(o, gs_ref[...], sga_ref[:, cols], cvp_ref[:, cols])


def _attn_sample(qkf, vf, cache_k, cache_v, page_table, sga, cvp, g_sub, lams, n_seq, n_new):
    n_pages = page_table.shape[1]
    n_pool = cache_k.shape[0]
    width = N_HEADS * V_DIM
    grp = SAMPLE_PAGES_PER_STEP
    kc = jnp.transpose(cache_k, (0, 2, 3, 4, 1)).reshape(n_pool, width, PAGE_SIZE)
    pt = page_table.reshape(-1)
    n_rows = N_HEADS * 2 * n_new
    row = pl.BlockSpec((n_new, width), lambda b, p, pt: (b, 0))

    def page_index(g):
        return lambda b, p, pt: (pt[b * n_pages + p * grp + g], 0, 0)

    def page_index4(g):
        return lambda b, p, pt: (pt[b * n_pages + p * grp + g], 0, 0, 0)

    k_pages = [pl.BlockSpec((None, width, PAGE_SIZE), page_index(g)) for g in range(grp)]
    v_pages = [pl.BlockSpec((None, PAGE_SIZE, N_HEADS, V_DIM), page_index4(g)) for g in range(grp)]
    vec64 = pl.BlockSpec((1, HEAD_DIM), lambda b, p, pt: (0, 0))
    grid_spec = pltpu.PrefetchScalarGridSpec(
        num_scalar_prefetch=1,
        grid=(n_seq, n_pages // grp),
        in_specs=[pl.BlockSpec((None, n_new, width), lambda b, p, pt: (0, b, 0)),
                  pl.BlockSpec((None, n_new, width), lambda b, p, pt: (1, b, 0)),
                  row, *k_pages, *v_pages, row, row,
                  pl.BlockSpec((1, V_DIM), lambda b, p, pt: (0, 0)),
                  vec64, vec64, vec64, vec64],
        out_specs=row,
        scratch_shapes=[pltpu.VMEM((n_rows, width), BF16),
                        pltpu.VMEM((n_rows, V_DIM), F32), pltpu.VMEM((n_rows, V_DIM), F32),
                        pltpu.VMEM((n_rows, V_DIM), F32)])
    return pl.pallas_call(
        functools.partial(_attn_sample_kernel, n_new=n_new, n_group=grp),
        out_shape=jax.ShapeDtypeStruct((n_seq * n_new, width), F32),
        grid_spec=grid_spec,
        compiler_params=_params("parallel", "arbitrary"),
        name="attn_sample",
    )(pt, qkf, qkf, vf, *([kc] * grp), *([cache_v] * grp), sga, cvp, g_sub, *lams)


def _out_kernel(mix_ref, x_ref, wo_ref, g1_ref, g_ref, sc_ref, sh_ref, wrh_ref, wrl_ref, br_ref,
                xm_ref, h2_ref, idx_ref, gt_ref):
    y = jnp.dot(mix_ref[...].astype(BF16), wo_ref[...], preferred_element_type=F32)
    xm = x_ref[...] + g1_ref[0] * y
    xm_ref[...] = xm
    h2 = _rms_mod(xm, g_ref[...], sc_ref[0], sh_ref[0])
    hi = h2.astype(BF16)
    h2_ref[...] = _pack_pairs(hi.astype(F32))
    lo = (h2 - hi.astype(F32)).astype(BF16)
    logits = (jnp.dot(hi, wrh_ref[...], preferred_element_type=F32)
              + jnp.dot(lo, wrh_ref[...], preferred_element_type=F32)
              + jnp.dot(hi, wrl_ref[...], preferred_element_type=F32)) + br_ref[...]
    lane = lax.broadcasted_iota(jnp.int32, logits.shape, 1).astype(F32)
    vals, idxs = [], []
    for _ in range(TOP_K):
        m = jnp.max(logits, axis=-1, keepdims=True)
        ix = jnp.min(jnp.where(logits == m, lane, float(LANES)), axis=-1, keepdims=True)
        logits = jnp.where(lane == ix, -jnp.inf, logits)
        vals.append(m)
        idxs.append(ix)
    es = [jnp.exp(v - vals[0]) for v in vals]
    denom = es[0] + es[1] + es[2] + es[3]
    idx_out = jnp.zeros(logits.shape, F32)
    gt_out = jnp.zeros(logits.shape, F32)
    for k in range(TOP_K):
        idx_out = jnp.where(lane == float(k), idxs[k], idx_out)
        gt_out = jnp.where(lane == float(k), es[k] / denom, gt_out)
    idx_ref[...] = idx_out.astype(jnp.int32)
    gt_ref[...] = gt_out


def _out_proj(rows, mix, x, w_o_b, gate1, g, scale, shift, wr_hi, wr_lo, b_r):
    t = rows.n_tokens
    full = lambda shape: pl.BlockSpec(shape, lambda i: (0,) * len(shape))
    return pl.pallas_call(
        _out_kernel,
        out_shape=(jax.ShapeDtypeStruct((t, D_MODEL), F32),
                   jax.ShapeDtypeStruct((t, PACKED), jnp.uint32),
                   jax.ShapeDtypeStruct((t, LANES), jnp.int32),
                   jax.ShapeDtypeStruct((t, LANES), F32)),
        grid=(rows.n_tiles,),
        in_specs=[rows.row_spec(), rows.row_spec(), full((D_MODEL, D_MODEL)),
                  rows.mod_spec(), full((1, D_MODEL)), rows.mod_spec(), rows.mod_spec(),
                  full((D_MODEL, LANES)), full((D_MODEL, LANES)), full((1, LANES))],
        out_specs=(rows.row_spec(), rows.row_spec(PACKED), rows.row_spec(LANES), rows.row_spec(LANES)),
        compiler_params=_params("parallel"),
        name="out_proj_router",
    )(mix, x, w_o_b, gate1, g, scale, shift, wr_hi, wr_lo, b_r)


def _pack_pairs(x):
    bits = pltpu.bitcast(x, jnp.uint32)
    n = x.shape[1] // 2
    return (bits[:, :n] >> 16) | (bits[:, n:] & jnp.uint32(0xFFFF0000))


def _unpack_pairs(w):
    lo = pltpu.bitcast(w << 16, F32)
    hi = pltpu.bitcast(w & jnp.uint32(0xFFFF0000), F32)
    return jnp.concatenate([lo, hi], axis=1)


def _route_kernel(idx_ref, dest_ref, cnt_ref, run_ref, start_ref):
    ph = pl.program_id(0)
    i = pl.program_id(1)
    idx = idx_ref[...]
    tm = idx.shape[0]
    lane = lax.broadcasted_iota(jnp.int32, idx.shape, 1)
    onehots = [(lane == idx[:, k:k + 1]).astype(F32) for k in range(TOP_K)]
    member = onehots[0] + onehots[1] + onehots[2] + onehots[3]
    tile_count = jnp.sum(member, axis=0, keepdims=True)

    @pl.when((ph == 0) & (i == 0))
    def _():
        cnt_ref[...] = jnp.zeros(cnt_ref.shape, F32)

    @pl.when(ph == 0)
    def _():
        cnt_ref[...] = cnt_ref[...] + tile_count

    @pl.when((ph == 1) & (i == 0))
    def _():
        cnt = cnt_ref[...]
        padded = jnp.floor((cnt + (MOE_BLOCK - 1)) * (1.0 / MOE_BLOCK)) * MOE_BLOCK
        l1 = lax.broadcasted_iota(jnp.int32, cnt.shape, 1)
        incl = padded
        for s in (1, 2, 4, 8, 16, 32, 64):
            incl = incl + jnp.where(l1 >= s, pltpu.roll(incl, s, 1), 0.0)
        start_ref[...] = incl - padded
        run_ref[...] = jnp.zeros(run_ref.shape, F32)

    @pl.when(ph == 1)
    def _():
        r = lax.broadcasted_iota(jnp.int32, (tm, tm), 0)
        c = lax.broadcasted_iota(jnp.int32, (tm, tm), 1)
        earlier = (c < r).astype(BF16)
        before = jnp.dot(earlier, member.astype(BF16), preferred_element_type=F32)
        base = before + run_ref[0:1] + start_ref[0:1]
        out = jnp.zeros(idx.shape, F32)
        for k in range(TOP_K):
            d = jnp.sum(onehots[k] * base, axis=1, keepdims=True)
            out = jnp.where(lane == k, d, out)
        dest_ref[...] = out.astype(jnp.int32)
        run_ref[...] = run_ref[...] + tile_count


def _route(rows, idx):
    tm = rows.tm
    dest, counts = pl.pallas_call(
        _route_kernel,
        out_shape=(jax.ShapeDtypeStruct((rows.n_tokens, LANES), jnp.int32),
                   jax.ShapeDtypeStruct((SUBLANES, LANES), F32)),
        grid=(2, rows.n_tiles),
        in_specs=[pl.BlockSpec((tm, LANES), lambda ph, i: (i, 0))],
        out_specs=(pl.BlockSpec((tm, LANES), lambda ph, i: (i * ph, 0)),
                   pl.BlockSpec((SUBLANES, LANES), lambda ph, i: (0, 0))),
        scratch_shapes=[pltpu.VMEM((SUBLANES, LANES), F32), pltpu.VMEM((SUBLANES, LANES), F32)],
        compiler_params=_params("arbitrary", "arbitrary"),
        name="moe_route",
    )(idx)
    return dest, counts


def _dispatch_kernel(dest_ref, h_ref, zero_ref, xs_ref, sem):
    del zero_ref
    i = pl.program_id(0)
    tm = h_ref.shape[0]

    def body(t, carry):
        for k in range(TOP_K):
            d = dest_ref[(i * tm + t) * TOP_K + k]
            pltpu.make_async_copy(h_ref.at[pl.ds(t, 1)], xs_ref.at[pl.ds(d, 1)], sem).start()
        return carry

    lax.fori_loop(0, tm, body, 0, unroll=8)
    pltpu.make_async_copy(xs_ref.at[pl.ds(0, tm * TOP_K)], xs_ref.at[pl.ds(0, tm * TOP_K)], sem).wait()


def _dispatch(rows, dest_flat, h2u, n_rows):
    tm = rows.tm
    zeros = jnp.zeros((n_rows, PACKED), jnp.uint32)
    grid_spec = pltpu.PrefetchScalarGridSpec(
        num_scalar_prefetch=1, grid=(rows.n_tiles,),
        in_specs=[pl.BlockSpec((tm, PACKED), lambda i, d: (i, 0)),
                  pl.BlockSpec(memory_space=pl.ANY)],
        out_specs=pl.BlockSpec(memory_space=pl.ANY),
        scratch_shapes=[pltpu.SemaphoreType.DMA(())])
    return pl.pallas_call(
        _dispatch_kernel,
        out_shape=jax.ShapeDtypeStruct((n_rows, PACKED), jnp.uint32),
        grid_spec=grid_spec,
        input_output_aliases={2: 0},
        compiler_params=_params("arbitrary"),
        name="moe_dispatch",
    )(dest_flat, h2u, zeros)


def _expert_kernel(be_ref, nu_ref, x_ref, wgu_ref, wd_ref, bgu_ref, bd_ref, y_ref):
    i = pl.program_id(0)

    @pl.when(i < nu_ref[0])
    def _():
        x = _unpack_pairs(x_ref[...]).astype(BF16)
        gu = jnp.dot(x, wgu_ref[...], preferred_element_type=F32) + bgu_ref[...]
        lane = lax.broadcasted_iota(jnp.int32, (1, gu.shape[1]), 1)
        gate = jnp.minimum(gu, SWIGLU_LIMIT)
        glu = gate * jax.nn.sigmoid(SWIGLU_ALPHA * gate)
        up1 = jnp.clip(gu, -SWIGLU_LIMIT, SWIGLU_LIMIT) + 1.0
        t = jnp.where((lane & 1) == 0, glu, up1)
        a = (t * pltpu.roll(t, gu.shape[1] - 1, 1)).astype(BF16)
        y = jnp.dot(a, wd_ref[...], preferred_element_type=F32) + bd_ref[...]
        y_ref[...] = _pack_pairs(y.astype(BF16).astype(F32))

    @pl.when(i >= nu_ref[0])
    def _():
        y_ref[...] = jnp.zeros(y_ref.shape, y_ref.dtype)


def _experts(x_sorted, block_e, n_used, w_gu, w_d2, b_gu, b_d):
    rows = x_sorted.shape[0]
    n_blocks = rows // MOE_BLOCK
    d_gu = w_gu.shape[2]
    by_expert = lambda i, be, nu: (be[i], 0, 0)
    xspec = pl.BlockSpec((MOE_BLOCK, PACKED), lambda i, be, nu: (i, 0))
    grid_spec = pltpu.PrefetchScalarGridSpec(
        num_scalar_prefetch=2, grid=(n_blocks,),
        in_specs=[xspec,
                  pl.BlockSpec((None, D_MODEL, d_gu), by_expert),
                  pl.BlockSpec((None, d_gu, D_MODEL), by_expert),
                  pl.BlockSpec((None, 1, d_gu), by_expert),
                  pl.BlockSpec((None, 1, D_MODEL), by_expert)],
        out_specs=xspec)
    return pl.pallas_call(
        _expert_kernel,
        out_shape=jax.ShapeDtypeStruct((rows, PACKED), jnp.uint32),
        grid_spec=grid_spec,
        compiler_params=_params("arbitrary"),
        name="experts",
    )(block_e, n_used, x_sorted, w_gu, w_d2, b_gu, b_d)


def _combine_kernel(dest_ref, xm_ref, gt_ref, g2_ref, yb_ref, o_ref, rows_ref, sem):
    i = pl.program_id(0)
    tm = xm_ref.shape[0]

    def body(t, carry):
        for k in range(TOP_K):
            d = dest_ref[(i * tm + t) * TOP_K + k]
            pltpu.make_async_copy(yb_ref.at[pl.ds(d, 1)], rows_ref.at[k, pl.ds(t, 1)], sem).start()
        return carry

    lax.fori_loop(0, tm, body, 0, unroll=8)
    pltpu.make_async_copy(rows_ref, rows_ref, sem).wait()
    gt = gt_ref[...]
    acc = jnp.zeros(xm_ref.shape, F32)
    for k in range(TOP_K):
        acc = acc + gt[:, k:k + 1] * _unpack_pairs(rows_ref[k])
    o_ref[...] = xm_ref[...] + g2_ref[0] * acc


def _combine(rows, dest_flat, xm, gates, gate2, y_buf):
    tm = rows.tm
    grid_spec = pltpu.PrefetchScalarGridSpec(
        num_scalar_prefetch=1, grid=(rows.n_tiles,),
        in_specs=[pl.BlockSpec((tm, D_MODEL), lambda i, d: (i, 0)),
                  pl.BlockSpec((tm, LANES), lambda i, d: (i, 0)),
                  rows.mod_spec(),
                  pl.BlockSpec(memory_space=pl.ANY)],
        out_specs=pl.BlockSpec((tm, D_MODEL), lambda i, d: (i, 0)),
        scratch_shapes=[pltpu.VMEM((TOP_K, tm, PACKED), jnp.uint32),
                        pltpu.SemaphoreType.DMA(())])
    return pl.pallas_call(
        _combine_kernel,
        out_shape=jax.ShapeDtypeStruct((rows.n_tokens, D_MODEL), F32),
        grid_spec=grid_spec,
        compiler_params=_params("arbitrary"),
        name="moe_combine",
    )(dest_flat, xm, gates, gate2, y_buf)


def _moe(rows, xm, h2u, idx, gates, gate2, moe_w):
    a = rows.n_tokens * TOP_K
    n_blocks = -(-a // MOE_BLOCK) + N_EXPERTS
    dest, counts = _route(rows, idx)
    dest_flat = dest[:, :TOP_K].reshape(a)
    cnt = counts[0, :N_EXPERTS].astype(jnp.int32)
    pad_end = jnp.cumsum((cnt + MOE_BLOCK - 1) // MOE_BLOCK * MOE_BLOCK)
    block_e = jnp.minimum(
        jnp.searchsorted(pad_end, jnp.arange(n_blocks, dtype=jnp.int32) * MOE_BLOCK, side='right'),
        N_EXPERTS - 1).astype(jnp.int32)
    n_used = (pad_end[-1:] // MOE_BLOCK).astype(jnp.int32)
    x_sorted = _dispatch(rows, dest_flat, h2u, n_blocks * MOE_BLOCK)
    y_buf = _experts(x_sorted, block_e, n_used, *moe_w)
    return _combine(rows, dest_flat, xm, gates, gate2, y_buf)


def _group(rows, x, ada, state_rows, qk_and_attend, shared):
    (g_mix, w_in_b, w_conv, w_o_b, g_ffn, wr_hi, wr_lo, b_r, moe_w, mid_dtype) = shared
    shift1, scale1, gate1, shift2, scale2, gate2 = [rows.mod_array(m) for m in jnp.split(ada, 6, axis=-1)]
    v_f, v_b = _v_proj(rows, x, g_mix, scale1, shift1, w_in_b)
    sga, cvp, tail = _conv_proj(rows, x, g_mix, scale1, shift1, w_in_b, w_conv, state_rows, mid_dtype)
    mix, k_out = qk_and_attend(x, scale1, shift1, v_f, v_b, sga, cvp)
    xm, h2, idx, gates = _out_proj(rows, mix, x, w_o_b, gate1, g_ffn, scale2, shift2, wr_hi, wr_lo, b_r)
    y = _moe(rows, xm, h2, idx, gates, gate2, moe_w)
    return y, k_out, v_f, tail


def kernel(x_prompt, x_sample, c_prompt, c_sample, cache_k, cache_v, state_conv, page_table, w_ada, b_ada, g_norm_mix, w_in, g_q, g_k, lambda_q1, lambda_k1, lambda_q2, lambda_k2, g_subln, w_conv, w_o, g_norm_ffn, w_router, b_router, w_gate_up, b_gate_up, w_down, b_down):
    assert w_in.shape[0] == 1, "single-layer stack"
    batch, seq, _ = x_prompt.shape
    n_seq, n_new, _ = x_sample.shape
    tp, ts = batch * seq, n_seq * n_new
    n_chunks = D_MODEL // HEAD_DIM

    ada = _ada(jnp.concatenate([c_prompt, c_sample], axis=0), w_ada[0], b_ada[0][None])

    w_in_b = w_in[0].astype(BF16)
    w_kt = w_in[0][:, D_MODEL:2 * D_MODEL].T.astype(BF16)
    w_o_b = w_o[0].astype(BF16)
    gqk = jnp.stack([jnp.tile(g_q[0], n_chunks), jnp.tile(g_k[0], n_chunks)])[:, None, :]
    gk_col = jnp.tile(g_k[0], n_chunks)[:, None]
    blk = jnp.arange(D_MODEL, dtype=jnp.int32) // HEAD_DIM
    pmat = (blk[:, None] == blk[None, :]).astype(BF16)
    wr = jnp.pad(w_router[0], ((0, 0), (0, LANES - N_EXPERTS)))
    wr_hi = wr.astype(BF16)
    wr_lo = (wr - wr_hi.astype(F32)).astype(BF16)
    b_r = jnp.pad(b_router[0], (0, LANES - N_EXPERTS), constant_values=NEG_BIG)[None]
    w_dn = w_down[0].astype(BF16)
    w_d2 = jnp.stack([w_dn, jnp.zeros_like(w_dn)], axis=2).reshape(N_EXPERTS, 2 * w_dn.shape[1], D_MODEL)
    moe_w = (w_gate_up[0].astype(BF16), w_d2, b_gate_up[0][:, None, :], b_down[0][:, None, :])
    g_mix = g_norm_mix[0][None]
    g_ffn = g_norm_ffn[0][None]
    g_sub = g_subln[0][None]
    lams = (lambda_q1[0][None], lambda_k1[0][None], lambda_q2[0][None], lambda_k2[0][None])

    def shared(mid_dtype):
        return (g_mix, w_in_b, w_conv[0], w_o_b, g_ffn, wr_hi, wr_lo, b_r, moe_w, mid_dtype)

    rows_p = _Rows(tp, seq)

    def attend_p(x, scale1, shift1, v_f, v_b, sga, cvp):
        q_b = _q_proj(rows_p, x, g_mix, scale1, shift1, w_in_b, gqk, pmat)
        kt_f, kt_b = _kt_proj(rows_p, x, g_mix, scale1, shift1, w_kt, gk_col, batch, seq)
        mix = _attn_prompt(q_b, kt_b, v_b, sga, cvp, g_sub, lams, batch, seq)
        return mix, kt_f

    y_p, kt_p, v_p, tail_p = _group(rows_p, x_prompt.reshape(tp, D_MODEL), ada[:batch], None, attend_p, shared(BF16))

    rows_s = _Rows(ts, n_new)
    st = state_conv[0]
    zeros = jnp.zeros((n_seq, n_new - 2, D_MODEL), F32)
    s1 = jnp.concatenate([st[:, 1:2], jnp.zeros((n_seq, n_new - 1, D_MODEL), F32)], axis=1).reshape(ts, D_MODEL)
    s2 = jnp.concatenate([st, zeros], axis=1).reshape(ts, D_MODEL)

    def attend_s(x, scale1, shift1, v_f, v_b, sga, cvp):
        _, qk_f = _qk_proj(rows_s, x, g_mix, scale1, shift1, w_in_b, gqk, pmat)
        mix = _attn_sample(qk_f, v_f, cache_k[0], cache_v[0], page_table, sga, cvp, g_sub, lams, n_seq, n_new)
        return mix, qk_f[1]

    y_s, k_s, v_s, tail_s = _group(rows_s, x_sample.reshape(ts, D_MODEL), ada[batch:], (s1, s2), attend_s, shared(F32))

    tail_s = tail_s.reshape(n_seq, n_new, D_MODEL)
    k_p = kt_p.reshape(1, batch, N_HEADS, 2, HEAD_DIM, seq).transpose(0, 1, 5, 2, 3, 4)
    return (y_p.reshape(batch, seq, D_MODEL),
            y_s.reshape(n_seq, n_new, D_MODEL),
            k_p,
            v_p.reshape(1, batch, seq, N_HEADS, V_DIM),
            tail_p[:, SUBLANES - (CONV_WIDTH - 1):][None],
            k_s.reshape(1, n_seq, n_new, N_HEADS, 2, HEAD_DIM),
            v_s.reshape(1, n_seq, n_new, N_HEADS, V_DIM),
            tail_s[:, n_new - (CONV_WIDTH - 1):][None])
```

```python
import functools
import math

import jax
import jax.numpy as jnp
from jax import lax
from jax.experimental import pallas as pl
from jax.experimental.pallas import tpu as pltpu

F32 = jnp.float32
BF16 = jnp.bfloat16

D_MODEL = 1024
HEAD_DIM = 64
V_DIM = 2 * HEAD_DIM
N_HEADS = D_MODEL // V_DIM
ATTN_SCALE = HEAD_DIM ** -0.5
CONV_WIDTH = 3
PAGE_SIZE = 128
N_EXPERTS = 32
TOP_K = 4
SWIGLU_LIMIT = 7.0
SWIGLU_ALPHA = 1.702
NORM_EPS = 1e-6
LAMBDA_INIT = 0.8 - 0.6 * math.exp(-0.3 * 0)

VMEM_LIMIT_BYTES = 48 * 1024 * 1024
EXPERT_VMEM_LIMIT_BYTES = 56 * 1024 * 1024
LANES = 128
PACKED = D_MODEL // 2
SUBLANES = 8

ROW_TILE_PROMPT = 512
ATTN_BLOCK = 512
MOE_BLOCK = 256
ADA_COL_TILE = 1536
SAMPLE_PAGES_PER_STEP = 4
NEG_BIG = -1e30

_NT = (((1,), (1,)), ((), ()))


def _params(*sem):
    return pltpu.CompilerParams(dimension_semantics=sem, vmem_limit_bytes=VMEM_LIMIT_BYTES)


def _rms_mod(x, g, scale, shift):
    ms = jnp.mean(x * x, axis=-1, keepdims=True)
    return (x * lax.rsqrt(ms + NORM_EPS) * g) * (1.0 + scale) + shift


def _ada_kernel(c_ref, w_ref, b_ref, o_ref):
    c = c_ref[...]
    s = (c * jax.nn.sigmoid(c)).astype(BF16)
    o_ref[...] = jnp.dot(s, w_ref[...].astype(BF16), preferred_element_type=F32) + b_ref[...]


def _ada(c_all, w_ada, b_ada):
    n = c_all.shape[0]
    width = w_ada.shape[1]
    return pl.pallas_call(
        _ada_kernel,
        out_shape=jax.ShapeDtypeStruct((n, width), F32),
        grid=(width // ADA_COL_TILE,),
        in_specs=[pl.BlockSpec((n, D_MODEL), lambda j: (0, 0)),
                  pl.BlockSpec((D_MODEL, ADA_COL_TILE), lambda j: (0, j)),
                  pl.BlockSpec((1, ADA_COL_TILE), lambda j: (0, j))],
        out_specs=pl.BlockSpec((n, ADA_COL_TILE), lambda j: (0, j)),
        compiler_params=_params("arbitrary"),
        name="ada",
    )(c_all, w_ada, b_ada)


class _Rows:
    def __init__(self, n_tokens, rows_per_batch):
        if rows_per_batch >= ROW_TILE_PROMPT:
            self.tm = ROW_TILE_PROMPT
            self.tiles_per_batch = rows_per_batch // self.tm
            self.mod_rows = 1
        else:
            self.tm = n_tokens
            self.tiles_per_batch = None
            self.mod_rows = n_tokens
        self.n_tokens = n_tokens
        self.rows_per_batch = rows_per_batch
        self.n_tiles = n_tokens // self.tm

    def mod_array(self, m):
        if self.tiles_per_batch is not None:
            return m[:, None, :]
        return jnp.repeat(m, self.rows_per_batch, axis=0)[None]

    def mod_spec(self):
        if self.tiles_per_batch is not None:
            tpb = self.tiles_per_batch
            return pl.BlockSpec((1, 1, D_MODEL), lambda i, *_: (i // tpb, 0, 0))
        return pl.BlockSpec((1, self.mod_rows, D_MODEL), lambda i, *_: (0, 0, 0))

    def row_spec(self, width=D_MODEL):
        return pl.BlockSpec((self.tm, width), lambda i, *_: (i, 0))


def _qk_kernel(x_ref, g_ref, sc_ref, sh_ref, w_ref, gqk_ref, p_ref, qkb_ref, qkf_ref, h_ref):
    j = pl.program_id(1)

    @pl.when(j == 0)
    def _():
        h_ref[...] = _rms_mod(x_ref[...], g_ref[...], sc_ref[0], sh_ref[0]).astype(BF16)

    z = jnp.dot(h_ref[...], w_ref[...], preferred_element_type=F32)
    ss = jnp.dot((z * z).astype(BF16), p_ref[...], preferred_element_type=F32)
    zn = z * lax.rsqrt(ss * (1.0 / HEAD_DIM) + NORM_EPS) * gqk_ref[0]
    qkf_ref[0] = zn
    scale = jnp.where(j == 0, ATTN_SCALE, 1.0)
    qkb_ref[...] = (zn * scale).astype(BF16)


def _qk_proj(rows, x, g, scale, shift, w_in_b, gqk, pmat):
    t = rows.n_tokens
    return pl.pallas_call(
        _qk_kernel,
        out_shape=(jax.ShapeDtypeStruct((t, 2 * D_MODEL), BF16),
                   jax.ShapeDtypeStruct((2, t, D_MODEL), F32)),
        grid=(rows.n_tiles, 2),
        in_specs=[rows.row_spec(),
                  pl.BlockSpec((1, D_MODEL), lambda i, j: (0, 0)),
                  rows.mod_spec(), rows.mod_spec(),
                  pl.BlockSpec((D_MODEL, D_MODEL), lambda i, j: (0, j)),
                  pl.BlockSpec((1, 1, D_MODEL), lambda i, j: (j, 0, 0)),
                  pl.BlockSpec((D_MODEL, D_MODEL), lambda i, j: (0, 0))],
        out_specs=(pl.BlockSpec((rows.tm, D_MODEL), lambda i, j: (i, j)),
                   pl.BlockSpec((1, rows.tm, D_MODEL), lambda i, j: (j, i, 0))),
        scratch_shapes=[pltpu.VMEM((rows.tm, D_MODEL), BF16)],
        compiler_params=_params("parallel", "arbitrary"),
        name="qk_proj",
    )(x, g, scale, shift, w_in_b, gqk, pmat)


def _q_kernel(x_ref, g_ref, sc_ref, sh_ref, w_ref, gq_ref, p_ref, qb_ref):
    h = _rms_mod(x_ref[...], g_ref[...], sc_ref[0], sh_ref[0]).astype(BF16)
    z = jnp.dot(h, w_ref[...], preferred_element_type=F32)
    ss = jnp.dot((z * z).astype(BF16), p_ref[...], preferred_element_type=F32)
    zn = z * lax.rsqrt(ss * (1.0 / HEAD_DIM) + NORM_EPS) * gq_ref[0]
    qb_ref[...] = (zn * ATTN_SCALE).astype(BF16)


def _q_proj(rows, x, g, scale, shift, w_in_b, gqk, pmat):
    return pl.pallas_call(
        _q_kernel,
        out_shape=jax.ShapeDtypeStruct((rows.n_tokens, D_MODEL), BF16),
        grid=(rows.n_tiles,),
        in_specs=[rows.row_spec(),
                  pl.BlockSpec((1, D_MODEL), lambda i: (0, 0)),
                  rows.mod_spec(), rows.mod_spec(),
                  pl.BlockSpec((D_MODEL, D_MODEL), lambda i: (0, 0)),
                  pl.BlockSpec((1, 1, D_MODEL), lambda i: (0, 0, 0)),
                  pl.BlockSpec((D_MODEL, D_MODEL), lambda i: (0, 0))],
        out_specs=rows.row_spec(),
        compiler_params=_params("parallel"),
        name="q_proj",
    )(x, g, scale, shift, w_in_b, gqk, pmat)


def _kt_kernel(x_ref, g_ref, sc_ref, sh_ref, wt_ref, gk_ref, kf_ref, kb_ref):
    h = _rms_mod(x_ref[...], g_ref[...], sc_ref[0], sh_ref[0]).astype(BF16)
    zt = lax.dot_general(wt_ref[...], h, _NT, preferred_element_type=F32)
    tm = zt.shape[1]
    z3 = zt.reshape(D_MODEL // HEAD_DIM, HEAD_DIM, tm)
    ss = jnp.sum(z3 * z3, axis=1, keepdims=True)
    g3 = gk_ref[...].reshape(D_MODEL // HEAD_DIM, HEAD_DIM, 1)
    zn = (z3 * lax.rsqrt(ss * (1.0 / HEAD_DIM) + NORM_EPS) * g3).reshape(D_MODEL, tm)
    kf_ref[...] = zn
    kb_ref[...] = zn.astype(BF16)


def _kt_proj(rows, x, g, scale, shift, w_kt, gk_col, batch, seq):
    tpb = rows.tiles_per_batch
    out_spec = pl.BlockSpec((None, D_MODEL, rows.tm), lambda i: (i // tpb, 0, i % tpb))
    return pl.pallas_call(
        _kt_kernel,
        out_shape=(jax.ShapeDtypeStruct((batch, D_MODEL, seq), F32),
                   jax.ShapeDtypeStruct((batch, D_MODEL, seq), BF16)),
        grid=(rows.n_tiles,),
        in_specs=[rows.row_spec(),
                  pl.BlockSpec((1, D_MODEL), lambda i: (0, 0)),
                  rows.mod_spec(), rows.mod_spec(),
                  pl.BlockSpec((D_MODEL, D_MODEL), lambda i: (0, 0)),
                  pl.BlockSpec((D_MODEL, 1), lambda i: (0, 0))],
        out_specs=(out_spec, out_spec),
        compiler_params=_params("parallel"),
        name="kt_proj",
    )(x, g, scale, shift, w_kt, gk_col)


def _v_kernel(x_ref, g_ref, sc_ref, sh_ref, w_ref, vf_ref, vb_ref):
    h = _rms_mod(x_ref[...], g_ref[...], sc_ref[0], sh_ref[0]).astype(BF16)
    z = jnp.dot(h, w_ref[...], preferred_element_type=F32)
    vf_ref[...] = z
    vb_ref[...] = z.astype(BF16)


def _v_proj(rows, x, g, scale, shift, w_in_b):
    t = rows.n_tokens
    return pl.pallas_call(
        _v_kernel,
        out_shape=(jax.ShapeDtypeStruct((t, D_MODEL), F32),
                   jax.ShapeDtypeStruct((t, D_MODEL), BF16)),
        grid=(rows.n_tiles,),
        in_specs=[rows.row_spec(),
                  pl.BlockSpec((1, D_MODEL), lambda i: (0, 0)),
                  rows.mod_spec(), rows.mod_spec(),
                  pl.BlockSpec((D_MODEL, D_MODEL), lambda i: (0, 2))],
        out_specs=(rows.row_spec(), rows.row_spec()),
        compiler_params=_params("parallel"),
        name="v_proj",
    )(x, g, scale, shift, w_in_b)


def _conv_kernel(*refs, tiles_per_batch, rows_per_batch, tail_rows):
    if tiles_per_batch is None:
        (x_ref, g_ref, sc_ref, sh_ref, w_ref, wc_ref, s1_ref, s2_ref,
         sga_ref, cvp_ref, tail_ref, h_ref, a_ref, b_ref, carry_ref) = refs
    else:
        (x_ref, g_ref, sc_ref, sh_ref, w_ref, wc_ref,
         sga_ref, cvp_ref, tail_ref, h_ref, a_ref, b_ref, carry_ref) = refs
    i = pl.program_id(0)
    j = pl.program_id(1)

    @pl.when(j == 0)
    def _():
        h_ref[...] = _rms_mod(x_ref[...], g_ref[...], sc_ref[0], sh_ref[0]).astype(BF16)

    z = jnp.dot(h_ref[...], w_ref[...], preferred_element_type=F32)

    @pl.when(j == 0)
    def _():
        a_ref[...] = z

    @pl.when(j == 1)
    def _():
        b_ref[...] = z

    @pl.when(j == 2)
    def _():
        u = b_ref[...] * z
        tm = u.shape[0]
        row = lax.broadcasted_iota(jnp.int32, (tm, 1), 0)
        r1 = pltpu.roll(u, 1, 0)
        r2 = pltpu.roll(u, 2, 0)
        if tiles_per_batch is None:
            t = row & (rows_per_batch - 1)
            u1 = jnp.where(t >= 1, r1, s1_ref[...])
            u2 = jnp.where(t >= 2, r2, s2_ref[...])
        else:
            first = (i % tiles_per_batch) == 0
            c = jnp.where(first, 0.0, carry_ref[...])
            u1 = jnp.where(row == 0, c[7:8], r1)
            u2 = jnp.where(row == 0, c[6:7], jnp.where(row == 1, c[7:8], r2))
            carry_ref[...] = u[tm - SUBLANES:]
        wc = wc_ref[...]
        yc = wc[0:1] * u2 + wc[1:2] * u1 + wc[2:3] * u
        a_ref[...] = a_ref[...] * yc
        tail_ref[0] = u[tm - tail_rows:]

    @pl.when(j == 3)
    def _():
        sga_ref[...] = jax.nn.sigmoid(z).astype(sga_ref.dtype)

    @pl.when(j == 4)
    def _():
        cvp_ref[...] = (jax.nn.sigmoid(z) * a_ref[...]).astype(cvp_ref.dtype)


def _conv_proj(rows, x, g, scale, shift, w_in_b, w_conv, state_rows, out_dtype):
    t = rows.n_tokens
    sample_mode = rows.tiles_per_batch is None
    tail_rows = rows.tm if sample_mode else SUBLANES
    n_tail_blocks = 1 if sample_mode else t // rows.rows_per_batch
    in_specs = [rows.row_spec(),
                pl.BlockSpec((1, D_MODEL), lambda i, j: (0, 0)),
                rows.mod_spec(), rows.mod_spec(),
                pl.BlockSpec((D_MODEL, D_MODEL), lambda i, j: (0, 3 + j)),
                pl.BlockSpec((CONV_WIDTH, D_MODEL), lambda i, j: (0, 0))]
    args = [x, g, scale, shift, w_in_b, w_conv]
    if sample_mode:
        in_specs += [rows.row_spec(), rows.row_spec()]
        args += list(state_rows)
        tail_spec = pl.BlockSpec((1, tail_rows, D_MODEL), lambda i, j: (0, 0, 0))
    else:
        tpb = rows.tiles_per_batch
        tail_spec = pl.BlockSpec((1, tail_rows, D_MODEL), lambda i, j: (i // tpb, 0, 0))
    kern = functools.partial(_conv_kernel, tiles_per_batch=rows.tiles_per_batch,
                             rows_per_batch=rows.rows_per_batch, tail_rows=tail_rows)
    return pl.pallas_call(
        kern,
        out_shape=(jax.ShapeDtypeStruct((t, D_MODEL), out_dtype),
                   jax.ShapeDtypeStruct((t, D_MODEL), out_dtype),
                   jax.ShapeDtypeStruct((n_tail_blocks, tail_rows, D_MODEL), F32)),
        grid=(rows.n_tiles, 5),
        in_specs=in_specs,
        out_specs=(rows.row_spec(), rows.row_spec(), tail_spec),
        scratch_shapes=[pltpu.VMEM((rows.tm, D_MODEL), BF16),
                        pltpu.VMEM((rows.tm, D_MODEL), F32),
                        pltpu.VMEM((rows.tm, D_MODEL), F32),
                        pltpu.VMEM((SUBLANES, D_MODEL), F32)],
        compiler_params=_params("arbitrary", "arbitrary"),
        name="conv_proj",
    )(*args)


def _lambda_value(lq1, lk1, lq2, lk2):
    e1 = jnp.exp(jnp.sum(lq1 * lk1, axis=-1, keepdims=True))
    e2 = jnp.exp(jnp.sum(lq2 * lk2, axis=-1, keepdims=True))
    return e1 - e2 + LAMBDA_INIT


def _subln_mix(o, g_sub, sga, cvp):
    ms = jnp.mean(o * o, axis=-1, keepdims=True)
    attn = (o * lax.rsqrt(ms + NORM_EPS) * g_sub) * (1.0 - LAMBDA_INIT)
    return sga * attn + cvp


def _flash_update(s, v, m_ref, l_ref, acc_ref):
    m_prev = m_ref[...]
    m_next = jnp.maximum(m_prev, jnp.max(s, axis=1, keepdims=True))
    alpha = jnp.exp(m_prev - m_next)
    p = jnp.exp(s - jnp.concatenate([m_next] * (s.shape[1] // LANES), axis=1))
    l_ref[...] = alpha * l_ref[...] + jnp.sum(p, axis=1, keepdims=True)
    acc_ref[...] = alpha * acc_ref[...] + jnp.dot(p.astype(BF16), v, preferred_element_type=F32)
    m_ref[...] = m_next


def _attn_prompt_kernel(qi_ref, ki_ref, q_ref, kt_ref, v_ref, sga_ref, cvp_ref, gs_ref,
                        lq1_ref, lk1_ref, lq2_ref, lk2_ref,
                        o_ref, m0_ref, l0_ref, a0_ref, m1_ref, l1_ref, a1_ref):
    qi = qi_ref[pl.program_id(2)]
    ki = ki_ref[pl.program_id(2)]

    @pl.when(ki == 0)
    def _():
        for m_ref, l_ref, a_ref in ((m0_ref, l0_ref, a0_ref), (m1_ref, l1_ref, a1_ref)):
            m_ref[...] = jnp.full(m_ref.shape, -jnp.inf, F32)
            l_ref[...] = jnp.zeros(l_ref.shape, F32)
            a_ref[...] = jnp.zeros(a_ref.shape, F32)

    def step(masked):
        q = q_ref[...]
        kt = kt_ref[...]
        v = v_ref[...]
        lane = lax.broadcasted_iota(jnp.int32, (1, V_DIM), 1)
        zero = jnp.zeros((), BF16)
        qs = (jnp.where(lane < HEAD_DIM, q, zero), jnp.where(lane >= HEAD_DIM, q, zero))
        states = ((m0_ref, l0_ref, a0_ref), (m1_ref, l1_ref, a1_ref))
        for qm, (m_ref, l_ref, a_ref) in zip(qs, states):
            s = jnp.dot(qm, kt, preferred_element_type=F32)
            if masked:
                r = lax.broadcasted_iota(jnp.int32, s.shape, 0)
                c = lax.broadcasted_iota(jnp.int32, s.shape, 1)
                s = jnp.where(c <= r, s, -jnp.inf)
            _flash_update(s, v, m_ref, l_ref, a_ref)

    @pl.when(ki < qi)
    def _():
        step(False)

    @pl.when(ki == qi)
    def _():
        step(True)
        lam = _lambda_value(lq1_ref[...], lk1_ref[...], lq2_ref[...], lk2_ref[...])
        o = a0_ref[...] / l0_ref[...] - lam * (a1_ref[...] / l1_ref[...])
        mix = _subln_mix(o, gs_ref[...], sga_ref[...].astype(F32), cvp_ref[...].astype(F32))
        o_ref[...] = mix.astype(o_ref.dtype)


def _attn_prompt(q_b, kt_b, v_b, sga, cvp, g_sub, lams, batch, seq):
    nb = seq // ATTN_BLOCK
    tq = ATTN_BLOCK
    pairs = [(qi, ki) for qi in range(nb) for ki in range(qi + 1)]
    qi_tab = jnp.array([p[0] for p in pairs], jnp.int32)
    ki_tab = jnp.array([p[1] for p in pairs], jnp.int32)
    q_spec = pl.BlockSpec((tq, V_DIM), lambda b, h, p, qt, kt: (b * nb + qt[p], h))
    k_spec = pl.BlockSpec((None, V_DIM, tq), lambda b, h, p, qt, kt: (b, h, kt[p]))
    v_spec = pl.BlockSpec((tq, V_DIM), lambda b, h, p, qt, kt: (b * nb + kt[p], h))
    vec64 = pl.BlockSpec((1, HEAD_DIM), lambda b, h, p, qt, kt: (0, 0))
    grid_spec = pltpu.PrefetchScalarGridSpec(
        num_scalar_prefetch=2,
        grid=(batch, N_HEADS, len(pairs)),
        in_specs=[q_spec, k_spec, v_spec, q_spec, q_spec,
                  pl.BlockSpec((1, V_DIM), lambda b, h, p, qt, kt: (0, 0)),
                  vec64, vec64, vec64, vec64],
        out_specs=q_spec,
        scratch_shapes=[pltpu.VMEM((tq, V_DIM), F32)] * 6)
    return pl.pallas_call(
        _attn_prompt_kernel,
        out_shape=jax.ShapeDtypeStruct((batch * seq, D_MODEL), BF16),
        grid_spec=grid_spec,
        compiler_params=_params("parallel", "parallel", "arbitrary"),
        name="attn_prompt",
    )(qi_tab, ki_tab, q_b, kt_b, v_b, sga, cvp, g_sub, *lams)


def _attn_sample_kernel(pt_ref, q_ref, kn_ref, vn_ref, *rest, n_new, n_group):
    kc_refs = rest[:n_group]
    vc_refs = rest[n_group:2 * n_group]
    (spread_ref, own_ref, sga_ref, cvp_ref, gs_ref, lq1_ref, lk1_ref, lq2_ref, lk2_ref,
     o_ref, qbd_ref, m_ref, l_ref, acc_ref) = rest[2 * n_group:]
    p = pl.program_id(1)
    rows_per_head = 2 * n_new

    @pl.when(p == 0)
    def _():
        q = q_ref[...] * ATTN_SCALE
        qrep = jnp.concatenate([q] * (N_HEADS * 2), axis=0)
        r = lax.broadcasted_iota(jnp.int32, qrep.shape, 0)
        c = lax.broadcasted_iota(jnp.int32, qrep.shape, 1)
        qbd_ref[...] = jnp.where(c // HEAD_DIM == r // n_new, qrep, 0.0).astype(BF16)
        m_ref[...] = jnp.full(m_ref.shape, -jnp.inf, F32)
        l_ref[...] = jnp.zeros(l_ref.shape, F32)
        acc_ref[...] = jnp.zeros(acc_ref.shape, F32)

    def update(s, pv_of):
        m_prev = m_ref[...]
        m_next = jnp.maximum(m_prev, jnp.max(s, axis=1, keepdims=True))
        alpha = jnp.exp(m_prev - m_next)
        pr = jnp.exp(s - m_next[:, :s.shape[1]])
        l_ref[...] = alpha * l_ref[...] + jnp.sum(pr, axis=1, keepdims=True)
        acc_ref[...] = alpha * acc_ref[...] + pv_of(pr.astype(BF16))
        m_ref[...] = m_next

    def page_pv(pb, v_ref):
        pe = jnp.dot(pb, spread_ref[...], preferred_element_type=F32).astype(BF16) * own_ref[...]
        v2 = v_ref[...].reshape(PAGE_SIZE * N_HEADS, V_DIM).astype(BF16)
        return jnp.dot(pe, v2, preferred_element_type=F32)

    for g in range(n_group):
        s = jnp.dot(qbd_ref[...], kc_refs[g][...].astype(BF16), preferred_element_type=F32)
        update(s, functools.partial(page_pv, v_ref=vc_refs[g]))

    @pl.when(p == pl.num_programs(1) - 1)
    def _():
        s_new = lax.dot_general(qbd_ref[...], kn_ref[...].astype(BF16), _NT, preferred_element_type=F32)
        r = lax.broadcasted_iota(jnp.int32, s_new.shape, 0)
        c = lax.broadcasted_iota(jnp.int32, s_new.shape, 1)
        s_new = jnp.where(c <= (r & (n_new - 1)), s_new, -jnp.inf)
        def new_pv(pb):
            return jnp.concatenate(
                [jnp.dot(pb[h * rows_per_head:(h + 1) * rows_per_head].astype(F32),
                         vn_ref[:, h * V_DIM:(h + 1) * V_DIM].astype(BF16).astype(F32),
                         preferred_element_type=F32) for h in range(N_HEADS)], axis=0)

        update(s_new, new_pv)
        lam = _lambda_value(lq1_ref[...], lk1_ref[...], lq2_ref[...], lk2_ref[...])
        acc = acc_ref[...] / l_ref[...]
        for h in range(N_HEADS):
            cols = slice(h * V_DIM, (h + 1) * V_DIM)
            r0 = h * rows_per_head
            o = acc[r0:r0 + n_new] - lam * acc[r0 + n_new:r0 + rows_per_head]
            o_ref[:, cols] = _subln_mix(o, gs_ref[...], sga_ref[:, cols], cvp_ref[:, cols])


def _attn_sample(qkf, vf, cache_k, cache_v, page_table, sga, cvp, g_sub, lams, n_seq, n_new):
    n_pages = page_table.shape[1]
    n_pool = cache_k.shape[0]
    width = N_HEADS * V_DIM
    grp = SAMPLE_PAGES_PER_STEP
    kc = jnp.transpose(cache_k, (0, 2, 3, 4, 1)).reshape(n_pool, width, PAGE_SIZE)
    pt = page_table.reshape(-1)
    n_rows = N_HEADS * 2 * n_new
    row = pl.BlockSpec((n_new, width), lambda b, p, pt: (b, 0))

    def page_index(g):
        return lambda b, p, pt: (pt[b * n_pages + p * grp + g], 0, 0)

    def page_index4(g):
        return lambda b, p, pt: (pt[b * n_pages + p * grp + g], 0, 0, 0)

    k_pages = [pl.BlockSpec((None, width, PAGE_SIZE), page_index(g)) for g in range(grp)]
    v_pages = [pl.BlockSpec((None, PAGE_SIZE, N_HEADS, V_DIM), page_index4(g)) for g in range(grp)]
    vec64 = pl.BlockSpec((1, HEAD_DIM), lambda b, p, pt: (0, 0))
    col = jnp.arange(PAGE_SIZE * N_HEADS, dtype=jnp.int32)
    spread = (col[None, :] // N_HEADS == jnp.arange(PAGE_SIZE, dtype=jnp.int32)[:, None]).astype(BF16)
    own = (col[None, :] % N_HEADS == jnp.arange(n_rows, dtype=jnp.int32)[:, None] // (2 * n_new)).astype(BF16)
    const = lambda shape: pl.BlockSpec(shape, lambda b, p, pt: (0, 0))
    grid_spec = pltpu.PrefetchScalarGridSpec(
        num_scalar_prefetch=1,
        grid=(n_seq, n_pages // grp),
        in_specs=[pl.BlockSpec((None, n_new, width), lambda b, p, pt: (0, b, 0)),
                  pl.BlockSpec((None, n_new, width), lambda b, p, pt: (1, b, 0)),
                  row, *k_pages, *v_pages, const(spread.shape), const(own.shape), row, row,
                  pl.BlockSpec((1, V_DIM), lambda b, p, pt: (0, 0)),
                  vec64, vec64, vec64, vec64],
        out_specs=row,
        scratch_shapes=[pltpu.VMEM((n_rows, width), BF16),
                        pltpu.VMEM((n_rows, V_DIM), F32), pltpu.VMEM((n_rows, V_DIM), F32),
                        pltpu.VMEM((n_rows, V_DIM), F32)])
    return pl.pallas_call(
        functools.partial(_attn_sample_kernel, n_new=n_new, n_group=grp),
        out_shape=jax.ShapeDtypeStruct((n_seq * n_new, width), F32),
        grid_spec=grid_spec,
        compiler_params=_params("parallel", "arbitrary"),
        name="attn_sample",
    )(pt, qkf, qkf, vf, *([kc] * grp), *([cache_v] * grp), spread, own, sga, cvp, g_sub, *lams)


def _out_kernel(mix_ref, x_ref, wo_ref, g1_ref, g_ref, sc_ref, sh_ref, wrh_ref, wrl_ref, br_ref,
                xm_ref, h2_ref, idx_ref, gt_ref):
    y = jnp.dot(mix_ref[...].astype(BF16), wo_ref[...], preferred_element_type=F32)
    xm = x_ref[...] + g1_ref[0] * y
    xm_ref[...] = xm
    h2 = _rms_mod(xm, g_ref[...], sc_ref[0], sh_ref[0])
    hi = h2.astype(BF16)
    h2_ref[...] = _pack_pairs(hi.astype(F32))
    lo = (h2 - hi.astype(F32)).astype(BF16)
    logits = (jnp.dot(hi, wrh_ref[...], preferred_element_type=F32)
              + jnp.dot(lo, wrh_ref[...], preferred_element_type=F32)
              + jnp.dot(hi, wrl_ref[...], preferred_element_type=F32)) + br_ref[...]
    lane = lax.broadcasted_iota(jnp.int32, logits.shape, 1).astype(F32)
    vals, idxs = [], []
    for _ in range(TOP_K):
        m = jnp.max(logits, axis=-1, keepdims=True)
        ix = jnp.min(jnp.where(logits == m, lane, float(LANES)), axis=-1, keepdims=True)
        logits = jnp.where(lane == ix, -jnp.inf, logits)
        vals.append(m)
        idxs.append(ix)
    es = [jnp.exp(v - vals[0]) for v in vals]
    denom = es[0] + es[1] + es[2] + es[3]
    idx_out = jnp.zeros(logits.shape, F32)
    gt_out = jnp.zeros(logits.shape, F32)
    for k in range(TOP_K):
        idx_out = jnp.where(lane == float(k), idxs[k], idx_out)
        gt_out = jnp.where(lane == float(k), es[k] / denom, gt_out)
    idx_ref[...] = idx_out.astype(jnp.int32)
    gt_ref[...] = gt_out


def _out_proj(rows, mix, x, w_o_b, gate1, g, scale, shift, wr_hi, wr_lo, b_r):
    t = rows.n_tokens
    full = lambda shape: pl.BlockSpec(shape, lambda i: (0,) * len(shape))
    return pl.pallas_call(
        _out_kernel,
        out_shape=(jax.ShapeDtypeStruct((t, D_MODEL), F32),
                   jax.ShapeDtypeStruct((t, PACKED), jnp.uint32),
                   jax.ShapeDtypeStruct((t, LANES), jnp.int32),
                   jax.ShapeDtypeStruct((t, LANES), F32)),
        grid=(rows.n_tiles,),
        in_specs=[rows.row_spec(), rows.row_spec(), full((D_MODEL, D_MODEL)),
                  rows.mod_spec(), full((1, D_MODEL)), rows.mod_spec(), rows.mod_spec(),
                  full((D_MODEL, LANES)), full((D_MODEL, LANES)), full((1, LANES))],
        out_specs=(rows.row_spec(), rows.row_spec(PACKED), rows.row_spec(LANES), rows.row_spec(LANES)),
        compiler_params=_params("parallel"),
        name="out_proj_router",
    )(mix, x, w_o_b, gate1, g, scale, shift, wr_hi, wr_lo, b_r)


def _pack_pairs(x):
    bits = pltpu.bitcast(x, jnp.uint32)
    n = x.shape[1] // 2
    return (bits[:, :n] >> 16) | (bits[:, n:] & jnp.uint32(0xFFFF0000))


def _unpack_pairs(w):
    lo = pltpu.bitcast(w << 16, F32)
    hi = pltpu.bitcast(w & jnp.uint32(0xFFFF0000), F32)
    return jnp.concatenate([lo, hi], axis=1)


def _route_kernel(idx_ref, dest_ref, cnt_ref, run_ref, start_ref):
    ph = pl.program_id(0)
    i = pl.program_id(1)
    idx = idx_ref[...]
    tm = idx.shape[0]
    lane = lax.broadcasted_iota(jnp.int32, idx.shape, 1)
    onehots = [(lane == idx[:, k:k + 1]).astype(F32) for k in range(TOP_K)]
    member = onehots[0] + onehots[1] + onehots[2] + onehots[3]
    tile_count = jnp.sum(member, axis=0, keepdims=True)

    @pl.when((ph == 0) & (i == 0))
    def _():
        cnt_ref[...] = jnp.zeros(cnt_ref.shape, F32)

    @pl.when(ph == 0)
    def _():
        cnt_ref[...] = cnt_ref[...] + tile_count

    @pl.when((ph == 1) & (i == 0))
    def _():
        cnt = cnt_ref[...]
        padded = jnp.floor((cnt + (MOE_BLOCK - 1)) * (1.0 / MOE_BLOCK)) * MOE_BLOCK
        l1 = lax.broadcasted_iota(jnp.int32, cnt.shape, 1)
        incl = padded
        for s in (1, 2, 4, 8, 16, 32, 64):
            incl = incl + jnp.where(l1 >= s, pltpu.roll(incl, s, 1), 0.0)
        start_ref[...] = incl - padded
        run_ref[...] = jnp.zeros(run_ref.shape, F32)

    @pl.when(ph == 1)
    def _():
        r = lax.broadcasted_iota(jnp.int32, (tm, tm), 0)
        c = lax.broadcasted_iota(jnp.int32, (tm, tm), 1)
        earlier = (c < r).astype(BF16)
        before = jnp.dot(earlier, member.astype(BF16), preferred_element_type=F32)
        base = before + run_ref[0:1] + start_ref[0:1]
        out = jnp.zeros(idx.shape, F32)
        for k in range(TOP_K):
            d = jnp.sum(onehots[k] * base, axis=1, keepdims=True)
            out = jnp.where(lane == k, d, out)
        dest_ref[...] = out.astype(jnp.int32)
        run_ref[...] = run_ref[...] + tile_count


def _route(rows, idx):
    tm = rows.tm
    dest, counts = pl.pallas_call(
        _route_kernel,
        out_shape=(jax.ShapeDtypeStruct((rows.n_tokens, LANES), jnp.int32),
                   jax.ShapeDtypeStruct((SUBLANES, LANES), F32)),
        grid=(2, rows.n_tiles),
        in_specs=[pl.BlockSpec((tm, LANES), lambda ph, i: (i, 0))],
        out_specs=(pl.BlockSpec((tm, LANES), lambda ph, i: (i * ph, 0)),
                   pl.BlockSpec((SUBLANES, LANES), lambda ph, i: (0, 0))),
        scratch_shapes=[pltpu.VMEM((SUBLANES, LANES), F32), pltpu.VMEM((SUBLANES, LANES), F32)],
        compiler_params=_params("arbitrary", "arbitrary"),
        name="moe_route",
    )(idx)
    return dest, counts


def _dispatch_kernel(dest_ref, h_ref, zero_ref, xs_ref, sem):
    del zero_ref
    i = pl.program_id(0)
    tm = h_ref.shape[0]

    def body(t, carry):
        for k in range(TOP_K):
            d = dest_ref[(i * tm + t) * TOP_K + k]
            pltpu.make_async_copy(h_ref.at[pl.ds(t, 1)], xs_ref.at[pl.ds(d, 1)], sem).start()
        return carry

    lax.fori_loop(0, tm, body, 0, unroll=8)
    pltpu.make_async_copy(xs_ref.at[pl.ds(0, tm * TOP_K)], xs_ref.at[pl.ds(0, tm * TOP_K)], sem).wait()


def _dispatch(rows, dest_flat, h2u, n_rows):
    tm = rows.tm
    zeros = jnp.zeros((n_rows, PACKED), jnp.uint32)
    grid_spec = pltpu.PrefetchScalarGridSpec(
        num_scalar_prefetch=1, grid=(rows.n_tiles,),
        in_specs=[pl.BlockSpec((tm, PACKED), lambda i, d: (i, 0)),
                  pl.BlockSpec(memory_space=pl.ANY)],
        out_specs=pl.BlockSpec(memory_space=pl.ANY),
        scratch_shapes=[pltpu.SemaphoreType.DMA(())])
    return pl.pallas_call(
        _dispatch_kernel,
        out_shape=jax.ShapeDtypeStruct((n_rows, PACKED), jnp.uint32),
        grid_spec=grid_spec,
        input_output_aliases={2: 0},
        compiler_params=_params("arbitrary"),
        name="moe_dispatch",
    )(dest_flat, h2u, zeros)


def _expert_kernel(be_ref, nu_ref, x_ref, wgu_ref, wd_ref, bgu_ref, bd_ref, y_ref, wgu_s, wd_s):
    i = pl.program_id(0)
    used = i < nu_ref[0]
    new_expert = (i == 0) | (be_ref[i] != be_ref[jnp.maximum(i - 1, 0)])

    @pl.when(used & new_expert)
    def _():
        wgu_s[...] = wgu_ref[...].astype(BF16)
        wd_s[...] = wd_ref[...].astype(BF16).astype(F32)

    @pl.when(used)
    def _():
        x = _unpack_pairs(x_ref[...]).astype(BF16)
        gu = jnp.dot(x, wgu_s[...], preferred_element_type=F32) + bgu_ref[...]
        n = gu.shape[1]
        even = (lax.broadcasted_iota(jnp.int32, (1, n), 1) & 1) == 0
        gate = jnp.minimum(gu, SWIGLU_LIMIT)
        glu = gate * jax.nn.sigmoid(SWIGLU_ALPHA * gate)
        up1 = jnp.clip(gu, -SWIGLU_LIMIT, SWIGLU_LIMIT) + 1.0
        t = jnp.where(even, glu, up1)
        a = (t * jnp.where(even, pltpu.roll(t, n - 1, 1), pltpu.roll(t, 1, 1))).astype(BF16)
        wd2 = pltpu.bitcast(wd_s[...], BF16)
        y = jnp.dot(a, wd2, preferred_element_type=F32) + bd_ref[...]
        y_ref[...] = _pack_pairs(y.astype(BF16).astype(F32))

    @pl.when(i >= nu_ref[0])
    def _():
        y_ref[...] = jnp.zeros(y_ref.shape, y_ref.dtype)


def _experts(x_sorted, block_e, n_used, w_gu, w_dn, b_gu, b_d):
    rows = x_sorted.shape[0]
    n_blocks = rows // MOE_BLOCK
    d_gu = w_gu.shape[2]
    d_ff = w_dn.shape[1]
    by_expert = lambda i, be, nu: (be[i], 0, 0)
    xspec = pl.BlockSpec((MOE_BLOCK, PACKED), lambda i, be, nu: (i, 0))
    grid_spec = pltpu.PrefetchScalarGridSpec(
        num_scalar_prefetch=2, grid=(n_blocks,),
        in_specs=[xspec,
                  pl.BlockSpec((None, D_MODEL, d_gu), by_expert),
                  pl.BlockSpec((None, d_ff, D_MODEL), by_expert),
                  pl.BlockSpec((None, 1, d_gu), by_expert),
                  pl.BlockSpec((None, 1, D_MODEL), by_expert)],
        out_specs=xspec,
        scratch_shapes=[pltpu.VMEM((D_MODEL, d_gu), BF16), pltpu.VMEM((d_ff, D_MODEL), F32)])
    return pl.pallas_call(
        _expert_kernel,
        out_shape=jax.ShapeDtypeStruct((rows, PACKED), jnp.uint32),
        grid_spec=grid_spec,
        compiler_params=pltpu.CompilerParams(dimension_semantics=("arbitrary",),
                                             vmem_limit_bytes=EXPERT_VMEM_LIMIT_BYTES),
        name="experts",
    )(block_e, n_used, x_sorted, w_gu, w_dn, b_gu, b_d)


def _combine_kernel(dest_ref, xm_ref, gt_ref, g2_ref, yb_ref, o_ref, rows_ref, sem):
    i = pl.program_id(0)
    tm = xm_ref.shape[0]

    def body(t, carry):
        for k in range(TOP_K):
            d = dest_ref[(i * tm + t) * TOP_K + k]
            pltpu.make_async_copy(yb_ref.at[pl.ds(d, 1)], rows_ref.at[k, pl.ds(t, 1)], sem).start()
        return carry

    lax.fori_loop(0, tm, body, 0, unroll=8)
    pltpu.make_async_copy(rows_ref, rows_ref, sem).wait()
    gt = gt_ref[...]
    acc = jnp.zeros(xm_ref.shape, F32)
    for k in range(TOP_K):
        acc = acc + gt[:, k:k + 1] * _unpack_pairs(rows_ref[k])
    o_ref[...] = xm_ref[...] + g2_ref[0] * acc


def _combine(rows, dest_flat, xm, gates, gate2, y_buf):
    tm = rows.tm
    grid_spec = pltpu.PrefetchScalarGridSpec(
        num_scalar_prefetch=1, grid=(rows.n_tiles,),
        in_specs=[pl.BlockSpec((tm, D_MODEL), lambda i, d: (i, 0)),
                  pl.BlockSpec((tm, LANES), lambda i, d: (i, 0)),
                  rows.mod_spec(),
                  pl.BlockSpec(memory_space=pl.ANY)],
        out_specs=pl.BlockSpec((tm, D_MODEL), lambda i, d: (i, 0)),
        scratch_shapes=[pltpu.VMEM((TOP_K, tm, PACKED), jnp.uint32),
                        pltpu.SemaphoreType.DMA(())])
    return pl.pallas_call(
        _combine_kernel,
        out_shape=jax.ShapeDtypeStruct((rows.n_tokens, D_MODEL), F32),
        grid_spec=grid_spec,
        compiler_params=_params("arbitrary"),
        name="moe_combine",
    )(dest_flat, xm, gates, gate2, y_buf)


def _moe(rows, xm, h2u, idx, gates, gate2, moe_w):
    a = rows.n_tokens * TOP_K
    n_blocks = -(-a // MOE_BLOCK) + N_EXPERTS
    dest, counts = _route(rows, idx)
    dest_flat = dest[:, :TOP_K].reshape(a)
    cnt = counts[0, :N_EXPERTS].astype(jnp.int32)
    pad_end = jnp.cumsum((cnt + MOE_BLOCK - 1) // MOE_BLOCK * MOE_BLOCK)
    block_row = jnp.arange(n_blocks, dtype=jnp.int32) * MOE_BLOCK
    block_e = jnp.minimum(jnp.sum((pad_end[None, :] <= block_row[:, None]).astype(jnp.int32), axis=1),
                          N_EXPERTS - 1)
    n_used = (pad_end[-1:] // MOE_BLOCK).astype(jnp.int32)
    x_sorted = _dispatch(rows, dest_flat, h2u, n_blocks * MOE_BLOCK)
    y_buf = _experts(x_sorted, block_e, n_used, *moe_w)
    return _combine(rows, dest_flat, xm, gates, gate2, y_buf)


def _group(rows, x, ada, state_rows, qk_and_attend, shared):
    (g_mix, w_in_b, w_conv, w_o_b, g_ffn, wr_hi, wr_lo, b_r, moe_w, mid_dtype) = shared
    shift1, scale1, gate1, shift2, scale2, gate2 = [rows.mod_array(m) for m in jnp.split(ada, 6, axis=-1)]
    v_f, v_b = _v_proj(rows, x, g_mix, scale1, shift1, w_in_b)
    sga, cvp, tail = _conv_proj(rows, x, g_mix, scale1, shift1, w_in_b, w_conv, state_rows, mid_dtype)
    mix, k_out = qk_and_attend(x, scale1, shift1, v_f, v_b, sga, cvp)
    xm, h2, idx, gates = _out_proj(rows, mix, x, w_o_b, gate1, g_ffn, scale2, shift2, wr_hi, wr_lo, b_r)
    y = _moe(rows, xm, h2, idx, gates, gate2, moe_w)
    return y, k_out, v_f, tail


def kernel(x_prompt, x_sample, c_prompt, c_sample, cache_k, cache_v, state_conv, page_table, w_ada, b_ada, g_norm_mix, w_in, g_q, g_k, lambda_q1, lambda_k1, lambda_q2, lambda_k2, g_subln, w_conv, w_o, g_norm_ffn, w_router, b_router, w_gate_up, b_gate_up, w_down, b_down):
    assert w_in.shape[0] == 1, "single-layer stack"
    batch, seq, _ = x_prompt.shape
    n_seq, n_new, _ = x_sample.shape
    tp, ts = batch * seq, n_seq * n_new
    n_chunks = D_MODEL // HEAD_DIM

    ada = _ada(jnp.concatenate([c_prompt, c_sample], axis=0), w_ada[0], b_ada[0][None])

    w_in_b = w_in[0].astype(BF16)
    w_kt = w_in[0][:, D_MODEL:2 * D_MODEL].T.astype(BF16)
    w_o_b = w_o[0].astype(BF16)
    gqk = jnp.stack([jnp.tile(g_q[0], n_chunks), jnp.tile(g_k[0], n_chunks)])[:, None, :]
    gk_col = jnp.tile(g_k[0], n_chunks)[:, None]
    blk = jnp.arange(D_MODEL, dtype=jnp.int32) // HEAD_DIM
    pmat = (blk[:, None] == blk[None, :]).astype(BF16)
    wr = jnp.pad(w_router[0], ((0, 0), (0, LANES - N_EXPERTS)))
    wr_hi = wr.astype(BF16)
    wr_lo = (wr - wr_hi.astype(F32)).astype(BF16)
    b_r = jnp.pad(b_router[0], (0, LANES - N_EXPERTS), constant_values=NEG_BIG)[None]
    moe_w = (w_gate_up[0], w_down[0], b_gate_up[0][:, None, :], b_down[0][:, None, :])
    g_mix = g_norm_mix[0][None]
    g_ffn = g_norm_ffn[0][None]
    g_sub = g_subln[0][None]
    lams = (lambda_q1[0][None], lambda_k1[0][None], lambda_q2[0][None], lambda_k2[0][None])

    def shared(mid_dtype):
        return (g_mix, w_in_b, w_conv[0], w_o_b, g_ffn, wr_hi, wr_lo, b_r, moe_w, mid_dtype)

    rows_p = _Rows(tp, seq)

    def attend_p(x, scale1, shift1, v_f, v_b, sga, cvp):
        q_b = _q_proj(rows_p, x, g_mix, scale1, shift1, w_in_b, gqk, pmat)
        kt_f, kt_b = _kt_proj(rows_p, x, g_mix, scale1, shift1, w_kt, gk_col, batch, seq)
        mix = _attn_prompt(q_b, kt_b, v_b, sga, cvp, g_sub, lams, batch, seq)
        return mix, kt_f

    y_p, kt_p, v_p, tail_p = _group(rows_p, x_prompt.reshape(tp, D_MODEL), ada[:batch], None, attend_p, shared(BF16))

    rows_s = _Rows(ts, n_new)
    st = state_conv[0]
    zeros = jnp.zeros((n_seq, n_new - 2, D_MODEL), F32)
    s1 = jnp.concatenate([st[:, 1:2], jnp.zeros((n_seq, n_new - 1, D_MODEL), F32)], axis=1).reshape(ts, D_MODEL)
    s2 = jnp.concatenate([st, zeros], axis=1).reshape(ts, D_MODEL)

    def attend_s(x, scale1, shift1, v_f, v_b, sga, cvp):
        _, qk_f = _qk_proj(rows_s, x, g_mix, scale1, shift1, w_in_b, gqk, pmat)
        mix = _attn_sample(qk_f, v_f, cache_k[0], cache_v[0], page_table, sga, cvp, g_sub, lams, n_seq, n_new)
        return mix, qk_f[1]

    y_s, k_s, v_s, tail_s = _group(rows_s, x_sample.reshape(ts, D_MODEL), ada[batch:], (s1, s2), attend_s, shared(F32))

    tail_s = tail_s.reshape(n_seq, n_new, D_MODEL)
    k_p = kt_p.reshape(1, batch, N_HEADS, 2, HEAD_DIM, seq).transpose(0, 1, 5, 2, 3, 4)
    return (y_p.reshape(batch, seq, D_MODEL),
            y_s.reshape(n_seq, n_new, D_MODEL),
            k_p,
            v_p.reshape(1, batch, seq, N_HEADS, V_DIM),
            tail_p[:, SUBLANES - (CONV_WIDTH - 1):][None],
            k_s.reshape(1, n_seq, n_new, N_HEADS, 2, HEAD_DIM),
            v_s.reshape(1, n_seq, n_new, N_HEADS, V_DIM),
            tail_s[:, n_new - (CONV_WIDTH - 1):][None])
```

```python
import functools
import math

import jax
import jax.numpy as jnp
from jax import lax
from jax.experimental import pallas as pl
from jax.experimental.pallas import tpu as pltpu

F32 = jnp.float32
BF16 = jnp.bfloat16

D_MODEL = 1024
HEAD_DIM = 64
V_DIM = 2 * HEAD_DIM
N_HEADS = D_MODEL // V_DIM
ATTN_SCALE = HEAD_DIM ** -0.5
CONV_WIDTH = 3
PAGE_SIZE = 128
N_EXPERTS = 32
TOP_K = 4
SWIGLU_LIMIT = 7.0
SWIGLU_ALPHA = 1.702
NORM_EPS = 1e-6
LAMBDA_INIT = 0.8 - 0.6 * math.exp(-0.3 * 0)

VMEM_LIMIT_BYTES = 48 * 1024 * 1024
EXPERT_VMEM_LIMIT_BYTES = 56 * 1024 * 1024
LANES = 128
PACKED = D_MODEL // 2
SUBLANES = 8

ROW_TILE_PROMPT = 512
ATTN_BLOCK = 512
MOE_BLOCK = 256
EXPERT_XPOSE_CHUNK = 512
ADA_COL_TILE = 1536
SAMPLE_PAGES_PER_STEP = 8
NEG_BIG = -1e30

_NT = (((1,), (1,)), ((), ()))


def _params(*sem):
    return pltpu.CompilerParams(dimension_semantics=sem, vmem_limit_bytes=VMEM_LIMIT_BYTES)


def _rms_mod(x, g, scale, shift):
    ms = jnp.mean(x * x, axis=-1, keepdims=True)
    return (x * lax.rsqrt(ms + NORM_EPS) * g) * (1.0 + scale) + shift


def _ada_kernel(c_ref, w_ref, b_ref, o_ref):
    c = c_ref[...]
    s = (c * jax.nn.sigmoid(c)).astype(BF16)
    o_ref[...] = jnp.dot(s, w_ref[...].astype(BF16), preferred_element_type=F32) + b_ref[...]


def _ada(c_all, w_ada, b_ada):
    n = c_all.shape[0]
    width = w_ada.shape[1]
    return pl.pallas_call(
        _ada_kernel,
        out_shape=jax.ShapeDtypeStruct((n, width), F32),
        grid=(width // ADA_COL_TILE,),
        in_specs=[pl.BlockSpec((n, D_MODEL), lambda j: (0, 0)),
                  pl.BlockSpec((D_MODEL, ADA_COL_TILE), lambda j: (0, j)),
                  pl.BlockSpec((1, ADA_COL_TILE), lambda j: (0, j))],
        out_specs=pl.BlockSpec((n, ADA_COL_TILE), lambda j: (0, j)),
        compiler_params=_params("arbitrary"),
        name="ada",
    )(c_all, w_ada, b_ada)


class _Rows:
    def __init__(self, n_tokens, rows_per_batch):
        if rows_per_batch >= ROW_TILE_PROMPT:
            self.tm = ROW_TILE_PROMPT
            self.tiles_per_batch = rows_per_batch // self.tm
            self.mod_rows = 1
        else:
            self.tm = n_tokens
            self.tiles_per_batch = None
            self.mod_rows = n_tokens
        self.n_tokens = n_tokens
        self.rows_per_batch = rows_per_batch
        self.n_tiles = n_tokens // self.tm

    def mod_array(self, m):
        if self.tiles_per_batch is not None:
            return m[:, None, :]
        return jnp.repeat(m, self.rows_per_batch, axis=0)[None]

    def mod_spec(self):
        if self.tiles_per_batch is not None:
            tpb = self.tiles_per_batch
            return pl.BlockSpec((1, 1, D_MODEL), lambda i, *_: (i // tpb, 0, 0))
        return pl.BlockSpec((1, self.mod_rows, D_MODEL), lambda i, *_: (0, 0, 0))

    def row_spec(self, width=D_MODEL):
        return pl.BlockSpec((self.tm, width), lambda i, *_: (i, 0))


def _qk_kernel(x_ref, g_ref, sc_ref, sh_ref, w_ref, gqk_ref, p_ref, qkb_ref, qkf_ref, h_ref):
    j = pl.program_id(1)

    @pl.when(j == 0)
    def _():
        h_ref[...] = _rms_mod(x_ref[...], g_ref[...], sc_ref[0], sh_ref[0]).astype(BF16)

    z = jnp.dot(h_ref[...], w_ref[...], preferred_element_type=F32)
    ss = jnp.dot((z * z).astype(BF16), p_ref[...], preferred_element_type=F32)
    zn = z * lax.rsqrt(ss * (1.0 / HEAD_DIM) + NORM_EPS) * gqk_ref[0]
    qkf_ref[0] = zn
    scale = jnp.where(j == 0, ATTN_SCALE, 1.0)
    qkb_ref[...] = (zn * scale).astype(BF16)


def _qk_proj(rows, x, g, scale, shift, w_in_b, gqk, pmat):
    t = rows.n_tokens
    return pl.pallas_call(
        _qk_kernel,
        out_shape=(jax.ShapeDtypeStruct((t, 2 * D_MODEL), BF16),
                   jax.ShapeDtypeStruct((2, t, D_MODEL), F32)),
        grid=(rows.n_tiles, 2),
        in_specs=[rows.row_spec(),
                  pl.BlockSpec((1, D_MODEL), lambda i, j: (0, 0)),
                  rows.mod_spec(), rows.mod_spec(),
                  pl.BlockSpec((D_MODEL, D_MODEL), lambda i, j: (0, j)),
                  pl.BlockSpec((1, 1, D_MODEL), lambda i, j: (j, 0, 0)),
                  pl.BlockSpec((D_MODEL, D_MODEL), lambda i, j: (0, 0))],
        out_specs=(pl.BlockSpec((rows.tm, D_MODEL), lambda i, j: (i, j)),
                   pl.BlockSpec((1, rows.tm, D_MODEL), lambda i, j: (j, i, 0))),
        scratch_shapes=[pltpu.VMEM((rows.tm, D_MODEL), BF16)],
        compiler_params=_params("parallel", "arbitrary"),
        name="qk_proj",
    )(x, g, scale, shift, w_in_b, gqk, pmat)


def _q_kernel(x_ref, g_ref, sc_ref, sh_ref, w_ref, gq_ref, p_ref, qb_ref):
    h = _rms_mod(x_ref[...], g_ref[...], sc_ref[0], sh_ref[0]).astype(BF16)
    z = jnp.dot(h, w_ref[...], preferred_element_type=F32)
    ss = jnp.dot((z * z).astype(BF16), p_ref[...], preferred_element_type=F32)
    zn = z * lax.rsqrt(ss * (1.0 / HEAD_DIM) + NORM_EPS) * gq_ref[0]
    qb_ref[...] = (zn * ATTN_SCALE).astype(BF16)


def _q_proj(rows, x, g, scale, shift, w_in_b, gqk, pmat):
    return pl.pallas_call(
        _q_kernel,
        out_shape=jax.ShapeDtypeStruct((rows.n_tokens, D_MODEL), BF16),
        grid=(rows.n_tiles,),
        in_specs=[rows.row_spec(),
                  pl.BlockSpec((1, D_MODEL), lambda i: (0, 0)),
                  rows.mod_spec(), rows.mod_spec(),
                  pl.BlockSpec((D_MODEL, D_MODEL), lambda i: (0, 0)),
                  pl.BlockSpec((1, 1, D_MODEL), lambda i: (0, 0, 0)),
                  pl.BlockSpec((D_MODEL, D_MODEL), lambda i: (0, 0))],
        out_specs=rows.row_spec(),
        compiler_params=_params("parallel"),
        name="q_proj",
    )(x, g, scale, shift, w_in_b, gqk, pmat)


def _kt_kernel(x_ref, g_ref, sc_ref, sh_ref, wt_ref, gk_ref, kf_ref, kb_ref):
    h = _rms_mod(x_ref[...], g_ref[...], sc_ref[0], sh_ref[0]).astype(BF16)
    zt = lax.dot_general(wt_ref[...], h, _NT, preferred_element_type=F32)
    tm = zt.shape[1]
    z3 = zt.reshape(D_MODEL // HEAD_DIM, HEAD_DIM, tm)
    ss = jnp.sum(z3 * z3, axis=1, keepdims=True)
    g3 = gk_ref[...].reshape(D_MODEL // HEAD_DIM, HEAD_DIM, 1)
    zn = (z3 * lax.rsqrt(ss * (1.0 / HEAD_DIM) + NORM_EPS) * g3).reshape(D_MODEL, tm)
    kf_ref[...] = zn
    kb_ref[...] = zn.astype(BF16)


def _kt_proj(rows, x, g, scale, shift, w_kt, gk_col, batch, seq):
    tpb = rows.tiles_per_batch
    out_spec = pl.BlockSpec((None, D_MODEL, rows.tm), lambda i: (i // tpb, 0, i % tpb))
    return pl.pallas_call(
        _kt_kernel,
        out_shape=(jax.ShapeDtypeStruct((batch, D_MODEL, seq), F32),
                   jax.ShapeDtypeStruct((batch, D_MODEL, seq), BF16)),
        grid=(rows.n_tiles,),
        in_specs=[rows.row_spec(),
                  pl.BlockSpec((1, D_MODEL), lambda i: (0, 0)),
                  rows.mod_spec(), rows.mod_spec(),
                  pl.BlockSpec((D_MODEL, D_MODEL), lambda i: (0, 0)),
                  pl.BlockSpec((D_MODEL, 1), lambda i: (0, 0))],
        out_specs=(out_spec, out_spec),
        compiler_params=_params("parallel"),
        name="kt_proj",
    )(x, g, scale, shift, w_kt, gk_col)


def _v_kernel(x_ref, g_ref, sc_ref, sh_ref, w_ref, vf_ref, vb_ref):
    h = _rms_mod(x_ref[...], g_ref[...], sc_ref[0], sh_ref[0]).astype(BF16)
    z = jnp.dot(h, w_ref[...], preferred_element_type=F32)
    vf_ref[...] = z
    vb_ref[...] = z.astype(BF16)


def _v_proj(rows, x, g, scale, shift, w_in_b):
    t = rows.n_tokens
    return pl.pallas_call(
        _v_kernel,
        out_shape=(jax.ShapeDtypeStruct((t, D_MODEL), F32),
                   jax.ShapeDtypeStruct((t, D_MODEL), BF16)),
        grid=(rows.n_tiles,),
        in_specs=[rows.row_spec(),
                  pl.BlockSpec((1, D_MODEL), lambda i: (0, 0)),
                  rows.mod_spec(), rows.mod_spec(),
                  pl.BlockSpec((D_MODEL, D_MODEL), lambda i: (0, 2))],
        out_specs=(rows.row_spec(), rows.row_spec()),
        compiler_params=_params("parallel"),
        name="v_proj",
    )(x, g, scale, shift, w_in_b)


def _conv_kernel(*refs, tiles_per_batch, rows_per_batch, tail_rows):
    if tiles_per_batch is None:
        (x_ref, g_ref, sc_ref, sh_ref, w_ref, wc_ref, s1_ref, s2_ref,
         sga_ref, cvp_ref, tail_ref, h_ref, a_ref, b_ref, carry_ref) = refs
    else:
        (x_ref, g_ref, sc_ref, sh_ref, w_ref, wc_ref,
         sga_ref, cvp_ref, tail_ref, h_ref, a_ref, b_ref, carry_ref) = refs
    i = pl.program_id(0)
    j = pl.program_id(1)

    @pl.when(j == 0)
    def _():
        h_ref[...] = _rms_mod(x_ref[...], g_ref[...], sc_ref[0], sh_ref[0]).astype(BF16)

    z = jnp.dot(h_ref[...], w_ref[...], preferred_element_type=F32)

    @pl.when(j == 0)
    def _():
        a_ref[...] = z

    @pl.when(j == 1)
    def _():
        b_ref[...] = z

    @pl.when(j == 2)
    def _():
        u = b_ref[...] * z
        tm = u.shape[0]
        row = lax.broadcasted_iota(jnp.int32, (tm, 1), 0)
        r1 = pltpu.roll(u, 1, 0)
        r2 = pltpu.roll(u, 2, 0)
        if tiles_per_batch is None:
            t = row & (rows_per_batch - 1)
            u1 = jnp.where(t >= 1, r1, s1_ref[...])
            u2 = jnp.where(t >= 2, r2, s2_ref[...])
        else:
            first = (i % tiles_per_batch) == 0
            c = jnp.where(first, 0.0, carry_ref[...])
            u1 = jnp.where(row == 0, c[7:8], r1)
            u2 = jnp.where(row == 0, c[6:7], jnp.where(row == 1, c[7:8], r2))
            carry_ref[...] = u[tm - SUBLANES:]
        wc = wc_ref[...]
        yc = wc[0:1] * u2 + wc[1:2] * u1 + wc[2:3] * u
        a_ref[...] = a_ref[...] * yc
        tail_ref[0] = u[tm - tail_rows:]

    @pl.when(j == 3)
    def _():
        sga_ref[...] = jax.nn.sigmoid(z).astype(sga_ref.dtype)

    @pl.when(j == 4)
    def _():
        cvp_ref[...] = (jax.nn.sigmoid(z) * a_ref[...]).astype(cvp_ref.dtype)


def _conv_proj(rows, x, g, scale, shift, w_in_b, w_conv, state_rows, out_dtype):
    t = rows.n_tokens
    sample_mode = rows.tiles_per_batch is None
    tail_rows = rows.tm if sample_mode else SUBLANES
    n_tail_blocks = 1 if sample_mode else t // rows.rows_per_batch
    in_specs = [rows.row_spec(),
                pl.BlockSpec((1, D_MODEL), lambda i, j: (0, 0)),
                rows.mod_spec(), rows.mod_spec(),
                pl.BlockSpec((D_MODEL, D_MODEL), lambda i, j: (0, 3 + j)),
                pl.BlockSpec((CONV_WIDTH, D_MODEL), lambda i, j: (0, 0))]
    args = [x, g, scale, shift, w_in_b, w_conv]
    if sample_mode:
        in_specs += [rows.row_spec(), rows.row_spec()]
        args += list(state_rows)
        tail_spec = pl.BlockSpec((1, tail_rows, D_MODEL), lambda i, j: (0, 0, 0))
    else:
        tpb = rows.tiles_per_batch
        tail_spec = pl.BlockSpec((1, tail_rows, D_MODEL), lambda i, j: (i // tpb, 0, 0))
    kern = functools.partial(_conv_kernel, tiles_per_batch=rows.tiles_per_batch,
                             rows_per_batch=rows.rows_per_batch, tail_rows=tail_rows)
    return pl.pallas_call(
        kern,
        out_shape=(jax.ShapeDtypeStruct((t, D_MODEL), out_dtype),
                   jax.ShapeDtypeStruct((t, D_MODEL), out_dtype),
                   jax.ShapeDtypeStruct((n_tail_blocks, tail_rows, D_MODEL), F32)),
        grid=(rows.n_tiles, 5),
        in_specs=in_specs,
        out_specs=(rows.row_spec(), rows.row_spec(), tail_spec),
        scratch_shapes=[pltpu.VMEM((rows.tm, D_MODEL), BF16),
                        pltpu.VMEM((rows.tm, D_MODEL), F32),
                        pltpu.VMEM((rows.tm, D_MODEL), F32),
                        pltpu.VMEM((SUBLANES, D_MODEL), F32)],
        compiler_params=_params("arbitrary", "arbitrary"),
        name="conv_proj",
    )(*args)


def _lambda_value(lq1, lk1, lq2, lk2):
    e1 = jnp.exp(jnp.sum(lq1 * lk1, axis=-1, keepdims=True))
    e2 = jnp.exp(jnp.sum(lq2 * lk2, axis=-1, keepdims=True))
    return e1 - e2 + LAMBDA_INIT


def _subln_mix(o, g_sub, sga, cvp):
    ms = jnp.mean(o * o, axis=-1, keepdims=True)
    attn = (o * lax.rsqrt(ms + NORM_EPS) * g_sub) * (1.0 - LAMBDA_INIT)
    return sga * attn + cvp


def _flash_update(s, v, m_ref, l_ref, acc_ref):
    m_prev = m_ref[...]
    m_next = jnp.maximum(m_prev, jnp.max(s, axis=1, keepdims=True))
    alpha = jnp.exp(m_prev - m_next)
    p = jnp.exp(s - jnp.concatenate([m_next] * (s.shape[1] // LANES), axis=1))
    l_ref[...] = alpha * l_ref[...] + jnp.sum(p, axis=1, keepdims=True)
    acc_ref[...] = alpha * acc_ref[...] + jnp.dot(p.astype(BF16), v, preferred_element_type=F32)
    m_ref[...] = m_next


def _attn_prompt_kernel(qi_ref, ki_ref, q_ref, kt_ref, v_ref, sga_ref, cvp_ref, gs_ref,
                        lq1_ref, lk1_ref, lq2_ref, lk2_ref,
                        o_ref, m0_ref, l0_ref, a0_ref, m1_ref, l1_ref, a1_ref):
    qi = qi_ref[pl.program_id(2)]
    ki = ki_ref[pl.program_id(2)]

    @pl.when(ki == 0)
    def _():
        for m_ref, l_ref, a_ref in ((m0_ref, l0_ref, a0_ref), (m1_ref, l1_ref, a1_ref)):
            m_ref[...] = jnp.full(m_ref.shape, -jnp.inf, F32)
            l_ref[...] = jnp.zeros(l_ref.shape, F32)
            a_ref[...] = jnp.zeros(a_ref.shape, F32)

    def step(masked):
        q = q_ref[...]
        kt = kt_ref[...]
        v = v_ref[...]
        lane = lax.broadcasted_iota(jnp.int32, (1, V_DIM), 1)
        zero = jnp.zeros((), BF16)
        qs = (jnp.where(lane < HEAD_DIM, q, zero), jnp.where(lane >= HEAD_DIM, q, zero))
        states = ((m0_ref, l0_ref, a0_ref), (m1_ref, l1_ref, a1_ref))
        for qm, (m_ref, l_ref, a_ref) in zip(qs, states):
            s = jnp.dot(qm, kt, preferred_element_type=F32)
            if masked:
                r = lax.broadcasted_iota(jnp.int32, s.shape, 0)
                c = lax.broadcasted_iota(jnp.int32, s.shape, 1)
                s = jnp.where(c <= r, s, -jnp.inf)
            _flash_update(s, v, m_ref, l_ref, a_ref)

    @pl.when(ki < qi)
    def _():
        step(False)

    @pl.when(ki == qi)
    def _():
        step(True)
        lam = _lambda_value(lq1_ref[...], lk1_ref[...], lq2_ref[...], lk2_ref[...])
        o = a0_ref[...] / l0_ref[...] - lam * (a1_ref[...] / l1_ref[...])
        mix = _subln_mix(o, gs_ref[...], sga_ref[...].astype(F32), cvp_ref[...].astype(F32))
        o_ref[...] = mix.astype(o_ref.dtype)


def _attn_prompt(q_b, kt_b, v_b, sga, cvp, g_sub, lams, batch, seq):
    nb = seq // ATTN_BLOCK
    tq = ATTN_BLOCK
    pairs = [(qi, ki) for qi in range(nb) for ki in range(qi + 1)]
    qi_tab = jnp.array([p[0] for p in pairs], jnp.int32)
    ki_tab = jnp.array([p[1] for p in pairs], jnp.int32)
    q_spec = pl.BlockSpec((tq, V_DIM), lambda b, h, p, qt, kt: (b * nb + qt[p], h))
    k_spec = pl.BlockSpec((None, V_DIM, tq), lambda b, h, p, qt, kt: (b, h, kt[p]))
    v_spec = pl.BlockSpec((tq, V_DIM), lambda b, h, p, qt, kt: (b * nb + kt[p], h))
    vec64 = pl.BlockSpec((1, HEAD_DIM), lambda b, h, p, qt, kt: (0, 0))
    grid_spec = pltpu.PrefetchScalarGridSpec(
        num_scalar_prefetch=2,
        grid=(batch, N_HEADS, len(pairs)),
        in_specs=[q_spec, k_spec, v_spec, q_spec, q_spec,
                  pl.BlockSpec((1, V_DIM), lambda b, h, p, qt, kt: (0, 0)),
                  vec64, vec64, vec64, vec64],
        out_specs=q_spec,
        scratch_shapes=[pltpu.VMEM((tq, V_DIM), F32)] * 6)
    return pl.pallas_call(
        _attn_prompt_kernel,
        out_shape=jax.ShapeDtypeStruct((batch * seq, D_MODEL), BF16),
        grid_spec=grid_spec,
        compiler_params=_params("parallel", "parallel", "arbitrary"),
        name="attn_prompt",
    )(qi_tab, ki_tab, q_b, kt_b, v_b, sga, cvp, g_sub, *lams)


def _attn_sample_kernel(pt_ref, q_ref, kn_ref, vn_ref, *rest, n_new, n_group):
    kc_refs = rest[:n_group]
    vc_refs = rest[n_group:2 * n_group]
    (spread_ref, own_ref, sga_ref, cvp_ref, gs_ref, lq1_ref, lk1_ref, lq2_ref, lk2_ref,
     o_ref, qbd_ref, m_ref, l_ref, acc_ref) = rest[2 * n_group:]
    p = pl.program_id(1)
    rows_per_head = 2 * n_new

    @pl.when(p == 0)
    def _():
        q = q_ref[...] * ATTN_SCALE
        qrep = jnp.concatenate([q] * (N_HEADS * 2), axis=0)
        r = lax.broadcasted_iota(jnp.int32, qrep.shape, 0)
        c = lax.broadcasted_iota(jnp.int32, qrep.shape, 1)
        qbd_ref[...] = jnp.where(c // HEAD_DIM == r // n_new, qrep, 0.0).astype(BF16)
        m_ref[...] = jnp.full(m_ref.shape, -jnp.inf, F32)
        l_ref[...] = jnp.zeros(l_ref.shape, F32)
        acc_ref[...] = jnp.zeros(acc_ref.shape, F32)

    def update(s, pv_of):
        m_prev = m_ref[...]
        m_next = jnp.maximum(m_prev, jnp.max(s, axis=1, keepdims=True))
        alpha = jnp.exp(m_prev - m_next)
        width = s.shape[1]
        m_wide = m_next[:, :width] if width <= LANES else jnp.concatenate([m_next] * (width // LANES), axis=1)
        pr = jnp.exp(s - m_wide)
        l_ref[...] = alpha * l_ref[...] + jnp.sum(pr, axis=1, keepdims=True)
        acc_ref[...] = alpha * acc_ref[...] + pv_of(pr.astype(BF16))
        m_ref[...] = m_next

    def page_pv(pb, v_ref):
        pe = jnp.dot(pb, spread_ref[...], preferred_element_type=F32).astype(BF16) * own_ref[...]
        v2 = v_ref[...].reshape(PAGE_SIZE * N_HEADS, V_DIM).astype(BF16)
        return jnp.dot(pe, v2, preferred_element_type=F32)

    qbd = qbd_ref[...]
    s = jnp.concatenate([jnp.dot(qbd, kc_refs[g][...].astype(BF16), preferred_element_type=F32)
                         for g in range(n_group)], axis=1)

    def pages_pv(pb):
        out = page_pv(pb[:, :PAGE_SIZE], vc_refs[0])
        for g in range(1, n_group):
            out = out + page_pv(pb[:, g * PAGE_SIZE:(g + 1) * PAGE_SIZE], vc_refs[g])
        return out

    update(s, pages_pv)

    @pl.when(p == pl.num_programs(1) - 1)
    def _():
        s_new = lax.dot_general(qbd_ref[...], kn_ref[...].astype(BF16), _NT, preferred_element_type=F32)
        r = lax.broadcasted_iota(jnp.int32, s_new.shape, 0)
        c = lax.broadcasted_iota(jnp.int32, s_new.shape, 1)
        s_new = jnp.where(c <= (r & (n_new - 1)), s_new, -jnp.inf)
        def new_pv(pb):
            return jnp.concatenate(
                [jnp.dot(pb[h * rows_per_head:(h + 1) * rows_per_head].astype(F32),
                         vn_ref[:, h * V_DIM:(h + 1) * V_DIM].astype(BF16).astype(F32),
                         preferred_element_type=F32) for h in range(N_HEADS)], axis=0)

        update(s_new, new_pv)
        lam = _lambda_value(lq1_ref[...], lk1_ref[...], lq2_ref[...], lk2_ref[...])
        acc = acc_ref[...] / l_ref[...]
        for h in range(N_HEADS):
            cols = slice(h * V_DIM, (h + 1) * V_DIM)
            r0 = h * rows_per_head
            o = acc[r0:r0 + n_new] - lam * acc[r0 + n_new:r0 + rows_per_head]
            o_ref[:, cols] = _subln_mix(o, gs_ref[...], sga_ref[:, cols], cvp_ref[:, cols])


def _attn_sample(qkf, vf, cache_k, cache_v, page_table, sga, cvp, g_sub, lams, n_seq, n_new):
    n_pages = page_table.shape[1]
    n_pool = cache_k.shape[0]
    width = N_HEADS * V_DIM
    grp = SAMPLE_PAGES_PER_STEP
    kc = jnp.transpose(cache_k, (0, 2, 3, 4, 1)).reshape(n_pool, width, PAGE_SIZE)
    pt = page_table.reshape(-1)
    n_rows = N_HEADS * 2 * n_new
    row = pl.BlockSpec((n_new, width), lambda b, p, pt: (b, 0))

    def page_index(g):
        return lambda b, p, pt: (pt[b * n_pages + p * grp + g], 0, 0)

    def page_index4(g):
        return lambda b, p, pt: (pt[b * n_pages + p * grp + g], 0, 0, 0)

    k_pages = [pl.BlockSpec((None, width, PAGE_SIZE), page_index(g)) for g in range(grp)]
    v_pages = [pl.BlockSpec((None, PAGE_SIZE, N_HEADS, V_DIM), page_index4(g)) for g in range(grp)]
    vec64 = pl.BlockSpec((1, HEAD_DIM), lambda b, p, pt: (0, 0))
    col = jnp.arange(PAGE_SIZE * N_HEADS, dtype=jnp.int32)
    spread = (col[None, :] // N_HEADS == jnp.arange(PAGE_SIZE, dtype=jnp.int32)[:, None]).astype(BF16)
    own = (col[None, :] % N_HEADS == jnp.arange(n_rows, dtype=jnp.int32)[:, None] // (2 * n_new)).astype(BF16)
    const = lambda shape: pl.BlockSpec(shape, lambda b, p, pt: (0, 0))
    grid_spec = pltpu.PrefetchScalarGridSpec(
        num_scalar_prefetch=1,
        grid=(n_seq, n_pages // grp),
        in_specs=[pl.BlockSpec((None, n_new, width), lambda b, p, pt: (0, b, 0)),
                  pl.BlockSpec((None, n_new, width), lambda b, p, pt: (1, b, 0)),
                  row, *k_pages, *v_pages, const(spread.shape), const(own.shape), row, row,
                  pl.BlockSpec((1, V_DIM), lambda b, p, pt: (0, 0)),
                  vec64, vec64, vec64, vec64],
        out_specs=row,
        scratch_shapes=[pltpu.VMEM((n_rows, width), BF16),
                        pltpu.VMEM((n_rows, V_DIM), F32), pltpu.VMEM((n_rows, V_DIM), F32),
                        pltpu.VMEM((n_rows, V_DIM), F32)])
    return pl.pallas_call(
        functools.partial(_attn_sample_kernel, n_new=n_new, n_group=grp),
        out_shape=jax.ShapeDtypeStruct((n_seq * n_new, width), F32),
        grid_spec=grid_spec,
        compiler_params=_params("parallel", "arbitrary"),
        name="attn_sample",
    )(pt, qkf, qkf, vf, *([kc] * grp), *([cache_v] * grp), spread, own, sga, cvp, g_sub, *lams)


def _out_kernel(mix_ref, x_ref, wo_ref, g1_ref, g_ref, sc_ref, sh_ref, wrh_ref, wrl_ref, br_ref,
                xm_ref, h2_ref, idx_ref, gt_ref):
    y = jnp.dot(mix_ref[...].astype(BF16), wo_ref[...], preferred_element_type=F32)
    xm = x_ref[...] + g1_ref[0] * y
    xm_ref[...] = xm
    h2 = _rms_mod(xm, g_ref[...], sc_ref[0], sh_ref[0])
    hi = h2.astype(BF16)
    h2_ref[...] = _pack_pairs(hi.astype(F32))
    lo = (h2 - hi.astype(F32)).astype(BF16)
    logits = (jnp.dot(hi, wrh_ref[...], preferred_element_type=F32)
              + jnp.dot(lo, wrh_ref[...], preferred_element_type=F32)
              + jnp.dot(hi, wrl_ref[...], preferred_element_type=F32)) + br_ref[...]
    lane = lax.broadcasted_iota(jnp.int32, logits.shape, 1).astype(F32)
    vals, idxs = [], []
    for _ in range(TOP_K):
        m = jnp.max(logits, axis=-1, keepdims=True)
        ix = jnp.min(jnp.where(logits == m, lane, float(LANES)), axis=-1, keepdims=True)
        logits = jnp.where(lane == ix, -jnp.inf, logits)
        vals.append(m)
        idxs.append(ix)
    es = [jnp.exp(v - vals[0]) for v in vals]
    denom = es[0] + es[1] + es[2] + es[3]
    idx_out = jnp.zeros(logits.shape, F32)
    gt_out = jnp.zeros(logits.shape, F32)
    for k in range(TOP_K):
        idx_out = jnp.where(lane == float(k), idxs[k], idx_out)
        gt_out = jnp.where(lane == float(k), es[k] / denom, gt_out)
    idx_ref[...] = idx_out.astype(jnp.int32)
    gt_ref[...] = gt_out


def _out_proj(rows, mix, x, w_o_b, gate1, g, scale, shift, wr_hi, wr_lo, b_r):
    t = rows.n_tokens
    full = lambda shape: pl.BlockSpec(shape, lambda i: (0,) * len(shape))
    return pl.pallas_call(
        _out_kernel,
        out_shape=(jax.ShapeDtypeStruct((t, D_MODEL), F32),
                   jax.ShapeDtypeStruct((t, PACKED), jnp.uint32),
                   jax.ShapeDtypeStruct((t, LANES), jnp.int32),
                   jax.ShapeDtypeStruct((t, LANES), F32)),
        grid=(rows.n_tiles,),
        in_specs=[rows.row_spec(), rows.row_spec(), full((D_MODEL, D_MODEL)),
                  rows.mod_spec(), full((1, D_MODEL)), rows.mod_spec(), rows.mod_spec(),
                  full((D_MODEL, LANES)), full((D_MODEL, LANES)), full((1, LANES))],
        out_specs=(rows.row_spec(), rows.row_spec(PACKED), rows.row_spec(LANES), rows.row_spec(LANES)),
        compiler_params=_params("parallel"),
        name="out_proj_router",
    )(mix, x, w_o_b, gate1, g, scale, shift, wr_hi, wr_lo, b_r)


def _pack_pairs(x):
    bits = pltpu.bitcast(x, jnp.uint32)
    n = x.shape[1] // 2
    return (bits[:, :n] >> 16) | (bits[:, n:] & jnp.uint32(0xFFFF0000))


def _unpack_pairs(w):
    lo = pltpu.bitcast(w << 16, F32)
    hi = pltpu.bitcast(w & jnp.uint32(0xFFFF0000), F32)
    return jnp.concatenate([lo, hi], axis=1)


def _route_kernel(idx_ref, dest_ref, cnt_ref, run_ref, start_ref):
    ph = pl.program_id(0)
    i = pl.program_id(1)
    idx = idx_ref[...]
    tm = idx.shape[0]
    lane = lax.broadcasted_iota(jnp.int32, idx.shape, 1)
    onehots = [(lane == idx[:, k:k + 1]).astype(F32) for k in range(TOP_K)]
    member = onehots[0] + onehots[1] + onehots[2] + onehots[3]
    tile_count = jnp.sum(member, axis=0, keepdims=True)

    @pl.when((ph == 0) & (i == 0))
    def _():
        cnt_ref[...] = jnp.zeros(cnt_ref.shape, F32)

    @pl.when(ph == 0)
    def _():
        cnt_ref[...] = cnt_ref[...] + tile_count

    @pl.when((ph == 1) & (i == 0))
    def _():
        cnt = cnt_ref[...]
        padded = jnp.floor((cnt + (MOE_BLOCK - 1)) * (1.0 / MOE_BLOCK)) * MOE_BLOCK
        l1 = lax.broadcasted_iota(jnp.int32, cnt.shape, 1)
        incl = padded
        for s in (1, 2, 4, 8, 16, 32, 64):
            incl = incl + jnp.where(l1 >= s, pltpu.roll(incl, s, 1), 0.0)
        start_ref[...] = incl - padded
        run_ref[...] = jnp.zeros(run_ref.shape, F32)

    @pl.when(ph == 1)
    def _():
        r = lax.broadcasted_iota(jnp.int32, (tm, tm), 0)
        c = lax.broadcasted_iota(jnp.int32, (tm, tm), 1)
        earlier = (c < r).astype(BF16)
        before = jnp.dot(earlier, member.astype(BF16), preferred_element_type=F32)
        base = before + run_ref[0:1] + start_ref[0:1]
        out = jnp.zeros(idx.shape, F32)
        for k in range(TOP_K):
            d = jnp.sum(onehots[k] * base, axis=1, keepdims=True)
            out = jnp.where(lane == k, d, out)
        dest_ref[...] = out.astype(jnp.int32)
        run_ref[...] = run_ref[...] + tile_count


def _route(rows, idx):
    tm = rows.tm
    dest, counts = pl.pallas_call(
        _route_kernel,
        out_shape=(jax.ShapeDtypeStruct((rows.n_tokens, LANES), jnp.int32),
                   jax.ShapeDtypeStruct((SUBLANES, LANES), F32)),
        grid=(2, rows.n_tiles),
        in_specs=[pl.BlockSpec((tm, LANES), lambda ph, i: (i, 0))],
        out_specs=(pl.BlockSpec((tm, LANES), lambda ph, i: (i * ph, 0)),
                   pl.BlockSpec((SUBLANES, LANES), lambda ph, i: (0, 0))),
        scratch_shapes=[pltpu.VMEM((SUBLANES, LANES), F32), pltpu.VMEM((SUBLANES, LANES), F32)],
        compiler_params=_params("arbitrary", "arbitrary"),
        name="moe_route",
    )(idx)
    return dest, counts


def _dispatch_kernel(dest_ref, h_ref, zero_ref, xs_ref, sem):
    del zero_ref
    i = pl.program_id(0)
    tm = h_ref.shape[0]

    def body(t, carry):
        for k in range(TOP_K):
            d = dest_ref[(i * tm + t) * TOP_K + k]
            pltpu.make_async_copy(h_ref.at[pl.ds(t, 1)], xs_ref.at[pl.ds(d, 1)], sem).start()
        return carry

    lax.fori_loop(0, tm, body, 0, unroll=8)
    pltpu.make_async_copy(xs_ref.at[pl.ds(0, tm * TOP_K)], xs_ref.at[pl.ds(0, tm * TOP_K)], sem).wait()


def _dispatch(rows, dest_flat, h2u, n_rows):
    tm = rows.tm
    zeros = jnp.zeros((n_rows, PACKED), jnp.uint32)
    grid_spec = pltpu.PrefetchScalarGridSpec(
        num_scalar_prefetch=1, grid=(rows.n_tiles,),
        in_specs=[pl.BlockSpec((tm, PACKED), lambda i, d: (i, 0)),
                  pl.BlockSpec(memory_space=pl.ANY)],
        out_specs=pl.BlockSpec(memory_space=pl.ANY),
        scratch_shapes=[pltpu.SemaphoreType.DMA(())])
    return pl.pallas_call(
        _dispatch_kernel,
        out_shape=jax.ShapeDtypeStruct((n_rows, PACKED), jnp.uint32),
        grid_spec=grid_spec,
        input_output_aliases={2: 0},
        compiler_params=_params("arbitrary"),
        name="moe_dispatch",
    )(dest_flat, h2u, zeros)


def _expert_kernel(be_ref, nu_ref, x_ref, wgu_ref, wd_ref, bg_ref, bu_ref, bd_ref, y_ref,
                   wt_s, wg_s, wu_s, wd_s):
    i = pl.program_id(0)
    used = i < nu_ref[0]
    new_expert = (i == 0) | (be_ref[i] != be_ref[jnp.maximum(i - 1, 0)])

    @pl.when(used & new_expert)
    def _():
        n = wgu_ref.shape[1]
        n_lane_tiles = wt_s.shape[0]
        for c in range(n // EXPERT_XPOSE_CHUNK):
            cols = slice(c * EXPERT_XPOSE_CHUNK, (c + 1) * EXPERT_XPOSE_CHUNK)
            wt = wgu_ref[:, cols].T
            for j in range(n_lane_tiles):
                wt_s[j, cols, :] = wt[:, j * LANES:(j + 1) * LANES]
        for j in range(n_lane_tiles):
            lanes = slice(j * LANES, (j + 1) * LANES)
            wg_s[:, lanes] = wt_s[j, pl.ds(0, n // 2, stride=2), :].astype(BF16)
            wu_s[:, lanes] = wt_s[j, pl.ds(1, n // 2, stride=2), :].astype(BF16)
        wd_s[...] = wd_ref[...].astype(BF16)

    @pl.when(used)
    def _():
        x = _unpack_pairs(x_ref[...]).astype(BF16)
        g = lax.dot_general(x, wg_s[...], _NT, preferred_element_type=F32) + bg_ref[...]
        u = lax.dot_general(x, wu_s[...], _NT, preferred_element_type=F32) + bu_ref[...]
        gate = jnp.minimum(g, SWIGLU_LIMIT)
        up = jnp.clip(u, -SWIGLU_LIMIT, SWIGLU_LIMIT)
        glu = gate * jax.nn.sigmoid(SWIGLU_ALPHA * gate)
        a = ((up + 1.0) * glu).astype(BF16)
        y = jnp.dot(a, wd_s[...], preferred_element_type=F32) + bd_ref[...]
        y_ref[...] = _pack_pairs(y.astype(BF16).astype(F32))

    @pl.when(i >= nu_ref[0])
    def _():
        y_ref[...] = jnp.zeros(y_ref.shape, y_ref.dtype)


def _experts(x_sorted, block_e, n_used, w_gu, w_dn, b_g, b_u, b_d):
    rows = x_sorted.shape[0]
    n_blocks = rows // MOE_BLOCK
    d_gu = w_gu.shape[2]
    d_ff = w_dn.shape[1]
    by_expert = lambda i, be, nu: (be[i], 0, 0)
    xspec = pl.BlockSpec((MOE_BLOCK, PACKED), lambda i, be, nu: (i, 0))
    grid_spec = pltpu.PrefetchScalarGridSpec(
        num_scalar_prefetch=2, grid=(n_blocks,),
        in_specs=[xspec,
                  pl.BlockSpec((None, D_MODEL, d_gu), by_expert),
                  pl.BlockSpec((None, d_ff, D_MODEL), by_expert),
                  pl.BlockSpec((None, 1, d_ff), by_expert),
                  pl.BlockSpec((None, 1, d_ff), by_expert),
                  pl.BlockSpec((None, 1, D_MODEL), by_expert)],
        out_specs=xspec,
        scratch_shapes=[pltpu.VMEM((D_MODEL // LANES, d_gu, LANES), F32),
                        pltpu.VMEM((d_ff, D_MODEL), BF16), pltpu.VMEM((d_ff, D_MODEL), BF16),
                        pltpu.VMEM((d_ff, D_MODEL), BF16)])
    return pl.pallas_call(
        _expert_kernel,
        out_shape=jax.ShapeDtypeStruct((rows, PACKED), jnp.uint32),
        grid_spec=grid_spec,
        compiler_params=pltpu.CompilerParams(dimension_semantics=("arbitrary",),
                                             vmem_limit_bytes=EXPERT_VMEM_LIMIT_BYTES),
        name="experts",
    )(block_e, n_used, x_sorted, w_gu, w_dn, b_g, b_u, b_d)


def _combine_kernel(dest_ref, xm_ref, gt_ref, g2_ref, yb_ref, o_ref, rows_ref, sem):
    i = pl.program_id(0)
    tm = xm_ref.shape[0]

    def body(t, carry):
        for k in range(TOP_K):
            d = dest_ref[(i * tm + t) * TOP_K + k]
            pltpu.make_async_copy(yb_ref.at[pl.ds(d, 1)], rows_ref.at[k, pl.ds(t, 1)], sem).start()
        return carry

    lax.fori_loop(0, tm, body, 0, unroll=8)
    pltpu.make_async_copy(rows_ref, rows_ref, sem).wait()
    gt = gt_ref[...]
    acc = jnp.zeros(xm_ref.shape, F32)
    for k in range(TOP_K):
        acc = acc + gt[:, k:k + 1] * _unpack_pairs(rows_ref[k])
    o_ref[...] = xm_ref[...] + g2_ref[0] * acc


def _combine(rows, dest_flat, xm, gates, gate2, y_buf):
    tm = rows.tm
    grid_spec = pltpu.PrefetchScalarGridSpec(
        num_scalar_prefetch=1, grid=(rows.n_tiles,),
        in_specs=[pl.BlockSpec((tm, D_MODEL), lambda i, d: (i, 0)),
                  pl.BlockSpec((tm, LANES), lambda i, d: (i, 0)),
                  rows.mod_spec(),
                  pl.BlockSpec(memory_space=pl.ANY)],
        out_specs=pl.BlockSpec((tm, D_MODEL), lambda i, d: (i, 0)),
        scratch_shapes=[pltpu.VMEM((TOP_K, tm, PACKED), jnp.uint32),
                        pltpu.SemaphoreType.DMA(())])
    return pl.pallas_call(
        _combine_kernel,
        out_shape=jax.ShapeDtypeStruct((rows.n_tokens, D_MODEL), F32),
        grid_spec=grid_spec,
        compiler_params=_params("arbitrary"),
        name="moe_combine",
    )(dest_flat, xm, gates, gate2, y_buf)


def _moe(rows, xm, h2u, idx, gates, gate2, moe_w):
    a = rows.n_tokens * TOP_K
    n_blocks = -(-a // MOE_BLOCK) + N_EXPERTS
    dest, counts = _route(rows, idx)
    dest_flat = dest[:, :TOP_K].reshape(a)
    cnt = counts[0, :N_EXPERTS].astype(jnp.int32)
    pad_end = jnp.cumsum((cnt + MOE_BLOCK - 1) // MOE_BLOCK * MOE_BLOCK)
    block_row = jnp.arange(n_blocks, dtype=jnp.int32) * MOE_BLOCK
    block_e = jnp.minimum(jnp.sum((pad_end[None, :] <= block_row[:, None]).astype(jnp.int32), axis=1),
                          N_EXPERTS - 1)
    n_used = (pad_end[-1:] // MOE_BLOCK).astype(jnp.int32)
    x_sorted = _dispatch(rows, dest_flat, h2u, n_blocks * MOE_BLOCK)
    y_buf = _experts(x_sorted, block_e, n_used, *moe_w)
    return _combine(rows, dest_flat, xm, gates, gate2, y_buf)


def _group(rows, x, ada, state_rows, qk_and_attend, shared):
    (g_mix, w_in_b, w_conv, w_o_b, g_ffn, wr_hi, wr_lo, b_r, moe_w, mid_dtype) = shared
    shift1, scale1, gate1, shift2, scale2, gate2 = [rows.mod_array(m) for m in jnp.split(ada, 6, axis=-1)]
    v_f, v_b = _v_proj(rows, x, g_mix, scale1, shift1, w_in_b)
    sga, cvp, tail = _conv_proj(rows, x, g_mix, scale1, shift1, w_in_b, w_conv, state_rows, mid_dtype)
    mix, k_out = qk_and_attend(x, scale1, shift1, v_f, v_b, sga, cvp)
    xm, h2, idx, gates = _out_proj(rows, mix, x, w_o_b, gate1, g_ffn, scale2, shift2, wr_hi, wr_lo, b_r)
    y = _moe(rows, xm, h2, idx, gates, gate2, moe_w)
    return y, k_out, v_f, tail


def kernel(x_prompt, x_sample, c_prompt, c_sample, cache_k, cache_v, state_conv, page_table, w_ada, b_ada, g_norm_mix, w_in, g_q, g_k, lambda_q1, lambda_k1, lambda_q2, lambda_k2, g_subln, w_conv, w_o, g_norm_ffn, w_router, b_router, w_gate_up, b_gate_up, w_down, b_down):
    assert w_in.shape[0] == 1, "single-layer stack"
    batch, seq, _ = x_prompt.shape
    n_seq, n_new, _ = x_sample.shape
    tp, ts = batch * seq, n_seq * n_new
    n_chunks = D_MODEL // HEAD_DIM

    ada = _ada(jnp.concatenate([c_prompt, c_sample], axis=0), w_ada[0], b_ada[0][None])

    w_in_b = w_in[0].astype(BF16)
    w_kt = w_in[0][:, D_MODEL:2 * D_MODEL].T.astype(BF16)
    w_o_b = w_o[0].astype(BF16)
    gqk = jnp.stack([jnp.tile(g_q[0], n_chunks), jnp.tile(g_k[0], n_chunks)])[:, None, :]
    gk_col = jnp.tile(g_k[0], n_chunks)[:, None]
    blk = jnp.arange(D_MODEL, dtype=jnp.int32) // HEAD_DIM
    pmat = (blk[:, None] == blk[None, :]).astype(BF16)
    wr = jnp.pad(w_router[0], ((0, 0), (0, LANES - N_EXPERTS)))
    wr_hi = wr.astype(BF16)
    wr_lo = (wr - wr_hi.astype(F32)).astype(BF16)
    b_r = jnp.pad(b_router[0], (0, LANES - N_EXPERTS), constant_values=NEG_BIG)[None]
    moe_w = (w_gate_up[0], w_down[0], b_gate_up[0][:, None, 0::2], b_gate_up[0][:, None, 1::2],
             b_down[0][:, None, :])
    g_mix = g_norm_mix[0][None]
    g_ffn = g_norm_ffn[0][None]
    g_sub = g_subln[0][None]
    lams = (lambda_q1[0][None], lambda_k1[0][None], lambda_q2[0][None], lambda_k2[0][None])

    def shared(mid_dtype):
        return (g_mix, w_in_b, w_conv[0], w_o_b, g_ffn, wr_hi, wr_lo, b_r, moe_w, mid_dtype)

    rows_p = _Rows(tp, seq)

    def attend_p(x, scale1, shift1, v_f, v_b, sga, cvp):
        q_b = _q_proj(rows_p, x, g_mix, scale1, shift1, w_in_b, gqk, pmat)
        kt_f, kt_b = _kt_proj(rows_p, x, g_mix, scale1, shift1, w_kt, gk_col, batch, seq)
        mix = _attn_prompt(q_b, kt_b, v_b, sga, cvp, g_sub, lams, batch, seq)
        return mix, kt_f

    y_p, kt_p, v_p, tail_p = _group(rows_p, x_prompt.reshape(tp, D_MODEL), ada[:batch], None, attend_p, shared(BF16))

    rows_s = _Rows(ts, n_new)
    st = state_conv[0]
    zeros = jnp.zeros((n_seq, n_new - 2, D_MODEL), F32)
    s1 = jnp.concatenate([st[:, 1:2], jnp.zeros((n_seq, n_new - 1, D_MODEL), F32)], axis=1).reshape(ts, D_MODEL)
    s2 = jnp.concatenate([st, zeros], axis=1).reshape(ts, D_MODEL)

    def attend_s(x, scale1, shift1, v_f, v_b, sga, cvp):
        _, qk_f = _qk_proj(rows_s, x, g_mix, scale1, shift1, w_in_b, gqk, pmat)
        mix = _attn_sample(qk_f, v_f, cache_k[0], cache_v[0], page_table, sga, cvp, g_sub, lams, n_seq, n_new)
        return mix, qk_f[1]

    y_s, k_s, v_s, tail_s = _group(rows_s, x_sample.reshape(ts, D_MODEL), ada[batch:], (s1, s2), attend_s, shared(F32))

    tail_s = tail_s.reshape(n_seq, n_new, D_MODEL)
    k_p = kt_p.reshape(1, batch, N_HEADS, 2, HEAD_DIM, seq).transpose(0, 1, 5, 2, 3, 4)
    return (y_p.reshape(batch, seq, D_MODEL),
            y_s.reshape(n_seq, n_new, D_MODEL),
            k_p,
            v_p.reshape(1, batch, seq, N_HEADS, V_DIM),
            tail_p[:, SUBLANES - (CONV_WIDTH - 1):][None],
            k_s.reshape(1, n_seq, n_new, N_HEADS, 2, HEAD_DIM),
            v_s.reshape(1, n_seq, n_new, N_HEADS, V_DIM),
            tail_s[:, n_new - (CONV_WIDTH - 1):][None])
```

```python
import functools
import math

import jax
import jax.numpy as jnp
from jax import lax
from jax.experimental import pallas as pl
from jax.experimental.pallas import tpu as pltpu

F32 = jnp.float32
BF16 = jnp.bfloat16

D_MODEL = 1024
HEAD_DIM = 64
V_DIM = 2 * HEAD_DIM
N_HEADS = D_MODEL // V_DIM
ATTN_SCALE = HEAD_DIM ** -0.5
CONV_WIDTH = 3
PAGE_SIZE = 128
N_EXPERTS = 32
TOP_K = 4
SWIGLU_LIMIT = 7.0
SWIGLU_ALPHA = 1.702
NORM_EPS = 1e-6
LAMBDA_INIT = 0.8 - 0.6 * math.exp(-0.3 * 0)

VMEM_LIMIT_BYTES = 48 * 1024 * 1024
EXPERT_VMEM_LIMIT_BYTES = 56 * 1024 * 1024
LANES = 128
PACKED = D_MODEL // 2
SUBLANES = 8

ROW_TILE_PROMPT = 512
ATTN_BLOCK = 512
MOE_BLOCK = 512
MOE_TOKEN_TILE = 256
EXPERT_XPOSE_CHUNK = 512
EXPERT_FF_CHUNK = 256
ADA_COL_TILE = 1536
SAMPLE_PAGES_PER_STEP = 8
NEG_BIG = -1e30

_NT = (((1,), (1,)), ((), ()))


def _params(*sem):
    return pltpu.CompilerParams(dimension_semantics=sem, vmem_limit_bytes=VMEM_LIMIT_BYTES)


def _rms_mod(x, g, scale, shift):
    ms = jnp.mean(x * x, axis=-1, keepdims=True)
    return (x * lax.rsqrt(ms + NORM_EPS) * g) * (1.0 + scale) + shift


def _ada_kernel(c_ref, w_ref, b_ref, o_ref):
    c = c_ref[...]
    s = (c * jax.nn.sigmoid(c)).astype(BF16)
    o_ref[...] = jnp.dot(s, w_ref[...].astype(BF16), preferred_element_type=F32) + b_ref[...]


def _ada(c_all, w_ada, b_ada):
    n = c_all.shape[0]
    width = w_ada.shape[1]
    return pl.pallas_call(
        _ada_kernel,
        out_shape=jax.ShapeDtypeStruct((n, width), F32),
        grid=(width // ADA_COL_TILE,),
        in_specs=[pl.BlockSpec((n, D_MODEL), lambda j: (0, 0)),
                  pl.BlockSpec((D_MODEL, ADA_COL_TILE), lambda j: (0, j)),
                  pl.BlockSpec((1, ADA_COL_TILE), lambda j: (0, j))],
        out_specs=pl.BlockSpec((n, ADA_COL_TILE), lambda j: (0, j)),
        compiler_params=_params("arbitrary"),
        name="ada",
    )(c_all, w_ada, b_ada)


class _Rows:
    def __init__(self, n_tokens, rows_per_batch):
        if rows_per_batch >= ROW_TILE_PROMPT:
            self.tm = ROW_TILE_PROMPT
            self.tiles_per_batch = rows_per_batch // self.tm
            self.mod_rows = 1
        else:
            self.tm = n_tokens
            self.tiles_per_batch = None
            self.mod_rows = n_tokens
        self.n_tokens = n_tokens
        self.rows_per_batch = rows_per_batch
        self.n_tiles = n_tokens // self.tm

    def mod_array(self, m):
        if self.tiles_per_batch is not None:
            return m[:, None, :]
        return jnp.repeat(m, self.rows_per_batch, axis=0)[None]

    def mod_spec(self, clamp=False):
        if self.tiles_per_batch is not None:
            tpb, last = self.tiles_per_batch, self.n_tiles - 1
            if clamp:
                return pl.BlockSpec((1, 1, D_MODEL), lambda i, *_: (jnp.minimum(i, last) // tpb, 0, 0))
            return pl.BlockSpec((1, 1, D_MODEL), lambda i, *_: (i // tpb, 0, 0))
        return pl.BlockSpec((1, self.mod_rows, D_MODEL), lambda i, *_: (0, 0, 0))

    def row_spec(self, width=D_MODEL, clamp=False):
        if clamp:
            last = self.n_tiles - 1
            return pl.BlockSpec((self.tm, width), lambda i, *_: (jnp.minimum(i, last), 0))
        return pl.BlockSpec((self.tm, width), lambda i, *_: (i, 0))


def _qk_kernel(x_ref, g_ref, sc_ref, sh_ref, w_ref, gqk_ref, p_ref, qkb_ref, qkf_ref, h_ref):
    j = pl.program_id(1)

    @pl.when(j == 0)
    def _():
        h_ref[...] = _rms_mod(x_ref[...], g_ref[...], sc_ref[0], sh_ref[0]).astype(BF16)

    z = jnp.dot(h_ref[...], w_ref[...], preferred_element_type=F32)
    ss = jnp.dot((z * z).astype(BF16), p_ref[...], preferred_element_type=F32)
    zn = z * lax.rsqrt(ss * (1.0 / HEAD_DIM) + NORM_EPS) * gqk_ref[0]
    qkf_ref[0] = zn
    scale = jnp.where(j == 0, ATTN_SCALE, 1.0)
    qkb_ref[...] = (zn * scale).astype(BF16)


def _qk_proj(rows, x, g, scale, shift, w_in_b, gqk, pmat):
    t = rows.n_tokens
    return pl.pallas_call(
        _qk_kernel,
        out_shape=(jax.ShapeDtypeStruct((t, 2 * D_MODEL), BF16),
                   jax.ShapeDtypeStruct((2, t, D_MODEL), F32)),
        grid=(rows.n_tiles, 2),
        in_specs=[rows.row_spec(),
                  pl.BlockSpec((1, D_MODEL), lambda i, j: (0, 0)),
                  rows.mod_spec(), rows.mod_spec(),
                  pl.BlockSpec((D_MODEL, D_MODEL), lambda i, j: (0, j)),
                  pl.BlockSpec((1, 1, D_MODEL), lambda i, j: (j, 0, 0)),
                  pl.BlockSpec((D_MODEL, D_MODEL), lambda i, j: (0, 0))],
        out_specs=(pl.BlockSpec((rows.tm, D_MODEL), lambda i, j: (i, j)),
                   pl.BlockSpec((1, rows.tm, D_MODEL), lambda i, j: (j, i, 0))),
        scratch_shapes=[pltpu.VMEM((rows.tm, D_MODEL), BF16)],
        compiler_params=_params("parallel", "arbitrary"),
        name="qk_proj",
    )(x, g, scale, shift, w_in_b, gqk, pmat)


def _q_kernel(x_ref, g_ref, sc_ref, sh_ref, w_ref, gq_ref, p_ref, qb_ref):
    h = _rms_mod(x_ref[...], g_ref[...], sc_ref[0], sh_ref[0]).astype(BF16)
    z = jnp.dot(h, w_ref[...], preferred_element_type=F32)
    ss = jnp.dot((z * z).astype(BF16), p_ref[...], preferred_element_type=F32)
    zn = z * lax.rsqrt(ss * (1.0 / HEAD_DIM) + NORM_EPS) * gq_ref[0]
    qb_ref[...] = (zn * ATTN_SCALE).astype(BF16)


def _q_proj(rows, x, g, scale, shift, w_in_b, gqk, pmat):
    return pl.pallas_call(
        _q_kernel,
        out_shape=jax.ShapeDtypeStruct((rows.n_tokens, D_MODEL), BF16),
        grid=(rows.n_tiles,),
        in_specs=[rows.row_spec(),
                  pl.BlockSpec((1, D_MODEL), lambda i: (0, 0)),
                  rows.mod_spec(), rows.mod_spec(),
                  pl.BlockSpec((D_MODEL, D_MODEL), lambda i: (0, 0)),
                  pl.BlockSpec((1, 1, D_MODEL), lambda i: (0, 0, 0)),
                  pl.BlockSpec((D_MODEL, D_MODEL), lambda i: (0, 0))],
        out_specs=rows.row_spec(),
        compiler_params=_params("parallel"),
        name="q_proj",
    )(x, g, scale, shift, w_in_b, gqk, pmat)


def _kt_kernel(x_ref, g_ref, sc_ref, sh_ref, wt_ref, gk_ref, kf_ref, kb_ref):
    h = _rms_mod(x_ref[...], g_ref[...], sc_ref[0], sh_ref[0]).astype(BF16)
    zt = lax.dot_general(wt_ref[...], h, _NT, preferred_element_type=F32)
    tm = zt.shape[1]
    z3 = zt.reshape(D_MODEL // HEAD_DIM, HEAD_DIM, tm)
    ss = jnp.sum(z3 * z3, axis=1, keepdims=True)
    g3 = gk_ref[...].reshape(D_MODEL // HEAD_DIM, HEAD_DIM, 1)
    zn = (z3 * lax.rsqrt(ss * (1.0 / HEAD_DIM) + NORM_EPS) * g3).reshape(D_MODEL, tm)
    kf_ref[...] = zn
    kb_ref[...] = zn.astype(BF16)


def _kt_proj(rows, x, g, scale, shift, w_kt, gk_col, batch, seq):
    tpb = rows.tiles_per_batch
    out_spec = pl.BlockSpec((None, D_MODEL, rows.tm), lambda i: (i // tpb, 0, i % tpb))
    return pl.pallas_call(
        _kt_kernel,
        out_shape=(jax.ShapeDtypeStruct((batch, D_MODEL, seq), F32),
                   jax.ShapeDtypeStruct((batch, D_MODEL, seq), BF16)),
        grid=(rows.n_tiles,),
        in_specs=[rows.row_spec(),
                  pl.BlockSpec((1, D_MODEL), lambda i: (0, 0)),
                  rows.mod_spec(), rows.mod_spec(),
                  pl.BlockSpec((D_MODEL, D_MODEL), lambda i: (0, 0)),
                  pl.BlockSpec((D_MODEL, 1), lambda i: (0, 0))],
        out_specs=(out_spec, out_spec),
        compiler_params=_params("parallel"),
        name="kt_proj",
    )(x, g, scale, shift, w_kt, gk_col)


def _v_kernel(x_ref, g_ref, sc_ref, sh_ref, w_ref, vf_ref, vb_ref):
    h = _rms_mod(x_ref[...], g_ref[...], sc_ref[0], sh_ref[0]).astype(BF16)
    z = jnp.dot(h, w_ref[...], preferred_element_type=F32)
    vf_ref[...] = z
    vb_ref[...] = z.astype(BF16)


def _v_proj(rows, x, g, scale, shift, w_in_b):
    t = rows.n_tokens
    return pl.pallas_call(
        _v_kernel,
        out_shape=(jax.ShapeDtypeStruct((t, D_MODEL), F32),
                   jax.ShapeDtypeStruct((t, D_MODEL), BF16)),
        grid=(rows.n_tiles,),
        in_specs=[rows.row_spec(),
                  pl.BlockSpec((1, D_MODEL), lambda i: (0, 0)),
                  rows.mod_spec(), rows.mod_spec(),
                  pl.BlockSpec((D_MODEL, D_MODEL), lambda i: (0, 2))],
        out_specs=(rows.row_spec(), rows.row_spec()),
        compiler_params=_params("parallel"),
        name="v_proj",
    )(x, g, scale, shift, w_in_b)


def _conv_kernel(*refs, tiles_per_batch, rows_per_batch, tail_rows):
    if tiles_per_batch is None:
        (x_ref, g_ref, sc_ref, sh_ref, w_ref, wc_ref, s1_ref, s2_ref,
         sga_ref, cvp_ref, tail_ref, h_ref, a_ref, b_ref, carry_ref) = refs
    else:
        (x_ref, g_ref, sc_ref, sh_ref, w_ref, wc_ref,
         sga_ref, cvp_ref, tail_ref, h_ref, a_ref, b_ref, carry_ref) = refs
    i = pl.program_id(0)
    j = pl.program_id(1)

    @pl.when(j == 0)
    def _():
        h_ref[...] = _rms_mod(x_ref[...], g_ref[...], sc_ref[0], sh_ref[0]).astype(BF16)

    z = jnp.dot(h_ref[...], w_ref[...], preferred_element_type=F32)

    @pl.when(j == 0)
    def _():
        a_ref[...] = z

    @pl.when(j == 1)
    def _():
        b_ref[...] = z

    @pl.when(j == 2)
    def _():
        u = b_ref[...] * z
        tm = u.shape[0]
        row = lax.broadcasted_iota(jnp.int32, (tm, 1), 0)
        r1 = pltpu.roll(u, 1, 0)
        r2 = pltpu.roll(u, 2, 0)
        if tiles_per_batch is None:
            t = row & (rows_per_batch - 1)
            u1 = jnp.where(t >= 1, r1, s1_ref[...])
            u2 = jnp.where(t >= 2, r2, s2_ref[...])
        else:
            first = (i % tiles_per_batch) == 0
            c = jnp.where(first, 0.0, carry_ref[...])
            u1 = jnp.where(row == 0, c[7:8], r1)
            u2 = jnp.where(row == 0, c[6:7], jnp.where(row == 1, c[7:8], r2))
            carry_ref[...] = u[tm - SUBLANES:]
        wc = wc_ref[...]
        yc = wc[0:1] * u2 + wc[1:2] * u1 + wc[2:3] * u
        a_ref[...] = a_ref[...] * yc
        tail_ref[0] = u[tm - tail_rows:]

    @pl.when(j == 3)
    def _():
        sga_ref[...] = jax.nn.sigmoid(z).astype(sga_ref.dtype)

    @pl.when(j == 4)
    def _():
        cvp_ref[...] = (jax.nn.sigmoid(z) * a_ref[...]).astype(cvp_ref.dtype)


def _conv_proj(rows, x, g, scale, shift, w_in_b, w_conv, state_rows, out_dtype):
    t = rows.n_tokens
    sample_mode = rows.tiles_per_batch is None
    tail_rows = rows.tm if sample_mode else SUBLANES
    n_tail_blocks = 1 if sample_mode else t // rows.rows_per_batch
    in_specs = [rows.row_spec(),
                pl.BlockSpec((1, D_MODEL), lambda i, j: (0, 0)),
                rows.mod_spec(), rows.mod_spec(),
                pl.BlockSpec((D_MODEL, D_MODEL), lambda i, j: (0, 3 + j)),
                pl.BlockSpec((CONV_WIDTH, D_MODEL), lambda i, j: (0, 0))]
    args = [x, g, scale, shift, w_in_b, w_conv]
    if sample_mode:
        in_specs += [rows.row_spec(), rows.row_spec()]
        args += list(state_rows)
        tail_spec = pl.BlockSpec((1, tail_rows, D_MODEL), lambda i, j: (0, 0, 0))
    else:
        tpb = rows.tiles_per_batch
        tail_spec = pl.BlockSpec((1, tail_rows, D_MODEL), lambda i, j: (i // tpb, 0, 0))
    kern = functools.partial(_conv_kernel, tiles_per_batch=rows.tiles_per_batch,
                             rows_per_batch=rows.rows_per_batch, tail_rows=tail_rows)
    return pl.pallas_call(
        kern,
        out_shape=(jax.ShapeDtypeStruct((t, D_MODEL), out_dtype),
                   jax.ShapeDtypeStruct((t, D_MODEL), out_dtype),
                   jax.ShapeDtypeStruct((n_tail_blocks, tail_rows, D_MODEL), F32)),
        grid=(rows.n_tiles, 5),
        in_specs=in_specs,
        out_specs=(rows.row_spec(), rows.row_spec(), tail_spec),
        scratch_shapes=[pltpu.VMEM((rows.tm, D_MODEL), BF16),
                        pltpu.VMEM((rows.tm, D_MODEL), F32),
                        pltpu.VMEM((rows.tm, D_MODEL), F32),
                        pltpu.VMEM((SUBLANES, D_MODEL), F32)],
        compiler_params=_params("arbitrary", "arbitrary"),
        name="conv_proj",
    )(*args)


def _lambda_value(lq1, lk1, lq2, lk2):
    e1 = jnp.exp(jnp.sum(lq1 * lk1, axis=-1, keepdims=True))
    e2 = jnp.exp(jnp.sum(lq2 * lk2, axis=-1, keepdims=True))
    return e1 - e2 + LAMBDA_INIT


def _subln_mix(o, g_sub, sga, cvp):
    ms = jnp.mean(o * o, axis=-1, keepdims=True)
    attn = (o * lax.rsqrt(ms + NORM_EPS) * g_sub) * (1.0 - LAMBDA_INIT)
    return sga * attn + cvp


def _flash_update(s, v, m_ref, l_ref, acc_ref):
    m_prev = m_ref[...]
    m_next = jnp.maximum(m_prev, jnp.max(s, axis=1, keepdims=True))
    alpha = jnp.exp(m_prev - m_next)
    p = jnp.exp(s - jnp.concatenate([m_next] * (s.shape[1] // LANES), axis=1))
    l_ref[...] = alpha * l_ref[...] + jnp.sum(p, axis=1, keepdims=True)
    acc_ref[...] = alpha * acc_ref[...] + jnp.dot(p.astype(BF16), v, preferred_element_type=F32)
    m_ref[...] = m_next


def _attn_prompt_kernel(qi_ref, ki_ref, q_ref, kt_ref, v_ref, sga_ref, cvp_ref, gs_ref,
                        lq1_ref, lk1_ref, lq2_ref, lk2_ref,
                        o_ref, m0_ref, l0_ref, a0_ref, m1_ref, l1_ref, a1_ref):
    qi = qi_ref[pl.program_id(2)]
    ki = ki_ref[pl.program_id(2)]

    @pl.when(ki == 0)
    def _():
        for m_ref, l_ref, a_ref in ((m0_ref, l0_ref, a0_ref), (m1_ref, l1_ref, a1_ref)):
            m_ref[...] = jnp.full(m_ref.shape, -jnp.inf, F32)
            l_ref[...] = jnp.zeros(l_ref.shape, F32)
            a_ref[...] = jnp.zeros(a_ref.shape, F32)

    def step(masked):
        q = q_ref[...]
        kt = kt_ref[...]
        v = v_ref[...]
        lane = lax.broadcasted_iota(jnp.int32, (1, V_DIM), 1)
        zero = jnp.zeros((), BF16)
        qs = (jnp.where(lane < HEAD_DIM, q, zero), jnp.where(lane >= HEAD_DIM, q, zero))
        states = ((m0_ref, l0_ref, a0_ref), (m1_ref, l1_ref, a1_ref))
        for qm, (m_ref, l_ref, a_ref) in zip(qs, states):
            s = jnp.dot(qm, kt, preferred_element_type=F32)
            if masked:
                r = lax.broadcasted_iota(jnp.int32, s.shape, 0)
                c = lax.broadcasted_iota(jnp.int32, s.shape, 1)
                s = jnp.where(c <= r, s, -jnp.inf)
            _flash_update(s, v, m_ref, l_ref, a_ref)

    @pl.when(ki < qi)
    def _():
        step(False)

    @pl.when(ki == qi)
    def _():
        step(True)
        lam = _lambda_value(lq1_ref[...], lk1_ref[...], lq2_ref[...], lk2_ref[...])
        o = a0_ref[...] / l0_ref[...] - lam * (a1_ref[...] / l1_ref[...])
        mix = _subln_mix(o, gs_ref[...], sga_ref[...].astype(F32), cvp_ref[...].astype(F32))
        o_ref[...] = mix.astype(o_ref.dtype)


def _attn_prompt(q_b, kt_b, v_b, sga, cvp, g_sub, lams, batch, seq):
    nb = seq // ATTN_BLOCK
    tq = ATTN_BLOCK
    pairs = [(qi, ki) for qi in range(nb) for ki in range(qi + 1)]
    qi_tab = jnp.array([p[0] for p in pairs], jnp.int32)
    ki_tab = jnp.array([p[1] for p in pairs], jnp.int32)
    q_spec = pl.BlockSpec((tq, V_DIM), lambda b, h, p, qt, kt: (b * nb + qt[p], h))
    k_spec = pl.BlockSpec((None, V_DIM, tq), lambda b, h, p, qt, kt: (b, h, kt[p]))
    v_spec = pl.BlockSpec((tq, V_DIM), lambda b, h, p, qt, kt: (b * nb + kt[p], h))
    vec64 = pl.BlockSpec((1, HEAD_DIM), lambda b, h, p, qt, kt: (0, 0))
    grid_spec = pltpu.PrefetchScalarGridSpec(
        num_scalar_prefetch=2,
        grid=(batch, N_HEADS, len(pairs)),
        in_specs=[q_spec, k_spec, v_spec, q_spec, q_spec,
                  pl.BlockSpec((1, V_DIM), lambda b, h, p, qt, kt: (0, 0)),
                  vec64, vec64, vec64, vec64],
        out_specs=q_spec,
        scratch_shapes=[pltpu.VMEM((tq, V_DIM), F32)] * 6)
    return pl.pallas_call(
        _attn_prompt_kernel,
        out_shape=jax.ShapeDtypeStruct((batch * seq, D_MODEL), BF16),
        grid_spec=grid_spec,
        compiler_params=_params("parallel", "parallel", "arbitrary"),
        name="attn_prompt",
    )(qi_tab, ki_tab, q_b, kt_b, v_b, sga, cvp, g_sub, *lams)


def _attn_sample_kernel(pt_ref, q_ref, kn_ref, vn_ref, *rest, n_new, n_group):
    kc_refs = rest[:n_group]
    vc_refs = rest[n_group:2 * n_group]
    (spread_ref, own_ref, sga_ref, cvp_ref, gs_ref, lq1_ref, lk1_ref, lq2_ref, lk2_ref,
     o_ref, qbd_ref, m_ref, l_ref, acc_ref) = rest[2 * n_group:]
    p = pl.program_id(1)
    rows_per_head = 2 * n_new

    @pl.when(p == 0)
    def _():
        q = q_ref[...] * ATTN_SCALE
        qrep = jnp.concatenate([q] * (N_HEADS * 2), axis=0)
        r = lax.broadcasted_iota(jnp.int32, qrep.shape, 0)
        c = lax.broadcasted_iota(jnp.int32, qrep.shape, 1)
        qbd_ref[...] = jnp.where(c // HEAD_DIM == r // n_new, qrep, 0.0).astype(BF16)
        m_ref[...] = jnp.full(m_ref.shape, -jnp.inf, F32)
        l_ref[...] = jnp.zeros(l_ref.shape, F32)
        acc_ref[...] = jnp.zeros(acc_ref.shape, F32)

    def update(s, pv_of):
        m_prev = m_ref[...]
        m_next = jnp.maximum(m_prev, jnp.max(s, axis=1, keepdims=True))
        alpha = jnp.exp(m_prev - m_next)
        width = s.shape[1]
        m_wide = m_next[:, :width] if width <= LANES else jnp.concatenate([m_next] * (width // LANES), axis=1)
        pr = jnp.exp(s - m_wide)
        l_ref[...] = alpha * l_ref[...] + jnp.sum(pr, axis=1, keepdims=True)
        acc_ref[...] = alpha * acc_ref[...] + pv_of(pr.astype(BF16))
        m_ref[...] = m_next

    def page_pv(pb, v_ref):
        pe = jnp.dot(pb, spread_ref[...], preferred_element_type=F32).astype(BF16) * own_ref[...]
        v2 = v_ref[...].reshape(PAGE_SIZE * N_HEADS, V_DIM).astype(BF16)
        return jnp.dot(pe, v2, preferred_element_type=F32)

    qbd = qbd_ref[...]
    s = jnp.concatenate([jnp.dot(qbd, kc_refs[g][...].astype(BF16), preferred_element_type=F32)
                         for g in range(n_group)], axis=1)

    def pages_pv(pb):
        out = page_pv(pb[:, :PAGE_SIZE], vc_refs[0])
        for g in range(1, n_group):
            out = out + page_pv(pb[:, g * PAGE_SIZE:(g + 1) * PAGE_SIZE], vc_refs[g])
        return out

    update(s, pages_pv)

    @pl.when(p == pl.num_programs(1) - 1)
    def _():
        s_new = lax.dot_general(qbd_ref[...], kn_ref[...].astype(BF16), _NT, preferred_element_type=F32)
        r = lax.broadcasted_iota(jnp.int32, s_new.shape, 0)
        c = lax.broadcasted_iota(jnp.int32, s_new.shape, 1)
        s_new = jnp.where(c <= (r & (n_new - 1)), s_new, -jnp.inf)
        def new_pv(pb):
            return jnp.concatenate(
                [jnp.dot(pb[h * rows_per_head:(h + 1) * rows_per_head].astype(F32),
                         vn_ref[:, h * V_DIM:(h + 1) * V_DIM].astype(BF16).astype(F32),
                         preferred_element_type=F32) for h in range(N_HEADS)], axis=0)

        update(s_new, new_pv)
        lam = _lambda_value(lq1_ref[...], lk1_ref[...], lq2_ref[...], lk2_ref[...])
        acc = acc_ref[...] / l_ref[...]
        for h in range(N_HEADS):
            cols = slice(h * V_DIM, (h + 1) * V_DIM)
            r0 = h * rows_per_head
            o = acc[r0:r0 + n_new] - lam * acc[r0 + n_new:r0 + rows_per_head]
            o_ref[:, cols] = _subln_mix(o, gs_ref[...], sga_ref[:, cols], cvp_ref[:, cols])


def _attn_sample(qkf, vf, cache_k, cache_v, page_table, sga, cvp, g_sub, lams, n_seq, n_new):
    n_pages = page_table.shape[1]
    n_pool = cache_k.shape[0]
    width = N_HEADS * V_DIM
    grp = SAMPLE_PAGES_PER_STEP
    kc = jnp.transpose(cache_k, (0, 2, 3, 4, 1)).reshape(n_pool, width, PAGE_SIZE)
    pt = page_table.reshape(-1)
    n_rows = N_HEADS * 2 * n_new
    row = pl.BlockSpec((n_new, width), lambda b, p, pt: (b, 0))

    def page_index(g):
        return lambda b, p, pt: (pt[b * n_pages + p * grp + g], 0, 0)

    def page_index4(g):
        return lambda b, p, pt: (pt[b * n_pages + p * grp + g], 0, 0, 0)

    k_pages = [pl.BlockSpec((None, width, PAGE_SIZE), page_index(g)) for g in range(grp)]
    v_pages = [pl.BlockSpec((None, PAGE_SIZE, N_HEADS, V_DIM), page_index4(g)) for g in range(grp)]
    vec64 = pl.BlockSpec((1, HEAD_DIM), lambda b, p, pt: (0, 0))
    col = jnp.arange(PAGE_SIZE * N_HEADS, dtype=jnp.int32)
    spread = (col[None, :] // N_HEADS == jnp.arange(PAGE_SIZE, dtype=jnp.int32)[:, None]).astype(BF16)
    own = (col[None, :] % N_HEADS == jnp.arange(n_rows, dtype=jnp.int32)[:, None] // (2 * n_new)).astype(BF16)
    const = lambda shape: pl.BlockSpec(shape, lambda b, p, pt: (0, 0))
    grid_spec = pltpu.PrefetchScalarGridSpec(
        num_scalar_prefetch=1,
        grid=(n_seq, n_pages // grp),
        in_specs=[pl.BlockSpec((None, n_new, width), lambda b, p, pt: (0, b, 0)),
                  pl.BlockSpec((None, n_new, width), lambda b, p, pt: (1, b, 0)),
                  row, *k_pages, *v_pages, const(spread.shape), const(own.shape), row, row,
                  pl.BlockSpec((1, V_DIM), lambda b, p, pt: (0, 0)),
                  vec64, vec64, vec64, vec64],
        out_specs=row,
        scratch_shapes=[pltpu.VMEM((n_rows, width), BF16),
                        pltpu.VMEM((n_rows, V_DIM), F32), pltpu.VMEM((n_rows, V_DIM), F32),
                        pltpu.VMEM((n_rows, V_DIM), F32)])
    return pl.pallas_call(
        functools.partial(_attn_sample_kernel, n_new=n_new, n_group=grp),
        out_shape=jax.ShapeDtypeStruct((n_seq * n_new, width), F32),
        grid_spec=grid_spec,
        compiler_params=_params("parallel", "arbitrary"),
        name="attn_sample",
    )(pt, qkf, qkf, vf, *([kc] * grp), *([cache_v] * grp), spread, own, sga, cvp, g_sub, *lams)


def _out_kernel(mix_ref, x_ref, wo_ref, g1_ref, g_ref, sc_ref, sh_ref, wrh_ref, wrl_ref, br_ref, *refs, n_real_tiles):
    outs = refs[-4:]
    i = pl.program_id(0)

    @pl.when(i < n_real_tiles)
    def _():
        _out_tile(mix_ref, x_ref, wo_ref, g1_ref, g_ref, sc_ref, sh_ref, wrh_ref, wrl_ref, br_ref, *outs)

    @pl.when(i >= n_real_tiles)
    def _():
        for r in outs:
            r[...] = jnp.zeros(r.shape, r.dtype)


def _out_tile(mix_ref, x_ref, wo_ref, g1_ref, g_ref, sc_ref, sh_ref, wrh_ref, wrl_ref, br_ref,
              xm_ref, h2_ref, idx_ref, gt_ref):
    y = jnp.dot(mix_ref[...].astype(BF16), wo_ref[...], preferred_element_type=F32)
    xm = x_ref[...] + g1_ref[0] * y
    xm_ref[...] = xm
    h2 = _rms_mod(xm, g_ref[...], sc_ref[0], sh_ref[0])
    hi = h2.astype(BF16)
    h2_ref[...] = _pack_pairs(hi.astype(F32))
    lo = (h2 - hi.astype(F32)).astype(BF16)
    logits = (jnp.dot(hi, wrh_ref[...], preferred_element_type=F32)
              + jnp.dot(lo, wrh_ref[...], preferred_element_type=F32)
              + jnp.dot(hi, wrl_ref[...], preferred_element_type=F32)) + br_ref[...]
    lane = lax.broadcasted_iota(jnp.int32, logits.shape, 1).astype(F32)
    vals, idxs = [], []
    for _ in range(TOP_K):
        m = jnp.max(logits, axis=-1, keepdims=True)
        ix = jnp.min(jnp.where(logits == m, lane, float(LANES)), axis=-1, keepdims=True)
        logits = jnp.where(lane == ix, -jnp.inf, logits)
        vals.append(m)
        idxs.append(ix)
    es = [jnp.exp(v - vals[0]) for v in vals]
    denom = es[0] + es[1] + es[2] + es[3]
    idx_out = jnp.zeros(logits.shape, F32)
    gt_out = jnp.zeros(logits.shape, F32)
    for k in range(TOP_K):
        idx_out = jnp.where(lane == float(k), idxs[k], idx_out)
        gt_out = jnp.where(lane == float(k), es[k] / denom, gt_out)
    idx_ref[...] = idx_out.astype(jnp.int32)
    gt_ref[...] = gt_out


def _out_proj(rows, mix, x, w_o_b, gate1, g, scale, shift, wr_hi, wr_lo, b_r, total_tokens, first_row, shared_bufs):
    full = lambda shape: pl.BlockSpec(shape, lambda i: (0,) * len(shape))
    off = first_row // rows.tm
    creates = shared_bufs is None
    out_row = lambda width: pl.BlockSpec((rows.tm, width), lambda i: (i + off, 0))
    in_specs = [rows.row_spec(clamp=creates), rows.row_spec(clamp=creates), full((D_MODEL, D_MODEL)),
                rows.mod_spec(clamp=creates), full((1, D_MODEL)), rows.mod_spec(clamp=creates),
                rows.mod_spec(clamp=creates),
                full((D_MODEL, LANES)), full((D_MODEL, LANES)), full((1, LANES))]
    args = [mix, x, w_o_b, gate1, g, scale, shift, wr_hi, wr_lo, b_r]
    aliases = {}
    if not creates:
        aliases = {len(args) + j: j for j in range(len(shared_bufs))}
        in_specs += [pl.BlockSpec(memory_space=pl.ANY)] * len(shared_bufs)
        args += list(shared_bufs)
    extra = 1 if creates and total_tokens > rows.n_tokens else 0
    assert total_tokens - rows.n_tokens <= rows.tm or not creates, "other group must fit the one extra tile"
    return pl.pallas_call(
        functools.partial(_out_kernel, n_real_tiles=rows.n_tiles),
        out_shape=(jax.ShapeDtypeStruct((total_tokens, D_MODEL), F32),
                   jax.ShapeDtypeStruct((total_tokens, PACKED), jnp.uint32),
                   jax.ShapeDtypeStruct((total_tokens, LANES), jnp.int32),
                   jax.ShapeDtypeStruct((total_tokens, LANES), F32)),
        grid=(rows.n_tiles + extra,),
        in_specs=in_specs,
        out_specs=(out_row(D_MODEL), out_row(PACKED), out_row(LANES), out_row(LANES)),
        input_output_aliases=aliases,
        compiler_params=_params("arbitrary"),
        name="out_proj_router",
    )(*args)


def _pack_pairs(x):
    bits = pltpu.bitcast(x, jnp.uint32)
    n = x.shape[1] // 2
    return (bits[:, :n] >> 16) | (bits[:, n:] & jnp.uint32(0xFFFF0000))


def _unpack_pairs(w):
    lo = pltpu.bitcast(w << 16, F32)
    hi = pltpu.bitcast(w & jnp.uint32(0xFFFF0000), F32)
    return jnp.concatenate([lo, hi], axis=1)


def _route_kernel(idx_ref, dest_ref, cnt_ref, run_ref, start_ref):
    ph = pl.program_id(0)
    i = pl.program_id(1)
    idx = idx_ref[...]
    tm = idx.shape[0]
    lane = lax.broadcasted_iota(jnp.int32, idx.shape, 1)
    onehots = [(lane == idx[:, k:k + 1]).astype(F32) for k in range(TOP_K)]
    member = onehots[0] + onehots[1] + onehots[2] + onehots[3]
    tile_count = jnp.sum(member, axis=0, keepdims=True)

    @pl.when((ph == 0) & (i == 0))
    def _():
        cnt_ref[...] = jnp.zeros(cnt_ref.shape, F32)

    @pl.when(ph == 0)
    def _():
        cnt_ref[...] = cnt_ref[...] + tile_count

    @pl.when((ph == 1) & (i == 0))
    def _():
        cnt = cnt_ref[...]
        padded = jnp.floor((cnt + (MOE_BLOCK - 1)) * (1.0 / MOE_BLOCK)) * MOE_BLOCK
        l1 = lax.broadcasted_iota(jnp.int32, cnt.shape, 1)
        incl = padded
        for s in (1, 2, 4, 8, 16, 32, 64):
            incl = incl + jnp.where(l1 >= s, pltpu.roll(incl, s, 1), 0.0)
        start_ref[...] = incl - padded
        run_ref[...] = jnp.zeros(run_ref.shape, F32)

    @pl.when(ph == 1)
    def _():
        r = lax.broadcasted_iota(jnp.int32, (tm, tm), 0)
        c = lax.broadcasted_iota(jnp.int32, (tm, tm), 1)
        earlier = (c < r).astype(BF16)
        before = jnp.dot(earlier, member.astype(BF16), preferred_element_type=F32)
        base = before + run_ref[0:1] + start_ref[0:1]
        out = jnp.zeros(idx.shape, F32)
        for k in range(TOP_K):
            d = jnp.sum(onehots[k] * base, axis=1, keepdims=True)
            out = jnp.where(lane == k, d, out)
        dest_ref[...] = out.astype(jnp.int32)
        run_ref[...] = run_ref[...] + tile_count


def _route(rows, idx):
    tm = rows.tm
    dest, counts = pl.pallas_call(
        _route_kernel,
        out_shape=(jax.ShapeDtypeStruct((rows.n_tokens, LANES), jnp.int32),
                   jax.ShapeDtypeStruct((SUBLANES, LANES), F32)),
        grid=(2, rows.n_tiles),
        in_specs=[pl.BlockSpec((tm, LANES), lambda ph, i: (i, 0))],
        out_specs=(pl.BlockSpec((tm, LANES), lambda ph, i: (i * ph, 0)),
                   pl.BlockSpec((SUBLANES, LANES), lambda ph, i: (0, 0))),
        scratch_shapes=[pltpu.VMEM((SUBLANES, LANES), F32), pltpu.VMEM((SUBLANES, LANES), F32)],
        compiler_params=_params("arbitrary", "arbitrary"),
        name="moe_route",
    )(idx)
    return dest, counts


def _dispatch_kernel(dest_ref, h_ref, zero_ref, xs_ref, sem):
    del zero_ref
    i = pl.program_id(0)
    tm = h_ref.shape[0]

    def body(t, carry):
        for k in range(TOP_K):
            d = dest_ref[(i * tm + t) * TOP_K + k]
            pltpu.make_async_copy(h_ref.at[pl.ds(t, 1)], xs_ref.at[pl.ds(d, 1)], sem).start()
        return carry

    lax.fori_loop(0, tm, body, 0, unroll=8)
    pltpu.make_async_copy(xs_ref.at[pl.ds(0, tm * TOP_K)], xs_ref.at[pl.ds(0, tm * TOP_K)], sem).wait()


def _dispatch(rows, dest_flat, h2u, n_rows):
    tm = rows.tm
    zeros = jnp.zeros((n_rows, PACKED), jnp.uint32)
    grid_spec = pltpu.PrefetchScalarGridSpec(
        num_scalar_prefetch=1, grid=(rows.n_tiles,),
        in_specs=[pl.BlockSpec((tm, PACKED), lambda i, d: (i, 0)),
                  pl.BlockSpec(memory_space=pl.ANY)],
        out_specs=pl.BlockSpec(memory_space=pl.ANY),
        scratch_shapes=[pltpu.SemaphoreType.DMA(())])
    return pl.pallas_call(
        _dispatch_kernel,
        out_shape=jax.ShapeDtypeStruct((n_rows, PACKED), jnp.uint32),
        grid_spec=grid_spec,
        input_output_aliases={2: 0},
        compiler_params=_params("arbitrary"),
        name="moe_dispatch",
    )(dest_flat, h2u, zeros)


def _expert_kernel(be_ref, nu_ref, x_ref, wgu_ref, wd_ref, bg_ref, bu_ref, bd_ref, y_ref,
                   wt_s, wg_s, wu_s, wd_s):
    i = pl.program_id(0)
    used = i < nu_ref[0]
    new_expert = (i == 0) | (be_ref[i] != be_ref[jnp.maximum(i - 1, 0)])

    @pl.when(used & new_expert)
    def _():
        n = wgu_ref.shape[1]
        n_lane_tiles = wt_s.shape[0]
        for c in range(n // EXPERT_XPOSE_CHUNK):
            cols = slice(c * EXPERT_XPOSE_CHUNK, (c + 1) * EXPERT_XPOSE_CHUNK)
            wt = wgu_ref[:, cols].T
            for j in range(n_lane_tiles):
                wt_s[j, cols, :] = wt[:, j * LANES:(j + 1) * LANES]
        for j in range(n_lane_tiles):
            lanes = slice(j * LANES, (j + 1) * LANES)
            wg_s[:, lanes] = wt_s[j, pl.ds(0, n // 2, stride=2), :].astype(BF16)
            wu_s[:, lanes] = wt_s[j, pl.ds(1, n // 2, stride=2), :].astype(BF16)
        wd_s[...] = wd_ref[...].astype(BF16)

    @pl.when(used)
    def _():
        x = _unpack_pairs(x_ref[...]).astype(BF16)
        y = bd_ref[...]
        for c in range(wd_s.shape[0] // EXPERT_FF_CHUNK):
            ff = slice(c * EXPERT_FF_CHUNK, (c + 1) * EXPERT_FF_CHUNK)
            g = lax.dot_general(x, wg_s[ff, :], _NT, preferred_element_type=F32) + bg_ref[:, ff]
            u = lax.dot_general(x, wu_s[ff, :], _NT, preferred_element_type=F32) + bu_ref[:, ff]
            gate = jnp.minimum(g, SWIGLU_LIMIT)
            up = jnp.clip(u, -SWIGLU_LIMIT, SWIGLU_LIMIT)
            glu = gate * jax.nn.sigmoid(SWIGLU_ALPHA * gate)
            a = ((up + 1.0) * glu).astype(BF16)
            y = y + jnp.dot(a, wd_s[ff, :], preferred_element_type=F32)
        y_ref[...] = _pack_pairs(y.astype(BF16).astype(F32))

    @pl.when(i >= nu_ref[0])
    def _():
        y_ref[...] = jnp.zeros(y_ref.shape, y_ref.dtype)


def _experts(x_sorted, block_e, n_used, w_gu, w_dn, b_g, b_u, b_d):
    rows = x_sorted.shape[0]
    n_blocks = rows // MOE_BLOCK
    d_gu = w_gu.shape[2]
    d_ff = w_dn.shape[1]
    by_expert = lambda i, be, nu: (be[i], 0, 0)
    xspec = pl.BlockSpec((MOE_BLOCK, PACKED), lambda i, be, nu: (i, 0))
    grid_spec = pltpu.PrefetchScalarGridSpec(
        num_scalar_prefetch=2, grid=(n_blocks,),
        in_specs=[xspec,
                  pl.BlockSpec((None, D_MODEL, d_gu), by_expert),
                  pl.BlockSpec((None, d_ff, D_MODEL), by_expert),
                  pl.BlockSpec((None, 1, d_ff), by_expert),
                  pl.BlockSpec((None, 1, d_ff), by_expert),
                  pl.BlockSpec((None, 1, D_MODEL), by_expert)],
        out_specs=xspec,
        scratch_shapes=[pltpu.VMEM((D_MODEL // LANES, d_gu, LANES), F32),
                        pltpu.VMEM((d_ff, D_MODEL), BF16), pltpu.VMEM((d_ff, D_MODEL), BF16),
                        pltpu.VMEM((d_ff, D_MODEL), BF16)])
    return pl.pallas_call(
        _expert_kernel,
        out_shape=jax.ShapeDtypeStruct((rows, PACKED), jnp.uint32),
        grid_spec=grid_spec,
        compiler_params=pltpu.CompilerParams(dimension_semantics=("arbitrary",),
                                             vmem_limit_bytes=EXPERT_VMEM_LIMIT_BYTES),
        name="experts",
    )(block_e, n_used, x_sorted, w_gu, w_dn, b_g, b_u, b_d)


def _combine_kernel(dest_ref, xm_ref, gt_ref, g2p_ref, g2s_ref, yb_ref, yp_ref, ys_ref, rows_ref, sem,
                    *, n_prompt_tiles):
    i = pl.program_id(0)
    tm = xm_ref.shape[0]

    def body(t, carry):
        for k in range(TOP_K):
            d = dest_ref[(i * tm + t) * TOP_K + k]
            pltpu.make_async_copy(yb_ref.at[pl.ds(d, 1)], rows_ref.at[k, pl.ds(t, 1)], sem).start()
        return carry

    lax.fori_loop(0, tm, body, 0, unroll=8)
    pltpu.make_async_copy(rows_ref, rows_ref, sem).wait()
    gt = gt_ref[...]
    acc = jnp.zeros(xm_ref.shape, F32)
    for k in range(TOP_K):
        acc = acc + gt[:, k:k + 1] * _unpack_pairs(rows_ref[k])
    is_prompt = i < n_prompt_tiles
    y = xm_ref[...] + jnp.where(is_prompt, g2p_ref[0], g2s_ref[0]) * acc

    @pl.when(is_prompt)
    def _():
        yp_ref[...] = y

    @pl.when(jnp.logical_not(is_prompt))
    def _():
        ys_ref[...] = y


def _combine(tiles, dest_flat, xm, gates, gate2_p, gate2_s, y_buf, n_prompt, rows_per_seq):
    tm = tiles.tm
    n_prompt_tiles = n_prompt // tm
    n_sample = tiles.n_tokens - n_prompt
    assert n_sample == tm and gate2_s.shape[1] == tm, "the sample group is one token tile"
    tiles_per_seq = rows_per_seq // tm
    last_seq = gate2_p.shape[0] - 1
    grid_spec = pltpu.PrefetchScalarGridSpec(
        num_scalar_prefetch=1, grid=(tiles.n_tiles,),
        in_specs=[pl.BlockSpec((tm, D_MODEL), lambda i, d: (i, 0)),
                  pl.BlockSpec((tm, LANES), lambda i, d: (i, 0)),
                  pl.BlockSpec((1, 1, D_MODEL), lambda i, d: (jnp.minimum(i // tiles_per_seq, last_seq), 0, 0)),
                  pl.BlockSpec((1, tm, D_MODEL), lambda i, d: (0, 0, 0)),
                  pl.BlockSpec(memory_space=pl.ANY)],
        out_specs=(pl.BlockSpec((tm, D_MODEL), lambda i, d: (jnp.minimum(i, n_prompt_tiles - 1), 0)),
                   pl.BlockSpec((tm, D_MODEL), lambda i, d: (0, 0))),
        scratch_shapes=[pltpu.VMEM((TOP_K, tm, PACKED), jnp.uint32),
                        pltpu.SemaphoreType.DMA(())])
    return pl.pallas_call(
        functools.partial(_combine_kernel, n_prompt_tiles=n_prompt_tiles),
        out_shape=(jax.ShapeDtypeStruct((n_prompt, D_MODEL), F32),
                   jax.ShapeDtypeStruct((n_sample, D_MODEL), F32)),
        grid_spec=grid_spec,
        compiler_params=_params("arbitrary"),
        name="moe_combine",
    )(dest_flat, xm, gates, gate2_p, gate2_s, y_buf)


class _Tiles:
    def __init__(self, n_tokens, tm):
        self.n_tokens, self.tm, self.n_tiles = n_tokens, tm, n_tokens // tm


def _moe(xm, h2u, idx, gates, gate2_p, gate2_s, moe_w, n_prompt, rows_per_seq):
    rows = _Tiles(xm.shape[0], MOE_TOKEN_TILE)
    a = rows.n_tokens * TOP_K
    n_blocks = -(-a // MOE_BLOCK) + N_EXPERTS
    dest, counts = _route(rows, idx)
    dest_flat = dest[:, :TOP_K].reshape(a)
    cnt = counts[0, :N_EXPERTS].astype(jnp.int32)
    pad_end = jnp.cumsum((cnt + MOE_BLOCK - 1) // MOE_BLOCK * MOE_BLOCK)
    block_row = jnp.arange(n_blocks, dtype=jnp.int32) * MOE_BLOCK
    block_e = jnp.minimum(jnp.sum((pad_end[None, :] <= block_row[:, None]).astype(jnp.int32), axis=1),
                          N_EXPERTS - 1)
    n_used = (pad_end[-1:] // MOE_BLOCK).astype(jnp.int32)
    x_sorted = _dispatch(rows, dest_flat, h2u, n_blocks * MOE_BLOCK)
    y_buf = _experts(x_sorted, block_e, n_used, *moe_w)
    return _combine(rows, dest_flat, xm, gates, gate2_p, gate2_s, y_buf, n_prompt, rows_per_seq)


def _group(rows, x, ada, state_rows, qk_and_attend, shared, total_tokens, first_row, token_bufs):
    (g_mix, w_in_b, w_conv, w_o_b, g_ffn, wr_hi, wr_lo, b_r, mid_dtype) = shared
    shift1, scale1, gate1, shift2, scale2, gate2 = [rows.mod_array(m) for m in jnp.split(ada, 6, axis=-1)]
    v_f, v_b = _v_proj(rows, x, g_mix, scale1, shift1, w_in_b)
    sga, cvp, tail = _conv_proj(rows, x, g_mix, scale1, shift1, w_in_b, w_conv, state_rows, mid_dtype)
    mix, k_out = qk_and_attend(x, scale1, shift1, v_f, v_b, sga, cvp)
    token_bufs = _out_proj(rows, mix, x, w_o_b, gate1, g_ffn, scale2, shift2, wr_hi, wr_lo, b_r,
                           total_tokens, first_row, token_bufs)
    return token_bufs, gate2, k_out, v_f, tail


def kernel(x_prompt, x_sample, c_prompt, c_sample, cache_k, cache_v, state_conv, page_table, w_ada, b_ada, g_norm_mix, w_in, g_q, g_k, lambda_q1, lambda_k1, lambda_q2, lambda_k2, g_subln, w_conv, w_o, g_norm_ffn, w_router, b_router, w_gate_up, b_gate_up, w_down, b_down):
    assert w_in.shape[0] == 1, "single-layer stack"
    batch, seq, _ = x_prompt.shape
    n_seq, n_new, _ = x_sample.shape
    tp, ts = batch * seq, n_seq * n_new
    n_chunks = D_MODEL // HEAD_DIM

    ada = _ada(jnp.concatenate([c_prompt, c_sample], axis=0), w_ada[0], b_ada[0][None])

    w_in_b = w_in[0].astype(BF16)
    w_kt = w_in[0][:, D_MODEL:2 * D_MODEL].T.astype(BF16)
    w_o_b = w_o[0].astype(BF16)
    gqk = jnp.stack([jnp.tile(g_q[0], n_chunks), jnp.tile(g_k[0], n_chunks)])[:, None, :]
    gk_col = jnp.tile(g_k[0], n_chunks)[:, None]
    blk = jnp.arange(D_MODEL, dtype=jnp.int32) // HEAD_DIM
    pmat = (blk[:, None] == blk[None, :]).astype(BF16)
    wr = jnp.pad(w_router[0], ((0, 0), (0, LANES - N_EXPERTS)))
    wr_hi = wr.astype(BF16)
    wr_lo = (wr - wr_hi.astype(F32)).astype(BF16)
    b_r = jnp.pad(b_router[0], (0, LANES - N_EXPERTS), constant_values=NEG_BIG)[None]
    moe_w = (w_gate_up[0], w_down[0], b_gate_up[0][:, None, 0::2], b_gate_up[0][:, None, 1::2],
             b_down[0][:, None, :])
    g_mix = g_norm_mix[0][None]
    g_ffn = g_norm_ffn[0][None]
    g_sub = g_subln[0][None]
    lams = (lambda_q1[0][None], lambda_k1[0][None], lambda_q2[0][None], lambda_k2[0][None])

    def shared(mid_dtype):
        return (g_mix, w_in_b, w_conv[0], w_o_b, g_ffn, wr_hi, wr_lo, b_r, mid_dtype)

    rows_p = _Rows(tp, seq)

    def attend_p(x, scale1, shift1, v_f, v_b, sga, cvp):
        q_b = _q_proj(rows_p, x, g_mix, scale1, shift1, w_in_b, gqk, pmat)
        kt_f, kt_b = _kt_proj(rows_p, x, g_mix, scale1, shift1, w_kt, gk_col, batch, seq)
        mix = _attn_prompt(q_b, kt_b, v_b, sga, cvp, g_sub, lams, batch, seq)
        return mix, kt_f

    bufs, gate2_p, kt_p, v_p, tail_p = _group(rows_p, x_prompt.reshape(tp, D_MODEL), ada[:batch], None, attend_p,
                                              shared(BF16), tp + ts, 0, None)

    rows_s = _Rows(ts, n_new)
    st = state_conv[0]
    zeros = jnp.zeros((n_seq, n_new - 2, D_MODEL), F32)
    s1 = jnp.concatenate([st[:, 1:2], jnp.zeros((n_seq, n_new - 1, D_MODEL), F32)], axis=1).reshape(ts, D_MODEL)
    s2 = jnp.concatenate([st, zeros], axis=1).reshape(ts, D_MODEL)

    def attend_s(x, scale1, shift1, v_f, v_b, sga, cvp):
        _, qk_f = _qk_proj(rows_s, x, g_mix, scale1, shift1, w_in_b, gqk, pmat)
        mix = _attn_sample(qk_f, v_f, cache_k[0], cache_v[0], page_table, sga, cvp, g_sub, lams, n_seq, n_new)
        return mix, qk_f[1]

    bufs, gate2_s, k_s, v_s, tail_s = _group(rows_s, x_sample.reshape(ts, D_MODEL), ada[batch:], (s1, s2), attend_s,
                                             shared(F32), tp + ts, tp, bufs)

    y_p, y_s = _moe(*bufs, gate2_p, gate2_s, moe_w, tp, seq)

    tail_s = tail_s.reshape(n_seq, n_new, D_MODEL)
    k_p = kt_p.reshape(1, batch, N_HEADS, 2, HEAD_DIM, seq).transpose(0, 1, 5, 2, 3, 4)
    return (y_p.reshape(batch, seq, D_MODEL),
            y_s.reshape(n_seq, n_new, D_MODEL),
            k_p,
            v_p.reshape(1, batch, seq, N_HEADS, V_DIM),
            tail_p[:, SUBLANES - (CONV_WIDTH - 1):][None],
            k_s.reshape(1, n_seq, n_new, N_HEADS, 2, HEAD_DIM),
            v_s.reshape(1, n_seq, n_new, N_HEADS, V_DIM),
            tail_s[:, n_new - (CONV_WIDTH - 1):][None])
```

```python
import functools
import math

import jax
import jax.numpy as jnp
from jax import lax
from jax.experimental import pallas as pl
from jax.experimental.pallas import tpu as pltpu

F32 = jnp.float32
BF16 = jnp.bfloat16

D_MODEL = 1024
HEAD_DIM = 64
V_DIM = 2 * HEAD_DIM
N_HEADS = D_MODEL // V_DIM
ATTN_SCALE = HEAD_DIM ** -0.5
LOG2_E = math.log2(math.e)
CONV_WIDTH = 3
PAGE_SIZE = 128
N_EXPERTS = 32
TOP_K = 4
SWIGLU_LIMIT = 7.0
SWIGLU_ALPHA = 1.702
NORM_EPS = 1e-6
LAMBDA_INIT = 0.8 - 0.6 * math.exp(-0.3 * 0)

VMEM_LIMIT_BYTES = 48 * 1024 * 1024
EXPERT_VMEM_LIMIT_BYTES = 56 * 1024 * 1024
LANES = 128
PACKED = D_MODEL // 2
SUBLANES = 8

ROW_TILE_PROMPT = 512
ATTN_BLOCK = 512
MOE_BLOCK = 512
MOE_TOKEN_TILE = 256
EXPERT_XPOSE_CHUNK = 512
EXPERT_FF_CHUNK = 256
ADA_COL_TILE = 1536
SAMPLE_PAGES_PER_STEP = 8
NEG_BIG = -1e30

_NT = (((1,), (1,)), ((), ()))


def _params(*sem):
    return pltpu.CompilerParams(dimension_semantics=sem, vmem_limit_bytes=VMEM_LIMIT_BYTES)


def _rms_mod(x, g, scale, shift):
    ms = jnp.mean(x * x, axis=-1, keepdims=True)
    return (x * lax.rsqrt(ms + NORM_EPS) * g) * (1.0 + scale) + shift


def _ada_kernel(c_ref, w_ref, b_ref, o_ref):
    c = c_ref[...]
    s = (c * jax.nn.sigmoid(c)).astype(BF16)
    o_ref[...] = jnp.dot(s, w_ref[...].astype(BF16), preferred_element_type=F32) + b_ref[...]


def _ada(c_all, w_ada, b_ada):
    n = c_all.shape[0]
    width = w_ada.shape[1]
    return pl.pallas_call(
        _ada_kernel,
        out_shape=jax.ShapeDtypeStruct((n, width), F32),
        grid=(width // ADA_COL_TILE,),
        in_specs=[pl.BlockSpec((n, D_MODEL), lambda j: (0, 0)),
                  pl.BlockSpec((D_MODEL, ADA_COL_TILE), lambda j: (0, j)),
                  pl.BlockSpec((1, ADA_COL_TILE), lambda j: (0, j))],
        out_specs=pl.BlockSpec((n, ADA_COL_TILE), lambda j: (0, j)),
        compiler_params=_params("arbitrary"),
        name="ada",
    )(c_all, w_ada, b_ada)


class _Rows:
    def __init__(self, n_tokens, rows_per_batch):
        if rows_per_batch >= ROW_TILE_PROMPT:
            self.tm = ROW_TILE_PROMPT
            self.tiles_per_batch = rows_per_batch // self.tm
            self.mod_rows = 1
        else:
            self.tm = n_tokens
            self.tiles_per_batch = None
            self.mod_rows = n_tokens
        self.n_tokens = n_tokens
        self.rows_per_batch = rows_per_batch
        self.n_tiles = n_tokens // self.tm

    def mod_array(self, m):
        if self.tiles_per_batch is not None:
            return m[:, None, :]
        return jnp.repeat(m, self.rows_per_batch, axis=0)[None]

    def mod_spec(self, clamp=False):
        if self.tiles_per_batch is not None:
            tpb, last = self.tiles_per_batch, self.n_tiles - 1
            if clamp:
                return pl.BlockSpec((1, 1, D_MODEL), lambda i, *_: (jnp.minimum(i, last) // tpb, 0, 0))
            return pl.BlockSpec((1, 1, D_MODEL), lambda i, *_: (i // tpb, 0, 0))
        return pl.BlockSpec((1, self.mod_rows, D_MODEL), lambda i, *_: (0, 0, 0))

    def row_spec(self, width=D_MODEL, clamp=False):
        if clamp:
            last = self.n_tiles - 1
            return pl.BlockSpec((self.tm, width), lambda i, *_: (jnp.minimum(i, last), 0))
        return pl.BlockSpec((self.tm, width), lambda i, *_: (i, 0))


def _qk_kernel(x_ref, g_ref, sc_ref, sh_ref, w_ref, gqk_ref, p_ref, qkb_ref, qkf_ref, h_ref):
    j = pl.program_id(1)

    @pl.when(j == 0)
    def _():
        h_ref[...] = _rms_mod(x_ref[...], g_ref[...], sc_ref[0], sh_ref[0]).astype(BF16)

    z = jnp.dot(h_ref[...], w_ref[...], preferred_element_type=F32)
    ss = jnp.dot((z * z).astype(BF16), p_ref[...], preferred_element_type=F32)
    zn = z * lax.rsqrt(ss * (1.0 / HEAD_DIM) + NORM_EPS) * gqk_ref[0]
    qkf_ref[0] = zn
    scale = jnp.where(j == 0, ATTN_SCALE, 1.0)
    qkb_ref[...] = (zn * scale).astype(BF16)


def _qk_proj(rows, x, g, scale, shift, w_in_b, gqk, pmat):
    t = rows.n_tokens
    return pl.pallas_call(
        _qk_kernel,
        out_shape=(jax.ShapeDtypeStruct((t, 2 * D_MODEL), BF16),
                   jax.ShapeDtypeStruct((2, t, D_MODEL), F32)),
        grid=(rows.n_tiles, 2),
        in_specs=[rows.row_spec(),
                  pl.BlockSpec((1, D_MODEL), lambda i, j: (0, 0)),
                  rows.mod_spec(), rows.mod_spec(),
                  pl.BlockSpec((D_MODEL, D_MODEL), lambda i, j: (0, j)),
                  pl.BlockSpec((1, 1, D_MODEL), lambda i, j: (j, 0, 0)),
                  pl.BlockSpec((D_MODEL, D_MODEL), lambda i, j: (0, 0))],
        out_specs=(pl.BlockSpec((rows.tm, D_MODEL), lambda i, j: (i, j)),
                   pl.BlockSpec((1, rows.tm, D_MODEL), lambda i, j: (j, i, 0))),
        scratch_shapes=[pltpu.VMEM((rows.tm, D_MODEL), BF16)],
        compiler_params=_params("parallel", "arbitrary"),
        name="qk_proj",
    )(x, g, scale, shift, w_in_b, gqk, pmat)


def _q_kernel(x_ref, g_ref, sc_ref, sh_ref, w_ref, gq_ref, p_ref, qb_ref):
    h = _rms_mod(x_ref[...], g_ref[...], sc_ref[0], sh_ref[0]).astype(BF16)
    z = jnp.dot(h, w_ref[...], preferred_element_type=F32)
    ss = jnp.dot((z * z).astype(BF16), p_ref[...], preferred_element_type=F32)
    zn = z * lax.rsqrt(ss * (1.0 / HEAD_DIM) + NORM_EPS) * gq_ref[0]
    qb_ref[...] = (zn * (ATTN_SCALE * LOG2_E)).astype(BF16)


def _q_proj(rows, x, g, scale, shift, w_in_b, gqk, pmat):
    return pl.pallas_call(
        _q_kernel,
        out_shape=jax.ShapeDtypeStruct((rows.n_tokens, D_MODEL), BF16),
        grid=(rows.n_tiles,),
        in_specs=[rows.row_spec(),
                  pl.BlockSpec((1, D_MODEL), lambda i: (0, 0)),
                  rows.mod_spec(), rows.mod_spec(),
                  pl.BlockSpec((D_MODEL, D_MODEL), lambda i: (0, 0)),
                  pl.BlockSpec((1, 1, D_MODEL), lambda i: (0, 0, 0)),
                  pl.BlockSpec((D_MODEL, D_MODEL), lambda i: (0, 0))],
        out_specs=rows.row_spec(),
        compiler_params=_params("parallel"),
        name="q_proj",
    )(x, g, scale, shift, w_in_b, gqk, pmat)


def _kt_kernel(x_ref, g_ref, sc_ref, sh_ref, wt_ref, gk_ref, kf_ref, kb_ref):
    h = _rms_mod(x_ref[...], g_ref[...], sc_ref[0], sh_ref[0]).astype(BF16)
    zt = lax.dot_general(wt_ref[...], h, _NT, preferred_element_type=F32)
    tm = zt.shape[1]
    z3 = zt.reshape(D_MODEL // HEAD_DIM, HEAD_DIM, tm)
    ss = jnp.sum(z3 * z3, axis=1, keepdims=True)
    g3 = gk_ref[...].reshape(D_MODEL // HEAD_DIM, HEAD_DIM, 1)
    zn = (z3 * lax.rsqrt(ss * (1.0 / HEAD_DIM) + NORM_EPS) * g3).reshape(D_MODEL, tm)
    kf_ref[...] = zn
    kb_ref[...] = zn.astype(BF16)


def _kt_proj(rows, x, g, scale, shift, w_kt, gk_col, batch, seq):
    tpb = rows.tiles_per_batch
    out_spec = pl.BlockSpec((None, D_MODEL, rows.tm), lambda i: (i // tpb, 0, i % tpb))
    return pl.pallas_call(
        _kt_kernel,
        out_shape=(jax.ShapeDtypeStruct((batch, D_MODEL, seq), F32),
                   jax.ShapeDtypeStruct((batch, D_MODEL, seq), BF16)),
        grid=(rows.n_tiles,),
        in_specs=[rows.row_spec(),
                  pl.BlockSpec((1, D_MODEL), lambda i: (0, 0)),
                  rows.mod_spec(), rows.mod_spec(),
                  pl.BlockSpec((D_MODEL, D_MODEL), lambda i: (0, 0)),
                  pl.BlockSpec((D_MODEL, 1), lambda i: (0, 0))],
        out_specs=(out_spec, out_spec),
        compiler_params=_params("parallel"),
        name="kt_proj",
    )(x, g, scale, shift, w_kt, gk_col)


def _v_kernel(x_ref, g_ref, sc_ref, sh_ref, w_ref, vf_ref, vb_ref):
    h = _rms_mod(x_ref[...], g_ref[...], sc_ref[0], sh_ref[0]).astype(BF16)
    z = jnp.dot(h, w_ref[...], preferred_element_type=F32)
    vf_ref[...] = z
    vb_ref[...] = z.astype(BF16)


def _v_proj(rows, x, g, scale, shift, w_in_b):
    t = rows.n_tokens
    return pl.pallas_call(
        _v_kernel,
        out_shape=(jax.ShapeDtypeStruct((t, D_MODEL), F32),
                   jax.ShapeDtypeStruct((t, D_MODEL), BF16)),
        grid=(rows.n_tiles,),
        in_specs=[rows.row_spec(),
                  pl.BlockSpec((1, D_MODEL), lambda i: (0, 0)),
                  rows.mod_spec(), rows.mod_spec(),
                  pl.BlockSpec((D_MODEL, D_MODEL), lambda i: (0, 2))],
        out_specs=(rows.row_spec(), rows.row_spec()),
        compiler_params=_params("parallel"),
        name="v_proj",
    )(x, g, scale, shift, w_in_b)


def _conv_kernel(*refs, tiles_per_batch, rows_per_batch, tail_rows):
    if tiles_per_batch is None:
        (x_ref, g_ref, sc_ref, sh_ref, w_ref, wc_ref, s1_ref, s2_ref,
         sga_ref, cvp_ref, tail_ref, h_ref, a_ref, b_ref, carry_ref) = refs
    else:
        (x_ref, g_ref, sc_ref, sh_ref, w_ref, wc_ref,
         sga_ref, cvp_ref, tail_ref, h_ref, a_ref, b_ref, carry_ref) = refs
    i = pl.program_id(0)
    j = pl.program_id(1)

    @pl.when(j == 0)
    def _():
        h_ref[...] = _rms_mod(x_ref[...], g_ref[...], sc_ref[0], sh_ref[0]).astype(BF16)

    z = jnp.dot(h_ref[...], w_ref[...], preferred_element_type=F32)

    @pl.when(j == 0)
    def _():
        a_ref[...] = z

    @pl.when(j == 1)
    def _():
        b_ref[...] = z

    @pl.when(j == 2)
    def _():
        u = b_ref[...] * z
        tm = u.shape[0]
        row = lax.broadcasted_iota(jnp.int32, (tm, 1), 0)
        r1 = pltpu.roll(u, 1, 0)
        r2 = pltpu.roll(u, 2, 0)
        if tiles_per_batch is None:
            t = row & (rows_per_batch - 1)
            u1 = jnp.where(t >= 1, r1, s1_ref[...])
            u2 = jnp.where(t >= 2, r2, s2_ref[...])
        else:
            first = (i % tiles_per_batch) == 0
            c = jnp.where(first, 0.0, carry_ref[...])
            u1 = jnp.where(row == 0, c[7:8], r1)
            u2 = jnp.where(row == 0, c[6:7], jnp.where(row == 1, c[7:8], r2))
            carry_ref[...] = u[tm - SUBLANES:]
        wc = wc_ref[...]
        yc = wc[0:1] * u2 + wc[1:2] * u1 + wc[2:3] * u
        a_ref[...] = a_ref[...] * yc
        tail_ref[0] = u[tm - tail_rows:]

    @pl.when(j == 3)
    def _():
        sga_ref[...] = jax.nn.sigmoid(z).astype(sga_ref.dtype)

    @pl.when(j == 4)
    def _():
        cvp_ref[...] = (jax.nn.sigmoid(z) * a_ref[...]).astype(cvp_ref.dtype)


def _conv_proj(rows, x, g, scale, shift, w_in_b, w_conv, state_rows, out_dtype):
    t = rows.n_tokens
    sample_mode = rows.tiles_per_batch is None
    tail_rows = rows.tm if sample_mode else SUBLANES
    n_tail_blocks = 1 if sample_mode else t // rows.rows_per_batch
    in_specs = [rows.row_spec(),
                pl.BlockSpec((1, D_MODEL), lambda i, j: (0, 0)),
                rows.mod_spec(), rows.mod_spec(),
                pl.BlockSpec((D_MODEL, D_MODEL), lambda i, j: (0, 3 + j)),
                pl.BlockSpec((CONV_WIDTH, D_MODEL), lambda i, j: (0, 0))]
    args = [x, g, scale, shift, w_in_b, w_conv]
    if sample_mode:
        in_specs += [rows.row_spec(), rows.row_spec()]
        args += list(state_rows)
        tail_spec = pl.BlockSpec((1, tail_rows, D_MODEL), lambda i, j: (0, 0, 0))
    else:
        tpb = rows.tiles_per_batch
        tail_spec = pl.BlockSpec((1, tail_rows, D_MODEL), lambda i, j: (i // tpb, 0, 0))
    kern = functools.partial(_conv_kernel, tiles_per_batch=rows.tiles_per_batch,
                             rows_per_batch=rows.rows_per_batch, tail_rows=tail_rows)
    return pl.pallas_call(
        kern,
        out_shape=(jax.ShapeDtypeStruct((t, D_MODEL), out_dtype),
                   jax.ShapeDtypeStruct((t, D_MODEL), out_dtype),
                   jax.ShapeDtypeStruct((n_tail_blocks, tail_rows, D_MODEL), F32)),
        grid=(rows.n_tiles, 5),
        in_specs=in_specs,
        out_specs=(rows.row_spec(), rows.row_spec(), tail_spec),
        scratch_shapes=[pltpu.VMEM((rows.tm, D_MODEL), BF16),
                        pltpu.VMEM((rows.tm, D_MODEL), F32),
                        pltpu.VMEM((rows.tm, D_MODEL), F32),
                        pltpu.VMEM((SUBLANES, D_MODEL), F32)],
        compiler_params=_params("arbitrary", "arbitrary"),
        name="conv_proj",
    )(*args)


def _lambda_value(lq1, lk1, lq2, lk2):
    e1 = jnp.exp(jnp.sum(lq1 * lk1, axis=-1, keepdims=True))
    e2 = jnp.exp(jnp.sum(lq2 * lk2, axis=-1, keepdims=True))
    return e1 - e2 + LAMBDA_INIT


def _subln_mix(o, g_sub, sga, cvp):
    ms = jnp.mean(o * o, axis=-1, keepdims=True)
    attn = (o * lax.rsqrt(ms + NORM_EPS) * g_sub) * (1.0 - LAMBDA_INIT)
    return sga * attn + cvp


def _flash_update(s, v_ones, m_ref, l_ref, acc_ref):
    m_prev = m_ref[...]
    m_next = jnp.maximum(m_prev, jnp.max(s, axis=1, keepdims=True))
    alpha = jnp.exp2(m_prev - m_next)
    p = jnp.exp2(s - jnp.concatenate([m_next] * (s.shape[1] // LANES), axis=1)).astype(BF16)
    pv = jnp.dot(p, v_ones, preferred_element_type=F32)
    acc_ref[...] = alpha * acc_ref[...] + pv[:, :V_DIM]
    l_ref[...] = alpha * l_ref[...] + pv[:, V_DIM:]
    m_ref[...] = m_next


def _attn_prompt_kernel(qi_ref, ki_ref, q_ref, kt_ref, v_ref, sga_ref, cvp_ref, gs_ref,
                        lq1_ref, lk1_ref, lq2_ref, lk2_ref,
                        o_ref, m0_ref, l0_ref, a0_ref, m1_ref, l1_ref, a1_ref):
    qi = qi_ref[pl.program_id(1)]
    ki = ki_ref[pl.program_id(1)]
    states = ((m0_ref, l0_ref, a0_ref), (m1_ref, l1_ref, a1_ref))

    @pl.when(ki == 0)
    def _():
        for m_ref, l_ref, a_ref in states:
            m_ref[...] = jnp.full(m_ref.shape, -jnp.inf, F32)
            l_ref[...] = jnp.zeros(l_ref.shape, F32)
            a_ref[...] = jnp.zeros(a_ref.shape, F32)

    tq, tk = q_ref.shape[0], kt_ref.shape[1]

    def step(masked):
        lane = lax.broadcasted_iota(jnp.int32, (1, V_DIM), 1)
        zero = jnp.zeros((), BF16)
        ones = jnp.ones((tk, V_DIM), BF16)
        if masked:
            r = lax.broadcasted_iota(jnp.int32, (tq, tk), 0)
            c = lax.broadcasted_iota(jnp.int32, (tq, tk), 1)
            keep = c <= r
        for h in range(N_HEADS):
            cols = slice(h * V_DIM, (h + 1) * V_DIM)
            q = q_ref[:, cols]
            kt = kt_ref[cols, :]
            v_ones = jnp.concatenate([v_ref[:, cols], ones], axis=1)
            qs = (jnp.where(lane < HEAD_DIM, q, zero), jnp.where(lane >= HEAD_DIM, q, zero))
            for qm, (m_ref, l_ref, a_ref) in zip(qs, states):
                s = jnp.dot(qm, kt, preferred_element_type=F32)
                if masked:
                    s = jnp.where(keep, s, -jnp.inf)
                _flash_update(s, v_ones, m_ref.at[h], l_ref.at[h], a_ref.at[h])

    @pl.when(ki < qi)
    def _():
        step(False)

    @pl.when(ki == qi)
    def _():
        step(True)
        lam = _lambda_value(lq1_ref[...], lk1_ref[...], lq2_ref[...], lk2_ref[...])
        for h in range(N_HEADS):
            cols = slice(h * V_DIM, (h + 1) * V_DIM)
            o = a0_ref[h] / l0_ref[h] - lam * (a1_ref[h] / l1_ref[h])
            mix = _subln_mix(o, gs_ref[...], sga_ref[:, cols].astype(F32), cvp_ref[:, cols].astype(F32))
            o_ref[:, cols] = mix.astype(o_ref.dtype)


def _attn_prompt(q_b, kt_b, v_b, sga, cvp, g_sub, lams, batch, seq):
    nb = seq // ATTN_BLOCK
    tq = ATTN_BLOCK
    pairs = [(qi, ki) for qi in range(nb) for ki in range(qi + 1)]
    qi_tab = jnp.array([p[0] for p in pairs], jnp.int32)
    ki_tab = jnp.array([p[1] for p in pairs], jnp.int32)
    q_spec = pl.BlockSpec((tq, D_MODEL), lambda b, p, qt, kt: (b * nb + qt[p], 0))
    k_spec = pl.BlockSpec((None, D_MODEL, tq), lambda b, p, qt, kt: (b, 0, kt[p]))
    v_spec = pl.BlockSpec((tq, D_MODEL), lambda b, p, qt, kt: (b * nb + kt[p], 0))
    vec64 = pl.BlockSpec((1, HEAD_DIM), lambda b, p, qt, kt: (0, 0))
    grid_spec = pltpu.PrefetchScalarGridSpec(
        num_scalar_prefetch=2,
        grid=(batch, len(pairs)),
        in_specs=[q_spec, k_spec, v_spec, q_spec, q_spec,
                  pl.BlockSpec((1, V_DIM), lambda b, p, qt, kt: (0, 0)),
                  vec64, vec64, vec64, vec64],
        out_specs=q_spec,
        scratch_shapes=[pltpu.VMEM((N_HEADS, tq, V_DIM), F32)] * 6)
    return pl.pallas_call(
        _attn_prompt_kernel,
        out_shape=jax.ShapeDtypeStruct((batch * seq, D_MODEL), BF16),
        grid_spec=grid_spec,
        compiler_params=_params("parallel", "arbitrary"),
        name="attn_prompt",
    )(qi_tab, ki_tab, q_b, kt_b, v_b, sga, cvp, g_sub, *lams)


def _attn_sample_kernel(pt_ref, q_ref, kn_ref, vn_ref, *rest, n_new, n_group):
    kc_refs = rest[:n_group]
    vc_refs = rest[n_group:2 * n_group]
    (spread_ref, own_ref, sga_ref, cvp_ref, gs_ref, lq1_ref, lk1_ref, lq2_ref, lk2_ref,
     o_ref, qbd_ref, m_ref, l_ref, acc_ref) = rest[2 * n_group:]
    p = pl.program_id(1)
    rows_per_head = 2 * n_new

    @pl.when(p == 0)
    def _():
        q = q_ref[...] * ATTN_SCALE
        qrep = jnp.concatenate([q] * (N_HEADS * 2), axis=0)
        r = lax.broadcasted_iota(jnp.int32, qrep.shape, 0)
        c = lax.broadcasted_iota(jnp.int32, qrep.shape, 1)
        qbd_ref[...] = jnp.where(c // HEAD_DIM == r // n_new, qrep, 0.0).astype(BF16)
        m_ref[...] = jnp.full(m_ref.shape, -jnp.inf, F32)
        l_ref[...] = jnp.zeros(l_ref.shape, F32)
        acc_ref[...] = jnp.zeros(acc_ref.shape, F32)

    def update(s, pv_of):
        m_prev = m_ref[...]
        m_next = jnp.maximum(m_prev, jnp.max(s, axis=1, keepdims=True))
        alpha = jnp.exp(m_prev - m_next)
        width = s.shape[1]
        m_wide = m_next[:, :width] if width <= LANES else jnp.concatenate([m_next] * (width // LANES), axis=1)
        pr = jnp.exp(s - m_wide)
        l_ref[...] = alpha * l_ref[...] + jnp.sum(pr, axis=1, keepdims=True)
        acc_ref[...] = alpha * acc_ref[...] + pv_of(pr.astype(BF16))
        m_ref[...] = m_next

    def page_pv(pb, v_ref):
        pe = jnp.dot(pb, spread_ref[...], preferred_element_type=F32).astype(BF16) * own_ref[...]
        v2 = v_ref[...].reshape(PAGE_SIZE * N_HEADS, V_DIM).astype(BF16)
        return jnp.dot(pe, v2, preferred_element_type=F32)

    qbd = qbd_ref[...]
    s = jnp.concatenate([jnp.dot(qbd, kc_refs[g][...].astype(BF16), preferred_element_type=F32)
                         for g in range(n_group)], axis=1)

    def pages_pv(pb):
        out = page_pv(pb[:, :PAGE_SIZE], vc_refs[0])
        for g in range(1, n_group):
            out = out + page_pv(pb[:, g * PAGE_SIZE:(g + 1) * PAGE_SIZE], vc_refs[g])
        return out

    update(s, pages_pv)

    @pl.when(p == pl.num_programs(1) - 1)
    def _():
        s_new = lax.dot_general(qbd_ref[...], kn_ref[...].astype(BF16), _NT, preferred_element_type=F32)
        r = lax.broadcasted_iota(jnp.int32, s_new.shape, 0)
        c = lax.broadcasted_iota(jnp.int32, s_new.shape, 1)
        s_new = jnp.where(c <= (r & (n_new - 1)), s_new, -jnp.inf)
        def new_pv(pb):
            return jnp.concatenate(
                [jnp.dot(pb[h * rows_per_head:(h + 1) * rows_per_head].astype(F32),
                         vn_ref[:, h * V_DIM:(h + 1) * V_DIM].astype(BF16).astype(F32),
                         preferred_element_type=F32) for h in range(N_HEADS)], axis=0)

        update(s_new, new_pv)
        lam = _lambda_value(lq1_ref[...], lk1_ref[...], lq2_ref[...], lk2_ref[...])
        acc = acc_ref[...] / l_ref[...]
        for h in range(N_HEADS):
            cols = slice(h * V_DIM, (h + 1) * V_DIM)
            r0 = h * rows_per_head
            o = acc[r0:r0 + n_new] - lam * acc[r0 + n_new:r0 + rows_per_head]
            o_ref[:, cols] = _subln_mix(o, gs_ref[...], sga_ref[:, cols], cvp_ref[:, cols])


def _attn_sample(qkf, vf, cache_k, cache_v, page_table, sga, cvp, g_sub, lams, n_seq, n_new):
    n_pages = page_table.shape[1]
    n_pool = cache_k.shape[0]
    width = N_HEADS * V_DIM
    grp = SAMPLE_PAGES_PER_STEP
    kc = jnp.transpose(cache_k, (0, 2, 3, 4, 1)).reshape(n_pool, width, PAGE_SIZE)
    pt = page_table.reshape(-1)
    n_rows = N_HEADS * 2 * n_new
    row = pl.BlockSpec((n_new, width), lambda b, p, pt: (b, 0))

    def page_index(g):
        return lambda b, p, pt: (pt[b * n_pages + p * grp + g], 0, 0)

    def page_index4(g):
        return lambda b, p, pt: (pt[b * n_pages + p * grp + g], 0, 0, 0)

    k_pages = [pl.BlockSpec((None, width, PAGE_SIZE), page_index(g)) for g in range(grp)]
    v_pages = [pl.BlockSpec((None, PAGE_SIZE, N_HEADS, V_DIM), page_index4(g)) for g in range(grp)]
    vec64 = pl.BlockSpec((1, HEAD_DIM), lambda b, p, pt: (0, 0))
    col = jnp.arange(PAGE_SIZE * N_HEADS, dtype=jnp.int32)
    spread = (col[None, :] // N_HEADS == jnp.arange(PAGE_SIZE, dtype=jnp.int32)[:, None]).astype(BF16)
    own = (col[None, :] % N_HEADS == jnp.arange(n_rows, dtype=jnp.int32)[:, None] // (2 * n_new)).astype(BF16)
    const = lambda shape: pl.BlockSpec(shape, lambda b, p, pt: (0, 0))
    grid_spec = pltpu.PrefetchScalarGridSpec(
        num_scalar_prefetch=1,
        grid=(n_seq, n_pages // grp),
        in_specs=[pl.BlockSpec((None, n_new, width), lambda b, p, pt: (0, b, 0)),
                  pl.BlockSpec((None, n_new, width), lambda b, p, pt: (1, b, 0)),
                  row, *k_pages, *v_pages, const(spread.shape), const(own.shape), row, row,
                  pl.BlockSpec((1, V_DIM), lambda b, p, pt: (0, 0)),
                  vec64, vec64, vec64, vec64],
        out_specs=row,
        scratch_shapes=[pltpu.VMEM((n_rows, width), BF16),
                        pltpu.VMEM((n_rows, V_DIM), F32), pltpu.VMEM((n_rows, V_DIM), F32),
                        pltpu.VMEM((n_rows, V_DIM), F32)])
    return pl.pallas_call(
        functools.partial(_attn_sample_kernel, n_new=n_new, n_group=grp),
        out_shape=jax.ShapeDtypeStruct((n_seq * n_new, width), F32),
        grid_spec=grid_spec,
        compiler_params=_params("parallel", "arbitrary"),
        name="attn_sample",
    )(pt, qkf, qkf, vf, *([kc] * grp), *([cache_v] * grp), spread, own, sga, cvp, g_sub, *lams)


def _out_kernel(mix_ref, x_ref, wo_ref, g1_ref, g_ref, sc_ref, sh_ref, wrh_ref, wrl_ref, br_ref, *refs, n_real_tiles):
    outs = refs[-4:]
    i = pl.program_id(0)

    @pl.when(i < n_real_tiles)
    def _():
        _out_tile(mix_ref, x_ref, wo_ref, g1_ref, g_ref, sc_ref, sh_ref, wrh_ref, wrl_ref, br_ref, *outs)

    @pl.when(i >= n_real_tiles)
    def _():
        for r in outs:
            r[...] = jnp.zeros(r.shape, r.dtype)


def _out_tile(mix_ref, x_ref, wo_ref, g1_ref, g_ref, sc_ref, sh_ref, wrh_ref, wrl_ref, br_ref,
              xm_ref, h2_ref, idx_ref, gt_ref):
    y = jnp.dot(mix_ref[...].astype(BF16), wo_ref[...], preferred_element_type=F32)
    xm = x_ref[...] + g1_ref[0] * y
    xm_ref[...] = xm
    h2 = _rms_mod(xm, g_ref[...], sc_ref[0], sh_ref[0])
    hi = h2.astype(BF16)
    h2_ref[...] = _pack_pairs(hi.astype(F32))
    lo = (h2 - hi.astype(F32)).astype(BF16)
    logits = (jnp.dot(hi, wrh_ref[...], preferred_element_type=F32)
              + jnp.dot(lo, wrh_ref[...], preferred_element_type=F32)
              + jnp.dot(hi, wrl_ref[...], preferred_element_type=F32)) + br_ref[...]
    lane = lax.broadcasted_iota(jnp.int32, logits.shape, 1).astype(F32)
    vals, idxs = [], []
    for _ in range(TOP_K):
        m = jnp.max(logits, axis=-1, keepdims=True)
        ix = jnp.min(jnp.where(logits == m, lane, float(LANES)), axis=-1, keepdims=True)
        logits = jnp.where(lane == ix, -jnp.inf, logits)
        vals.append(m)
        idxs.append(ix)
    es = [jnp.exp(v - vals[0]) for v in vals]
    denom = es[0] + es[1] + es[2] + es[3]
    idx_out = jnp.zeros(logits.shape, F32)
    gt_out = jnp.zeros(logits.shape, F32)
    for k in range(TOP_K):
        idx_out = jnp.where(lane == float(k), idxs[k], idx_out)
        gt_out = jnp.where(lane == float(k), es[k] / denom, gt_out)
    idx_ref[...] = idx_out.astype(jnp.int32)
    gt_ref[...] = gt_out


def _out_proj(rows, mix, x, w_o_b, gate1, g, scale, shift, wr_hi, wr_lo, b_r, total_tokens, first_row, shared_bufs):
    full = lambda shape: pl.BlockSpec(shape, lambda i: (0,) * len(shape))
    off = first_row // rows.tm
    creates = shared_bufs is None
    out_row = lambda width: pl.BlockSpec((rows.tm, width), lambda i: (i + off, 0))
    in_specs = [rows.row_spec(clamp=creates), rows.row_spec(clamp=creates), full((D_MODEL, D_MODEL)),
                rows.mod_spec(clamp=creates), full((1, D_MODEL)), rows.mod_spec(clamp=creates),
                rows.mod_spec(clamp=creates),
                full((D_MODEL, LANES)), full((D_MODEL, LANES)), full((1, LANES))]
    args = [mix, x, w_o_b, gate1, g, scale, shift, wr_hi, wr_lo, b_r]
    aliases = {}
    if not creates:
        aliases = {len(args) + j: j for j in range(len(shared_bufs))}
        in_specs += [pl.BlockSpec(memory_space=pl.ANY)] * len(shared_bufs)
        args += list(shared_bufs)
    extra = 1 if creates and total_tokens > rows.n_tokens else 0
    assert total_tokens - rows.n_tokens <= rows.tm or not creates, "other group must fit the one extra tile"
    return pl.pallas_call(
        functools.partial(_out_kernel, n_real_tiles=rows.n_tiles),
        out_shape=(jax.ShapeDtypeStruct((total_tokens, D_MODEL), F32),
                   jax.ShapeDtypeStruct((total_tokens, PACKED), jnp.uint32),
                   jax.ShapeDtypeStruct((total_tokens, LANES), jnp.int32),
                   jax.ShapeDtypeStruct((total_tokens, LANES), F32)),
        grid=(rows.n_tiles + extra,),
        in_specs=in_specs,
        out_specs=(out_row(D_MODEL), out_row(PACKED), out_row(LANES), out_row(LANES)),
        input_output_aliases=aliases,
        compiler_params=_params("arbitrary"),
        name="out_proj_router",
    )(*args)


def _pack_pairs(x):
    bits = pltpu.bitcast(x, jnp.uint32)
    n = x.shape[1] // 2
    return (bits[:, :n] >> 16) | (bits[:, n:] & jnp.uint32(0xFFFF0000))


def _unpack_pairs(w):
    lo = pltpu.bitcast(w << 16, F32)
    hi = pltpu.bitcast(w & jnp.uint32(0xFFFF0000), F32)
    return jnp.concatenate([lo, hi], axis=1)


def _route_kernel(idx_ref, dest_ref, cnt_ref, run_ref, start_ref):
    ph = pl.program_id(0)
    i = pl.program_id(1)
    idx = idx_ref[...]
    tm = idx.shape[0]
    lane = lax.broadcasted_iota(jnp.int32, idx.shape, 1)
    onehots = [(lane == idx[:, k:k + 1]).astype(F32) for k in range(TOP_K)]
    member = onehots[0] + onehots[1] + onehots[2] + onehots[3]
    tile_count = jnp.sum(member, axis=0, keepdims=True)

    @pl.when((ph == 0) & (i == 0))
    def _():
        cnt_ref[...] = jnp.zeros(cnt_ref.shape, F32)

    @pl.when(ph == 0)
    def _():
        cnt_ref[...] = cnt_ref[...] + tile_count

    @pl.when((ph == 1) & (i == 0))
    def _():
        cnt = cnt_ref[...]
        padded = jnp.floor((cnt + (MOE_BLOCK - 1)) * (1.0 / MOE_BLOCK)) * MOE_BLOCK
        l1 = lax.broadcasted_iota(jnp.int32, cnt.shape, 1)
        incl = padded
        for s in (1, 2, 4, 8, 16, 32, 64):
            incl = incl + jnp.where(l1 >= s, pltpu.roll(incl, s, 1), 0.0)
        start_ref[...] = incl - padded
        run_ref[...] = jnp.zeros(run_ref.shape, F32)

    @pl.when(ph == 1)
    def _():
        r = lax.broadcasted_iota(jnp.int32, (tm, tm), 0)
        c = lax.broadcasted_iota(jnp.int32, (tm, tm), 1)
        earlier = (c < r).astype(BF16)
        before = jnp.dot(earlier, member.astype(BF16), preferred_element_type=F32)
        base = before + run_ref[0:1] + start_ref[0:1]
        out = jnp.zeros(idx.shape, F32)
        for k in range(TOP_K):
            d = jnp.sum(onehots[k] * base, axis=1, keepdims=True)
            out = jnp.where(lane == k, d, out)
        dest_ref[...] = out.astype(jnp.int32)
        run_ref[...] = run_ref[...] + tile_count


def _route(rows, idx):
    tm = rows.tm
    dest, counts = pl.pallas_call(
        _route_kernel,
        out_shape=(jax.ShapeDtypeStruct((rows.n_tokens, LANES), jnp.int32),
                   jax.ShapeDtypeStruct((SUBLANES, LANES), F32)),
        grid=(2, rows.n_tiles),
        in_specs=[pl.BlockSpec((tm, LANES), lambda ph, i: (i, 0))],
        out_specs=(pl.BlockSpec((tm, LANES), lambda ph, i: (i * ph, 0)),
                   pl.BlockSpec((SUBLANES, LANES), lambda ph, i: (0, 0))),
        scratch_shapes=[pltpu.VMEM((SUBLANES, LANES), F32), pltpu.VMEM((SUBLANES, LANES), F32)],
        compiler_params=_params("arbitrary", "arbitrary"),
        name="moe_route",
    )(idx)
    return dest, counts


def _dispatch_kernel(dest_ref, h_ref, zero_ref, xs_ref, sem):
    del zero_ref
    i = pl.program_id(0)
    tm = h_ref.shape[0]

    def body(t, carry):
        for k in range(TOP_K):
            d = dest_ref[(i * tm + t) * TOP_K + k]
            pltpu.make_async_copy(h_ref.at[pl.ds(t, 1)], xs_ref.at[pl.ds(d, 1)], sem).start()
        return carry

    lax.fori_loop(0, tm, body, 0, unroll=8)
    pltpu.make_async_copy(xs_ref.at[pl.ds(0, tm * TOP_K)], xs_ref.at[pl.ds(0, tm * TOP_K)], sem).wait()


def _dispatch(rows, dest_flat, h2u, n_rows):
    tm = rows.tm
    zeros = jnp.zeros((n_rows, PACKED), jnp.uint32)
    grid_spec = pltpu.PrefetchScalarGridSpec(
        num_scalar_prefetch=1, grid=(rows.n_tiles,),
        in_specs=[pl.BlockSpec((tm, PACKED), lambda i, d: (i, 0)),
                  pl.BlockSpec(memory_space=pl.ANY)],
        out_specs=pl.BlockSpec(memory_space=pl.ANY),
        scratch_shapes=[pltpu.SemaphoreType.DMA(())])
    return pl.pallas_call(
        _dispatch_kernel,
        out_shape=jax.ShapeDtypeStruct((n_rows, PACKED), jnp.uint32),
        grid_spec=grid_spec,
        input_output_aliases={2: 0},
        compiler_params=_params("arbitrary"),
        name="moe_dispatch",
    )(dest_flat, h2u, zeros)


def _expert_kernel(be_ref, nu_ref, x_ref, wgu_ref, wd_ref, bg_ref, bu_ref, bd_ref, y_ref,
                   wt_s, wg_s, wu_s, wd_s):
    i = pl.program_id(0)
    used = i < nu_ref[0]
    new_expert = (i == 0) | (be_ref[i] != be_ref[jnp.maximum(i - 1, 0)])

    @pl.when(used & new_expert)
    def _():
        n = wgu_ref.shape[1]
        n_lane_tiles = wt_s.shape[0]
        for c in range(n // EXPERT_XPOSE_CHUNK):
            cols = slice(c * EXPERT_XPOSE_CHUNK, (c + 1) * EXPERT_XPOSE_CHUNK)
            wt = wgu_ref[:, cols].T
            for j in range(n_lane_tiles):
                wt_s[j, cols, :] = wt[:, j * LANES:(j + 1) * LANES]
        for j in range(n_lane_tiles):
            lanes = slice(j * LANES, (j + 1) * LANES)
            wg_s[:, lanes] = wt_s[j, pl.ds(0, n // 2, stride=2), :].astype(BF16)
            wu_s[:, lanes] = wt_s[j, pl.ds(1, n // 2, stride=2), :].astype(BF16)
        wd_s[...] = wd_ref[...].astype(BF16)

    @pl.when(used)
    def _():
        x = _unpack_pairs(x_ref[...]).astype(BF16)
        y = bd_ref[...]
        for c in range(wd_s.shape[0] // EXPERT_FF_CHUNK):
            ff = slice(c * EXPERT_FF_CHUNK, (c + 1) * EXPERT_FF_CHUNK)
            g = lax.dot_general(x, wg_s[ff, :], _NT, preferred_element_type=F32) + bg_ref[:, ff]
            u = lax.dot_general(x, wu_s[ff, :], _NT, preferred_element_type=F32) + bu_ref[:, ff]
            gate = jnp.minimum(g, SWIGLU_LIMIT)
            up = jnp.clip(u, -SWIGLU_LIMIT, SWIGLU_LIMIT)
            glu = gate * jax.nn.sigmoid(SWIGLU_ALPHA * gate)
            a = ((up + 1.0) * glu).astype(BF16)
            y = y + jnp.dot(a, wd_s[ff, :], preferred_element_type=F32)
        y_ref[...] = _pack_pairs(y.astype(BF16).astype(F32))

    @pl.when(i >= nu_ref[0])
    def _():
        y_ref[...] = jnp.zeros(y_ref.shape, y_ref.dtype)


def _experts(x_sorted, block_e, n_used, w_gu, w_dn, b_g, b_u, b_d):
    rows = x_sorted.shape[0]
    n_blocks = rows // MOE_BLOCK
    d_gu = w_gu.shape[2]
    d_ff = w_dn.shape[1]
    by_expert = lambda i, be, nu: (be[i], 0, 0)
    xspec = pl.BlockSpec((MOE_BLOCK, PACKED), lambda i, be, nu: (i, 0))
    grid_spec = pltpu.PrefetchScalarGridSpec(
        num_scalar_prefetch=2, grid=(n_blocks,),
        in_specs=[xspec,
                  pl.BlockSpec((None, D_MODEL, d_gu), by_expert),
                  pl.BlockSpec((None, d_ff, D_MODEL), by_expert),
                  pl.BlockSpec((None, 1, d_ff), by_expert),
                  pl.BlockSpec((None, 1, d_ff), by_expert),
                  pl.BlockSpec((None, 1, D_MODEL), by_expert)],
        out_specs=xspec,
        scratch_shapes=[pltpu.VMEM((D_MODEL // LANES, d_gu, LANES), F32),
                        pltpu.VMEM((d_ff, D_MODEL), BF16), pltpu.VMEM((d_ff, D_MODEL), BF16),
                        pltpu.VMEM((d_ff, D_MODEL), BF16)])
    return pl.pallas_call(
        _expert_kernel,
        out_shape=jax.ShapeDtypeStruct((rows, PACKED), jnp.uint32),
        grid_spec=grid_spec,
        compiler_params=pltpu.CompilerParams(dimension_semantics=("arbitrary",),
                                             vmem_limit_bytes=EXPERT_VMEM_LIMIT_BYTES),
        name="experts",
    )(block_e, n_used, x_sorted, w_gu, w_dn, b_g, b_u, b_d)


def _combine_kernel(dest_ref, xm_ref, gt_ref, g2p_ref, g2s_ref, yb_ref, yp_ref, ys_ref, rows_ref, sem,
                    *, n_prompt_tiles):
    i = pl.program_id(0)
    tm = xm_ref.shape[0]

    def body(t, carry):
        for k in range(TOP_K):
            d = dest_ref[(i * tm + t) * TOP_K + k]
            pltpu.make_async_copy(yb_ref.at[pl.ds(d, 1)], rows_ref.at[k, pl.ds(t, 1)], sem).start()
        return carry

    lax.fori_loop(0, tm, body, 0, unroll=8)
    pltpu.make_async_copy(rows_ref, rows_ref, sem).wait()
    gt = gt_ref[...]
    acc = jnp.zeros(xm_ref.shape, F32)
    for k in range(TOP_K):
        acc = acc + gt[:, k:k + 1] * _unpack_pairs(rows_ref[k])
    is_prompt = i < n_prompt_tiles
    y = xm_ref[...] + jnp.where(is_prompt, g2p_ref[0], g2s_ref[0]) * acc

    @pl.when(is_prompt)
    def _():
        yp_ref[...] = y

    @pl.when(jnp.logical_not(is_prompt))
    def _():
        ys_ref[...] = y


def _combine(tiles, dest_flat, xm, gates, gate2_p, gate2_s, y_buf, n_prompt, rows_per_seq):
    tm = tiles.tm
    n_prompt_tiles = n_prompt // tm
    n_sample = tiles.n_tokens - n_prompt
    assert n_sample == tm and gate2_s.shape[1] == tm, "the sample group is one token tile"
    tiles_per_seq = rows_per_seq // tm
    last_seq = gate2_p.shape[0] - 1
    grid_spec = pltpu.PrefetchScalarGridSpec(
        num_scalar_prefetch=1, grid=(tiles.n_tiles,),
        in_specs=[pl.BlockSpec((tm, D_MODEL), lambda i, d: (i, 0)),
                  pl.BlockSpec((tm, LANES), lambda i, d: (i, 0)),
                  pl.BlockSpec((1, 1, D_MODEL), lambda i, d: (jnp.minimum(i // tiles_per_seq, last_seq), 0, 0)),
                  pl.BlockSpec((1, tm, D_MODEL), lambda i, d: (0, 0, 0)),
                  pl.BlockSpec(memory_space=pl.ANY)],
        out_specs=(pl.BlockSpec((tm, D_MODEL), lambda i, d: (jnp.minimum(i, n_prompt_tiles - 1), 0)),
                   pl.BlockSpec((tm, D_MODEL), lambda i, d: (0, 0))),
        scratch_shapes=[pltpu.VMEM((TOP_K, tm, PACKED), jnp.uint32),
                        pltpu.SemaphoreType.DMA(())])
    return pl.pallas_call(
        functools.partial(_combine_kernel, n_prompt_tiles=n_prompt_tiles),
        out_shape=(jax.ShapeDtypeStruct((n_prompt, D_MODEL), F32),
                   jax.ShapeDtypeStruct((n_sample, D_MODEL), F32)),
        grid_spec=grid_spec,
        compiler_params=_params("arbitrary"),
        name="moe_combine",
    )(dest_flat, xm, gates, gate2_p, gate2_s, y_buf)


class _Tiles:
    def __init__(self, n_tokens, tm):
        self.n_tokens, self.tm, self.n_tiles = n_tokens, tm, n_tokens // tm


def _moe(xm, h2u, idx, gates, gate2_p, gate2_s, moe_w, n_prompt, rows_per_seq):
    rows = _Tiles(xm.shape[0], MOE_TOKEN_TILE)
    a = rows.n_tokens * TOP_K
    n_blocks = -(-a // MOE_BLOCK) + N_EXPERTS
    dest, counts = _route(rows, idx)
    dest_flat = dest[:, :TOP_K].reshape(a)
    cnt = counts[0, :N_EXPERTS].astype(jnp.int32)
    pad_end = jnp.cumsum((cnt + MOE_BLOCK - 1) // MOE_BLOCK * MOE_BLOCK)
    block_row = jnp.arange(n_blocks, dtype=jnp.int32) * MOE_BLOCK
    block_e = jnp.minimum(jnp.sum((pad_end[None, :] <= block_row[:, None]).astype(jnp.int32), axis=1),
                          N_EXPERTS - 1)
    n_used = (pad_end[-1:] // MOE_BLOCK).astype(jnp.int32)
    x_sorted = _dispatch(rows, dest_flat, h2u, n_blocks * MOE_BLOCK)
    y_buf = _experts(x_sorted, block_e, n_used, *moe_w)
    return _combine(rows, dest_flat, xm, gates, gate2_p, gate2_s, y_buf, n_prompt, rows_per_seq)


def _group(rows, x, ada, state_rows, qk_and_attend, shared, total_tokens, first_row, token_bufs):
    (g_mix, w_in_b, w_conv, w_o_b, g_ffn, wr_hi, wr_lo, b_r, mid_dtype) = shared
    shift1, scale1, gate1, shift2, scale2, gate2 = [rows.mod_array(m) for m in jnp.split(ada, 6, axis=-1)]
    v_f, v_b = _v_proj(rows, x, g_mix, scale1, shift1, w_in_b)
    sga, cvp, tail = _conv_proj(rows, x, g_mix, scale1, shift1, w_in_b, w_conv, state_rows, mid_dtype)
    mix, k_out = qk_and_attend(x, scale1, shift1, v_f, v_b, sga, cvp)
    token_bufs = _out_proj(rows, mix, x, w_o_b, gate1, g_ffn, scale2, shift2, wr_hi, wr_lo, b_r,
                           total_tokens, first_row, token_bufs)
    return token_bufs, gate2, k_out, v_f, tail


def kernel(x_prompt, x_sample, c_prompt, c_sample, cache_k, cache_v, state_conv, page_table, w_ada, b_ada, g_norm_mix, w_in, g_q, g_k, lambda_q1, lambda_k1, lambda_q2, lambda_k2, g_subln, w_conv, w_o, g_norm_ffn, w_router, b_router, w_gate_up, b_gate_up, w_down, b_down):
    assert w_in.shape[0] == 1, "single-layer stack"
    batch, seq, _ = x_prompt.shape
    n_seq, n_new, _ = x_sample.shape
    tp, ts = batch * seq, n_seq * n_new
    n_chunks = D_MODEL // HEAD_DIM

    ada = _ada(jnp.concatenate([c_prompt, c_sample], axis=0), w_ada[0], b_ada[0][None])

    w_in_b = w_in[0].astype(BF16)
    w_kt = w_in[0][:, D_MODEL:2 * D_MODEL].T.astype(BF16)
    w_o_b = w_o[0].astype(BF16)
    gqk = jnp.stack([jnp.tile(g_q[0], n_chunks), jnp.tile(g_k[0], n_chunks)])[:, None, :]
    gk_col = jnp.tile(g_k[0], n_chunks)[:, None]
    blk = jnp.arange(D_MODEL, dtype=jnp.int32) // HEAD_DIM
    pmat = (blk[:, None] == blk[None, :]).astype(BF16)
    wr = jnp.pad(w_router[0], ((0, 0), (0, LANES - N_EXPERTS)))
    wr_hi = wr.astype(BF16)
    wr_lo = (wr - wr_hi.astype(F32)).astype(BF16)
    b_r = jnp.pad(b_router[0], (0, LANES - N_EXPERTS), constant_values=NEG_BIG)[None]
    moe_w = (w_gate_up[0], w_down[0], b_gate_up[0][:, None, 0::2], b_gate_up[0][:, None, 1::2],
             b_down[0][:, None, :])
    g_mix = g_norm_mix[0][None]
    g_ffn = g_norm_ffn[0][None]
    g_sub = g_subln[0][None]
    lams = (lambda_q1[0][None], lambda_k1[0][None], lambda_q2[0][None], lambda_k2[0][None])

    def shared(mid_dtype):
        return (g_mix, w_in_b, w_conv[0], w_o_b, g_ffn, wr_hi, wr_lo, b_r, mid_dtype)

    rows_p = _Rows(tp, seq)

    def attend_p(x, scale1, shift1, v_f, v_b, sga, cvp):
        q_b = _q_proj(rows_p, x, g_mix, scale1, shift1, w_in_b, gqk, pmat)
        kt_f, kt_b = _kt_proj(rows_p, x, g_mix, scale1, shift1, w_kt, gk_col, batch, seq)
        mix = _attn_prompt(q_b, kt_b, v_b, sga, cvp, g_sub, lams, batch, seq)
        return mix, kt_f

    bufs, gate2_p, kt_p, v_p, tail_p = _group(rows_p, x_prompt.reshape(tp, D_MODEL), ada[:batch], None, attend_p,
                                              shared(BF16), tp + ts, 0, None)

    rows_s = _Rows(ts, n_new)
    st = state_conv[0]
    zeros = jnp.zeros((n_seq, n_new - 2, D_MODEL), F32)
    s1 = jnp.concatenate([st[:, 1:2], jnp.zeros((n_seq, n_new - 1, D_MODEL), F32)], axis=1).reshape(ts, D_MODEL)
    s2 = jnp.concatenate([st, zeros], axis=1).reshape(ts, D_MODEL)

    def attend_s(x, scale1, shift1, v_f, v_b, sga, cvp):
        _, qk_f = _qk_proj(rows_s, x, g_mix, scale1, shift1, w_in_b, gqk, pmat)
        mix = _attn_sample(qk_f, v_f, cache_k[0], cache_v[0], page_table, sga, cvp, g_sub, lams, n_seq, n_new)
        return mix, qk_f[1]

    bufs, gate2_s, k_s, v_s, tail_s = _group(rows_s, x_sample.reshape(ts, D_MODEL), ada[batch:], (s1, s2), attend_s,
                                             shared(F32), tp + ts, tp, bufs)

    y_p, y_s = _moe(*bufs, gate2_p, gate2_s, moe_w, tp, seq)

    tail_s = tail_s.reshape(n_seq, n_new, D_MODEL)
    k_p = kt_p.reshape(1, batch, N_HEADS, 2, HEAD_DIM, seq).transpose(0, 1, 5, 2, 3, 4)
    return (y_p.reshape(batch, seq, D_MODEL),
            y_s.reshape(n_seq, n_new, D_MODEL),
            k_p,
            v_p.reshape(1, batch, seq, N_HEADS, V_DIM),
            tail_p[:, SUBLANES - (CONV_WIDTH - 1):][None],
            k_s.reshape(1, n_seq, n_new, N_HEADS, 2, HEAD_DIM),
            v_s.reshape(1, n_seq, n_new, N_HEADS, V_DIM),
            tail_s[:, n_new - (CONV_WIDTH - 1):][None])
```

```python
import functools
import math

import jax
import jax.numpy as jnp
from jax import lax
from jax.experimental import pallas as pl
from jax.experimental.pallas import tpu as pltpu

F32 = jnp.float32
BF16 = jnp.bfloat16

D_MODEL = 1024
HEAD_DIM = 64
V_DIM = 2 * HEAD_DIM
N_HEADS = D_MODEL // V_DIM
ATTN_SCALE = HEAD_DIM ** -0.5
LOG2_E = math.log2(math.e)
CONV_WIDTH = 3
PAGE_SIZE = 128
N_EXPERTS = 32
TOP_K = 4
SWIGLU_LIMIT = 7.0
SWIGLU_ALPHA = 1.702
NORM_EPS = 1e-6
LAMBDA_INIT = 0.8 - 0.6 * math.exp(-0.3 * 0)

VMEM_LIMIT_BYTES = 48 * 1024 * 1024
EXPERT_VMEM_LIMIT_BYTES = 56 * 1024 * 1024
LANES = 128
PACKED = D_MODEL // 2
ROW_CHUNKS = PACKED // LANES
SUBLANES = 8

ROW_TILE_PROMPT = 512
ATTN_BLOCK = 512
MOE_BLOCK = 512
MOE_TOKEN_TILE = 256
EXPERT_XPOSE_CHUNK = 512
EXPERT_FF_CHUNK = 512
ADA_COL_TILE = 1536
SAMPLE_PAGES_PER_STEP = 16
NEG_BIG = -1e30

_NT = (((1,), (1,)), ((), ()))


def _params(*sem):
    return pltpu.CompilerParams(dimension_semantics=sem, vmem_limit_bytes=VMEM_LIMIT_BYTES)


def _rms_mod(x, g, scale, shift):
    ms = jnp.mean(x * x, axis=-1, keepdims=True)
    return (x * lax.rsqrt(ms + NORM_EPS) * g) * (1.0 + scale) + shift


def _ada_kernel(c_ref, w_ref, b_ref, o_ref):
    c = c_ref[...]
    s = (c * jax.nn.sigmoid(c)).astype(BF16)
    o_ref[...] = jnp.dot(s, w_ref[...].astype(BF16), preferred_element_type=F32) + b_ref[...]


def _ada(c_all, w_ada, b_ada):
    n = c_all.shape[0]
    width = w_ada.shape[1]
    return pl.pallas_call(
        _ada_kernel,
        out_shape=jax.ShapeDtypeStruct((n, width), F32),
        grid=(width // ADA_COL_TILE,),
        in_specs=[pl.BlockSpec((n, D_MODEL), lambda j: (0, 0)),
                  pl.BlockSpec((D_MODEL, ADA_COL_TILE), lambda j: (0, j)),
                  pl.BlockSpec((1, ADA_COL_TILE), lambda j: (0, j))],
        out_specs=pl.BlockSpec((n, ADA_COL_TILE), lambda j: (0, j)),
        compiler_params=_params("arbitrary"),
        name="ada",
    )(c_all, w_ada, b_ada)


class _Rows:
    def __init__(self, n_tokens, rows_per_batch):
        if rows_per_batch >= ROW_TILE_PROMPT:
            self.tm = ROW_TILE_PROMPT
            self.tiles_per_batch = rows_per_batch // self.tm
            self.mod_rows = 1
        else:
            self.tm = n_tokens
            self.tiles_per_batch = None
            self.mod_rows = n_tokens
        self.n_tokens = n_tokens
        self.rows_per_batch = rows_per_batch
        self.n_tiles = n_tokens // self.tm

    def mod_array(self, m):
        if self.tiles_per_batch is not None:
            return m[:, None, :]
        return jnp.repeat(m, self.rows_per_batch, axis=0)[None]

    def mod_spec(self, clamp=False):
        if self.tiles_per_batch is not None:
            tpb, last = self.tiles_per_batch, self.n_tiles - 1
            if clamp:
                return pl.BlockSpec((1, 1, D_MODEL), lambda i, *_: (jnp.minimum(i, last) // tpb, 0, 0))
            return pl.BlockSpec((1, 1, D_MODEL), lambda i, *_: (i // tpb, 0, 0))
        return pl.BlockSpec((1, self.mod_rows, D_MODEL), lambda i, *_: (0, 0, 0))

    def row_spec(self, width=D_MODEL, clamp=False):
        if clamp:
            last = self.n_tiles - 1
            return pl.BlockSpec((self.tm, width), lambda i, *_: (jnp.minimum(i, last), 0))
        return pl.BlockSpec((self.tm, width), lambda i, *_: (i, 0))


def _qk_kernel(x_ref, g_ref, sc_ref, sh_ref, w_ref, gqk_ref, p_ref, qkb_ref, qkf_ref, h_ref):
    j = pl.program_id(1)

    @pl.when(j == 0)
    def _():
        h_ref[...] = _rms_mod(x_ref[...], g_ref[...], sc_ref[0], sh_ref[0]).astype(BF16)

    z = jnp.dot(h_ref[...], w_ref[...], preferred_element_type=F32)
    ss = jnp.dot((z * z).astype(BF16), p_ref[...], preferred_element_type=F32)
    zn = z * lax.rsqrt(ss * (1.0 / HEAD_DIM) + NORM_EPS) * gqk_ref[0]
    qkf_ref[0] = zn
    scale = jnp.where(j == 0, ATTN_SCALE, 1.0)
    qkb_ref[...] = (zn * scale).astype(BF16)


def _qk_proj(rows, x, g, scale, shift, w_in_b, gqk, pmat):
    t = rows.n_tokens
    return pl.pallas_call(
        _qk_kernel,
        out_shape=(jax.ShapeDtypeStruct((t, 2 * D_MODEL), BF16),
                   jax.ShapeDtypeStruct((2, t, D_MODEL), F32)),
        grid=(rows.n_tiles, 2),
        in_specs=[rows.row_spec(),
                  pl.BlockSpec((1, D_MODEL), lambda i, j: (0, 0)),
                  rows.mod_spec(), rows.mod_spec(),
                  pl.BlockSpec((D_MODEL, D_MODEL), lambda i, j: (0, j)),
                  pl.BlockSpec((1, 1, D_MODEL), lambda i, j: (j, 0, 0)),
                  pl.BlockSpec((D_MODEL, D_MODEL), lambda i, j: (0, 0))],
        out_specs=(pl.BlockSpec((rows.tm, D_MODEL), lambda i, j: (i, j)),
                   pl.BlockSpec((1, rows.tm, D_MODEL), lambda i, j: (j, i, 0))),
        scratch_shapes=[pltpu.VMEM((rows.tm, D_MODEL), BF16)],
        compiler_params=_params("parallel", "arbitrary"),
        name="qk_proj",
    )(x, g, scale, shift, w_in_b, gqk, pmat)


def _q_kernel(x_ref, g_ref, sc_ref, sh_ref, w_ref, gq_ref, p_ref, qb_ref):
    h = _rms_mod(x_ref[...], g_ref[...], sc_ref[0], sh_ref[0]).astype(BF16)
    z = jnp.dot(h, w_ref[...], preferred_element_type=F32)
    ss = jnp.dot((z * z).astype(BF16), p_ref[...], preferred_element_type=F32)
    zn = z * lax.rsqrt(ss * (1.0 / HEAD_DIM) + NORM_EPS) * gq_ref[0]
    qb_ref[...] = (zn * (ATTN_SCALE * LOG2_E)).astype(BF16)


def _q_proj(rows, x, g, scale, shift, w_in_b, gqk, pmat):
    return pl.pallas_call(
        _q_kernel,
        out_shape=jax.ShapeDtypeStruct((rows.n_tokens, D_MODEL), BF16),
        grid=(rows.n_tiles,),
        in_specs=[rows.row_spec(),
                  pl.BlockSpec((1, D_MODEL), lambda i: (0, 0)),
                  rows.mod_spec(), rows.mod_spec(),
                  pl.BlockSpec((D_MODEL, D_MODEL), lambda i: (0, 0)),
                  pl.BlockSpec((1, 1, D_MODEL), lambda i: (0, 0, 0)),
                  pl.BlockSpec((D_MODEL, D_MODEL), lambda i: (0, 0))],
        out_specs=rows.row_spec(),
        compiler_params=_params("parallel"),
        name="q_proj",
    )(x, g, scale, shift, w_in_b, gqk, pmat)


def _kt_kernel(x_ref, g_ref, sc_ref, sh_ref, wt_ref, gk_ref, kf_ref, kb_ref):
    h = _rms_mod(x_ref[...], g_ref[...], sc_ref[0], sh_ref[0]).astype(BF16)
    zt = lax.dot_general(wt_ref[...], h, _NT, preferred_element_type=F32)
    tm = zt.shape[1]
    z3 = zt.reshape(D_MODEL // HEAD_DIM, HEAD_DIM, tm)
    ss = jnp.sum(z3 * z3, axis=1, keepdims=True)
    g3 = gk_ref[...].reshape(D_MODEL // HEAD_DIM, HEAD_DIM, 1)
    zn = (z3 * lax.rsqrt(ss * (1.0 / HEAD_DIM) + NORM_EPS) * g3).reshape(D_MODEL, tm)
    kf_ref[...] = zn
    kb_ref[...] = zn.astype(BF16)


def _kt_proj(rows, x, g, scale, shift, w_kt, gk_col, batch, seq):
    tpb = rows.tiles_per_batch
    out_spec = pl.BlockSpec((None, D_MODEL, rows.tm), lambda i: (i // tpb, 0, i % tpb))
    return pl.pallas_call(
        _kt_kernel,
        out_shape=(jax.ShapeDtypeStruct((batch, D_MODEL, seq), F32),
                   jax.ShapeDtypeStruct((batch, D_MODEL, seq), BF16)),
        grid=(rows.n_tiles,),
        in_specs=[rows.row_spec(),
                  pl.BlockSpec((1, D_MODEL), lambda i: (0, 0)),
                  rows.mod_spec(), rows.mod_spec(),
                  pl.BlockSpec((D_MODEL, D_MODEL), lambda i: (0, 0)),
                  pl.BlockSpec((D_MODEL, 1), lambda i: (0, 0))],
        out_specs=(out_spec, out_spec),
        compiler_params=_params("parallel"),
        name="kt_proj",
    )(x, g, scale, shift, w_kt, gk_col)


def _v_kernel(x_ref, g_ref, sc_ref, sh_ref, w_ref, vf_ref, vb_ref):
    h = _rms_mod(x_ref[...], g_ref[...], sc_ref[0], sh_ref[0]).astype(BF16)
    z = jnp.dot(h, w_ref[...], preferred_element_type=F32)
    vf_ref[...] = z
    vb_ref[...] = z.astype(BF16)


def _v_proj(rows, x, g, scale, shift, w_in_b):
    t = rows.n_tokens
    return pl.pallas_call(
        _v_kernel,
        out_shape=(jax.ShapeDtypeStruct((t, D_MODEL), F32),
                   jax.ShapeDtypeStruct((t, D_MODEL), BF16)),
        grid=(rows.n_tiles,),
        in_specs=[rows.row_spec(),
                  pl.BlockSpec((1, D_MODEL), lambda i: (0, 0)),
                  rows.mod_spec(), rows.mod_spec(),
                  pl.BlockSpec((D_MODEL, D_MODEL), lambda i: (0, 2))],
        out_specs=(rows.row_spec(), rows.row_spec()),
        compiler_params=_params("parallel"),
        name="v_proj",
    )(x, g, scale, shift, w_in_b)


def _conv_kernel(*refs, tiles_per_batch, rows_per_batch, tail_rows):
    if tiles_per_batch is None:
        (x_ref, g_ref, sc_ref, sh_ref, w_ref, wc_ref, s1_ref, s2_ref,
         sga_ref, cvp_ref, tail_ref, h_ref, a_ref, b_ref, carry_ref) = refs
    else:
        (x_ref, g_ref, sc_ref, sh_ref, w_ref, wc_ref,
         sga_ref, cvp_ref, tail_ref, h_ref, a_ref, b_ref, carry_ref) = refs
    i = pl.program_id(0)
    j = pl.program_id(1)

    @pl.when(j == 0)
    def _():
        h_ref[...] = _rms_mod(x_ref[...], g_ref[...], sc_ref[0], sh_ref[0]).astype(BF16)

    z = jnp.dot(h_ref[...], w_ref[...], preferred_element_type=F32)

    @pl.when(j == 0)
    def _():
        a_ref[...] = z

    @pl.when(j == 1)
    def _():
        b_ref[...] = z

    @pl.when(j == 2)
    def _():
        u = b_ref[...] * z
        tm = u.shape[0]
        row = lax.broadcasted_iota(jnp.int32, (tm, 1), 0)
        r1 = pltpu.roll(u, 1, 0)
        r2 = pltpu.roll(u, 2, 0)
        if tiles_per_batch is None:
            t = row & (rows_per_batch - 1)
            u1 = jnp.where(t >= 1, r1, s1_ref[...])
            u2 = jnp.where(t >= 2, r2, s2_ref[...])
        else:
            first = (i % tiles_per_batch) == 0
            c = jnp.where(first, 0.0, carry_ref[...])
            u1 = jnp.where(row == 0, c[7:8], r1)
            u2 = jnp.where(row == 0, c[6:7], jnp.where(row == 1, c[7:8], r2))
            carry_ref[...] = u[tm - SUBLANES:]
        wc = wc_ref[...]
        yc = wc[0:1] * u2 + wc[1:2] * u1 + wc[2:3] * u
        a_ref[...] = a_ref[...] * yc
        tail_ref[0] = u[tm - tail_rows:]

    @pl.when(j == 3)
    def _():
        sga_ref[...] = jax.nn.sigmoid(z).astype(sga_ref.dtype)

    @pl.when(j == 4)
    def _():
        cvp_ref[...] = (jax.nn.sigmoid(z) * a_ref[...]).astype(cvp_ref.dtype)


def _conv_proj(rows, x, g, scale, shift, w_in_b, w_conv, state_rows, out_dtype):
    t = rows.n_tokens
    sample_mode = rows.tiles_per_batch is None
    tail_rows = rows.tm if sample_mode else SUBLANES
    n_tail_blocks = 1 if sample_mode else t // rows.rows_per_batch
    in_specs = [rows.row_spec(),
                pl.BlockSpec((1, D_MODEL), lambda i, j: (0, 0)),
                rows.mod_spec(), rows.mod_spec(),
                pl.BlockSpec((D_MODEL, D_MODEL), lambda i, j: (0, 3 + j)),
                pl.BlockSpec((CONV_WIDTH, D_MODEL), lambda i, j: (0, 0))]
    args = [x, g, scale, shift, w_in_b, w_conv]
    if sample_mode:
        in_specs += [rows.row_spec(), rows.row_spec()]
        args += list(state_rows)
        tail_spec = pl.BlockSpec((1, tail_rows, D_MODEL), lambda i, j: (0, 0, 0))
    else:
        tpb = rows.tiles_per_batch
        tail_spec = pl.BlockSpec((1, tail_rows, D_MODEL), lambda i, j: (i // tpb, 0, 0))
    kern = functools.partial(_conv_kernel, tiles_per_batch=rows.tiles_per_batch,
                             rows_per_batch=rows.rows_per_batch, tail_rows=tail_rows)
    return pl.pallas_call(
        kern,
        out_shape=(jax.ShapeDtypeStruct((t, D_MODEL), out_dtype),
                   jax.ShapeDtypeStruct((t, D_MODEL), out_dtype),
                   jax.ShapeDtypeStruct((n_tail_blocks, tail_rows, D_MODEL), F32)),
        grid=(rows.n_tiles, 5),
        in_specs=in_specs,
        out_specs=(rows.row_spec(), rows.row_spec(), tail_spec),
        scratch_shapes=[pltpu.VMEM((rows.tm, D_MODEL), BF16),
                        pltpu.VMEM((rows.tm, D_MODEL), F32),
                        pltpu.VMEM((rows.tm, D_MODEL), F32),
                        pltpu.VMEM((SUBLANES, D_MODEL), F32)],
        compiler_params=_params("arbitrary", "arbitrary"),
        name="conv_proj",
    )(*args)


def _lambda_value(lq1, lk1, lq2, lk2):
    e1 = jnp.exp(jnp.sum(lq1 * lk1, axis=-1, keepdims=True))
    e2 = jnp.exp(jnp.sum(lq2 * lk2, axis=-1, keepdims=True))
    return e1 - e2 + LAMBDA_INIT


def _subln_mix(o, g_sub, sga, cvp):
    ms = jnp.mean(o * o, axis=-1, keepdims=True)
    attn = (o * lax.rsqrt(ms + NORM_EPS) * g_sub) * (1.0 - LAMBDA_INIT)
    return sga * attn + cvp


def _flash_update(s, v_ones, m_ref, l_ref, acc_ref):
    m_prev = m_ref[...]
    m_next = jnp.maximum(m_prev, jnp.max(s, axis=1, keepdims=True))
    alpha = jnp.exp2(m_prev - m_next)
    p = jnp.exp2(s - jnp.concatenate([m_next] * (s.shape[1] // LANES), axis=1)).astype(BF16)
    pv = jnp.dot(p, v_ones, preferred_element_type=F32)
    acc_ref[...] = alpha * acc_ref[...] + pv[:, :V_DIM]
    l_ref[...] = alpha * l_ref[...] + pv[:, V_DIM:]
    m_ref[...] = m_next


def _attn_prompt_kernel(qi_ref, ki_ref, q_ref, kt_ref, v_ref, sga_ref, cvp_ref, gs_ref,
                        lq1_ref, lk1_ref, lq2_ref, lk2_ref,
                        o_ref, m0_ref, l0_ref, a0_ref, m1_ref, l1_ref, a1_ref):
    qi = qi_ref[pl.program_id(1)]
    ki = ki_ref[pl.program_id(1)]
    states = ((m0_ref, l0_ref, a0_ref), (m1_ref, l1_ref, a1_ref))

    @pl.when(ki == 0)
    def _():
        for m_ref, l_ref, a_ref in states:
            m_ref[...] = jnp.full(m_ref.shape, -jnp.inf, F32)
            l_ref[...] = jnp.zeros(l_ref.shape, F32)
            a_ref[...] = jnp.zeros(a_ref.shape, F32)

    tq, tk = q_ref.shape[0], kt_ref.shape[1]

    def step(masked):
        lane = lax.broadcasted_iota(jnp.int32, (1, V_DIM), 1)
        zero = jnp.zeros((), BF16)
        ones = jnp.ones((tk, V_DIM), BF16)
        if masked:
            r = lax.broadcasted_iota(jnp.int32, (tq, tk), 0)
            c = lax.broadcasted_iota(jnp.int32, (tq, tk), 1)
            keep = c <= r
        for h in range(N_HEADS):
            cols = slice(h * V_DIM, (h + 1) * V_DIM)
            q = q_ref[:, cols]
            kt = kt_ref[cols, :]
            v_ones = jnp.concatenate([v_ref[:, cols], ones], axis=1)
            qs = (jnp.where(lane < HEAD_DIM, q, zero), jnp.where(lane >= HEAD_DIM, q, zero))
            for qm, (m_ref, l_ref, a_ref) in zip(qs, states):
                s = jnp.dot(qm, kt, preferred_element_type=F32)
                if masked:
                    s = jnp.where(keep, s, -jnp.inf)
                _flash_update(s, v_ones, m_ref.at[h], l_ref.at[h], a_ref.at[h])

    @pl.when(ki < qi)
    def _():
        step(False)

    @pl.when(ki == qi)
    def _():
        step(True)
        lam = _lambda_value(lq1_ref[...], lk1_ref[...], lq2_ref[...], lk2_ref[...])
        for h in range(N_HEADS):
            cols = slice(h * V_DIM, (h + 1) * V_DIM)
            o = a0_ref[h] / l0_ref[h] - lam * (a1_ref[h] / l1_ref[h])
            mix = _subln_mix(o, gs_ref[...], sga_ref[:, cols].astype(F32), cvp_ref[:, cols].astype(F32))
            o_ref[:, cols] = mix.astype(o_ref.dtype)


def _attn_prompt(q_b, kt_b, v_b, sga, cvp, g_sub, lams, batch, seq):
    nb = seq // ATTN_BLOCK
    tq = ATTN_BLOCK
    pairs = [(qi, ki) for qi in range(nb) for ki in range(qi + 1)]
    qi_tab = jnp.array([p[0] for p in pairs], jnp.int32)
    ki_tab = jnp.array([p[1] for p in pairs], jnp.int32)
    q_spec = pl.BlockSpec((tq, D_MODEL), lambda b, p, qt, kt: (b * nb + qt[p], 0))
    k_spec = pl.BlockSpec((None, D_MODEL, tq), lambda b, p, qt, kt: (b, 0, kt[p]))
    v_spec = pl.BlockSpec((tq, D_MODEL), lambda b, p, qt, kt: (b * nb + kt[p], 0))
    vec64 = pl.BlockSpec((1, HEAD_DIM), lambda b, p, qt, kt: (0, 0))
    grid_spec = pltpu.PrefetchScalarGridSpec(
        num_scalar_prefetch=2,
        grid=(batch, len(pairs)),
        in_specs=[q_spec, k_spec, v_spec, q_spec, q_spec,
                  pl.BlockSpec((1, V_DIM), lambda b, p, qt, kt: (0, 0)),
                  vec64, vec64, vec64, vec64],
        out_specs=q_spec,
        scratch_shapes=[pltpu.VMEM((N_HEADS, tq, V_DIM), F32)] * 6)
    return pl.pallas_call(
        _attn_prompt_kernel,
        out_shape=jax.ShapeDtypeStruct((batch * seq, D_MODEL), BF16),
        grid_spec=grid_spec,
        compiler_params=_params("parallel", "arbitrary"),
        name="attn_prompt",
    )(qi_tab, ki_tab, q_b, kt_b, v_b, sga, cvp, g_sub, *lams)


def _attn_sample_kernel(pt_ref, q_ref, kn_ref, vn_ref, *rest, n_new, n_group):
    kc_refs = rest[:n_group]
    vc_refs = rest[n_group:2 * n_group]
    (spread_ref, own_ref, sga_ref, cvp_ref, gs_ref, lq1_ref, lk1_ref, lq2_ref, lk2_ref,
     o_ref, qbd_ref, m_ref, l_ref, acc_ref) = rest[2 * n_group:]
    p = pl.program_id(1)
    rows_per_head = 2 * n_new

    @pl.when(p == 0)
    def _():
        q = q_ref[...] * ATTN_SCALE
        qrep = jnp.concatenate([q] * (N_HEADS * 2), axis=0)
        r = lax.broadcasted_iota(jnp.int32, qrep.shape, 0)
        c = lax.broadcasted_iota(jnp.int32, qrep.shape, 1)
        qbd_ref[...] = jnp.where(c // HEAD_DIM == r // n_new, qrep, 0.0).astype(BF16)
        m_ref[...] = jnp.full(m_ref.shape, -jnp.inf, F32)
        l_ref[...] = jnp.zeros(l_ref.shape, F32)
        acc_ref[...] = jnp.zeros(acc_ref.shape, F32)

    def update(s, pv_of):
        m_prev = m_ref[...]
        m_next = jnp.maximum(m_prev, jnp.max(s, axis=1, keepdims=True))
        alpha = jnp.exp(m_prev - m_next)
        width = s.shape[1]
        m_wide = m_next[:, :width] if width <= LANES else jnp.concatenate([m_next] * (width // LANES), axis=1)
        pr = jnp.exp(s - m_wide)
        l_ref[...] = alpha * l_ref[...] + jnp.sum(pr, axis=1, keepdims=True)
        acc_ref[...] = alpha * acc_ref[...] + pv_of(pr.astype(BF16))
        m_ref[...] = m_next

    def page_pv(pb, v_ref):
        pe = jnp.dot(pb, spread_ref[...], preferred_element_type=F32).astype(BF16) * own_ref[...]
        v2 = v_ref[...].reshape(PAGE_SIZE * N_HEADS, V_DIM).astype(BF16)
        return jnp.dot(pe, v2, preferred_element_type=F32)

    qbd = qbd_ref[...]
    s = jnp.concatenate([jnp.dot(qbd, kc_refs[g][...].astype(BF16), preferred_element_type=F32)
                         for g in range(n_group)], axis=1)

    def pages_pv(pb):
        out = page_pv(pb[:, :PAGE_SIZE], vc_refs[0])
        for g in range(1, n_group):
            out = out + page_pv(pb[:, g * PAGE_SIZE:(g + 1) * PAGE_SIZE], vc_refs[g])
        return out

    update(s, pages_pv)

    @pl.when(p == pl.num_programs(1) - 1)
    def _():
        s_new = lax.dot_general(qbd_ref[...], kn_ref[...].astype(BF16), _NT, preferred_element_type=F32)
        r = lax.broadcasted_iota(jnp.int32, s_new.shape, 0)
        c = lax.broadcasted_iota(jnp.int32, s_new.shape, 1)
        s_new = jnp.where(c <= (r & (n_new - 1)), s_new, -jnp.inf)
        def new_pv(pb):
            return jnp.concatenate(
                [jnp.dot(pb[h * rows_per_head:(h + 1) * rows_per_head].astype(F32),
                         vn_ref[:, h * V_DIM:(h + 1) * V_DIM].astype(BF16).astype(F32),
                         preferred_element_type=F32) for h in range(N_HEADS)], axis=0)

        update(s_new, new_pv)
        lam = _lambda_value(lq1_ref[...], lk1_ref[...], lq2_ref[...], lk2_ref[...])
        acc = acc_ref[...] / l_ref[...]
        for h in range(N_HEADS):
            cols = slice(h * V_DIM, (h + 1) * V_DIM)
            r0 = h * rows_per_head
            o = acc[r0:r0 + n_new] - lam * acc[r0 + n_new:r0 + rows_per_head]
            o_ref[:, cols] = _subln_mix(o, gs_ref[...], sga_ref[:, cols], cvp_ref[:, cols])


def _attn_sample(qkf, vf, cache_k, cache_v, page_table, sga, cvp, g_sub, lams, n_seq, n_new):
    n_pages = page_table.shape[1]
    n_pool = cache_k.shape[0]
    width = N_HEADS * V_DIM
    grp = SAMPLE_PAGES_PER_STEP
    kc = jnp.transpose(cache_k, (0, 2, 3, 4, 1)).reshape(n_pool, width, PAGE_SIZE)
    pt = page_table.reshape(-1)
    n_rows = N_HEADS * 2 * n_new
    row = pl.BlockSpec((n_new, width), lambda b, p, pt: (b, 0))

    def page_index(g):
        return lambda b, p, pt: (pt[b * n_pages + p * grp + g], 0, 0)

    def page_index4(g):
        return lambda b, p, pt: (pt[b * n_pages + p * grp + g], 0, 0, 0)

    k_pages = [pl.BlockSpec((None, width, PAGE_SIZE), page_index(g)) for g in range(grp)]
    v_pages = [pl.BlockSpec((None, PAGE_SIZE, N_HEADS, V_DIM), page_index4(g)) for g in range(grp)]
    vec64 = pl.BlockSpec((1, HEAD_DIM), lambda b, p, pt: (0, 0))
    col = jnp.arange(PAGE_SIZE * N_HEADS, dtype=jnp.int32)
    spread = (col[None, :] // N_HEADS == jnp.arange(PAGE_SIZE, dtype=jnp.int32)[:, None]).astype(BF16)
    own = (col[None, :] % N_HEADS == jnp.arange(n_rows, dtype=jnp.int32)[:, None] // (2 * n_new)).astype(BF16)
    const = lambda shape: pl.BlockSpec(shape, lambda b, p, pt: (0, 0))
    grid_spec = pltpu.PrefetchScalarGridSpec(
        num_scalar_prefetch=1,
        grid=(n_seq, n_pages // grp),
        in_specs=[pl.BlockSpec((None, n_new, width), lambda b, p, pt: (0, b, 0)),
                  pl.BlockSpec((None, n_new, width), lambda b, p, pt: (1, b, 0)),
                  row, *k_pages, *v_pages, const(spread.shape), const(own.shape), row, row,
                  pl.BlockSpec((1, V_DIM), lambda b, p, pt: (0, 0)),
                  vec64, vec64, vec64, vec64],
        out_specs=row,
        scratch_shapes=[pltpu.VMEM((n_rows, width), BF16),
                        pltpu.VMEM((n_rows, V_DIM), F32), pltpu.VMEM((n_rows, V_DIM), F32),
                        pltpu.VMEM((n_rows, V_DIM), F32)])
    return pl.pallas_call(
        functools.partial(_attn_sample_kernel, n_new=n_new, n_group=grp),
        out_shape=jax.ShapeDtypeStruct((n_seq * n_new, width), F32),
        grid_spec=grid_spec,
        compiler_params=_params("parallel", "arbitrary"),
        name="attn_sample",
    )(pt, qkf, qkf, vf, *([kc] * grp), *([cache_v] * grp), spread, own, sga, cvp, g_sub, *lams)


def _out_kernel(mix_ref, x_ref, wo_ref, g1_ref, g_ref, sc_ref, sh_ref, wrh_ref, wrl_ref, br_ref, *refs, n_real_tiles):
    outs = refs[-4:]
    i = pl.program_id(0)

    @pl.when(i < n_real_tiles)
    def _():
        _out_tile(mix_ref, x_ref, wo_ref, g1_ref, g_ref, sc_ref, sh_ref, wrh_ref, wrl_ref, br_ref, *outs)

    @pl.when(i >= n_real_tiles)
    def _():
        for r in outs:
            r[...] = jnp.zeros(r.shape, r.dtype)


def _out_tile(mix_ref, x_ref, wo_ref, g1_ref, g_ref, sc_ref, sh_ref, wrh_ref, wrl_ref, br_ref,
              xm_ref, h2_ref, idx_ref, gt_ref):
    y = jnp.dot(mix_ref[...].astype(BF16), wo_ref[...], preferred_element_type=F32)
    xm = x_ref[...] + g1_ref[0] * y
    xm_ref[...] = xm
    h2 = _rms_mod(xm, g_ref[...], sc_ref[0], sh_ref[0])
    hi = h2.astype(BF16)
    _store_token_rows(h2_ref, _pack_pairs(hi.astype(F32)))
    lo = (h2 - hi.astype(F32)).astype(BF16)
    logits = (jnp.dot(hi, wrh_ref[...], preferred_element_type=F32)
              + jnp.dot(lo, wrh_ref[...], preferred_element_type=F32)
              + jnp.dot(hi, wrl_ref[...], preferred_element_type=F32)) + br_ref[...]
    lane = lax.broadcasted_iota(jnp.int32, logits.shape, 1).astype(F32)
    vals, idxs = [], []
    for _ in range(TOP_K):
        m = jnp.max(logits, axis=-1, keepdims=True)
        ix = jnp.min(jnp.where(logits == m, lane, float(LANES)), axis=-1, keepdims=True)
        logits = jnp.where(lane == ix, -jnp.inf, logits)
        vals.append(m)
        idxs.append(ix)
    es = [jnp.exp(v - vals[0]) for v in vals]
    denom = es[0] + es[1] + es[2] + es[3]
    idx_out = jnp.zeros(logits.shape, F32)
    gt_out = jnp.zeros(logits.shape, F32)
    for k in range(TOP_K):
        idx_out = jnp.where(lane == float(k), idxs[k], idx_out)
        gt_out = jnp.where(lane == float(k), es[k] / denom, gt_out)
    idx_ref[...] = idx_out.astype(jnp.int32)
    gt_ref[...] = gt_out


def _out_proj(rows, mix, x, w_o_b, gate1, g, scale, shift, wr_hi, wr_lo, b_r, total_tokens, first_row, shared_bufs):
    full = lambda shape: pl.BlockSpec(shape, lambda i: (0,) * len(shape))
    off = first_row // rows.tm
    creates = shared_bufs is None
    out_row = lambda width: pl.BlockSpec((rows.tm, width), lambda i: (i + off, 0))
    packed_rows = pl.BlockSpec((rows.tm * ROW_CHUNKS, LANES), lambda i: (i + off, 0))
    in_specs = [rows.row_spec(clamp=creates), rows.row_spec(clamp=creates), full((D_MODEL, D_MODEL)),
                rows.mod_spec(clamp=creates), full((1, D_MODEL)), rows.mod_spec(clamp=creates),
                rows.mod_spec(clamp=creates),
                full((D_MODEL, LANES)), full((D_MODEL, LANES)), full((1, LANES))]
    args = [mix, x, w_o_b, gate1, g, scale, shift, wr_hi, wr_lo, b_r]
    aliases = {}
    if not creates:
        aliases = {len(args) + j: j for j in range(len(shared_bufs))}
        in_specs += [pl.BlockSpec(memory_space=pl.ANY)] * len(shared_bufs)
        args += list(shared_bufs)
    extra = 1 if creates and total_tokens > rows.n_tokens else 0
    assert total_tokens - rows.n_tokens <= rows.tm or not creates, "other group must fit the one extra tile"
    return pl.pallas_call(
        functools.partial(_out_kernel, n_real_tiles=rows.n_tiles),
        out_shape=(jax.ShapeDtypeStruct((total_tokens, D_MODEL), F32),
                   jax.ShapeDtypeStruct((total_tokens * ROW_CHUNKS, LANES), jnp.uint32),
                   jax.ShapeDtypeStruct((total_tokens, LANES), jnp.int32),
                   jax.ShapeDtypeStruct((total_tokens, LANES), F32)),
        grid=(rows.n_tiles + extra,),
        in_specs=in_specs,
        out_specs=(out_row(D_MODEL), packed_rows, out_row(LANES), out_row(LANES)),
        input_output_aliases=aliases,
        compiler_params=_params("arbitrary"),
        name="out_proj_router",
    )(*args)


def _pack_pairs(x):
    bits = pltpu.bitcast(x, jnp.uint32)
    n = x.shape[1] // 2
    return (bits[:, :n] >> 16) | (bits[:, n:] & jnp.uint32(0xFFFF0000))


def _unpack_pairs(w):
    lo = pltpu.bitcast(w << 16, F32)
    hi = pltpu.bitcast(w & jnp.uint32(0xFFFF0000), F32)
    return jnp.concatenate([lo, hi], axis=1)


def _store_token_rows(ref, packed):
    n = packed.shape[0]
    for c in range(ROW_CHUNKS):
        ref[pl.ds(c, n, stride=ROW_CHUNKS), :] = packed[:, c * LANES:(c + 1) * LANES]


def _load_token_rows(ref, n):
    return jnp.concatenate([ref[pl.ds(c, n, stride=ROW_CHUNKS), :] for c in range(ROW_CHUNKS)], axis=1)


def _route_kernel(idx_ref, dest_ref, cnt_ref, run_ref, start_ref):
    ph = pl.program_id(0)
    i = pl.program_id(1)
    idx = idx_ref[...]
    tm = idx.shape[0]
    lane = lax.broadcasted_iota(jnp.int32, idx.shape, 1)
    onehots = [(lane == idx[:, k:k + 1]).astype(F32) for k in range(TOP_K)]
    member = onehots[0] + onehots[1] + onehots[2] + onehots[3]
    tile_count = jnp.sum(member, axis=0, keepdims=True)

    @pl.when((ph == 0) & (i == 0))
    def _():
        cnt_ref[...] = jnp.zeros(cnt_ref.shape, F32)

    @pl.when(ph == 0)
    def _():
        cnt_ref[...] = cnt_ref[...] + tile_count

    @pl.when((ph == 1) & (i == 0))
    def _():
        cnt = cnt_ref[...]
        padded = jnp.floor((cnt + (MOE_BLOCK - 1)) * (1.0 / MOE_BLOCK)) * MOE_BLOCK
        l1 = lax.broadcasted_iota(jnp.int32, cnt.shape, 1)
        incl = padded
        for s in (1, 2, 4, 8, 16, 32, 64):
            incl = incl + jnp.where(l1 >= s, pltpu.roll(incl, s, 1), 0.0)
        start_ref[...] = incl - padded
        run_ref[...] = jnp.zeros(run_ref.shape, F32)

    @pl.when(ph == 1)
    def _():
        r = lax.broadcasted_iota(jnp.int32, (tm, tm), 0)
        c = lax.broadcasted_iota(jnp.int32, (tm, tm), 1)
        earlier = (c < r).astype(BF16)
        before = jnp.dot(earlier, member.astype(BF16), preferred_element_type=F32)
        base = before + run_ref[0:1] + start_ref[0:1]
        out = jnp.zeros(idx.shape, F32)
        for k in range(TOP_K):
            d = jnp.sum(onehots[k] * base, axis=1, keepdims=True)
            out = jnp.where(lane == k, d, out)
        dest_ref[...] = out.astype(jnp.int32)
        run_ref[...] = run_ref[...] + tile_count


def _route(rows, idx):
    tm = rows.tm
    dest, counts = pl.pallas_call(
        _route_kernel,
        out_shape=(jax.ShapeDtypeStruct((rows.n_tokens, LANES), jnp.int32),
                   jax.ShapeDtypeStruct((SUBLANES, LANES), F32)),
        grid=(2, rows.n_tiles),
        in_specs=[pl.BlockSpec((tm, LANES), lambda ph, i: (i, 0))],
        out_specs=(pl.BlockSpec((tm, LANES), lambda ph, i: (i * ph, 0)),
                   pl.BlockSpec((SUBLANES, LANES), lambda ph, i: (0, 0))),
        scratch_shapes=[pltpu.VMEM((SUBLANES, LANES), F32), pltpu.VMEM((SUBLANES, LANES), F32)],
        compiler_params=_params("arbitrary", "arbitrary"),
        name="moe_route",
    )(idx)
    return dest, counts


def _dispatch_kernel(dest_ref, h_ref, zero_ref, xs_ref, sem):
    del zero_ref
    i = pl.program_id(0)
    tm = h_ref.shape[0] // ROW_CHUNKS

    def body(t, carry):
        for k in range(TOP_K):
            d = dest_ref[(i * tm + t) * TOP_K + k]
            pltpu.make_async_copy(h_ref.at[pl.ds(t * ROW_CHUNKS, ROW_CHUNKS)],
                                  xs_ref.at[pl.ds(d * ROW_CHUNKS, ROW_CHUNKS)], sem).start()
        return carry

    lax.fori_loop(0, tm, body, 0, unroll=8)
    n_words = tm * TOP_K * ROW_CHUNKS
    pltpu.make_async_copy(xs_ref.at[pl.ds(0, n_words)], xs_ref.at[pl.ds(0, n_words)], sem).wait()


def _dispatch(rows, dest_flat, h2u, n_rows):
    tm = rows.tm
    zeros = jnp.zeros((n_rows * ROW_CHUNKS, LANES), jnp.uint32)
    grid_spec = pltpu.PrefetchScalarGridSpec(
        num_scalar_prefetch=1, grid=(rows.n_tiles,),
        in_specs=[pl.BlockSpec((tm * ROW_CHUNKS, LANES), lambda i, d: (i, 0)),
                  pl.BlockSpec(memory_space=pl.ANY)],
        out_specs=pl.BlockSpec(memory_space=pl.ANY),
        scratch_shapes=[pltpu.SemaphoreType.DMA(())])
    return pl.pallas_call(
        _dispatch_kernel,
        out_shape=jax.ShapeDtypeStruct((n_rows * ROW_CHUNKS, LANES), jnp.uint32),
        grid_spec=grid_spec,
        input_output_aliases={2: 0},
        compiler_params=_params("arbitrary"),
        name="moe_dispatch",
    )(dest_flat, h2u, zeros)


def _expert_kernel(be_ref, nu_ref, x_ref, wgu_ref, wd_ref, bg_ref, bu_ref, bd_ref, y_ref,
                   wt_s, wg_s, wu_s, wd_s):
    i = pl.program_id(0)
    used = i < nu_ref[0]
    new_expert = (i == 0) | (be_ref[i] != be_ref[jnp.maximum(i - 1, 0)])

    @pl.when(used & new_expert)
    def _():
        n = wgu_ref.shape[1]
        n_lane_tiles = wt_s.shape[0]
        for c in range(n // EXPERT_XPOSE_CHUNK):
            cols = slice(c * EXPERT_XPOSE_CHUNK, (c + 1) * EXPERT_XPOSE_CHUNK)
            wt = wgu_ref[:, cols].T
            for j in range(n_lane_tiles):
                wt_s[j, cols, :] = wt[:, j * LANES:(j + 1) * LANES]
        for j in range(n_lane_tiles):
            lanes = slice(j * LANES, (j + 1) * LANES)
            wg_s[:, lanes] = wt_s[j, pl.ds(0, n // 2, stride=2), :].astype(BF16)
            wu_s[:, lanes] = wt_s[j, pl.ds(1, n // 2, stride=2), :].astype(BF16)
        wd_s[...] = wd_ref[...].astype(BF16)

    @pl.when(used)
    def _():
        x = _unpack_pairs(_load_token_rows(x_ref, MOE_BLOCK)).astype(BF16)
        y = bd_ref[...]
        for c in range(wd_s.shape[0] // EXPERT_FF_CHUNK):
            ff = slice(c * EXPERT_FF_CHUNK, (c + 1) * EXPERT_FF_CHUNK)
            g = lax.dot_general(x, wg_s[ff, :], _NT, preferred_element_type=F32) + bg_ref[:, ff]
            u = lax.dot_general(x, wu_s[ff, :], _NT, preferred_element_type=F32) + bu_ref[:, ff]
            gate = jnp.minimum(g, SWIGLU_LIMIT)
            up = jnp.clip(u, -SWIGLU_LIMIT, SWIGLU_LIMIT)
            glu = gate * jax.nn.sigmoid(SWIGLU_ALPHA * gate)
            a = ((up + 1.0) * glu).astype(BF16)
            y = y + jnp.dot(a, wd_s[ff, :], preferred_element_type=F32)
        _store_token_rows(y_ref, _pack_pairs(y.astype(BF16).astype(F32)))

    @pl.when(i >= nu_ref[0])
    def _():
        y_ref[...] = jnp.zeros(y_ref.shape, y_ref.dtype)


def _experts(x_sorted, block_e, n_used, w_gu, w_dn, b_g, b_u, b_d):
    rows = x_sorted.shape[0] // ROW_CHUNKS
    n_blocks = rows // MOE_BLOCK
    d_gu = w_gu.shape[2]
    d_ff = w_dn.shape[1]
    by_expert = lambda i, be, nu: (be[i], 0, 0)
    xspec = pl.BlockSpec((MOE_BLOCK * ROW_CHUNKS, LANES), lambda i, be, nu: (i, 0))
    grid_spec = pltpu.PrefetchScalarGridSpec(
        num_scalar_prefetch=2, grid=(n_blocks,),
        in_specs=[xspec,
                  pl.BlockSpec((None, D_MODEL, d_gu), by_expert),
                  pl.BlockSpec((None, d_ff, D_MODEL), by_expert),
                  pl.BlockSpec((None, 1, d_ff), by_expert),
                  pl.BlockSpec((None, 1, d_ff), by_expert),
                  pl.BlockSpec((None, 1, D_MODEL), by_expert)],
        out_specs=xspec,
        scratch_shapes=[pltpu.VMEM((D_MODEL // LANES, d_gu, LANES), F32),
                        pltpu.VMEM((d_ff, D_MODEL), BF16), pltpu.VMEM((d_ff, D_MODEL), BF16),
                        pltpu.VMEM((d_ff, D_MODEL), BF16)])
    return pl.pallas_call(
        _expert_kernel,
        out_shape=jax.ShapeDtypeStruct((rows * ROW_CHUNKS, LANES), jnp.uint32),
        grid_spec=grid_spec,
        compiler_params=pltpu.CompilerParams(dimension_semantics=("arbitrary",),
                                             vmem_limit_bytes=EXPERT_VMEM_LIMIT_BYTES),
        name="experts",
    )(block_e, n_used, x_sorted, w_gu, w_dn, b_g, b_u, b_d)


def _combine_kernel(dest_ref, xm_ref, gt_ref, g2p_ref, g2s_ref, yb_ref, yp_ref, ys_ref, rows_ref, sem,
                    *, n_prompt_tiles):
    i = pl.program_id(0)
    tm = xm_ref.shape[0]

    def body(t, carry):
        for k in range(TOP_K):
            d = dest_ref[(i * tm + t) * TOP_K + k]
            pltpu.make_async_copy(yb_ref.at[pl.ds(d * ROW_CHUNKS, ROW_CHUNKS)],
                                  rows_ref.at[k, pl.ds(t * ROW_CHUNKS, ROW_CHUNKS)], sem).start()
        return carry

    lax.fori_loop(0, tm, body, 0, unroll=8)
    pltpu.make_async_copy(rows_ref, rows_ref, sem).wait()
    gt = gt_ref[...]
    acc = jnp.zeros(xm_ref.shape, F32)
    for k in range(TOP_K):
        acc = acc + gt[:, k:k + 1] * _unpack_pairs(_load_token_rows(rows_ref.at[k], tm))
    is_prompt = i < n_prompt_tiles
    y = xm_ref[...] + jnp.where(is_prompt, g2p_ref[0], g2s_ref[0]) * acc

    @pl.when(is_prompt)
    def _():
        yp_ref[...] = y

    @pl.when(jnp.logical_not(is_prompt))
    def _():
        ys_ref[...] = y


def _combine(tiles, dest_flat, xm, gates, gate2_p, gate2_s, y_buf, n_prompt, rows_per_seq):
    tm = tiles.tm
    n_prompt_tiles = n_prompt // tm
    n_sample = tiles.n_tokens - n_prompt
    assert n_sample == tm and gate2_s.shape[1] == tm, "the sample group is one token tile"
    tiles_per_seq = rows_per_seq // tm
    last_seq = gate2_p.shape[0] - 1
    grid_spec = pltpu.PrefetchScalarGridSpec(
        num_scalar_prefetch=1, grid=(tiles.n_tiles,),
        in_specs=[pl.BlockSpec((tm, D_MODEL), lambda i, d: (i, 0)),
                  pl.BlockSpec((tm, LANES), lambda i, d: (i, 0)),
                  pl.BlockSpec((1, 1, D_MODEL), lambda i, d: (jnp.minimum(i // tiles_per_seq, last_seq), 0, 0)),
                  pl.BlockSpec((1, tm, D_MODEL), lambda i, d: (0, 0, 0)),
                  pl.BlockSpec(memory_space=pl.ANY)],
        out_specs=(pl.BlockSpec((tm, D_MODEL), lambda i, d: (jnp.minimum(i, n_prompt_tiles - 1), 0)),
                   pl.BlockSpec((tm, D_MODEL), lambda i, d: (0, 0))),
        scratch_shapes=[pltpu.VMEM((TOP_K, tm * ROW_CHUNKS, LANES), jnp.uint32),
                        pltpu.SemaphoreType.DMA(())])
    return pl.pallas_call(
        functools.partial(_combine_kernel, n_prompt_tiles=n_prompt_tiles),
        out_shape=(jax.ShapeDtypeStruct((n_prompt, D_MODEL), F32),
                   jax.ShapeDtypeStruct((n_sample, D_MODEL), F32)),
        grid_spec=grid_spec,
        compiler_params=_params("arbitrary"),
        name="moe_combine",
    )(dest_flat, xm, gates, gate2_p, gate2_s, y_buf)


class _Tiles:
    def __init__(self, n_tokens, tm):
        self.n_tokens, self.tm, self.n_tiles = n_tokens, tm, n_tokens // tm


def _moe(xm, h2u, idx, gates, gate2_p, gate2_s, moe_w, n_prompt, rows_per_seq):
    rows = _Tiles(xm.shape[0], MOE_TOKEN_TILE)
    a = rows.n_tokens * TOP_K
    n_blocks = -(-a // MOE_BLOCK) + N_EXPERTS
    dest, counts = _route(rows, idx)
    dest_flat = dest[:, :TOP_K].reshape(a)
    cnt = counts[0, :N_EXPERTS].astype(jnp.int32)
    pad_end = jnp.cumsum((cnt + MOE_BLOCK - 1) // MOE_BLOCK * MOE_BLOCK)
    block_row = jnp.arange(n_blocks, dtype=jnp.int32) * MOE_BLOCK
    block_e = jnp.minimum(jnp.sum((pad_end[None, :] <= block_row[:, None]).astype(jnp.int32), axis=1),
                          N_EXPERTS - 1)
    n_used = (pad_end[-1:] // MOE_BLOCK).astype(jnp.int32)
    x_sorted = _dispatch(rows, dest_flat, h2u, n_blocks * MOE_BLOCK)
    y_buf = _experts(x_sorted, block_e, n_used, *moe_w)
    return _combine(rows, dest_flat, xm, gates, gate2_p, gate2_s, y_buf, n_prompt, rows_per_seq)


def _group(rows, x, ada, state_rows, qk_and_attend, shared, total_tokens, first_row, token_bufs):
    (g_mix, w_in_b, w_conv, w_o_b, g_ffn, wr_hi, wr_lo, b_r, mid_dtype) = shared
    shift1, scale1, gate1, shift2, scale2, gate2 = [rows.mod_array(m) for m in jnp.split(ada, 6, axis=-1)]
    v_f, v_b = _v_proj(rows, x, g_mix, scale1, shift1, w_in_b)
    sga, cvp, tail = _conv_proj(rows, x, g_mix, scale1, shift1, w_in_b, w_conv, state_rows, mid_dtype)
    mix, k_out = qk_and_attend(x, scale1, shift1, v_f, v_b, sga, cvp)
    token_bufs = _out_proj(rows, mix, x, w_o_b, gate1, g_ffn, scale2, shift2, wr_hi, wr_lo, b_r,
                           total_tokens, first_row, token_bufs)
    return token_bufs, gate2, k_out, v_f, tail


def kernel(x_prompt, x_sample, c_prompt, c_sample, cache_k, cache_v, state_conv, page_table, w_ada, b_ada, g_norm_mix, w_in, g_q, g_k, lambda_q1, lambda_k1, lambda_q2, lambda_k2, g_subln, w_conv, w_o, g_norm_ffn, w_router, b_router, w_gate_up, b_gate_up, w_down, b_down):
    assert w_in.shape[0] == 1, "single-layer stack"
    batch, seq, _ = x_prompt.shape
    n_seq, n_new, _ = x_sample.shape
    tp, ts = batch * seq, n_seq * n_new
    n_chunks = D_MODEL // HEAD_DIM

    ada = _ada(jnp.concatenate([c_prompt, c_sample], axis=0), w_ada[0], b_ada[0][None])

    w_in_b = w_in[0].astype(BF16)
    w_kt = w_in[0][:, D_MODEL:2 * D_MODEL].T.astype(BF16)
    w_o_b = w_o[0].astype(BF16)
    gqk = jnp.stack([jnp.tile(g_q[0], n_chunks), jnp.tile(g_k[0], n_chunks)])[:, None, :]
    gk_col = jnp.tile(g_k[0], n_chunks)[:, None]
    blk = jnp.arange(D_MODEL, dtype=jnp.int32) // HEAD_DIM
    pmat = (blk[:, None] == blk[None, :]).astype(BF16)
    wr = jnp.pad(w_router[0], ((0, 0), (0, LANES - N_EXPERTS)))
    wr_hi = wr.astype(BF16)
    wr_lo = (wr - wr_hi.astype(F32)).astype(BF16)
    b_r = jnp.pad(b_router[0], (0, LANES - N_EXPERTS), constant_values=NEG_BIG)[None]
    moe_w = (w_gate_up[0], w_down[0], b_gate_up[0][:, None, 0::2], b_gate_up[0][:, None, 1::2],
             b_down[0][:, None, :])
    g_mix = g_norm_mix[0][None]
    g_ffn = g_norm_ffn[0][None]
    g_sub = g_subln[0][None]
    lams = (lambda_q1[0][None], lambda_k1[0][None], lambda_q2[0][None], lambda_k2[0][None])

    def shared(mid_dtype):
        return (g_mix, w_in_b, w_conv[0], w_o_b, g_ffn, wr_hi, wr_lo, b_r, mid_dtype)

    rows_p = _Rows(tp, seq)

    def attend_p(x, scale1, shift1, v_f, v_b, sga, cvp):
        q_b = _q_proj(rows_p, x, g_mix, scale1, shift1, w_in_b, gqk, pmat)
        kt_f, kt_b = _kt_proj(rows_p, x, g_mix, scale1, shift1, w_kt, gk_col, batch, seq)
        mix = _attn_prompt(q_b, kt_b, v_b, sga, cvp, g_sub, lams, batch, seq)
        return mix, kt_f

    bufs, gate2_p, kt_p, v_p, tail_p = _group(rows_p, x_prompt.reshape(tp, D_MODEL), ada[:batch], None, attend_p,
                                              shared(BF16), tp + ts, 0, None)

    rows_s = _Rows(ts, n_new)
    st = state_conv[0]
    zeros = jnp.zeros((n_seq, n_new - 2, D_MODEL), F32)
    s1 = jnp.concatenate([st[:, 1:2], jnp.zeros((n_seq, n_new - 1, D_MODEL), F32)], axis=1).reshape(ts, D_MODEL)
    s2 = jnp.concatenate([st, zeros], axis=1).reshape(ts, D_MODEL)

    def attend_s(x, scale1, shift1, v_f, v_b, sga, cvp):
        _, qk_f = _qk_proj(rows_s, x, g_mix, scale1, shift1, w_in_b, gqk, pmat)
        mix = _attn_sample(qk_f, v_f, cache_k[0], cache_v[0], page_table, sga, cvp, g_sub, lams, n_seq, n_new)
        return mix, qk_f[1]

    bufs, gate2_s, k_s, v_s, tail_s = _group(rows_s, x_sample.reshape(ts, D_MODEL), ada[batch:], (s1, s2), attend_s,
                                             shared(F32), tp + ts, tp, bufs)

    y_p, y_s = _moe(*bufs, gate2_p, gate2_s, moe_w, tp, seq)

    tail_s = tail_s.reshape(n_seq, n_new, D_MODEL)
    k_p = kt_p.reshape(1, batch, N_HEADS, 2, HEAD_DIM, seq).transpose(0, 1, 5, 2, 3, 4)
    return (y_p.reshape(batch, seq, D_MODEL),
            y_s.reshape(n_seq, n_new, D_MODEL),
            k_p,
            v_p.reshape(1, batch, seq, N_HEADS, V_DIM),
            tail_p[:, SUBLANES - (CONV_WIDTH - 1):][None],
            k_s.reshape(1, n_seq, n_new, N_HEADS, 2, HEAD_DIM),
            v_s.reshape(1, n_seq, n_new, N_HEADS, V_DIM),
            tail_s[:, n_new - (CONV_WIDTH - 1):][None])
```

```python
import functools
import math

import jax
import jax.numpy as jnp
from jax import lax
from jax.experimental import pallas as pl
from jax.experimental.pallas import tpu as pltpu

F32 = jnp.float32
BF16 = jnp.bfloat16

D_MODEL = 1024
HEAD_DIM = 64
V_DIM = 2 * HEAD_DIM
N_HEADS = D_MODEL // V_DIM
ATTN_SCALE = HEAD_DIM ** -0.5
LOG2_E = math.log2(math.e)
CONV_WIDTH = 3
PAGE_SIZE = 128
N_EXPERTS = 32
TOP_K = 4
SWIGLU_LIMIT = 7.0
SWIGLU_ALPHA = 1.702
NORM_EPS = 1e-6
LAMBDA_INIT = 0.8 - 0.6 * math.exp(-0.3 * 0)

VMEM_LIMIT_BYTES = 48 * 1024 * 1024
EXPERT_VMEM_LIMIT_BYTES = 56 * 1024 * 1024
LANES = 128
PACKED = D_MODEL // 2
ROW_CHUNKS = PACKED // LANES
SUBLANES = 8

ROW_TILE_PROMPT = 512
ATTN_BLOCK = 512
MOE_BLOCK = 512
MOE_TOKEN_TILE = 256
EXPERT_XPOSE_CHUNK = 512
EXPERT_FF_CHUNK = 512
ADA_COL_TILE = 1536
SAMPLE_PAGES_PER_STEP = 16
NEG_BIG = -1e30

_NT = (((1,), (1,)), ((), ()))


def _params(*sem):
    return pltpu.CompilerParams(dimension_semantics=sem, vmem_limit_bytes=VMEM_LIMIT_BYTES)


def _rms_mod(x, g, scale, shift):
    ms = jnp.mean(x * x, axis=-1, keepdims=True)
    return (x * lax.rsqrt(ms + NORM_EPS) * g) * (1.0 + scale) + shift


def _ada_kernel(c_ref, w_ref, b_ref, o_ref):
    c = c_ref[...]
    s = (c * jax.nn.sigmoid(c)).astype(BF16)
    o_ref[...] = jnp.dot(s, w_ref[...].astype(BF16), preferred_element_type=F32) + b_ref[...]


def _ada(c_all, w_ada, b_ada):
    n = c_all.shape[0]
    width = w_ada.shape[1]
    return pl.pallas_call(
        _ada_kernel,
        out_shape=jax.ShapeDtypeStruct((n, width), F32),
        grid=(width // ADA_COL_TILE,),
        in_specs=[pl.BlockSpec((n, D_MODEL), lambda j: (0, 0)),
                  pl.BlockSpec((D_MODEL, ADA_COL_TILE), lambda j: (0, j)),
                  pl.BlockSpec((1, ADA_COL_TILE), lambda j: (0, j))],
        out_specs=pl.BlockSpec((n, ADA_COL_TILE), lambda j: (0, j)),
        compiler_params=_params("arbitrary"),
        name="ada",
    )(c_all, w_ada, b_ada)


class _Rows:
    def __init__(self, n_tokens, rows_per_batch):
        if rows_per_batch >= ROW_TILE_PROMPT:
            self.tm = ROW_TILE_PROMPT
            self.tiles_per_batch = rows_per_batch // self.tm
            self.mod_rows = 1
        else:
            self.tm = n_tokens
            self.tiles_per_batch = None
            self.mod_rows = n_tokens
        self.n_tokens = n_tokens
        self.rows_per_batch = rows_per_batch
        self.n_tiles = n_tokens // self.tm

    def mod_array(self, m):
        if self.tiles_per_batch is not None:
            return m[:, None, :]
        return jnp.repeat(m, self.rows_per_batch, axis=0)[None]

    def mod_spec(self, clamp=False):
        if self.tiles_per_batch is not None:
            tpb, last = self.tiles_per_batch, self.n_tiles - 1
            if clamp:
                return pl.BlockSpec((1, 1, D_MODEL), lambda i, *_: (jnp.minimum(i, last) // tpb, 0, 0))
            return pl.BlockSpec((1, 1, D_MODEL), lambda i, *_: (i // tpb, 0, 0))
        return pl.BlockSpec((1, self.mod_rows, D_MODEL), lambda i, *_: (0, 0, 0))

    def row_spec(self, width=D_MODEL, clamp=False):
        if clamp:
            last = self.n_tiles - 1
            return pl.BlockSpec((self.tm, width), lambda i, *_: (jnp.minimum(i, last), 0))
        return pl.BlockSpec((self.tm, width), lambda i, *_: (i, 0))


def _qk_kernel(x_ref, g_ref, sc_ref, sh_ref, w_ref, gqk_ref, p_ref, qkb_ref, qkf_ref, h_ref):
    j = pl.program_id(1)

    @pl.when(j == 0)
    def _():
        h_ref[...] = _rms_mod(x_ref[...], g_ref[...], sc_ref[0], sh_ref[0]).astype(BF16)

    z = jnp.dot(h_ref[...], w_ref[...], preferred_element_type=F32)
    ss = jnp.dot((z * z).astype(BF16), p_ref[...], preferred_element_type=F32)
    zn = z * lax.rsqrt(ss * (1.0 / HEAD_DIM) + NORM_EPS) * gqk_ref[0]
    qkf_ref[0] = zn
    scale = jnp.where(j == 0, ATTN_SCALE, 1.0)
    qkb_ref[...] = (zn * scale).astype(BF16)


def _qk_proj(rows, x, g, scale, shift, w_in_b, gqk, pmat):
    t = rows.n_tokens
    return pl.pallas_call(
        _qk_kernel,
        out_shape=(jax.ShapeDtypeStruct((t, 2 * D_MODEL), BF16),
                   jax.ShapeDtypeStruct((2, t, D_MODEL), F32)),
        grid=(rows.n_tiles, 2),
        in_specs=[rows.row_spec(),
                  pl.BlockSpec((1, D_MODEL), lambda i, j: (0, 0)),
                  rows.mod_spec(), rows.mod_spec(),
                  pl.BlockSpec((D_MODEL, D_MODEL), lambda i, j: (0, j)),
                  pl.BlockSpec((1, 1, D_MODEL), lambda i, j: (j, 0, 0)),
                  pl.BlockSpec((D_MODEL, D_MODEL), lambda i, j: (0, 0))],
        out_specs=(pl.BlockSpec((rows.tm, D_MODEL), lambda i, j: (i, j)),
                   pl.BlockSpec((1, rows.tm, D_MODEL), lambda i, j: (j, i, 0))),
        scratch_shapes=[pltpu.VMEM((rows.tm, D_MODEL), BF16)],
        compiler_params=_params("parallel", "arbitrary"),
        name="qk_proj",
    )(x, g, scale, shift, w_in_b, gqk, pmat)


def _qkt_kernel(x_ref, g_ref, sc_ref, sh_ref, wq_ref, gq_ref, p_ref, wt_ref, gk_ref, qb_ref, kf_ref, kb_ref):
    h = _rms_mod(x_ref[...], g_ref[...], sc_ref[0], sh_ref[0]).astype(BF16)
    z = jnp.dot(h, wq_ref[...], preferred_element_type=F32)
    ss = jnp.dot((z * z).astype(BF16), p_ref[...], preferred_element_type=F32)
    zn = z * lax.rsqrt(ss * (1.0 / HEAD_DIM) + NORM_EPS) * gq_ref[0]
    qb_ref[...] = (zn * (ATTN_SCALE * LOG2_E)).astype(BF16)
    zt = lax.dot_general(wt_ref[...], h, _NT, preferred_element_type=F32)
    tm = zt.shape[1]
    z3 = zt.reshape(D_MODEL // HEAD_DIM, HEAD_DIM, tm)
    s3 = jnp.sum(z3 * z3, axis=1, keepdims=True)
    g3 = gk_ref[...].reshape(D_MODEL // HEAD_DIM, HEAD_DIM, 1)
    kn = (z3 * lax.rsqrt(s3 * (1.0 / HEAD_DIM) + NORM_EPS) * g3).reshape(D_MODEL, tm)
    kf_ref[...] = kn
    kb_ref[...] = kn.astype(BF16)


def _qkt_proj(rows, x, g, scale, shift, w_in_b, gqk, pmat, w_kt, gk_col, batch, seq):
    tpb = rows.tiles_per_batch
    once = pl.Buffered(1)
    kt_spec = pl.BlockSpec((None, D_MODEL, rows.tm), lambda i: (i // tpb, 0, i % tpb))
    return pl.pallas_call(
        _qkt_kernel,
        out_shape=(jax.ShapeDtypeStruct((rows.n_tokens, D_MODEL), BF16),
                   jax.ShapeDtypeStruct((batch, D_MODEL, seq), F32),
                   jax.ShapeDtypeStruct((batch, D_MODEL, seq), BF16)),
        grid=(rows.n_tiles,),
        in_specs=[rows.row_spec(),
                  pl.BlockSpec((1, D_MODEL), lambda i: (0, 0)),
                  rows.mod_spec(), rows.mod_spec(),
                  pl.BlockSpec((D_MODEL, D_MODEL), lambda i: (0, 0), pipeline_mode=once),
                  pl.BlockSpec((1, 1, D_MODEL), lambda i: (0, 0, 0)),
                  pl.BlockSpec((D_MODEL, D_MODEL), lambda i: (0, 0), pipeline_mode=once),
                  pl.BlockSpec((D_MODEL, D_MODEL), lambda i: (0, 0), pipeline_mode=once),
                  pl.BlockSpec((D_MODEL, 1), lambda i: (0, 0))],
        out_specs=(rows.row_spec(), kt_spec, kt_spec),
        compiler_params=_params("parallel"),
        name="qkt_proj",
    )(x, g, scale, shift, w_in_b, gqk, pmat, w_kt, gk_col)


def _vconv_kernel(*refs, tiles_per_batch, rows_per_batch, tail_rows):
    if tiles_per_batch is None:
        (x_ref, g_ref, sc_ref, sh_ref, w_ref, wc_ref, s1_ref, s2_ref,
         vf_ref, vb_ref, sga_ref, cvp_ref, tail_ref, carry_ref) = refs
    else:
        (x_ref, g_ref, sc_ref, sh_ref, w_ref, wc_ref,
         vf_ref, vb_ref, sga_ref, cvp_ref, tail_ref, carry_ref) = refs
    i = pl.program_id(0)
    h = _rms_mod(x_ref[...], g_ref[...], sc_ref[0], sh_ref[0]).astype(BF16)

    def proj(k):
        return jnp.dot(h, w_ref[:, k * D_MODEL:(k + 1) * D_MODEL], preferred_element_type=F32)

    v = proj(0)
    vf_ref[...] = v
    vb_ref[...] = v.astype(BF16)
    sga_ref[...] = jax.nn.sigmoid(proj(4)).astype(sga_ref.dtype)
    u = proj(2) * proj(3)
    tm = u.shape[0]
    row = lax.broadcasted_iota(jnp.int32, (tm, 1), 0)
    r1 = pltpu.roll(u, 1, 0)
    r2 = pltpu.roll(u, 2, 0)
    if tiles_per_batch is None:
        t = row & (rows_per_batch - 1)
        u1 = jnp.where(t >= 1, r1, s1_ref[...])
        u2 = jnp.where(t >= 2, r2, s2_ref[...])
    else:
        first = (i % tiles_per_batch) == 0
        c = jnp.where(first, 0.0, carry_ref[...])
        u1 = jnp.where(row == 0, c[7:8], r1)
        u2 = jnp.where(row == 0, c[6:7], jnp.where(row == 1, c[7:8], r2))
        carry_ref[...] = u[tm - SUBLANES:]
    tail_ref[0] = u[tm - tail_rows:]
    wc = wc_ref[...]
    yc = wc[0:1] * u2 + wc[1:2] * u1 + wc[2:3] * u
    cvp_ref[...] = (jax.nn.sigmoid(proj(5)) * (proj(1) * yc)).astype(cvp_ref.dtype)


def _vconv_proj(rows, x, g, scale, shift, w_vconv, w_conv, state_rows, out_dtype):
    t = rows.n_tokens
    sample_mode = rows.tiles_per_batch is None
    tail_rows = rows.tm if sample_mode else SUBLANES
    n_tail_blocks = 1 if sample_mode else t // rows.rows_per_batch
    in_specs = [rows.row_spec(),
                pl.BlockSpec((1, D_MODEL), lambda i: (0, 0)),
                rows.mod_spec(), rows.mod_spec(),
                pl.BlockSpec(w_vconv.shape, lambda i: (0, 0), pipeline_mode=pl.Buffered(1)),
                pl.BlockSpec((CONV_WIDTH, D_MODEL), lambda i: (0, 0))]
    args = [x, g, scale, shift, w_vconv, w_conv]
    if sample_mode:
        in_specs += [rows.row_spec(), rows.row_spec()]
        args += list(state_rows)
        tail_spec = pl.BlockSpec((1, tail_rows, D_MODEL), lambda i: (0, 0, 0))
    else:
        tpb = rows.tiles_per_batch
        tail_spec = pl.BlockSpec((1, tail_rows, D_MODEL), lambda i: (i // tpb, 0, 0))
    kern = functools.partial(_vconv_kernel, tiles_per_batch=rows.tiles_per_batch,
                             rows_per_batch=rows.rows_per_batch, tail_rows=tail_rows)
    return pl.pallas_call(
        kern,
        out_shape=(jax.ShapeDtypeStruct((t, D_MODEL), F32),
                   jax.ShapeDtypeStruct((t, D_MODEL), BF16),
                   jax.ShapeDtypeStruct((t, D_MODEL), out_dtype),
                   jax.ShapeDtypeStruct((t, D_MODEL), out_dtype),
                   jax.ShapeDtypeStruct((n_tail_blocks, tail_rows, D_MODEL), F32)),
        grid=(rows.n_tiles,),
        in_specs=in_specs,
        out_specs=(rows.row_spec(), rows.row_spec(), rows.row_spec(), rows.row_spec(), tail_spec),
        scratch_shapes=[pltpu.VMEM((SUBLANES, D_MODEL), F32)],
        compiler_params=_params("arbitrary"),
        name="vconv_proj",
    )(*args)


def _lambda_value(lq1, lk1, lq2, lk2):
    e1 = jnp.exp(jnp.sum(lq1 * lk1, axis=-1, keepdims=True))
    e2 = jnp.exp(jnp.sum(lq2 * lk2, axis=-1, keepdims=True))
    return e1 - e2 + LAMBDA_INIT


def _subln_mix(o, g_sub, sga, cvp):
    ms = jnp.mean(o * o, axis=-1, keepdims=True)
    attn = (o * lax.rsqrt(ms + NORM_EPS) * g_sub) * (1.0 - LAMBDA_INIT)
    return sga * attn + cvp


def _flash_update(s, v_ones, m_ref, l_ref, acc_ref):
    m_prev = m_ref[...]
    m_next = jnp.maximum(m_prev, jnp.max(s, axis=1, keepdims=True))
    alpha = jnp.exp2(m_prev - m_next)
    p = jnp.exp2(s - jnp.concatenate([m_next] * (s.shape[1] // LANES), axis=1)).astype(BF16)
    pv = jnp.dot(p, v_ones, preferred_element_type=F32)
    acc_ref[...] = alpha * acc_ref[...] + pv[:, :V_DIM]
    l_ref[...] = alpha * l_ref[...] + pv[:, V_DIM:]
    m_ref[...] = m_next


def _attn_prompt_kernel(qi_ref, ki_ref, q_ref, kt_ref, v_ref, sga_ref, cvp_ref, gs_ref,
                        lq1_ref, lk1_ref, lq2_ref, lk2_ref,
                        o_ref, m0_ref, l0_ref, a0_ref, m1_ref, l1_ref, a1_ref):
    qi = qi_ref[pl.program_id(1)]
    ki = ki_ref[pl.program_id(1)]
    states = ((m0_ref, l0_ref, a0_ref), (m1_ref, l1_ref, a1_ref))

    @pl.when(ki == 0)
    def _():
        for m_ref, l_ref, a_ref in states:
            m_ref[...] = jnp.full(m_ref.shape, -jnp.inf, F32)
            l_ref[...] = jnp.zeros(l_ref.shape, F32)
            a_ref[...] = jnp.zeros(a_ref.shape, F32)

    tq, tk = q_ref.shape[0], kt_ref.shape[1]

    def step(masked):
        lane = lax.broadcasted_iota(jnp.int32, (1, V_DIM), 1)
        zero = jnp.zeros((), BF16)
        ones = jnp.ones((tk, V_DIM), BF16)
        if masked:
            r = lax.broadcasted_iota(jnp.int32, (tq, tk), 0)
            c = lax.broadcasted_iota(jnp.int32, (tq, tk), 1)
            keep = c <= r
        for h in range(N_HEADS):
            cols = slice(h * V_DIM, (h + 1) * V_DIM)
            q = q_ref[:, cols]
            kt = kt_ref[cols, :]
            v_ones = jnp.concatenate([v_ref[:, cols], ones], axis=1)
            qs = (jnp.where(lane < HEAD_DIM, q, zero), jnp.where(lane >= HEAD_DIM, q, zero))
            for qm, (m_ref, l_ref, a_ref) in zip(qs, states):
                s = jnp.dot(qm, kt, preferred_element_type=F32)
                if masked:
                    s = jnp.where(keep, s, -jnp.inf)
                _flash_update(s, v_ones, m_ref.at[h], l_ref.at[h], a_ref.at[h])

    @pl.when(ki < qi)
    def _():
        step(False)

    @pl.when(ki == qi)
    def _():
        step(True)
        lam = _lambda_value(lq1_ref[...], lk1_ref[...], lq2_ref[...], lk2_ref[...])
        for h in range(N_HEADS):
            cols = slice(h * V_DIM, (h + 1) * V_DIM)
            o = a0_ref[h] / l0_ref[h] - lam * (a1_ref[h] / l1_ref[h])
            mix = _subln_mix(o, gs_ref[...], sga_ref[:, cols].astype(F32), cvp_ref[:, cols].astype(F32))
            o_ref[:, cols] = mix.astype(o_ref.dtype)


def _attn_prompt(q_b, kt_b, v_b, sga, cvp, g_sub, lams, batch, seq):
    nb = seq // ATTN_BLOCK
    tq = ATTN_BLOCK
    pairs = [(qi, ki) for qi in range(nb) for ki in range(qi + 1)]
    qi_tab = jnp.array([p[0] for p in pairs], jnp.int32)
    ki_tab = jnp.array([p[1] for p in pairs], jnp.int32)
    q_spec = pl.BlockSpec((tq, D_MODEL), lambda b, p, qt, kt: (b * nb + qt[p], 0))
    k_spec = pl.BlockSpec((None, D_MODEL, tq), lambda b, p, qt, kt: (b, 0, kt[p]))
    v_spec = pl.BlockSpec((tq, D_MODEL), lambda b, p, qt, kt: (b * nb + kt[p], 0))
    vec64 = pl.BlockSpec((1, HEAD_DIM), lambda b, p, qt, kt: (0, 0))
    grid_spec = pltpu.PrefetchScalarGridSpec(
        num_scalar_prefetch=2,
        grid=(batch, len(pairs)),
        in_specs=[q_spec, k_spec, v_spec, q_spec, q_spec,
                  pl.BlockSpec((1, V_DIM), lambda b, p, qt, kt: (0, 0)),
                  vec64, vec64, vec64, vec64],
        out_specs=q_spec,
        scratch_shapes=[pltpu.VMEM((N_HEADS, tq, V_DIM), F32)] * 6)
    return pl.pallas_call(
        _attn_prompt_kernel,
        out_shape=jax.ShapeDtypeStruct((batch * seq, D_MODEL), BF16),
        grid_spec=grid_spec,
        compiler_params=_params("parallel", "arbitrary"),
        name="attn_prompt",
    )(qi_tab, ki_tab, q_b, kt_b, v_b, sga, cvp, g_sub, *lams)


def _attn_sample_kernel(pt_ref, q_ref, kn_ref, vn_ref, *rest, n_new, n_group):
    kc_refs = rest[:n_group]
    vc_refs = rest[n_group:2 * n_group]
    (spread_ref, own_ref, sga_ref, cvp_ref, gs_ref, lq1_ref, lk1_ref, lq2_ref, lk2_ref,
     o_ref, qbd_ref, m_ref, l_ref, acc_ref) = rest[2 * n_group:]
    p = pl.program_id(1)
    rows_per_head = 2 * n_new

    @pl.when(p == 0)
    def _():
        q = q_ref[...] * ATTN_SCALE
        qrep = jnp.concatenate([q] * (N_HEADS * 2), axis=0)
        r = lax.broadcasted_iota(jnp.int32, qrep.shape, 0)
        c = lax.broadcasted_iota(jnp.int32, qrep.shape, 1)
        qbd_ref[...] = jnp.where(c // HEAD_DIM == r // n_new, qrep, 0.0).astype(BF16)
        m_ref[...] = jnp.full(m_ref.shape, -jnp.inf, F32)
        l_ref[...] = jnp.zeros(l_ref.shape, F32)
        acc_ref[...] = jnp.zeros(acc_ref.shape, F32)

    def update(s, pv_of):
        m_prev = m_ref[...]
        m_next = jnp.maximum(m_prev, jnp.max(s, axis=1, keepdims=True))
        alpha = jnp.exp(m_prev - m_next)
        width = s.shape[1]
        m_wide = m_next[:, :width] if width <= LANES else jnp.concatenate([m_next] * (width // LANES), axis=1)
        pr = jnp.exp(s - m_wide)
        l_ref[...] = alpha * l_ref[...] + jnp.sum(pr, axis=1, keepdims=True)
        acc_ref[...] = alpha * acc_ref[...] + pv_of(pr.astype(BF16))
        m_ref[...] = m_next

    def page_pv(pb, v_ref):
        pe = jnp.dot(pb, spread_ref[...], preferred_element_type=F32).astype(BF16) * own_ref[...]
        v2 = v_ref[...].reshape(PAGE_SIZE * N_HEADS, V_DIM).astype(BF16)
        return jnp.dot(pe, v2, preferred_element_type=F32)

    qbd = qbd_ref[...]
    s = jnp.concatenate([jnp.dot(qbd, kc_refs[g][...].astype(BF16), preferred_element_type=F32)
                         for g in range(n_group)], axis=1)

    def pages_pv(pb):
        out = page_pv(pb[:, :PAGE_SIZE], vc_refs[0])
        for g in range(1, n_group):
            out = out + page_pv(pb[:, g * PAGE_SIZE:(g + 1) * PAGE_SIZE], vc_refs[g])
        return out

    update(s, pages_pv)

    @pl.when(p == pl.num_programs(1) - 1)
    def _():
        s_new = lax.dot_general(qbd_ref[...], kn_ref[...].astype(BF16), _NT, preferred_element_type=F32)
        r = lax.broadcasted_iota(jnp.int32, s_new.shape, 0)
        c = lax.broadcasted_iota(jnp.int32, s_new.shape, 1)
        s_new = jnp.where(c <= (r & (n_new - 1)), s_new, -jnp.inf)
        def new_pv(pb):
            return jnp.concatenate(
                [jnp.dot(pb[h * rows_per_head:(h + 1) * rows_per_head].astype(F32),
                         vn_ref[:, h * V_DIM:(h + 1) * V_DIM].astype(BF16).astype(F32),
                         preferred_element_type=F32) for h in range(N_HEADS)], axis=0)

        update(s_new, new_pv)
        lam = _lambda_value(lq1_ref[...], lk1_ref[...], lq2_ref[...], lk2_ref[...])
        acc = acc_ref[...] / l_ref[...]
        for h in range(N_HEADS):
            cols = slice(h * V_DIM, (h + 1) * V_DIM)
            r0 = h * rows_per_head
            o = acc[r0:r0 + n_new] - lam * acc[r0 + n_new:r0 + rows_per_head]
            o_ref[:, cols] = _subln_mix(o, gs_ref[...], sga_ref[:, cols], cvp_ref[:, cols])


def _attn_sample(qkf, vf, cache_k, cache_v, page_table, sga, cvp, g_sub, lams, n_seq, n_new):
    n_pages = page_table.shape[1]
    n_pool = cache_k.shape[0]
    width = N_HEADS * V_DIM
    grp = SAMPLE_PAGES_PER_STEP
    kc = jnp.transpose(cache_k, (0, 2, 3, 4, 1)).reshape(n_pool, width, PAGE_SIZE)
    pt = page_table.reshape(-1)
    n_rows = N_HEADS * 2 * n_new
    row = pl.BlockSpec((n_new, width), lambda b, p, pt: (b, 0))

    def page_index(g):
        return lambda b, p, pt: (pt[b * n_pages + p * grp + g], 0, 0)

    def page_index4(g):
        return lambda b, p, pt: (pt[b * n_pages + p * grp + g], 0, 0, 0)

    k_pages = [pl.BlockSpec((None, width, PAGE_SIZE), page_index(g)) for g in range(grp)]
    v_pages = [pl.BlockSpec((None, PAGE_SIZE, N_HEADS, V_DIM), page_index4(g)) for g in range(grp)]
    vec64 = pl.BlockSpec((1, HEAD_DIM), lambda b, p, pt: (0, 0))
    col = jnp.arange(PAGE_SIZE * N_HEADS, dtype=jnp.int32)
    spread = (col[None, :] // N_HEADS == jnp.arange(PAGE_SIZE, dtype=jnp.int32)[:, None]).astype(BF16)
    own = (col[None, :] % N_HEADS == jnp.arange(n_rows, dtype=jnp.int32)[:, None] // (2 * n_new)).astype(BF16)
    const = lambda shape: pl.BlockSpec(shape, lambda b, p, pt: (0, 0))
    grid_spec = pltpu.PrefetchScalarGridSpec(
        num_scalar_prefetch=1,
        grid=(n_seq, n_pages // grp),
        in_specs=[pl.BlockSpec((None, n_new, width), lambda b, p, pt: (0, b, 0)),
                  pl.BlockSpec((None, n_new, width), lambda b, p, pt: (1, b, 0)),
                  row, *k_pages, *v_pages, const(spread.shape), const(own.shape), row, row,
                  pl.BlockSpec((1, V_DIM), lambda b, p, pt: (0, 0)),
                  vec64, vec64, vec64, vec64],
        out_specs=row,
        scratch_shapes=[pltpu.VMEM((n_rows, width), BF16),
                        pltpu.VMEM((n_rows, V_DIM), F32), pltpu.VMEM((n_rows, V_DIM), F32),
                        pltpu.VMEM((n_rows, V_DIM), F32)])
    return pl.pallas_call(
        functools.partial(_attn_sample_kernel, n_new=n_new, n_group=grp),
        out_shape=jax.ShapeDtypeStruct((n_seq * n_new, width), F32),
        grid_spec=grid_spec,
        compiler_params=_params("parallel", "arbitrary"),
        name="attn_sample",
    )(pt, qkf, qkf, vf, *([kc] * grp), *([cache_v] * grp), spread, own, sga, cvp, g_sub, *lams)


def _out_kernel(mix_ref, x_ref, wo_ref, g1_ref, g_ref, sc_ref, sh_ref, wrh_ref, wrl_ref, br_ref, *refs, n_real_tiles):
    outs = refs[-4:]
    i = pl.program_id(0)

    @pl.when(i < n_real_tiles)
    def _():
        _out_tile(mix_ref, x_ref, wo_ref, g1_ref, g_ref, sc_ref, sh_ref, wrh_ref, wrl_ref, br_ref, *outs)

    @pl.when(i >= n_real_tiles)
    def _():
        for r in outs:
            r[...] = jnp.zeros(r.shape, r.dtype)


def _out_tile(mix_ref, x_ref, wo_ref, g1_ref, g_ref, sc_ref, sh_ref, wrh_ref, wrl_ref, br_ref,
              xm_ref, h2_ref, idx_ref, gt_ref):
    y = jnp.dot(mix_ref[...].astype(BF16), wo_ref[...], preferred_element_type=F32)
    xm = x_ref[...] + g1_ref[0] * y
    xm_ref[...] = xm
    h2 = _rms_mod(xm, g_ref[...], sc_ref[0], sh_ref[0])
    hi = h2.astype(BF16)
    _store_token_rows(h2_ref, _pack_pairs(hi.astype(F32)))
    lo = (h2 - hi.astype(F32)).astype(BF16)
    logits = (jnp.dot(hi, wrh_ref[...], preferred_element_type=F32)
              + jnp.dot(lo, wrh_ref[...], preferred_element_type=F32)
              + jnp.dot(hi, wrl_ref[...], preferred_element_type=F32)) + br_ref[...]
    lane = lax.broadcasted_iota(jnp.int32, logits.shape, 1).astype(F32)
    vals, idxs = [], []
    for _ in range(TOP_K):
        m = jnp.max(logits, axis=-1, keepdims=True)
        ix = jnp.min(jnp.where(logits == m, lane, float(LANES)), axis=-1, keepdims=True)
        logits = jnp.where(lane == ix, -jnp.inf, logits)
        vals.append(m)
        idxs.append(ix)
    es = [jnp.exp(v - vals[0]) for v in vals]
    denom = es[0] + es[1] + es[2] + es[3]
    idx_out = jnp.zeros(logits.shape, F32)
    gt_out = jnp.zeros(logits.shape, F32)
    for k in range(TOP_K):
        idx_out = jnp.where(lane == float(k), idxs[k], idx_out)
        gt_out = jnp.where(lane == float(k), es[k] / denom, gt_out)
    idx_ref[...] = idx_out.astype(jnp.int32)
    gt_ref[...] = gt_out


def _out_proj(rows, mix, x, w_o_b, gate1, g, scale, shift, wr_hi, wr_lo, b_r, total_tokens, first_row, shared_bufs):
    full = lambda shape: pl.BlockSpec(shape, lambda i: (0,) * len(shape))
    off = first_row // rows.tm
    creates = shared_bufs is None
    out_row = lambda width: pl.BlockSpec((rows.tm, width), lambda i: (i + off, 0))
    packed_rows = pl.BlockSpec((rows.tm * ROW_CHUNKS, LANES), lambda i: (i + off, 0))
    in_specs = [rows.row_spec(clamp=creates), rows.row_spec(clamp=creates), full((D_MODEL, D_MODEL)),
                rows.mod_spec(clamp=creates), full((1, D_MODEL)), rows.mod_spec(clamp=creates),
                rows.mod_spec(clamp=creates),
                full((D_MODEL, LANES)), full((D_MODEL, LANES)), full((1, LANES))]
    args = [mix, x, w_o_b, gate1, g, scale, shift, wr_hi, wr_lo, b_r]
    aliases = {}
    if not creates:
        aliases = {len(args) + j: j for j in range(len(shared_bufs))}
        in_specs += [pl.BlockSpec(memory_space=pl.ANY)] * len(shared_bufs)
        args += list(shared_bufs)
    extra = 1 if creates and total_tokens > rows.n_tokens else 0
    assert total_tokens - rows.n_tokens <= rows.tm or not creates, "other group must fit the one extra tile"
    return pl.pallas_call(
        functools.partial(_out_kernel, n_real_tiles=rows.n_tiles),
        out_shape=(jax.ShapeDtypeStruct((total_tokens, D_MODEL), F32),
                   jax.ShapeDtypeStruct((total_tokens * ROW_CHUNKS, LANES), jnp.uint32),
                   jax.ShapeDtypeStruct((total_tokens, LANES), jnp.int32),
                   jax.ShapeDtypeStruct((total_tokens, LANES), F32)),
        grid=(rows.n_tiles + extra,),
        in_specs=in_specs,
        out_specs=(out_row(D_MODEL), packed_rows, out_row(LANES), out_row(LANES)),
        input_output_aliases=aliases,
        compiler_params=_params("arbitrary"),
        name="out_proj_router",
    )(*args)


def _pack_pairs(x):
    bits = pltpu.bitcast(x, jnp.uint32)
    n = x.shape[1] // 2
    return (bits[:, :n] >> 16) | (bits[:, n:] & jnp.uint32(0xFFFF0000))


def _unpack_pairs(w):
    lo = pltpu.bitcast(w << 16, F32)
    hi = pltpu.bitcast(w & jnp.uint32(0xFFFF0000), F32)
    return jnp.concatenate([lo, hi], axis=1)


def _store_token_rows(ref, packed):
    n = packed.shape[0]
    for c in range(ROW_CHUNKS):
        ref[pl.ds(c, n, stride=ROW_CHUNKS), :] = packed[:, c * LANES:(c + 1) * LANES]


def _load_token_rows(ref, n):
    return jnp.concatenate([ref[pl.ds(c, n, stride=ROW_CHUNKS), :] for c in range(ROW_CHUNKS)], axis=1)


def _route_kernel(idx_ref, dest_ref, cnt_ref, run_ref, start_ref):
    ph = pl.program_id(0)
    i = pl.program_id(1)
    idx = idx_ref[...]
    tm = idx.shape[0]
    lane = lax.broadcasted_iota(jnp.int32, idx.shape, 1)
    onehots = [(lane == idx[:, k:k + 1]).astype(F32) for k in range(TOP_K)]
    member = onehots[0] + onehots[1] + onehots[2] + onehots[3]
    tile_count = jnp.sum(member, axis=0, keepdims=True)

    @pl.when((ph == 0) & (i == 0))
    def _():
        cnt_ref[...] = jnp.zeros(cnt_ref.shape, F32)

    @pl.when(ph == 0)
    def _():
        cnt_ref[...] = cnt_ref[...] + tile_count

    @pl.when((ph == 1) & (i == 0))
    def _():
        cnt = cnt_ref[...]
        padded = jnp.floor((cnt + (MOE_BLOCK - 1)) * (1.0 / MOE_BLOCK)) * MOE_BLOCK
        l1 = lax.broadcasted_iota(jnp.int32, cnt.shape, 1)
        incl = padded
        for s in (1, 2, 4, 8, 16, 32, 64):
            incl = incl + jnp.where(l1 >= s, pltpu.roll(incl, s, 1), 0.0)
        start_ref[...] = incl - padded
        run_ref[...] = jnp.zeros(run_ref.shape, F32)

    @pl.when(ph == 1)
    def _():
        r = lax.broadcasted_iota(jnp.int32, (tm, tm), 0)
        c = lax.broadcasted_iota(jnp.int32, (tm, tm), 1)
        earlier = (c < r).astype(BF16)
        before = jnp.dot(earlier, member.astype(BF16), preferred_element_type=F32)
        base = before + run_ref[0:1] + start_ref[0:1]
        out = jnp.zeros(idx.shape, F32)
        for k in range(TOP_K):
            d = jnp.sum(onehots[k] * base, axis=1, keepdims=True)
            out = jnp.where(lane == k, d, out)
        dest_ref[...] = out.astype(jnp.int32)
        run_ref[...] = run_ref[...] + tile_count


def _route(rows, idx):
    tm = rows.tm
    dest, counts = pl.pallas_call(
        _route_kernel,
        out_shape=(jax.ShapeDtypeStruct((rows.n_tokens, LANES), jnp.int32),
                   jax.ShapeDtypeStruct((SUBLANES, LANES), F32)),
        grid=(2, rows.n_tiles),
        in_specs=[pl.BlockSpec((tm, LANES), lambda ph, i: (i, 0))],
        out_specs=(pl.BlockSpec((tm, LANES), lambda ph, i: (i * ph, 0)),
                   pl.BlockSpec((SUBLANES, LANES), lambda ph, i: (0, 0))),
        scratch_shapes=[pltpu.VMEM((SUBLANES, LANES), F32), pltpu.VMEM((SUBLANES, LANES), F32)],
        compiler_params=_params("arbitrary", "arbitrary"),
        name="moe_route",
    )(idx)
    return dest, counts


def _dispatch_kernel(dest_ref, h_ref, zero_ref, xs_ref, sem):
    del zero_ref
    i = pl.program_id(0)
    tm = h_ref.shape[0] // ROW_CHUNKS

    def body(t, carry):
        for k in range(TOP_K):
            d = dest_ref[(i * tm + t) * TOP_K + k]
            pltpu.make_async_copy(h_ref.at[pl.ds(t * ROW_CHUNKS, ROW_CHUNKS)],
                                  xs_ref.at[pl.ds(d * ROW_CHUNKS, ROW_CHUNKS)], sem).start()
        return carry

    lax.fori_loop(0, tm, body, 0, unroll=8)
    n_words = tm * TOP_K * ROW_CHUNKS
    pltpu.make_async_copy(xs_ref.at[pl.ds(0, n_words)], xs_ref.at[pl.ds(0, n_words)], sem).wait()


def _dispatch(rows, dest_flat, h2u, n_rows):
    tm = rows.tm
    zeros = jnp.zeros((n_rows * ROW_CHUNKS, LANES), jnp.uint32)
    grid_spec = pltpu.PrefetchScalarGridSpec(
        num_scalar_prefetch=1, grid=(rows.n_tiles,),
        in_specs=[pl.BlockSpec((tm * ROW_CHUNKS, LANES), lambda i, d: (i, 0)),
                  pl.BlockSpec(memory_space=pl.ANY)],
        out_specs=pl.BlockSpec(memory_space=pl.ANY),
        scratch_shapes=[pltpu.SemaphoreType.DMA(())])
    return pl.pallas_call(
        _dispatch_kernel,
        out_shape=jax.ShapeDtypeStruct((n_rows * ROW_CHUNKS, LANES), jnp.uint32),
        grid_spec=grid_spec,
        input_output_aliases={2: 0},
        compiler_params=_params("arbitrary"),
        name="moe_dispatch",
    )(dest_flat, h2u, zeros)


def _expert_kernel(be_ref, nu_ref, x_ref, wgu_ref, wd_ref, bg_ref, bu_ref, bd_ref, y_ref,
                   wt_s, wg_s, wu_s, wd_s):
    i = pl.program_id(0)
    used = i < nu_ref[0]
    new_expert = (i == 0) | (be_ref[i] != be_ref[jnp.maximum(i - 1, 0)])

    @pl.when(used & new_expert)
    def _():
        n = wgu_ref.shape[1]
        n_lane_tiles = wt_s.shape[0]
        for c in range(n // EXPERT_XPOSE_CHUNK):
            cols = slice(c * EXPERT_XPOSE_CHUNK, (c + 1) * EXPERT_XPOSE_CHUNK)
            wt = wgu_ref[:, cols].T
            for j in range(n_lane_tiles):
                wt_s[j, cols, :] = wt[:, j * LANES:(j + 1) * LANES]
        for j in range(n_lane_tiles):
            lanes = slice(j * LANES, (j + 1) * LANES)
            wg_s[:, lanes] = wt_s[j, pl.ds(0, n // 2, stride=2), :].astype(BF16)
            wu_s[:, lanes] = wt_s[j, pl.ds(1, n // 2, stride=2), :].astype(BF16)
        wd_s[...] = wd_ref[...].astype(BF16)

    @pl.when(used)
    def _():
        x = _unpack_pairs(_load_token_rows(x_ref, MOE_BLOCK)).astype(BF16)
        y = bd_ref[...]
        for c in range(wd_s.shape[0] // EXPERT_FF_CHUNK):
            ff = slice(c * EXPERT_FF_CHUNK, (c + 1) * EXPERT_FF_CHUNK)
            g = lax.dot_general(x, wg_s[ff, :], _NT, preferred_element_type=F32) + bg_ref[:, ff]
            u = lax.dot_general(x, wu_s[ff, :], _NT, preferred_element_type=F32) + bu_ref[:, ff]
            gate = jnp.minimum(g, SWIGLU_LIMIT)
            up = jnp.clip(u, -SWIGLU_LIMIT, SWIGLU_LIMIT)
            glu = gate * jax.nn.sigmoid(SWIGLU_ALPHA * gate)
            a = ((up + 1.0) * glu).astype(BF16)
            y = y + jnp.dot(a, wd_s[ff, :], preferred_element_type=F32)
        _store_token_rows(y_ref, _pack_pairs(y.astype(BF16).astype(F32)))

    @pl.when(i >= nu_ref[0])
    def _():
        y_ref[...] = jnp.zeros(y_ref.shape, y_ref.dtype)


def _experts(x_sorted, block_e, n_used, w_gu, w_dn, b_g, b_u, b_d):
    rows = x_sorted.shape[0] // ROW_CHUNKS
    n_blocks = rows // MOE_BLOCK
    d_gu = w_gu.shape[2]
    d_ff = w_dn.shape[1]
    by_expert = lambda i, be, nu: (be[i], 0, 0)
    xspec = pl.BlockSpec((MOE_BLOCK * ROW_CHUNKS, LANES), lambda i, be, nu: (i, 0))
    grid_spec = pltpu.PrefetchScalarGridSpec(
        num_scalar_prefetch=2, grid=(n_blocks,),
        in_specs=[xspec,
                  pl.BlockSpec((None, D_MODEL, d_gu), by_expert),
                  pl.BlockSpec((None, d_ff, D_MODEL), by_expert),
                  pl.BlockSpec((None, 1, d_ff), by_expert),
                  pl.BlockSpec((None, 1, d_ff), by_expert),
                  pl.BlockSpec((None, 1, D_MODEL), by_expert)],
        out_specs=xspec,
        scratch_shapes=[pltpu.VMEM((D_MODEL // LANES, d_gu, LANES), F32),
                        pltpu.VMEM((d_ff, D_MODEL), BF16), pltpu.VMEM((d_ff, D_MODEL), BF16),
                        pltpu.VMEM((d_ff, D_MODEL), BF16)])
    return pl.pallas_call(
        _expert_kernel,
        out_shape=jax.ShapeDtypeStruct((rows * ROW_CHUNKS, LANES), jnp.uint32),
        grid_spec=grid_spec,
        compiler_params=pltpu.CompilerParams(dimension_semantics=("arbitrary",),
                                             vmem_limit_bytes=EXPERT_VMEM_LIMIT_BYTES),
        name="experts",
    )(block_e, n_used, x_sorted, w_gu, w_dn, b_g, b_u, b_d)


def _combine_kernel(dest_ref, xm_ref, gt_ref, g2p_ref, g2s_ref, yb_ref, yp_ref, ys_ref, rows_ref, sem,
                    *, n_prompt_tiles):
    i = pl.program_id(0)
    tm = xm_ref.shape[0]

    def body(t, carry):
        for k in range(TOP_K):
            d = dest_ref[(i * tm + t) * TOP_K + k]
            pltpu.make_async_copy(yb_ref.at[pl.ds(d * ROW_CHUNKS, ROW_CHUNKS)],
                                  rows_ref.at[k, pl.ds(t * ROW_CHUNKS, ROW_CHUNKS)], sem).start()
        return carry

    lax.fori_loop(0, tm, body, 0, unroll=8)
    pltpu.make_async_copy(rows_ref, rows_ref, sem).wait()
    gt = gt_ref[...]
    acc = jnp.zeros(xm_ref.shape, F32)
    for k in range(TOP_K):
        acc = acc + gt[:, k:k + 1] * _unpack_pairs(_load_token_rows(rows_ref.at[k], tm))
    is_prompt = i < n_prompt_tiles
    y = xm_ref[...] + jnp.where(is_prompt, g2p_ref[0], g2s_ref[0]) * acc

    @pl.when(is_prompt)
    def _():
        yp_ref[...] = y

    @pl.when(jnp.logical_not(is_prompt))
    def _():
        ys_ref[...] = y


def _combine(tiles, dest_flat, xm, gates, gate2_p, gate2_s, y_buf, n_prompt, rows_per_seq):
    tm = tiles.tm
    n_prompt_tiles = n_prompt // tm
    n_sample = tiles.n_tokens - n_prompt
    assert n_sample == tm and gate2_s.shape[1] == tm, "the sample group is one token tile"
    tiles_per_seq = rows_per_seq // tm
    last_seq = gate2_p.shape[0] - 1
    grid_spec = pltpu.PrefetchScalarGridSpec(
        num_scalar_prefetch=1, grid=(tiles.n_tiles,),
        in_specs=[pl.BlockSpec((tm, D_MODEL), lambda i, d: (i, 0)),
                  pl.BlockSpec((tm, LANES), lambda i, d: (i, 0)),
                  pl.BlockSpec((1, 1, D_MODEL), lambda i, d: (jnp.minimum(i // tiles_per_seq, last_seq), 0, 0)),
                  pl.BlockSpec((1, tm, D_MODEL), lambda i, d: (0, 0, 0)),
                  pl.BlockSpec(memory_space=pl.ANY)],
        out_specs=(pl.BlockSpec((tm, D_MODEL), lambda i, d: (jnp.minimum(i, n_prompt_tiles - 1), 0)),
                   pl.BlockSpec((tm, D_MODEL), lambda i, d: (0, 0))),
        scratch_shapes=[pltpu.VMEM((TOP_K, tm * ROW_CHUNKS, LANES), jnp.uint32),
                        pltpu.SemaphoreType.DMA(())])
    return pl.pallas_call(
        functools.partial(_combine_kernel, n_prompt_tiles=n_prompt_tiles),
        out_shape=(jax.ShapeDtypeStruct((n_prompt, D_MODEL), F32),
                   jax.ShapeDtypeStruct((n_sample, D_MODEL), F32)),
        grid_spec=grid_spec,
        compiler_params=_params("arbitrary"),
        name="moe_combine",
    )(dest_flat, xm, gates, gate2_p, gate2_s, y_buf)


class _Tiles:
    def __init__(self, n_tokens, tm):
        self.n_tokens, self.tm, self.n_tiles = n_tokens, tm, n_tokens // tm


def _moe(xm, h2u, idx, gates, gate2_p, gate2_s, moe_w, n_prompt, rows_per_seq):
    rows = _Tiles(xm.shape[0], MOE_TOKEN_TILE)
    a = rows.n_tokens * TOP_K
    n_blocks = -(-a // MOE_BLOCK) + N_EXPERTS
    dest, counts = _route(rows, idx)
    dest_flat = dest[:, :TOP_K].reshape(a)
    cnt = counts[0, :N_EXPERTS].astype(jnp.int32)
    pad_end = jnp.cumsum((cnt + MOE_BLOCK - 1) // MOE_BLOCK * MOE_BLOCK)
    block_row = jnp.arange(n_blocks, dtype=jnp.int32) * MOE_BLOCK
    block_e = jnp.minimum(jnp.sum((pad_end[None, :] <= block_row[:, None]).astype(jnp.int32), axis=1),
                          N_EXPERTS - 1)
    n_used = (pad_end[-1:] // MOE_BLOCK).astype(jnp.int32)
    x_sorted = _dispatch(rows, dest_flat, h2u, n_blocks * MOE_BLOCK)
    y_buf = _experts(x_sorted, block_e, n_used, *moe_w)
    return _combine(rows, dest_flat, xm, gates, gate2_p, gate2_s, y_buf, n_prompt, rows_per_seq)


def _group(rows, x, ada, state_rows, qk_and_attend, shared, total_tokens, first_row, token_bufs):
    (g_mix, w_vconv, w_conv, w_o_b, g_ffn, wr_hi, wr_lo, b_r, mid_dtype) = shared
    shift1, scale1, gate1, shift2, scale2, gate2 = [rows.mod_array(m) for m in jnp.split(ada, 6, axis=-1)]
    v_f, v_b, sga, cvp, tail = _vconv_proj(rows, x, g_mix, scale1, shift1, w_vconv, w_conv, state_rows, mid_dtype)
    mix, k_out = qk_and_attend(x, scale1, shift1, v_f, v_b, sga, cvp)
    token_bufs = _out_proj(rows, mix, x, w_o_b, gate1, g_ffn, scale2, shift2, wr_hi, wr_lo, b_r,
                           total_tokens, first_row, token_bufs)
    return token_bufs, gate2, k_out, v_f, tail


def kernel(x_prompt, x_sample, c_prompt, c_sample, cache_k, cache_v, state_conv, page_table, w_ada, b_ada, g_norm_mix, w_in, g_q, g_k, lambda_q1, lambda_k1, lambda_q2, lambda_k2, g_subln, w_conv, w_o, g_norm_ffn, w_router, b_router, w_gate_up, b_gate_up, w_down, b_down):
    assert w_in.shape[0] == 1, "single-layer stack"
    batch, seq, _ = x_prompt.shape
    n_seq, n_new, _ = x_sample.shape
    tp, ts = batch * seq, n_seq * n_new
    n_chunks = D_MODEL // HEAD_DIM

    ada = _ada(jnp.concatenate([c_prompt, c_sample], axis=0), w_ada[0], b_ada[0][None])

    w_in_b = w_in[0].astype(BF16)
    w_kt = w_in[0][:, D_MODEL:2 * D_MODEL].T.astype(BF16)
    w_vconv = w_in_b[:, 2 * D_MODEL:]
    w_o_b = w_o[0].astype(BF16)
    gqk = jnp.stack([jnp.tile(g_q[0], n_chunks), jnp.tile(g_k[0], n_chunks)])[:, None, :]
    gk_col = jnp.tile(g_k[0], n_chunks)[:, None]
    blk = jnp.arange(D_MODEL, dtype=jnp.int32) // HEAD_DIM
    pmat = (blk[:, None] == blk[None, :]).astype(BF16)
    wr = jnp.pad(w_router[0], ((0, 0), (0, LANES - N_EXPERTS)))
    wr_hi = wr.astype(BF16)
    wr_lo = (wr - wr_hi.astype(F32)).astype(BF16)
    b_r = jnp.pad(b_router[0], (0, LANES - N_EXPERTS), constant_values=NEG_BIG)[None]
    moe_w = (w_gate_up[0], w_down[0], b_gate_up[0][:, None, 0::2], b_gate_up[0][:, None, 1::2],
             b_down[0][:, None, :])
    g_mix = g_norm_mix[0][None]
    g_ffn = g_norm_ffn[0][None]
    g_sub = g_subln[0][None]
    lams = (lambda_q1[0][None], lambda_k1[0][None], lambda_q2[0][None], lambda_k2[0][None])

    def shared(mid_dtype):
        return (g_mix, w_vconv, w_conv[0], w_o_b, g_ffn, wr_hi, wr_lo, b_r, mid_dtype)

    rows_p = _Rows(tp, seq)

    def attend_p(x, scale1, shift1, v_f, v_b, sga, cvp):
        q_b, kt_f, kt_b = _qkt_proj(rows_p, x, g_mix, scale1, shift1, w_in_b, gqk, pmat, w_kt, gk_col, batch, seq)
        mix = _attn_prompt(q_b, kt_b, v_b, sga, cvp, g_sub, lams, batch, seq)
        return mix, kt_f

    bufs, gate2_p, kt_p, v_p, tail_p = _group(rows_p, x_prompt.reshape(tp, D_MODEL), ada[:batch], None, attend_p,
                                              shared(BF16), tp + ts, 0, None)

    rows_s = _Rows(ts, n_new)
    st = state_conv[0]
    zeros = jnp.zeros((n_seq, n_new - 2, D_MODEL), F32)
    s1 = jnp.concatenate([st[:, 1:2], jnp.zeros((n_seq, n_new - 1, D_MODEL), F32)], axis=1).reshape(ts, D_MODEL)
    s2 = jnp.concatenate([st, zeros], axis=1).reshape(ts, D_MODEL)

    def attend_s(x, scale1, shift1, v_f, v_b, sga, cvp):
        _, qk_f = _qk_proj(rows_s, x, g_mix, scale1, shift1, w_in_b, gqk, pmat)
        mix = _attn_sample(qk_f, v_f, cache_k[0], cache_v[0], page_table, sga, cvp, g_sub, lams, n_seq, n_new)
        return mix, qk_f[1]

    bufs, gate2_s, k_s, v_s, tail_s = _group(rows_s, x_sample.reshape(ts, D_MODEL), ada[batch:], (s1, s2), attend_s,
                                             shared(F32), tp + ts, tp, bufs)

    y_p, y_s = _moe(*bufs, gate2_p, gate2_s, moe_w, tp, seq)

    tail_s = tail_s.reshape(n_seq, n_new, D_MODEL)
    k_p = kt_p.reshape(1, batch, N_HEADS, 2, HEAD_DIM, seq).transpose(0, 1, 5, 2, 3, 4)
    return (y_p.reshape(batch, seq, D_MODEL),
            y_s.reshape(n_seq, n_new, D_MODEL),
            k_p,
            v_p.reshape(1, batch, seq, N_HEADS, V_DIM),
            tail_p[:, SUBLANES - (CONV_WIDTH - 1):][None],
            k_s.reshape(1, n_seq, n_new, N_HEADS, 2, HEAD_DIM),
            v_s.reshape(1, n_seq, n_new, N_HEADS, V_DIM),
            tail_s[:, n_new - (CONV_WIDTH - 1):][None])
```

```python
import functools
import math

import jax
import jax.numpy as jnp
from jax import lax
from jax.experimental import pallas as pl
from jax.experimental.pallas import tpu as pltpu

F32 = jnp.float32
BF16 = jnp.bfloat16

D_MODEL = 1024
HEAD_DIM = 64
V_DIM = 2 * HEAD_DIM
N_HEADS = D_MODEL // V_DIM
ATTN_SCALE = HEAD_DIM ** -0.5
LOG2_E = math.log2(math.e)
CONV_WIDTH = 3
PAGE_SIZE = 128
N_EXPERTS = 32
TOP_K = 4
SWIGLU_LIMIT = 7.0
SWIGLU_ALPHA = 1.702
NORM_EPS = 1e-6
LAMBDA_INIT = 0.8 - 0.6 * math.exp(-0.3 * 0)

VMEM_LIMIT_BYTES = 48 * 1024 * 1024
EXPERT_VMEM_LIMIT_BYTES = 56 * 1024 * 1024
LANES = 128
PACKED = D_MODEL // 2
ROW_CHUNKS = PACKED // LANES
SUBLANES = 8

ROW_TILE_PROMPT = 512
ATTN_BLOCK = 512
MOE_BLOCK = 512
MOE_TOKEN_TILE = 256
EXPERT_XPOSE_CHUNK = 512
EXPERT_FF_CHUNK = 512
ADA_COL_TILE = 1536
SAMPLE_PAGES_PER_STEP = 16
NEG_BIG = -1e30

_NT = (((1,), (1,)), ((), ()))


def _params(*sem):
    return pltpu.CompilerParams(dimension_semantics=sem, vmem_limit_bytes=VMEM_LIMIT_BYTES)


def _rms_mod(x, g, scale, shift):
    ms = jnp.mean(x * x, axis=-1, keepdims=True)
    return (x * lax.rsqrt(ms + NORM_EPS) * g) * (1.0 + scale) + shift


def _ada_kernel(c_ref, w_ref, b_ref, o_ref):
    c = c_ref[...]
    s = (c * jax.nn.sigmoid(c)).astype(BF16)
    o_ref[...] = jnp.dot(s, w_ref[...].astype(BF16), preferred_element_type=F32) + b_ref[...]


def _ada(c_all, w_ada, b_ada):
    n = c_all.shape[0]
    width = w_ada.shape[1]
    return pl.pallas_call(
        _ada_kernel,
        out_shape=jax.ShapeDtypeStruct((n, width), F32),
        grid=(width // ADA_COL_TILE,),
        in_specs=[pl.BlockSpec((n, D_MODEL), lambda j: (0, 0)),
                  pl.BlockSpec((D_MODEL, ADA_COL_TILE), lambda j: (0, j)),
                  pl.BlockSpec((1, ADA_COL_TILE), lambda j: (0, j))],
        out_specs=pl.BlockSpec((n, ADA_COL_TILE), lambda j: (0, j)),
        compiler_params=_params("arbitrary"),
        name="ada",
    )(c_all, w_ada, b_ada)


class _Rows:
    def __init__(self, n_tokens, rows_per_batch):
        if rows_per_batch >= ROW_TILE_PROMPT:
            self.tm = ROW_TILE_PROMPT
            self.tiles_per_batch = rows_per_batch // self.tm
            self.mod_rows = 1
        else:
            self.tm = n_tokens
            self.tiles_per_batch = None
            self.mod_rows = n_tokens
        self.n_tokens = n_tokens
        self.rows_per_batch = rows_per_batch
        self.n_tiles = n_tokens // self.tm

    def mod_array(self, m):
        if self.tiles_per_batch is not None:
            return m[:, None, :]
        return jnp.repeat(m, self.rows_per_batch, axis=0)[None]

    def mod_spec(self, clamp=False):
        if self.tiles_per_batch is not None:
            tpb, last = self.tiles_per_batch, self.n_tiles - 1
            if clamp:
                return pl.BlockSpec((1, 1, D_MODEL), lambda i, *_: (jnp.minimum(i, last) // tpb, 0, 0))
            return pl.BlockSpec((1, 1, D_MODEL), lambda i, *_: (i // tpb, 0, 0))
        return pl.BlockSpec((1, self.mod_rows, D_MODEL), lambda i, *_: (0, 0, 0))

    def row_spec(self, width=D_MODEL, clamp=False):
        if clamp:
            last = self.n_tiles - 1
            return pl.BlockSpec((self.tm, width), lambda i, *_: (jnp.minimum(i, last), 0))
        return pl.BlockSpec((self.tm, width), lambda i, *_: (i, 0))


def _qk_kernel(x_ref, g_ref, sc_ref, sh_ref, w_ref, gqk_ref, p_ref, qkb_ref, qkf_ref, h_ref):
    j = pl.program_id(1)

    @pl.when(j == 0)
    def _():
        h_ref[...] = _rms_mod(x_ref[...], g_ref[...], sc_ref[0], sh_ref[0]).astype(BF16)

    z = jnp.dot(h_ref[...], w_ref[...], preferred_element_type=F32)
    ss = jnp.dot((z * z).astype(BF16), p_ref[...], preferred_element_type=F32)
    zn = z * lax.rsqrt(ss * (1.0 / HEAD_DIM) + NORM_EPS) * gqk_ref[0]
    qkf_ref[0] = zn
    scale = jnp.where(j == 0, ATTN_SCALE, 1.0)
    qkb_ref[...] = (zn * scale).astype(BF16)


def _qk_proj(rows, x, g, scale, shift, w_in_b, gqk, pmat):
    t = rows.n_tokens
    return pl.pallas_call(
        _qk_kernel,
        out_shape=(jax.ShapeDtypeStruct((t, 2 * D_MODEL), BF16),
                   jax.ShapeDtypeStruct((2, t, D_MODEL), F32)),
        grid=(rows.n_tiles, 2),
        in_specs=[rows.row_spec(),
                  pl.BlockSpec((1, D_MODEL), lambda i, j: (0, 0)),
                  rows.mod_spec(), rows.mod_spec(),
                  pl.BlockSpec((D_MODEL, D_MODEL), lambda i, j: (0, j)),
                  pl.BlockSpec((1, 1, D_MODEL), lambda i, j: (j, 0, 0)),
                  pl.BlockSpec((D_MODEL, D_MODEL), lambda i, j: (0, 0))],
        out_specs=(pl.BlockSpec((rows.tm, D_MODEL), lambda i, j: (i, j)),
                   pl.BlockSpec((1, rows.tm, D_MODEL), lambda i, j: (j, i, 0))),
        scratch_shapes=[pltpu.VMEM((rows.tm, D_MODEL), BF16)],
        compiler_params=_params("parallel", "arbitrary"),
        name="qk_proj",
    )(x, g, scale, shift, w_in_b, gqk, pmat)


def _qkt_kernel(x_ref, g_ref, sc_ref, sh_ref, wq_ref, gq_ref, p_ref, wt_ref, gk_ref, qb_ref, kf_ref, kb_ref):
    h = _rms_mod(x_ref[...], g_ref[...], sc_ref[0], sh_ref[0]).astype(BF16)
    z = jnp.dot(h, wq_ref[...], preferred_element_type=F32)
    ss = jnp.dot((z * z).astype(BF16), p_ref[...], preferred_element_type=F32)
    zn = z * lax.rsqrt(ss * (1.0 / HEAD_DIM) + NORM_EPS) * gq_ref[0]
    qb_ref[...] = (zn * (ATTN_SCALE * LOG2_E)).astype(BF16)
    zt = lax.dot_general(wt_ref[...], h, _NT, preferred_element_type=F32)
    tm = zt.shape[1]
    z3 = zt.reshape(D_MODEL // HEAD_DIM, HEAD_DIM, tm)
    s3 = jnp.sum(z3 * z3, axis=1, keepdims=True)
    g3 = gk_ref[...].reshape(D_MODEL // HEAD_DIM, HEAD_DIM, 1)
    kn = (z3 * lax.rsqrt(s3 * (1.0 / HEAD_DIM) + NORM_EPS) * g3).reshape(D_MODEL, tm)
    kf_ref[...] = kn
    kb_ref[...] = kn.astype(BF16)


def _qkt_proj(rows, x, g, scale, shift, w_in_b, gqk, pmat, w_kt, gk_col, batch, seq):
    tpb = rows.tiles_per_batch
    once = pl.Buffered(1)
    kt_spec = pl.BlockSpec((None, D_MODEL, rows.tm), lambda i: (i // tpb, 0, i % tpb))
    return pl.pallas_call(
        _qkt_kernel,
        out_shape=(jax.ShapeDtypeStruct((rows.n_tokens, D_MODEL), BF16),
                   jax.ShapeDtypeStruct((batch, D_MODEL, seq), F32),
                   jax.ShapeDtypeStruct((batch, D_MODEL, seq), BF16)),
        grid=(rows.n_tiles,),
        in_specs=[rows.row_spec(),
                  pl.BlockSpec((1, D_MODEL), lambda i: (0, 0)),
                  rows.mod_spec(), rows.mod_spec(),
                  pl.BlockSpec((D_MODEL, D_MODEL), lambda i: (0, 0), pipeline_mode=once),
                  pl.BlockSpec((1, 1, D_MODEL), lambda i: (0, 0, 0)),
                  pl.BlockSpec((D_MODEL, D_MODEL), lambda i: (0, 0), pipeline_mode=once),
                  pl.BlockSpec((D_MODEL, D_MODEL), lambda i: (0, 0), pipeline_mode=once),
                  pl.BlockSpec((D_MODEL, 1), lambda i: (0, 0))],
        out_specs=(rows.row_spec(), kt_spec, kt_spec),
        compiler_params=_params("parallel"),
        name="qkt_proj",
    )(x, g, scale, shift, w_in_b, gqk, pmat, w_kt, gk_col)


def _vconv_kernel(*refs, tiles_per_batch, rows_per_batch, tail_rows):
    if tiles_per_batch is None:
        (x_ref, g_ref, sc_ref, sh_ref, w_ref, wc_ref, s1_ref, s2_ref,
         vf_ref, vb_ref, sga_ref, cvp_ref, tail_ref, carry_ref) = refs
    else:
        (x_ref, g_ref, sc_ref, sh_ref, w_ref, wc_ref,
         vf_ref, vb_ref, sga_ref, cvp_ref, tail_ref, carry_ref) = refs
    i = pl.program_id(0)
    h = _rms_mod(x_ref[...], g_ref[...], sc_ref[0], sh_ref[0]).astype(BF16)

    def proj(k):
        return jnp.dot(h, w_ref[:, k * D_MODEL:(k + 1) * D_MODEL], preferred_element_type=F32)

    v = proj(0)
    vf_ref[...] = v
    vb_ref[...] = v.astype(BF16)
    sga_ref[...] = jax.nn.sigmoid(proj(4)).astype(sga_ref.dtype)
    u = proj(2) * proj(3)
    tm = u.shape[0]
    row = lax.broadcasted_iota(jnp.int32, (tm, 1), 0)
    r1 = pltpu.roll(u, 1, 0)
    r2 = pltpu.roll(u, 2, 0)
    if tiles_per_batch is None:
        t = row & (rows_per_batch - 1)
        u1 = jnp.where(t >= 1, r1, s1_ref[...])
        u2 = jnp.where(t >= 2, r2, s2_ref[...])
    else:
        first = (i % tiles_per_batch) == 0
        c = jnp.where(first, 0.0, carry_ref[...])
        u1 = jnp.where(row == 0, c[7:8], r1)
        u2 = jnp.where(row == 0, c[6:7], jnp.where(row == 1, c[7:8], r2))
        carry_ref[...] = u[tm - SUBLANES:]
    tail_ref[0] = u[tm - tail_rows:]
    wc = wc_ref[...]
    yc = wc[0:1] * u2 + wc[1:2] * u1 + wc[2:3] * u
    cvp_ref[...] = (jax.nn.sigmoid(proj(5)) * (proj(1) * yc)).astype(cvp_ref.dtype)


def _vconv_proj(rows, x, g, scale, shift, w_vconv, w_conv, state_rows, out_dtype):
    t = rows.n_tokens
    sample_mode = rows.tiles_per_batch is None
    tail_rows = rows.tm if sample_mode else SUBLANES
    n_tail_blocks = 1 if sample_mode else t // rows.rows_per_batch
    in_specs = [rows.row_spec(),
                pl.BlockSpec((1, D_MODEL), lambda i: (0, 0)),
                rows.mod_spec(), rows.mod_spec(),
                pl.BlockSpec(w_vconv.shape, lambda i: (0, 0), pipeline_mode=pl.Buffered(1)),
                pl.BlockSpec((CONV_WIDTH, D_MODEL), lambda i: (0, 0))]
    args = [x, g, scale, shift, w_vconv, w_conv]
    if sample_mode:
        in_specs += [rows.row_spec(), rows.row_spec()]
        args += list(state_rows)
        tail_spec = pl.BlockSpec((1, tail_rows, D_MODEL), lambda i: (0, 0, 0))
    else:
        tpb = rows.tiles_per_batch
        tail_spec = pl.BlockSpec((1, tail_rows, D_MODEL), lambda i: (i // tpb, 0, 0))
    kern = functools.partial(_vconv_kernel, tiles_per_batch=rows.tiles_per_batch,
                             rows_per_batch=rows.rows_per_batch, tail_rows=tail_rows)
    return pl.pallas_call(
        kern,
        out_shape=(jax.ShapeDtypeStruct((t, D_MODEL), F32),
                   jax.ShapeDtypeStruct((t, D_MODEL), BF16),
                   jax.ShapeDtypeStruct((t, D_MODEL), out_dtype),
                   jax.ShapeDtypeStruct((t, D_MODEL), out_dtype),
                   jax.ShapeDtypeStruct((n_tail_blocks, tail_rows, D_MODEL), F32)),
        grid=(rows.n_tiles,),
        in_specs=in_specs,
        out_specs=(rows.row_spec(), rows.row_spec(), rows.row_spec(), rows.row_spec(), tail_spec),
        scratch_shapes=[pltpu.VMEM((SUBLANES, D_MODEL), F32)],
        compiler_params=_params("arbitrary"),
        name="vconv_proj",
    )(*args)


def _lambda_value(lq1, lk1, lq2, lk2):
    e1 = jnp.exp(jnp.sum(lq1 * lk1, axis=-1, keepdims=True))
    e2 = jnp.exp(jnp.sum(lq2 * lk2, axis=-1, keepdims=True))
    return e1 - e2 + LAMBDA_INIT


def _subln_mix(o, g_sub, sga, cvp):
    ms = jnp.mean(o * o, axis=-1, keepdims=True)
    attn = (o * lax.rsqrt(ms + NORM_EPS) * g_sub) * (1.0 - LAMBDA_INIT)
    return sga * attn + cvp


def _flash_update(s, v_ones, m_ref, l_ref, acc_ref):
    m_prev = m_ref[...]
    m_next = jnp.maximum(m_prev, jnp.max(s, axis=1, keepdims=True))
    alpha = jnp.exp2(m_prev - m_next)
    p = jnp.exp2(s - jnp.concatenate([m_next] * (s.shape[1] // LANES), axis=1)).astype(BF16)
    pv = jnp.dot(p, v_ones, preferred_element_type=F32)
    acc_ref[...] = alpha * acc_ref[...] + pv[:, :V_DIM]
    l_ref[...] = alpha * l_ref[...] + pv[:, V_DIM:]
    m_ref[...] = m_next


def _attn_prompt_kernel(qi_ref, ki_ref, q_ref, kt_ref, v_ref, sga_ref, cvp_ref, gs_ref,
                        lq1_ref, lk1_ref, lq2_ref, lk2_ref,
                        o_ref, m0_ref, l0_ref, a0_ref, m1_ref, l1_ref, a1_ref):
    qi = qi_ref[pl.program_id(1)]
    ki = ki_ref[pl.program_id(1)]
    states = ((m0_ref, l0_ref, a0_ref), (m1_ref, l1_ref, a1_ref))

    @pl.when(ki == 0)
    def _():
        for m_ref, l_ref, a_ref in states:
            m_ref[...] = jnp.full(m_ref.shape, -jnp.inf, F32)
            l_ref[...] = jnp.zeros(l_ref.shape, F32)
            a_ref[...] = jnp.zeros(a_ref.shape, F32)

    tq, tk = q_ref.shape[0], kt_ref.shape[1]

    def step(masked):
        lane = lax.broadcasted_iota(jnp.int32, (1, V_DIM), 1)
        zero = jnp.zeros((), BF16)
        ones = jnp.ones((tk, V_DIM), BF16)
        if masked:
            r = lax.broadcasted_iota(jnp.int32, (tq, tk), 0)
            c = lax.broadcasted_iota(jnp.int32, (tq, tk), 1)
            keep = c <= r
        for h in range(N_HEADS):
            cols = slice(h * V_DIM, (h + 1) * V_DIM)
            q = q_ref[:, cols]
            kt = kt_ref[cols, :]
            v_ones = jnp.concatenate([v_ref[:, cols], ones], axis=1)
            qs = (jnp.where(lane < HEAD_DIM, q, zero), jnp.where(lane >= HEAD_DIM, q, zero))
            for qm, (m_ref, l_ref, a_ref) in zip(qs, states):
                s = jnp.dot(qm, kt, preferred_element_type=F32)
                if masked:
                    s = jnp.where(keep, s, -jnp.inf)
                _flash_update(s, v_ones, m_ref.at[h], l_ref.at[h], a_ref.at[h])

    @pl.when(ki < qi)
    def _():
        step(False)

    @pl.when(ki == qi)
    def _():
        step(True)
        lam = _lambda_value(lq1_ref[...], lk1_ref[...], lq2_ref[...], lk2_ref[...])
        for h in range(N_HEADS):
            cols = slice(h * V_DIM, (h + 1) * V_DIM)
            o = a0_ref[h] / l0_ref[h] - lam * (a1_ref[h] / l1_ref[h])
            mix = _subln_mix(o, gs_ref[...], sga_ref[:, cols].astype(F32), cvp_ref[:, cols].astype(F32))
            o_ref[:, cols] = mix.astype(o_ref.dtype)


def _attn_prompt(q_b, kt_b, v_b, sga, cvp, g_sub, lams, batch, seq):
    nb = seq // ATTN_BLOCK
    tq = ATTN_BLOCK
    pairs = [(qi, ki) for qi in range(nb) for ki in range(qi + 1)]
    qi_tab = jnp.array([p[0] for p in pairs], jnp.int32)
    ki_tab = jnp.array([p[1] for p in pairs], jnp.int32)
    q_spec = pl.BlockSpec((tq, D_MODEL), lambda b, p, qt, kt: (b * nb + qt[p], 0))
    k_spec = pl.BlockSpec((None, D_MODEL, tq), lambda b, p, qt, kt: (b, 0, kt[p]))
    v_spec = pl.BlockSpec((tq, D_MODEL), lambda b, p, qt, kt: (b * nb + kt[p], 0))
    vec64 = pl.BlockSpec((1, HEAD_DIM), lambda b, p, qt, kt: (0, 0))
    grid_spec = pltpu.PrefetchScalarGridSpec(
        num_scalar_prefetch=2,
        grid=(batch, len(pairs)),
        in_specs=[q_spec, k_spec, v_spec, q_spec, q_spec,
                  pl.BlockSpec((1, V_DIM), lambda b, p, qt, kt: (0, 0)),
                  vec64, vec64, vec64, vec64],
        out_specs=q_spec,
        scratch_shapes=[pltpu.VMEM((N_HEADS, tq, V_DIM), F32)] * 6)
    return pl.pallas_call(
        _attn_prompt_kernel,
        out_shape=jax.ShapeDtypeStruct((batch * seq, D_MODEL), BF16),
        grid_spec=grid_spec,
        compiler_params=_params("parallel", "arbitrary"),
        name="attn_prompt",
    )(qi_tab, ki_tab, q_b, kt_b, v_b, sga, cvp, g_sub, *lams)


def _attn_sample_kernel(pt_ref, q_ref, kn_ref, vn_ref, *rest, n_new, n_group):
    kc_refs = rest[:n_group]
    vc_refs = rest[n_group:2 * n_group]
    (spread_ref, own_ref, sga_ref, cvp_ref, gs_ref, lq1_ref, lk1_ref, lq2_ref, lk2_ref,
     o_ref, qbd_ref, m_ref, l_ref, acc_ref) = rest[2 * n_group:]
    p = pl.program_id(1)
    rows_per_head = 2 * n_new

    @pl.when(p == 0)
    def _():
        q = q_ref[...] * ATTN_SCALE
        qrep = jnp.concatenate([q] * (N_HEADS * 2), axis=0)
        r = lax.broadcasted_iota(jnp.int32, qrep.shape, 0)
        c = lax.broadcasted_iota(jnp.int32, qrep.shape, 1)
        qbd_ref[...] = jnp.where(c // HEAD_DIM == r // n_new, qrep, 0.0).astype(BF16)
        m_ref[...] = jnp.full(m_ref.shape, -jnp.inf, F32)
        l_ref[...] = jnp.zeros(l_ref.shape, F32)
        acc_ref[...] = jnp.zeros(acc_ref.shape, F32)

    def update(s, pv_of):
        m_prev = m_ref[...]
        m_next = jnp.maximum(m_prev, jnp.max(s, axis=1, keepdims=True))
        alpha = jnp.exp(m_prev - m_next)
        width = s.shape[1]
        m_wide = m_next[:, :width] if width <= LANES else jnp.concatenate([m_next] * (width // LANES), axis=1)
        pr = jnp.exp(s - m_wide)
        l_ref[...] = alpha * l_ref[...] + jnp.sum(pr, axis=1, keepdims=True)
        acc_ref[...] = alpha * acc_ref[...] + pv_of(pr.astype(BF16))
        m_ref[...] = m_next

    def page_pv(pb, v_ref):
        pe = jnp.dot(pb, spread_ref[...], preferred_element_type=F32).astype(BF16) * own_ref[...]
        v2 = v_ref[...].reshape(PAGE_SIZE * N_HEADS, V_DIM).astype(BF16)
        return jnp.dot(pe, v2, preferred_element_type=F32)

    qbd = qbd_ref[...]
    s = jnp.concatenate([jnp.dot(qbd, kc_refs[g][...].astype(BF16), preferred_element_type=F32)
                         for g in range(n_group)], axis=1)

    def pages_pv(pb):
        out = page_pv(pb[:, :PAGE_SIZE], vc_refs[0])
        for g in range(1, n_group):
            out = out + page_pv(pb[:, g * PAGE_SIZE:(g + 1) * PAGE_SIZE], vc_refs[g])
        return out

    update(s, pages_pv)

    @pl.when(p == pl.num_programs(1) - 1)
    def _():
        s_new = lax.dot_general(qbd_ref[...], kn_ref[...].astype(BF16), _NT, preferred_element_type=F32)
        r = lax.broadcasted_iota(jnp.int32, s_new.shape, 0)
        c = lax.broadcasted_iota(jnp.int32, s_new.shape, 1)
        s_new = jnp.where(c <= (r & (n_new - 1)), s_new, -jnp.inf)
        def new_pv(pb):
            return jnp.concatenate(
                [jnp.dot(pb[h * rows_per_head:(h + 1) * rows_per_head].astype(F32),
                         vn_ref[:, h * V_DIM:(h + 1) * V_DIM].astype(BF16).astype(F32),
                         preferred_element_type=F32) for h in range(N_HEADS)], axis=0)

        update(s_new, new_pv)
        lam = _lambda_value(lq1_ref[...], lk1_ref[...], lq2_ref[...], lk2_ref[...])
        acc = acc_ref[...] / l_ref[...]
        for h in range(N_HEADS):
            cols = slice(h * V_DIM, (h + 1) * V_DIM)
            r0 = h * rows_per_head
            o = acc[r0:r0 + n_new] - lam * acc[r0 + n_new:r0 + rows_per_head]
            o_ref[:, cols] = _subln_mix(o, gs_ref[...], sga_ref[:, cols], cvp_ref[:, cols])


def _attn_sample(qkf, vf, cache_k, cache_v, page_table, sga, cvp, g_sub, lams, n_seq, n_new):
    n_pages = page_table.shape[1]
    n_pool = cache_k.shape[0]
    width = N_HEADS * V_DIM
    grp = SAMPLE_PAGES_PER_STEP
    kc = jnp.transpose(cache_k, (0, 2, 3, 4, 1)).reshape(n_pool, width, PAGE_SIZE)
    pt = page_table.reshape(-1)
    n_rows = N_HEADS * 2 * n_new
    row = pl.BlockSpec((n_new, width), lambda b, p, pt: (b, 0))

    def page_index(g):
        return lambda b, p, pt: (pt[b * n_pages + p * grp + g], 0, 0)

    def page_index4(g):
        return lambda b, p, pt: (pt[b * n_pages + p * grp + g], 0, 0, 0)

    k_pages = [pl.BlockSpec((None, width, PAGE_SIZE), page_index(g)) for g in range(grp)]
    v_pages = [pl.BlockSpec((None, PAGE_SIZE, N_HEADS, V_DIM), page_index4(g)) for g in range(grp)]
    vec64 = pl.BlockSpec((1, HEAD_DIM), lambda b, p, pt: (0, 0))
    col = jnp.arange(PAGE_SIZE * N_HEADS, dtype=jnp.int32)
    spread = (col[None, :] // N_HEADS == jnp.arange(PAGE_SIZE, dtype=jnp.int32)[:, None]).astype(BF16)
    own = (col[None, :] % N_HEADS == jnp.arange(n_rows, dtype=jnp.int32)[:, None] // (2 * n_new)).astype(BF16)
    const = lambda shape: pl.BlockSpec(shape, lambda b, p, pt: (0, 0))
    grid_spec = pltpu.PrefetchScalarGridSpec(
        num_scalar_prefetch=1,
        grid=(n_seq, n_pages // grp),
        in_specs=[pl.BlockSpec((None, n_new, width), lambda b, p, pt: (0, b, 0)),
                  pl.BlockSpec((None, n_new, width), lambda b, p, pt: (1, b, 0)),
                  row, *k_pages, *v_pages, const(spread.shape), const(own.shape), row, row,
                  pl.BlockSpec((1, V_DIM), lambda b, p, pt: (0, 0)),
                  vec64, vec64, vec64, vec64],
        out_specs=row,
        scratch_shapes=[pltpu.VMEM((n_rows, width), BF16),
                        pltpu.VMEM((n_rows, V_DIM), F32), pltpu.VMEM((n_rows, V_DIM), F32),
                        pltpu.VMEM((n_rows, V_DIM), F32)])
    return pl.pallas_call(
        functools.partial(_attn_sample_kernel, n_new=n_new, n_group=grp),
        out_shape=jax.ShapeDtypeStruct((n_seq * n_new, width), F32),
        grid_spec=grid_spec,
        compiler_params=_params("parallel", "arbitrary"),
        name="attn_sample",
    )(pt, qkf, qkf, vf, *([kc] * grp), *([cache_v] * grp), spread, own, sga, cvp, g_sub, *lams)


def _out_kernel(mix_ref, x_ref, wo_ref, g1_ref, g_ref, sc_ref, sh_ref, wrh_ref, wrl_ref, br_ref, *refs, n_real_tiles):
    outs = refs[-4:]
    i = pl.program_id(0)

    @pl.when(i < n_real_tiles)
    def _():
        _out_tile(mix_ref, x_ref, wo_ref, g1_ref, g_ref, sc_ref, sh_ref, wrh_ref, wrl_ref, br_ref, *outs)

    @pl.when(i >= n_real_tiles)
    def _():
        for r in outs:
            r[...] = jnp.zeros(r.shape, r.dtype)


def _out_tile(mix_ref, x_ref, wo_ref, g1_ref, g_ref, sc_ref, sh_ref, wrh_ref, wrl_ref, br_ref,
              xm_ref, h2_ref, idx_ref, gt_ref):
    y = jnp.dot(mix_ref[...].astype(BF16), wo_ref[...], preferred_element_type=F32)
    xm = x_ref[...] + g1_ref[0] * y
    xm_ref[...] = xm
    h2 = _rms_mod(xm, g_ref[...], sc_ref[0], sh_ref[0])
    hi = h2.astype(BF16)
    _store_token_rows(h2_ref, _pack_pairs(hi.astype(F32)))
    lo = (h2 - hi.astype(F32)).astype(BF16)
    logits = (jnp.dot(hi, wrh_ref[...], preferred_element_type=F32)
              + jnp.dot(lo, wrh_ref[...], preferred_element_type=F32)
              + jnp.dot(hi, wrl_ref[...], preferred_element_type=F32)) + br_ref[...]
    lane = lax.broadcasted_iota(jnp.int32, logits.shape, 1).astype(F32)
    vals, idxs = [], []
    for _ in range(TOP_K):
        m = jnp.max(logits, axis=-1, keepdims=True)
        ix = jnp.min(jnp.where(logits == m, lane, float(LANES)), axis=-1, keepdims=True)
        logits = jnp.where(lane == ix, -jnp.inf, logits)
        vals.append(m)
        idxs.append(ix)
    es = [jnp.exp(v - vals[0]) for v in vals]
    denom = es[0] + es[1] + es[2] + es[3]
    idx_out = jnp.zeros(logits.shape, F32)
    gt_out = jnp.zeros(logits.shape, F32)
    for k in range(TOP_K):
        idx_out = jnp.where(lane == float(k), idxs[k], idx_out)
        gt_out = jnp.where(lane == float(k), es[k] / denom, gt_out)
    idx_ref[...] = idx_out.astype(jnp.int32)
    gt_ref[...] = gt_out


def _out_proj(rows, mix, x, w_o_b, gate1, g, scale, shift, wr_hi, wr_lo, b_r, total_tokens, first_row, shared_bufs):
    full = lambda shape: pl.BlockSpec(shape, lambda i: (0,) * len(shape))
    off = first_row // rows.tm
    creates = shared_bufs is None
    out_row = lambda width: pl.BlockSpec((rows.tm, width), lambda i: (i + off, 0))
    packed_rows = pl.BlockSpec((rows.tm * ROW_CHUNKS, LANES), lambda i: (i + off, 0))
    in_specs = [rows.row_spec(clamp=creates), rows.row_spec(clamp=creates), full((D_MODEL, D_MODEL)),
                rows.mod_spec(clamp=creates), full((1, D_MODEL)), rows.mod_spec(clamp=creates),
                rows.mod_spec(clamp=creates),
                full((D_MODEL, LANES)), full((D_MODEL, LANES)), full((1, LANES))]
    args = [mix, x, w_o_b, gate1, g, scale, shift, wr_hi, wr_lo, b_r]
    aliases = {}
    if not creates:
        aliases = {len(args) + j: j for j in range(len(shared_bufs))}
        in_specs += [pl.BlockSpec(memory_space=pl.ANY)] * len(shared_bufs)
        args += list(shared_bufs)
    extra = 1 if creates and total_tokens > rows.n_tokens else 0
    assert total_tokens - rows.n_tokens <= rows.tm or not creates, "other group must fit the one extra tile"
    return pl.pallas_call(
        functools.partial(_out_kernel, n_real_tiles=rows.n_tiles),
        out_shape=(jax.ShapeDtypeStruct((total_tokens, D_MODEL), F32),
                   jax.ShapeDtypeStruct((total_tokens * ROW_CHUNKS, LANES), jnp.uint32),
                   jax.ShapeDtypeStruct((total_tokens, LANES), jnp.int32),
                   jax.ShapeDtypeStruct((total_tokens, LANES), F32)),
        grid=(rows.n_tiles + extra,),
        in_specs=in_specs,
        out_specs=(out_row(D_MODEL), packed_rows, out_row(LANES), out_row(LANES)),
        input_output_aliases=aliases,
        compiler_params=_params("arbitrary"),
        name="out_proj_router",
    )(*args)


def _pack_pairs(x):
    bits = pltpu.bitcast(x, jnp.uint32)
    n = x.shape[1] // 2
    return (bits[:, :n] >> 16) | (bits[:, n:] & jnp.uint32(0xFFFF0000))


def _unpack_pairs(w):
    lo = pltpu.bitcast(w << 16, F32)
    hi = pltpu.bitcast(w & jnp.uint32(0xFFFF0000), F32)
    return jnp.concatenate([lo, hi], axis=1)


def _store_token_rows(ref, packed):
    n = packed.shape[0]
    for c in range(ROW_CHUNKS):
        ref[pl.ds(c, n, stride=ROW_CHUNKS), :] = packed[:, c * LANES:(c + 1) * LANES]


def _load_token_rows(ref, n):
    return jnp.concatenate([ref[pl.ds(c, n, stride=ROW_CHUNKS), :] for c in range(ROW_CHUNKS)], axis=1)


def _route_kernel(idx_ref, dest_ref, cnt_ref, run_ref, start_ref):
    ph = pl.program_id(0)
    i = pl.program_id(1)
    idx = idx_ref[...]
    tm = idx.shape[0]
    lane = lax.broadcasted_iota(jnp.int32, idx.shape, 1)
    onehots = [(lane == idx[:, k:k + 1]).astype(F32) for k in range(TOP_K)]
    member = onehots[0] + onehots[1] + onehots[2] + onehots[3]
    tile_count = jnp.sum(member, axis=0, keepdims=True)

    @pl.when((ph == 0) & (i == 0))
    def _():
        cnt_ref[...] = jnp.zeros(cnt_ref.shape, F32)

    @pl.when(ph == 0)
    def _():
        cnt_ref[...] = cnt_ref[...] + tile_count

    @pl.when((ph == 1) & (i == 0))
    def _():
        cnt = cnt_ref[...]
        padded = jnp.floor((cnt + (MOE_BLOCK - 1)) * (1.0 / MOE_BLOCK)) * MOE_BLOCK
        l1 = lax.broadcasted_iota(jnp.int32, cnt.shape, 1)
        incl = padded
        for s in (1, 2, 4, 8, 16, 32, 64):
            incl = incl + jnp.where(l1 >= s, pltpu.roll(incl, s, 1), 0.0)
        start_ref[...] = incl - padded
        run_ref[...] = jnp.zeros(run_ref.shape, F32)

    @pl.when(ph == 1)
    def _():
        r = lax.broadcasted_iota(jnp.int32, (tm, tm), 0)
        c = lax.broadcasted_iota(jnp.int32, (tm, tm), 1)
        earlier = (c < r).astype(BF16)
        before = jnp.dot(earlier, member.astype(BF16), preferred_element_type=F32)
        base = before + run_ref[0:1] + start_ref[0:1]
        out = jnp.zeros(idx.shape, F32)
        for k in range(TOP_K):
            d = jnp.sum(onehots[k] * base, axis=1, keepdims=True)
            out = jnp.where(lane == k, d, out)
        dest_ref[...] = out.astype(jnp.int32)
        run_ref[...] = run_ref[...] + tile_count


def _route(rows, idx):
    tm = rows.tm
    dest, counts = pl.pallas_call(
        _route_kernel,
        out_shape=(jax.ShapeDtypeStruct((rows.n_tokens, LANES), jnp.int32),
                   jax.ShapeDtypeStruct((SUBLANES, LANES), F32)),
        grid=(2, rows.n_tiles),
        in_specs=[pl.BlockSpec((tm, LANES), lambda ph, i: (i, 0))],
        out_specs=(pl.BlockSpec((tm, LANES), lambda ph, i: (i * ph, 0)),
                   pl.BlockSpec((SUBLANES, LANES), lambda ph, i: (0, 0))),
        scratch_shapes=[pltpu.VMEM((SUBLANES, LANES), F32), pltpu.VMEM((SUBLANES, LANES), F32)],
        compiler_params=_params("arbitrary", "arbitrary"),
        name="moe_route",
    )(idx)
    return dest, counts


def _dispatch_kernel(dest_ref, h_ref, zero_ref, xs_ref, sem):
    del zero_ref
    i = pl.program_id(0)
    tm = h_ref.shape[0] // ROW_CHUNKS

    def body(t, carry):
        for k in range(TOP_K):
            d = dest_ref[(i * tm + t) * TOP_K + k]
            pltpu.make_async_copy(h_ref.at[pl.ds(t * ROW_CHUNKS, ROW_CHUNKS)],
                                  xs_ref.at[pl.ds(d * ROW_CHUNKS, ROW_CHUNKS)], sem).start(priority=k % 2)
        return carry

    lax.fori_loop(0, tm, body, 0, unroll=8)
    n_words = tm * TOP_K * ROW_CHUNKS
    pltpu.make_async_copy(xs_ref.at[pl.ds(0, n_words)], xs_ref.at[pl.ds(0, n_words)], sem).wait()


def _dispatch(rows, dest_flat, h2u, n_rows):
    tm = rows.tm
    zeros = jnp.zeros((n_rows * ROW_CHUNKS, LANES), jnp.uint32)
    grid_spec = pltpu.PrefetchScalarGridSpec(
        num_scalar_prefetch=1, grid=(rows.n_tiles,),
        in_specs=[pl.BlockSpec((tm * ROW_CHUNKS, LANES), lambda i, d: (i, 0)),
                  pl.BlockSpec(memory_space=pl.ANY)],
        out_specs=pl.BlockSpec(memory_space=pl.ANY),
        scratch_shapes=[pltpu.SemaphoreType.DMA(())])
    return pl.pallas_call(
        _dispatch_kernel,
        out_shape=jax.ShapeDtypeStruct((n_rows * ROW_CHUNKS, LANES), jnp.uint32),
        grid_spec=grid_spec,
        input_output_aliases={2: 0},
        compiler_params=_params("arbitrary"),
        name="moe_dispatch",
    )(dest_flat, h2u, zeros)


def _expert_kernel(be_ref, nu_ref, x_ref, wgu_ref, wd_ref, bg_ref, bu_ref, bd_ref, y_ref,
                   wt_s, wg_s, wu_s, wd_s):
    i = pl.program_id(0)
    used = i < nu_ref[0]
    new_expert = (i == 0) | (be_ref[i] != be_ref[jnp.maximum(i - 1, 0)])

    @pl.when(used & new_expert)
    def _():
        n = wgu_ref.shape[1]
        n_lane_tiles = wt_s.shape[0]
        for c in range(n // EXPERT_XPOSE_CHUNK):
            cols = slice(c * EXPERT_XPOSE_CHUNK, (c + 1) * EXPERT_XPOSE_CHUNK)
            wt = wgu_ref[:, cols].T
            for j in range(n_lane_tiles):
                wt_s[j, cols, :] = wt[:, j * LANES:(j + 1) * LANES]
        for j in range(n_lane_tiles):
            lanes = slice(j * LANES, (j + 1) * LANES)
            wg_s[:, lanes] = wt_s[j, pl.ds(0, n // 2, stride=2), :].astype(BF16)
            wu_s[:, lanes] = wt_s[j, pl.ds(1, n // 2, stride=2), :].astype(BF16)
        wd_s[...] = wd_ref[...].astype(BF16)

    @pl.when(used)
    def _():
        x = _unpack_pairs(_load_token_rows(x_ref, MOE_BLOCK)).astype(BF16)
        y = bd_ref[...]
        for c in range(wd_s.shape[0] // EXPERT_FF_CHUNK):
            ff = slice(c * EXPERT_FF_CHUNK, (c + 1) * EXPERT_FF_CHUNK)
            g = lax.dot_general(x, wg_s[ff, :], _NT, preferred_element_type=F32) + bg_ref[:, ff]
            u = lax.dot_general(x, wu_s[ff, :], _NT, preferred_element_type=F32) + bu_ref[:, ff]
            gate = jnp.minimum(g, SWIGLU_LIMIT)
            up = jnp.clip(u, -SWIGLU_LIMIT, SWIGLU_LIMIT)
            glu = gate * jax.nn.sigmoid(SWIGLU_ALPHA * gate)
            a = ((up + 1.0) * glu).astype(BF16)
            y = y + jnp.dot(a, wd_s[ff, :], preferred_element_type=F32)
        _store_token_rows(y_ref, _pack_pairs(y.astype(BF16).astype(F32)))

    @pl.when(i >= nu_ref[0])
    def _():
        y_ref[...] = jnp.zeros(y_ref.shape, y_ref.dtype)


def _experts(x_sorted, block_e, n_used, w_gu, w_dn, b_g, b_u, b_d):
    rows = x_sorted.shape[0] // ROW_CHUNKS
    n_blocks = rows // MOE_BLOCK
    d_gu = w_gu.shape[2]
    d_ff = w_dn.shape[1]
    by_expert = lambda i, be, nu: (be[i], 0, 0)
    xspec = pl.BlockSpec((MOE_BLOCK * ROW_CHUNKS, LANES), lambda i, be, nu: (i, 0))
    grid_spec = pltpu.PrefetchScalarGridSpec(
        num_scalar_prefetch=2, grid=(n_blocks,),
        in_specs=[xspec,
                  pl.BlockSpec((None, D_MODEL, d_gu), by_expert),
                  pl.BlockSpec((None, d_ff, D_MODEL), by_expert),
                  pl.BlockSpec((None, 1, d_ff), by_expert),
                  pl.BlockSpec((None, 1, d_ff), by_expert),
                  pl.BlockSpec((None, 1, D_MODEL), by_expert)],
        out_specs=xspec,
        scratch_shapes=[pltpu.VMEM((D_MODEL // LANES, d_gu, LANES), F32),
                        pltpu.VMEM((d_ff, D_MODEL), BF16), pltpu.VMEM((d_ff, D_MODEL), BF16),
                        pltpu.VMEM((d_ff, D_MODEL), BF16)])
    return pl.pallas_call(
        _expert_kernel,
        out_shape=jax.ShapeDtypeStruct((rows * ROW_CHUNKS, LANES), jnp.uint32),
        grid_spec=grid_spec,
        compiler_params=pltpu.CompilerParams(dimension_semantics=("arbitrary",),
                                             vmem_limit_bytes=EXPERT_VMEM_LIMIT_BYTES),
        name="experts",
    )(block_e, n_used, x_sorted, w_gu, w_dn, b_g, b_u, b_d)


def _combine_kernel(dest_ref, xm_ref, gt_ref, g2p_ref, g2s_ref, yb_ref, yp_ref, ys_ref, rows_ref, sem,
                    *, n_prompt_tiles):
    i = pl.program_id(0)
    tm = xm_ref.shape[0]

    def body(t, carry):
        for k in range(TOP_K):
            d = dest_ref[(i * tm + t) * TOP_K + k]
            pltpu.make_async_copy(yb_ref.at[pl.ds(d * ROW_CHUNKS, ROW_CHUNKS)],
                                  rows_ref.at[k, pl.ds(t * ROW_CHUNKS, ROW_CHUNKS)], sem).start(priority=k % 2)
        return carry

    lax.fori_loop(0, tm, body, 0, unroll=8)
    pltpu.make_async_copy(rows_ref, rows_ref, sem).wait()
    gt = gt_ref[...]
    acc = jnp.zeros(xm_ref.shape, F32)
    for k in range(TOP_K):
        acc = acc + gt[:, k:k + 1] * _unpack_pairs(_load_token_rows(rows_ref.at[k], tm))
    is_prompt = i < n_prompt_tiles
    y = xm_ref[...] + jnp.where(is_prompt, g2p_ref[0], g2s_ref[0]) * acc

    @pl.when(is_prompt)
    def _():
        yp_ref[...] = y

    @pl.when(jnp.logical_not(is_prompt))
    def _():
        ys_ref[...] = y


def _combine(tiles, dest_flat, xm, gates, gate2_p, gate2_s, y_buf, n_prompt, rows_per_seq):
    tm = tiles.tm
    n_prompt_tiles = n_prompt // tm
    n_sample = tiles.n_tokens - n_prompt
    assert n_sample == tm and gate2_s.shape[1] == tm, "the sample group is one token tile"
    tiles_per_seq = rows_per_seq // tm
    last_seq = gate2_p.shape[0] - 1
    grid_spec = pltpu.PrefetchScalarGridSpec(
        num_scalar_prefetch=1, grid=(tiles.n_tiles,),
        in_specs=[pl.BlockSpec((tm, D_MODEL), lambda i, d: (i, 0)),
                  pl.BlockSpec((tm, LANES), lambda i, d: (i, 0)),
                  pl.BlockSpec((1, 1, D_MODEL), lambda i, d: (jnp.minimum(i // tiles_per_seq, last_seq), 0, 0)),
                  pl.BlockSpec((1, tm, D_MODEL), lambda i, d: (0, 0, 0)),
                  pl.BlockSpec(memory_space=pl.ANY)],
        out_specs=(pl.BlockSpec((tm, D_MODEL), lambda i, d: (jnp.minimum(i, n_prompt_tiles - 1), 0)),
                   pl.BlockSpec((tm, D_MODEL), lambda i, d: (0, 0))),
        scratch_shapes=[pltpu.VMEM((TOP_K, tm * ROW_CHUNKS, LANES), jnp.uint32),
                        pltpu.SemaphoreType.DMA(())])
    return pl.pallas_call(
        functools.partial(_combine_kernel, n_prompt_tiles=n_prompt_tiles),
        out_shape=(jax.ShapeDtypeStruct((n_prompt, D_MODEL), F32),
                   jax.ShapeDtypeStruct((n_sample, D_MODEL), F32)),
        grid_spec=grid_spec,
        compiler_params=_params("arbitrary"),
        name="moe_combine",
    )(dest_flat, xm, gates, gate2_p, gate2_s, y_buf)


class _Tiles:
    def __init__(self, n_tokens, tm):
        self.n_tokens, self.tm, self.n_tiles = n_tokens, tm, n_tokens // tm


def _moe(xm, h2u, idx, gates, gate2_p, gate2_s, moe_w, n_prompt, rows_per_seq):
    rows = _Tiles(xm.shape[0], MOE_TOKEN_TILE)
    a = rows.n_tokens * TOP_K
    n_blocks = -(-a // MOE_BLOCK) + N_EXPERTS
    dest, counts = _route(rows, idx)
    dest_flat = dest[:, :TOP_K].reshape(a)
    cnt = counts[0, :N_EXPERTS].astype(jnp.int32)
    pad_end = jnp.cumsum((cnt + MOE_BLOCK - 1) // MOE_BLOCK * MOE_BLOCK)
    block_row = jnp.arange(n_blocks, dtype=jnp.int32) * MOE_BLOCK
    block_e = jnp.minimum(jnp.sum((pad_end[None, :] <= block_row[:, None]).astype(jnp.int32), axis=1),
                          N_EXPERTS - 1)
    n_used = (pad_end[-1:] // MOE_BLOCK).astype(jnp.int32)
    x_sorted = _dispatch(rows, dest_flat, h2u, n_blocks * MOE_BLOCK)
    y_buf = _experts(x_sorted, block_e, n_used, *moe_w)
    return _combine(rows, dest_flat, xm, gates, gate2_p, gate2_s, y_buf, n_prompt, rows_per_seq)


def _group(rows, x, ada, state_rows, qk_and_attend, shared, total_tokens, first_row, token_bufs):
    (g_mix, w_vconv, w_conv, w_o_b, g_ffn, wr_hi, wr_lo, b_r, mid_dtype) = shared
    shift1, scale1, gate1, shift2, scale2, gate2 = [rows.mod_array(m) for m in jnp.split(ada, 6, axis=-1)]
    v_f, v_b, sga, cvp, tail = _vconv_proj(rows, x, g_mix, scale1, shift1, w_vconv, w_conv, state_rows, mid_dtype)
    mix, k_out = qk_and_attend(x, scale1, shift1, v_f, v_b, sga, cvp)
    token_bufs = _out_proj(rows, mix, x, w_o_b, gate1, g_ffn, scale2, shift2, wr_hi, wr_lo, b_r,
                           total_tokens, first_row, token_bufs)
    return token_bufs, gate2, k_out, v_f, tail


def kernel(x_prompt, x_sample, c_prompt, c_sample, cache_k, cache_v, state_conv, page_table, w_ada, b_ada, g_norm_mix, w_in, g_q, g_k, lambda_q1, lambda_k1, lambda_q2, lambda_k2, g_subln, w_conv, w_o, g_norm_ffn, w_router, b_router, w_gate_up, b_gate_up, w_down, b_down):
    assert w_in.shape[0] == 1, "single-layer stack"
    batch, seq, _ = x_prompt.shape
    n_seq, n_new, _ = x_sample.shape
    tp, ts = batch * seq, n_seq * n_new
    n_chunks = D_MODEL // HEAD_DIM

    ada = _ada(jnp.concatenate([c_prompt, c_sample], axis=0), w_ada[0], b_ada[0][None])

    w_in_b = w_in[0].astype(BF16)
    w_kt = w_in[0][:, D_MODEL:2 * D_MODEL].T.astype(BF16)
    w_vconv = w_in_b[:, 2 * D_MODEL:]
    w_o_b = w_o[0].astype(BF16)
    gqk = jnp.stack([jnp.tile(g_q[0], n_chunks), jnp.tile(g_k[0], n_chunks)])[:, None, :]
    gk_col = jnp.tile(g_k[0], n_chunks)[:, None]
    blk = jnp.arange(D_MODEL, dtype=jnp.int32) // HEAD_DIM
    pmat = (blk[:, None] == blk[None, :]).astype(BF16)
    wr = jnp.pad(w_router[0], ((0, 0), (0, LANES - N_EXPERTS)))
    wr_hi = wr.astype(BF16)
    wr_lo = (wr - wr_hi.astype(F32)).astype(BF16)
    b_r = jnp.pad(b_router[0], (0, LANES - N_EXPERTS), constant_values=NEG_BIG)[None]
    moe_w = (w_gate_up[0], w_down[0], b_gate_up[0][:, None, 0::2], b_gate_up[0][:, None, 1::2],
             b_down[0][:, None, :])
    g_mix = g_norm_mix[0][None]
    g_ffn = g_norm_ffn[0][None]
    g_sub = g_subln[0][None]
    lams = (lambda_q1[0][None], lambda_k1[0][None], lambda_q2[0][None], lambda_k2[0][None])

    def shared(mid_dtype):
        return (g_mix, w_vconv, w_conv[0], w_o_b, g_ffn, wr_hi, wr_lo, b_r, mid_dtype)

    rows_p = _Rows(tp, seq)

    def attend_p(x, scale1, shift1, v_f, v_b, sga, cvp):
        q_b, kt_f, kt_b = _qkt_proj(rows_p, x, g_mix, scale1, shift1, w_in_b, gqk, pmat, w_kt, gk_col, batch, seq)
        mix = _attn_prompt(q_b, kt_b, v_b, sga, cvp, g_sub, lams, batch, seq)
        return mix, kt_f

    bufs, gate2_p, kt_p, v_p, tail_p = _group(rows_p, x_prompt.reshape(tp, D_MODEL), ada[:batch], None, attend_p,
                                              shared(BF16), tp + ts, 0, None)

    rows_s = _Rows(ts, n_new)
    st = state_conv[0]
    zeros = jnp.zeros((n_seq, n_new - 2, D_MODEL), F32)
    s1 = jnp.concatenate([st[:, 1:2], jnp.zeros((n_seq, n_new - 1, D_MODEL), F32)], axis=1).reshape(ts, D_MODEL)
    s2 = jnp.concatenate([st, zeros], axis=1).reshape(ts, D_MODEL)

    def attend_s(x, scale1, shift1, v_f, v_b, sga, cvp):
        _, qk_f = _qk_proj(rows_s, x, g_mix, scale1, shift1, w_in_b, gqk, pmat)
        mix = _attn_sample(qk_f, v_f, cache_k[0], cache_v[0], page_table, sga, cvp, g_sub, lams, n_seq, n_new)
        return mix, qk_f[1]

    bufs, gate2_s, k_s, v_s, tail_s = _group(rows_s, x_sample.reshape(ts, D_MODEL), ada[batch:], (s1, s2), attend_s,
                                             shared(F32), tp + ts, tp, bufs)

    y_p, y_s = _moe(*bufs, gate2_p, gate2_s, moe_w, tp, seq)

    tail_s = tail_s.reshape(n_seq, n_new, D_MODEL)
    k_p = kt_p.reshape(1, batch, N_HEADS, 2, HEAD_DIM, seq).transpose(0, 1, 5, 2, 3, 4)
    return (y_p.reshape(batch, seq, D_MODEL),
            y_s.reshape(n_seq, n_new, D_MODEL),
            k_p,
            v_p.reshape(1, batch, seq, N_HEADS, V_DIM),
            tail_p[:, SUBLANES - (CONV_WIDTH - 1):][None],
            k_s.reshape(1, n_seq, n_new, N_HEADS, 2, HEAD_DIM),
            v_s.reshape(1, n_seq, n_new, N_HEADS, V_DIM),
            tail_s[:, n_new - (CONV_WIDTH - 1):][None])
```

```python
import functools
import math

import jax
import jax.numpy as jnp
from jax import lax
from jax.experimental import pallas as pl
from jax.experimental.pallas import tpu as pltpu

F32 = jnp.float32
BF16 = jnp.bfloat16

D_MODEL = 1024
HEAD_DIM = 64
V_DIM = 2 * HEAD_DIM
N_HEADS = D_MODEL // V_DIM
ATTN_SCALE = HEAD_DIM ** -0.5
LOG2_E = math.log2(math.e)
CONV_WIDTH = 3
PAGE_SIZE = 128
N_EXPERTS = 32
TOP_K = 4
SWIGLU_LIMIT = 7.0
SWIGLU_ALPHA = 1.702
NORM_EPS = 1e-6
LAMBDA_INIT = 0.8 - 0.6 * math.exp(-0.3 * 0)

VMEM_LIMIT_BYTES = 48 * 1024 * 1024
EXPERT_VMEM_LIMIT_BYTES = 56 * 1024 * 1024
LANES = 128
PACKED = D_MODEL // 2
ROW_CHUNKS = PACKED // LANES
SUBLANES = 8

ROW_TILE_PROMPT = 512
ATTN_BLOCK = 512
MOE_BLOCK = 512
MOE_TOKEN_TILE = 256
MOE_WIDE_TOKEN_TILE = 1280
EXPERT_XPOSE_CHUNK = 512
EXPERT_FF_CHUNK = 512
ADA_COL_TILE = 1536
SAMPLE_PAGES_PER_STEP = 16
NEG_BIG = -1e30

_NT = (((1,), (1,)), ((), ()))


def _params(*sem):
    return pltpu.CompilerParams(dimension_semantics=sem, vmem_limit_bytes=VMEM_LIMIT_BYTES)


def _rms_mod(x, g, scale, shift):
    ms = jnp.mean(x * x, axis=-1, keepdims=True)
    return (x * lax.rsqrt(ms + NORM_EPS) * g) * (1.0 + scale) + shift


def _ada_kernel(c_ref, w_ref, b_ref, o_ref):
    c = c_ref[...]
    s = (c * jax.nn.sigmoid(c)).astype(BF16)
    o_ref[...] = jnp.dot(s, w_ref[...].astype(BF16), preferred_element_type=F32) + b_ref[...]


def _ada(c_all, w_ada, b_ada):
    n = c_all.shape[0]
    width = w_ada.shape[1]
    return pl.pallas_call(
        _ada_kernel,
        out_shape=jax.ShapeDtypeStruct((n, width), F32),
        grid=(width // ADA_COL_TILE,),
        in_specs=[pl.BlockSpec((n, D_MODEL), lambda j: (0, 0)),
                  pl.BlockSpec((D_MODEL, ADA_COL_TILE), lambda j: (0, j)),
                  pl.BlockSpec((1, ADA_COL_TILE), lambda j: (0, j))],
        out_specs=pl.BlockSpec((n, ADA_COL_TILE), lambda j: (0, j)),
        compiler_params=_params("arbitrary"),
        name="ada",
    )(c_all, w_ada, b_ada)


class _Rows:
    def __init__(self, n_tokens, rows_per_batch):
        if rows_per_batch >= ROW_TILE_PROMPT:
            self.tm = ROW_TILE_PROMPT
            self.tiles_per_batch = rows_per_batch // self.tm
            self.mod_rows = 1
        else:
            self.tm = n_tokens
            self.tiles_per_batch = None
            self.mod_rows = n_tokens
        self.n_tokens = n_tokens
        self.rows_per_batch = rows_per_batch
        self.n_tiles = n_tokens // self.tm

    def mod_array(self, m):
        if self.tiles_per_batch is not None:
            return m[:, None, :]
        return jnp.repeat(m, self.rows_per_batch, axis=0)[None]

    def mod_spec(self, clamp=False):
        if self.tiles_per_batch is not None:
            tpb, last = self.tiles_per_batch, self.n_tiles - 1
            if clamp:
                return pl.BlockSpec((1, 1, D_MODEL), lambda i, *_: (jnp.minimum(i, last) // tpb, 0, 0))
            return pl.BlockSpec((1, 1, D_MODEL), lambda i, *_: (i // tpb, 0, 0))
        return pl.BlockSpec((1, self.mod_rows, D_MODEL), lambda i, *_: (0, 0, 0))

    def row_spec(self, width=D_MODEL, clamp=False):
        if clamp:
            last = self.n_tiles - 1
            return pl.BlockSpec((self.tm, width), lambda i, *_: (jnp.minimum(i, last), 0))
        return pl.BlockSpec((self.tm, width), lambda i, *_: (i, 0))


def _qk_kernel(x_ref, g_ref, sc_ref, sh_ref, w_ref, gqk_ref, p_ref, qkb_ref, qkf_ref, h_ref):
    j = pl.program_id(1)

    @pl.when(j == 0)
    def _():
        h_ref[...] = _rms_mod(x_ref[...], g_ref[...], sc_ref[0], sh_ref[0]).astype(BF16)

    z = jnp.dot(h_ref[...], w_ref[...], preferred_element_type=F32)
    ss = jnp.dot((z * z).astype(BF16), p_ref[...], preferred_element_type=F32)
    zn = z * lax.rsqrt(ss * (1.0 / HEAD_DIM) + NORM_EPS) * gqk_ref[0]
    qkf_ref[0] = zn
    scale = jnp.where(j == 0, ATTN_SCALE, 1.0)
    qkb_ref[...] = (zn * scale).astype(BF16)


def _qk_proj(rows, x, g, scale, shift, w_in_b, gqk, pmat):
    t = rows.n_tokens
    return pl.pallas_call(
        _qk_kernel,
        out_shape=(jax.ShapeDtypeStruct((t, 2 * D_MODEL), BF16),
                   jax.ShapeDtypeStruct((2, t, D_MODEL), F32)),
        grid=(rows.n_tiles, 2),
        in_specs=[rows.row_spec(),
                  pl.BlockSpec((1, D_MODEL), lambda i, j: (0, 0)),
                  rows.mod_spec(), rows.mod_spec(),
                  pl.BlockSpec((D_MODEL, D_MODEL), lambda i, j: (0, j)),
                  pl.BlockSpec((1, 1, D_MODEL), lambda i, j: (j, 0, 0)),
                  pl.BlockSpec((D_MODEL, D_MODEL), lambda i, j: (0, 0))],
        out_specs=(pl.BlockSpec((rows.tm, D_MODEL), lambda i, j: (i, j)),
                   pl.BlockSpec((1, rows.tm, D_MODEL), lambda i, j: (j, i, 0))),
        scratch_shapes=[pltpu.VMEM((rows.tm, D_MODEL), BF16)],
        compiler_params=_params("parallel", "arbitrary"),
        name="qk_proj",
    )(x, g, scale, shift, w_in_b, gqk, pmat)


def _qkt_kernel(x_ref, g_ref, sc_ref, sh_ref, wq_ref, gq_ref, p_ref, wt_ref, gk_ref, qb_ref, kf_ref, kb_ref):
    h = _rms_mod(x_ref[...], g_ref[...], sc_ref[0], sh_ref[0]).astype(BF16)
    z = jnp.dot(h, wq_ref[...], preferred_element_type=F32)
    ss = jnp.dot((z * z).astype(BF16), p_ref[...], preferred_element_type=F32)
    zn = z * lax.rsqrt(ss * (1.0 / HEAD_DIM) + NORM_EPS) * gq_ref[0]
    qb_ref[...] = (zn * (ATTN_SCALE * LOG2_E)).astype(BF16)
    zt = lax.dot_general(wt_ref[...], h, _NT, preferred_element_type=F32)
    tm = zt.shape[1]
    z3 = zt.reshape(D_MODEL // HEAD_DIM, HEAD_DIM, tm)
    s3 = jnp.sum(z3 * z3, axis=1, keepdims=True)
    g3 = gk_ref[...].reshape(D_MODEL // HEAD_DIM, HEAD_DIM, 1)
    kn = (z3 * lax.rsqrt(s3 * (1.0 / HEAD_DIM) + NORM_EPS) * g3).reshape(D_MODEL, tm)
    kf_ref[...] = kn
    kb_ref[...] = kn.astype(BF16)


def _qkt_proj(rows, x, g, scale, shift, w_in_b, gqk, pmat, w_kt, gk_col, batch, seq):
    tpb = rows.tiles_per_batch
    once = pl.Buffered(1)
    kt_spec = pl.BlockSpec((None, D_MODEL, rows.tm), lambda i: (i // tpb, 0, i % tpb))
    return pl.pallas_call(
        _qkt_kernel,
        out_shape=(jax.ShapeDtypeStruct((rows.n_tokens, D_MODEL), BF16),
                   jax.ShapeDtypeStruct((batch, D_MODEL, seq), F32),
                   jax.ShapeDtypeStruct((batch, D_MODEL, seq), BF16)),
        grid=(rows.n_tiles,),
        in_specs=[rows.row_spec(),
                  pl.BlockSpec((1, D_MODEL), lambda i: (0, 0)),
                  rows.mod_spec(), rows.mod_spec(),
                  pl.BlockSpec((D_MODEL, D_MODEL), lambda i: (0, 0), pipeline_mode=once),
                  pl.BlockSpec((1, 1, D_MODEL), lambda i: (0, 0, 0)),
                  pl.BlockSpec((D_MODEL, D_MODEL), lambda i: (0, 0), pipeline_mode=once),
                  pl.BlockSpec((D_MODEL, D_MODEL), lambda i: (0, 0), pipeline_mode=once),
                  pl.BlockSpec((D_MODEL, 1), lambda i: (0, 0))],
        out_specs=(rows.row_spec(), kt_spec, kt_spec),
        compiler_params=_params("parallel"),
        name="qkt_proj",
    )(x, g, scale, shift, w_in_b, gqk, pmat, w_kt, gk_col)


def _vconv_kernel(*refs, tiles_per_batch, rows_per_batch, tail_rows):
    if tiles_per_batch is None:
        (x_ref, g_ref, sc_ref, sh_ref, w_ref, wc_ref, s1_ref, s2_ref,
         vf_ref, vb_ref, sga_ref, cvp_ref, tail_ref, carry_ref) = refs
    else:
        (x_ref, g_ref, sc_ref, sh_ref, w_ref, wc_ref,
         vf_ref, vb_ref, sga_ref, cvp_ref, tail_ref, carry_ref) = refs
    i = pl.program_id(0)
    h = _rms_mod(x_ref[...], g_ref[...], sc_ref[0], sh_ref[0]).astype(BF16)

    def proj(k):
        return jnp.dot(h, w_ref[:, k * D_MODEL:(k + 1) * D_MODEL], preferred_element_type=F32)

    v = proj(0)
    vf_ref[...] = v
    vb_ref[...] = v.astype(BF16)
    sga_ref[...] = jax.nn.sigmoid(proj(4)).astype(sga_ref.dtype)
    u = proj(2) * proj(3)
    tm = u.shape[0]
    row = lax.broadcasted_iota(jnp.int32, (tm, 1), 0)
    r1 = pltpu.roll(u, 1, 0)
    r2 = pltpu.roll(u, 2, 0)
    if tiles_per_batch is None:
        t = row & (rows_per_batch - 1)
        u1 = jnp.where(t >= 1, r1, s1_ref[...])
        u2 = jnp.where(t >= 2, r2, s2_ref[...])
    else:
        first = (i % tiles_per_batch) == 0
        c = jnp.where(first, 0.0, carry_ref[...])
        u1 = jnp.where(row == 0, c[7:8], r1)
        u2 = jnp.where(row == 0, c[6:7], jnp.where(row == 1, c[7:8], r2))
        carry_ref[...] = u[tm - SUBLANES:]
    tail_ref[0] = u[tm - tail_rows:]
    wc = wc_ref[...]
    yc = wc[0:1] * u2 + wc[1:2] * u1 + wc[2:3] * u
    cvp_ref[...] = (jax.nn.sigmoid(proj(5)) * (proj(1) * yc)).astype(cvp_ref.dtype)


def _vconv_proj(rows, x, g, scale, shift, w_vconv, w_conv, state_rows, out_dtype):
    t = rows.n_tokens
    sample_mode = rows.tiles_per_batch is None
    tail_rows = rows.tm if sample_mode else SUBLANES
    n_tail_blocks = 1 if sample_mode else t // rows.rows_per_batch
    in_specs = [rows.row_spec(),
                pl.BlockSpec((1, D_MODEL), lambda i: (0, 0)),
                rows.mod_spec(), rows.mod_spec(),
                pl.BlockSpec(w_vconv.shape, lambda i: (0, 0), pipeline_mode=pl.Buffered(1)),
                pl.BlockSpec((CONV_WIDTH, D_MODEL), lambda i: (0, 0))]
    args = [x, g, scale, shift, w_vconv, w_conv]
    if sample_mode:
        in_specs += [rows.row_spec(), rows.row_spec()]
        args += list(state_rows)
        tail_spec = pl.BlockSpec((1, tail_rows, D_MODEL), lambda i: (0, 0, 0))
    else:
        tpb = rows.tiles_per_batch
        tail_spec = pl.BlockSpec((1, tail_rows, D_MODEL), lambda i: (i // tpb, 0, 0))
    kern = functools.partial(_vconv_kernel, tiles_per_batch=rows.tiles_per_batch,
                             rows_per_batch=rows.rows_per_batch, tail_rows=tail_rows)
    return pl.pallas_call(
        kern,
        out_shape=(jax.ShapeDtypeStruct((t, D_MODEL), F32),
                   jax.ShapeDtypeStruct((t, D_MODEL), BF16),
                   jax.ShapeDtypeStruct((t, D_MODEL), out_dtype),
                   jax.ShapeDtypeStruct((t, D_MODEL), out_dtype),
                   jax.ShapeDtypeStruct((n_tail_blocks, tail_rows, D_MODEL), F32)),
        grid=(rows.n_tiles,),
        in_specs=in_specs,
        out_specs=(rows.row_spec(), rows.row_spec(), rows.row_spec(), rows.row_spec(), tail_spec),
        scratch_shapes=[pltpu.VMEM((SUBLANES, D_MODEL), F32)],
        compiler_params=_params("arbitrary"),
        name="vconv_proj",
    )(*args)


def _lambda_value(lq1, lk1, lq2, lk2):
    e1 = jnp.exp(jnp.sum(lq1 * lk1, axis=-1, keepdims=True))
    e2 = jnp.exp(jnp.sum(lq2 * lk2, axis=-1, keepdims=True))
    return e1 - e2 + LAMBDA_INIT


def _subln_mix(o, g_sub, sga, cvp):
    ms = jnp.mean(o * o, axis=-1, keepdims=True)
    attn = (o * lax.rsqrt(ms + NORM_EPS) * g_sub) * (1.0 - LAMBDA_INIT)
    return sga * attn + cvp


def _flash_update(s, v_ones, m_ref, l_ref, acc_ref):
    m_prev = m_ref[...]
    m_next = jnp.maximum(m_prev, jnp.max(s, axis=1, keepdims=True))
    alpha = jnp.exp2(m_prev - m_next)
    p = jnp.exp2(s - jnp.concatenate([m_next] * (s.shape[1] // LANES), axis=1)).astype(BF16)
    pv = jnp.dot(p, v_ones, preferred_element_type=F32)
    acc_ref[...] = alpha * acc_ref[...] + pv[:, :V_DIM]
    l_ref[...] = alpha * l_ref[...] + pv[:, V_DIM:]
    m_ref[...] = m_next


def _attn_prompt_kernel(qi_ref, ki_ref, q_ref, kt_ref, v_ref, sga_ref, cvp_ref, gs_ref,
                        lq1_ref, lk1_ref, lq2_ref, lk2_ref,
                        o_ref, m0_ref, l0_ref, a0_ref, m1_ref, l1_ref, a1_ref):
    qi = qi_ref[pl.program_id(1)]
    ki = ki_ref[pl.program_id(1)]
    states = ((m0_ref, l0_ref, a0_ref), (m1_ref, l1_ref, a1_ref))

    @pl.when(ki == 0)
    def _():
        for m_ref, l_ref, a_ref in states:
            m_ref[...] = jnp.full(m_ref.shape, -jnp.inf, F32)
            l_ref[...] = jnp.zeros(l_ref.shape, F32)
            a_ref[...] = jnp.zeros(a_ref.shape, F32)

    tq, tk = q_ref.shape[0], kt_ref.shape[1]

    def step(masked):
        lane = lax.broadcasted_iota(jnp.int32, (1, V_DIM), 1)
        zero = jnp.zeros((), BF16)
        ones = jnp.ones((tk, V_DIM), BF16)
        if masked:
            r = lax.broadcasted_iota(jnp.int32, (tq, tk), 0)
            c = lax.broadcasted_iota(jnp.int32, (tq, tk), 1)
            keep = c <= r
        for h in range(N_HEADS):
            cols = slice(h * V_DIM, (h + 1) * V_DIM)
            q = q_ref[:, cols]
            kt = kt_ref[cols, :]
            v_ones = jnp.concatenate([v_ref[:, cols], ones], axis=1)
            qs = (jnp.where(lane < HEAD_DIM, q, zero), jnp.where(lane >= HEAD_DIM, q, zero))
            for qm, (m_ref, l_ref, a_ref) in zip(qs, states):
                s = jnp.dot(qm, kt, preferred_element_type=F32)
                if masked:
                    s = jnp.where(keep, s, -jnp.inf)
                _flash_update(s, v_ones, m_ref.at[h], l_ref.at[h], a_ref.at[h])

    @pl.when(ki < qi)
    def _():
        step(False)

    @pl.when(ki == qi)
    def _():
        step(True)
        lam = _lambda_value(lq1_ref[...], lk1_ref[...], lq2_ref[...], lk2_ref[...])
        for h in range(N_HEADS):
            cols = slice(h * V_DIM, (h + 1) * V_DIM)
            o = a0_ref[h] / l0_ref[h] - lam * (a1_ref[h] / l1_ref[h])
            mix = _subln_mix(o, gs_ref[...], sga_ref[:, cols].astype(F32), cvp_ref[:, cols].astype(F32))
            o_ref[:, cols] = mix.astype(o_ref.dtype)


def _attn_prompt(q_b, kt_b, v_b, sga, cvp, g_sub, lams, batch, seq):
    nb = seq // ATTN_BLOCK
    tq = ATTN_BLOCK
    pairs = [(qi, ki) for qi in range(nb) for ki in range(qi + 1)]
    qi_tab = jnp.array([p[0] for p in pairs], jnp.int32)
    ki_tab = jnp.array([p[1] for p in pairs], jnp.int32)
    q_spec = pl.BlockSpec((tq, D_MODEL), lambda b, p, qt, kt: (b * nb + qt[p], 0))
    k_spec = pl.BlockSpec((None, D_MODEL, tq), lambda b, p, qt, kt: (b, 0, kt[p]))
    v_spec = pl.BlockSpec((tq, D_MODEL), lambda b, p, qt, kt: (b * nb + kt[p], 0))
    vec64 = pl.BlockSpec((1, HEAD_DIM), lambda b, p, qt, kt: (0, 0))
    grid_spec = pltpu.PrefetchScalarGridSpec(
        num_scalar_prefetch=2,
        grid=(batch, len(pairs)),
        in_specs=[q_spec, k_spec, v_spec, q_spec, q_spec,
                  pl.BlockSpec((1, V_DIM), lambda b, p, qt, kt: (0, 0)),
                  vec64, vec64, vec64, vec64],
        out_specs=q_spec,
        scratch_shapes=[pltpu.VMEM((N_HEADS, tq, V_DIM), F32)] * 6)
    return pl.pallas_call(
        _attn_prompt_kernel,
        out_shape=jax.ShapeDtypeStruct((batch * seq, D_MODEL), BF16),
        grid_spec=grid_spec,
        compiler_params=_params("parallel", "arbitrary"),
        name="attn_prompt",
    )(qi_tab, ki_tab, q_b, kt_b, v_b, sga, cvp, g_sub, *lams)


def _attn_sample_kernel(pt_ref, q_ref, kn_ref, vn_ref, *rest, n_new, n_group):
    kc_refs = rest[:n_group]
    vc_refs = rest[n_group:2 * n_group]
    (spread_ref, own_ref, sga_ref, cvp_ref, gs_ref, lq1_ref, lk1_ref, lq2_ref, lk2_ref,
     o_ref, qbd_ref, m_ref, l_ref, acc_ref) = rest[2 * n_group:]
    p = pl.program_id(1)
    rows_per_head = 2 * n_new

    @pl.when(p == 0)
    def _():
        q = q_ref[...] * ATTN_SCALE
        qrep = jnp.concatenate([q] * (N_HEADS * 2), axis=0)
        r = lax.broadcasted_iota(jnp.int32, qrep.shape, 0)
        c = lax.broadcasted_iota(jnp.int32, qrep.shape, 1)
        qbd_ref[...] = jnp.where(c // HEAD_DIM == r // n_new, qrep, 0.0).astype(BF16)
        m_ref[...] = jnp.full(m_ref.shape, -jnp.inf, F32)
        l_ref[...] = jnp.zeros(l_ref.shape, F32)
        acc_ref[...] = jnp.zeros(acc_ref.shape, F32)

    def update(s, pv_of):
        m_prev = m_ref[...]
        m_next = jnp.maximum(m_prev, jnp.max(s, axis=1, keepdims=True))
        alpha = jnp.exp(m_prev - m_next)
        width = s.shape[1]
        m_wide = m_next[:, :width] if width <= LANES else jnp.concatenate([m_next] * (width // LANES), axis=1)
        pr = jnp.exp(s - m_wide)
        l_ref[...] = alpha * l_ref[...] + jnp.sum(pr, axis=1, keepdims=True)
        acc_ref[...] = alpha * acc_ref[...] + pv_of(pr.astype(BF16))
        m_ref[...] = m_next

    def page_pv(pb, v_ref):
        pe = jnp.dot(pb, spread_ref[...], preferred_element_type=F32).astype(BF16) * own_ref[...]
        v2 = v_ref[...].reshape(PAGE_SIZE * N_HEADS, V_DIM).astype(BF16)
        return jnp.dot(pe, v2, preferred_element_type=F32)

    qbd = qbd_ref[...]
    s = jnp.concatenate([jnp.dot(qbd, kc_refs[g][...].astype(BF16), preferred_element_type=F32)
                         for g in range(n_group)], axis=1)

    def pages_pv(pb):
        out = page_pv(pb[:, :PAGE_SIZE], vc_refs[0])
        for g in range(1, n_group):
            out = out + page_pv(pb[:, g * PAGE_SIZE:(g + 1) * PAGE_SIZE], vc_refs[g])
        return out

    update(s, pages_pv)

    @pl.when(p == pl.num_programs(1) - 1)
    def _():
        s_new = lax.dot_general(qbd_ref[...], kn_ref[...].astype(BF16), _NT, preferred_element_type=F32)
        r = lax.broadcasted_iota(jnp.int32, s_new.shape, 0)
        c = lax.broadcasted_iota(jnp.int32, s_new.shape, 1)
        s_new = jnp.where(c <= (r & (n_new - 1)), s_new, -jnp.inf)
        def new_pv(pb):
            return jnp.concatenate(
                [jnp.dot(pb[h * rows_per_head:(h + 1) * rows_per_head].astype(F32),
                         vn_ref[:, h * V_DIM:(h + 1) * V_DIM].astype(BF16).astype(F32),
                         preferred_element_type=F32) for h in range(N_HEADS)], axis=0)

        update(s_new, new_pv)
        lam = _lambda_value(lq1_ref[...], lk1_ref[...], lq2_ref[...], lk2_ref[...])
        acc = acc_ref[...] / l_ref[...]
        for h in range(N_HEADS):
            cols = slice(h * V_DIM, (h + 1) * V_DIM)
            r0 = h * rows_per_head
            o = acc[r0:r0 + n_new] - lam * acc[r0 + n_new:r0 + rows_per_head]
            o_ref[:, cols] = _subln_mix(o, gs_ref[...], sga_ref[:, cols], cvp_ref[:, cols])


def _attn_sample(qkf, vf, cache_k, cache_v, page_table, sga, cvp, g_sub, lams, n_seq, n_new):
    n_pages = page_table.shape[1]
    n_pool = cache_k.shape[0]
    width = N_HEADS * V_DIM
    grp = SAMPLE_PAGES_PER_STEP
    kc = jnp.transpose(cache_k, (0, 2, 3, 4, 1)).reshape(n_pool, width, PAGE_SIZE)
    pt = page_table.reshape(-1)
    n_rows = N_HEADS * 2 * n_new
    row = pl.BlockSpec((n_new, width), lambda b, p, pt: (b, 0))

    def page_index(g):
        return lambda b, p, pt: (pt[b * n_pages + p * grp + g], 0, 0)

    def page_index4(g):
        return lambda b, p, pt: (pt[b * n_pages + p * grp + g], 0, 0, 0)

    k_pages = [pl.BlockSpec((None, width, PAGE_SIZE), page_index(g)) for g in range(grp)]
    v_pages = [pl.BlockSpec((None, PAGE_SIZE, N_HEADS, V_DIM), page_index4(g)) for g in range(grp)]
    vec64 = pl.BlockSpec((1, HEAD_DIM), lambda b, p, pt: (0, 0))
    col = jnp.arange(PAGE_SIZE * N_HEADS, dtype=jnp.int32)
    spread = (col[None, :] // N_HEADS == jnp.arange(PAGE_SIZE, dtype=jnp.int32)[:, None]).astype(BF16)
    own = (col[None, :] % N_HEADS == jnp.arange(n_rows, dtype=jnp.int32)[:, None] // (2 * n_new)).astype(BF16)
    const = lambda shape: pl.BlockSpec(shape, lambda b, p, pt: (0, 0))
    grid_spec = pltpu.PrefetchScalarGridSpec(
        num_scalar_prefetch=1,
        grid=(n_seq, n_pages // grp),
        in_specs=[pl.BlockSpec((None, n_new, width), lambda b, p, pt: (0, b, 0)),
                  pl.BlockSpec((None, n_new, width), lambda b, p, pt: (1, b, 0)),
                  row, *k_pages, *v_pages, const(spread.shape), const(own.shape), row, row,
                  pl.BlockSpec((1, V_DIM), lambda b, p, pt: (0, 0)),
                  vec64, vec64, vec64, vec64],
        out_specs=row,
        scratch_shapes=[pltpu.VMEM((n_rows, width), BF16),
                        pltpu.VMEM((n_rows, V_DIM), F32), pltpu.VMEM((n_rows, V_DIM), F32),
                        pltpu.VMEM((n_rows, V_DIM), F32)])
    return pl.pallas_call(
        functools.partial(_attn_sample_kernel, n_new=n_new, n_group=grp),
        out_shape=jax.ShapeDtypeStruct((n_seq * n_new, width), F32),
        grid_spec=grid_spec,
        compiler_params=_params("parallel", "arbitrary"),
        name="attn_sample",
    )(pt, qkf, qkf, vf, *([kc] * grp), *([cache_v] * grp), spread, own, sga, cvp, g_sub, *lams)


def _out_kernel(mix_ref, x_ref, wo_ref, g1_ref, g_ref, sc_ref, sh_ref, wrh_ref, wrl_ref, br_ref, *refs, n_real_tiles):
    outs = refs[-4:]
    i = pl.program_id(0)

    @pl.when(i < n_real_tiles)
    def _():
        _out_tile(mix_ref, x_ref, wo_ref, g1_ref, g_ref, sc_ref, sh_ref, wrh_ref, wrl_ref, br_ref, *outs)

    @pl.when(i >= n_real_tiles)
    def _():
        for r in outs:
            r[...] = jnp.zeros(r.shape, r.dtype)


def _out_tile(mix_ref, x_ref, wo_ref, g1_ref, g_ref, sc_ref, sh_ref, wrh_ref, wrl_ref, br_ref,
              xm_ref, h2_ref, idx_ref, gt_ref):
    y = jnp.dot(mix_ref[...].astype(BF16), wo_ref[...], preferred_element_type=F32)
    xm = x_ref[...] + g1_ref[0] * y
    xm_ref[...] = xm
    h2 = _rms_mod(xm, g_ref[...], sc_ref[0], sh_ref[0])
    hi = h2.astype(BF16)
    _store_token_rows(h2_ref, _pack_pairs(hi.astype(F32)))
    lo = (h2 - hi.astype(F32)).astype(BF16)
    logits = (jnp.dot(hi, wrh_ref[...], preferred_element_type=F32)
              + jnp.dot(lo, wrh_ref[...], preferred_element_type=F32)
              + jnp.dot(hi, wrl_ref[...], preferred_element_type=F32)) + br_ref[...]
    lane = lax.broadcasted_iota(jnp.int32, logits.shape, 1).astype(F32)
    vals, idxs = [], []
    for _ in range(TOP_K):
        m = jnp.max(logits, axis=-1, keepdims=True)
        ix = jnp.min(jnp.where(logits == m, lane, float(LANES)), axis=-1, keepdims=True)
        logits = jnp.where(lane == ix, -jnp.inf, logits)
        vals.append(m)
        idxs.append(ix)
    es = [jnp.exp(v - vals[0]) for v in vals]
    denom = es[0] + es[1] + es[2] + es[3]
    idx_out = jnp.zeros(logits.shape, F32)
    gt_out = jnp.zeros(logits.shape, F32)
    for k in range(TOP_K):
        idx_out = jnp.where(lane == float(k), idxs[k], idx_out)
        gt_out = jnp.where(lane == float(k), es[k] / denom, gt_out)
    idx_ref[...] = idx_out.astype(jnp.int32)
    gt_ref[...] = gt_out


def _out_proj(rows, mix, x, w_o_b, gate1, g, scale, shift, wr_hi, wr_lo, b_r, total_tokens, first_row, shared_bufs):
    full = lambda shape: pl.BlockSpec(shape, lambda i: (0,) * len(shape))
    off = first_row // rows.tm
    creates = shared_bufs is None
    out_row = lambda width: pl.BlockSpec((rows.tm, width), lambda i: (i + off, 0))
    packed_rows = pl.BlockSpec((rows.tm * ROW_CHUNKS, LANES), lambda i: (i + off, 0))
    in_specs = [rows.row_spec(clamp=creates), rows.row_spec(clamp=creates), full((D_MODEL, D_MODEL)),
                rows.mod_spec(clamp=creates), full((1, D_MODEL)), rows.mod_spec(clamp=creates),
                rows.mod_spec(clamp=creates),
                full((D_MODEL, LANES)), full((D_MODEL, LANES)), full((1, LANES))]
    args = [mix, x, w_o_b, gate1, g, scale, shift, wr_hi, wr_lo, b_r]
    aliases = {}
    if not creates:
        aliases = {len(args) + j: j for j in range(len(shared_bufs))}
        in_specs += [pl.BlockSpec(memory_space=pl.ANY)] * len(shared_bufs)
        args += list(shared_bufs)
    extra = 1 if creates and total_tokens > rows.n_tokens else 0
    assert total_tokens - rows.n_tokens <= rows.tm or not creates, "other group must fit the one extra tile"
    return pl.pallas_call(
        functools.partial(_out_kernel, n_real_tiles=rows.n_tiles),
        out_shape=(jax.ShapeDtypeStruct((total_tokens, D_MODEL), F32),
                   jax.ShapeDtypeStruct((total_tokens * ROW_CHUNKS, LANES), jnp.uint32),
                   jax.ShapeDtypeStruct((total_tokens, LANES), jnp.int32),
                   jax.ShapeDtypeStruct((total_tokens, LANES), F32)),
        grid=(rows.n_tiles + extra,),
        in_specs=in_specs,
        out_specs=(out_row(D_MODEL), packed_rows, out_row(LANES), out_row(LANES)),
        input_output_aliases=aliases,
        compiler_params=_params("arbitrary"),
        name="out_proj_router",
    )(*args)


def _pack_pairs(x):
    bits = pltpu.bitcast(x, jnp.uint32)
    n = x.shape[1] // 2
    return (bits[:, :n] >> 16) | (bits[:, n:] & jnp.uint32(0xFFFF0000))


def _unpack_pairs(w):
    lo = pltpu.bitcast(w << 16, F32)
    hi = pltpu.bitcast(w & jnp.uint32(0xFFFF0000), F32)
    return jnp.concatenate([lo, hi], axis=1)


def _store_token_rows(ref, packed):
    n = packed.shape[0]
    for c in range(ROW_CHUNKS):
        ref[pl.ds(c, n, stride=ROW_CHUNKS), :] = packed[:, c * LANES:(c + 1) * LANES]


def _load_token_rows(ref, n):
    return jnp.concatenate([ref[pl.ds(c, n, stride=ROW_CHUNKS), :] for c in range(ROW_CHUNKS)], axis=1)


def _route_kernel(idx_ref, dest_ref, cnt_ref, run_ref, start_ref):
    ph = pl.program_id(0)
    i = pl.program_id(1)
    idx = idx_ref[...]
    tm = idx.shape[0]
    lane = lax.broadcasted_iota(jnp.int32, idx.shape, 1)
    onehots = [(lane == idx[:, k:k + 1]).astype(F32) for k in range(TOP_K)]
    member = onehots[0] + onehots[1] + onehots[2] + onehots[3]
    tile_count = jnp.sum(member, axis=0, keepdims=True)

    @pl.when((ph == 0) & (i == 0))
    def _():
        cnt_ref[...] = jnp.zeros(cnt_ref.shape, F32)

    @pl.when(ph == 0)
    def _():
        cnt_ref[...] = cnt_ref[...] + tile_count

    @pl.when((ph == 1) & (i == 0))
    def _():
        cnt = cnt_ref[...]
        padded = jnp.floor((cnt + (MOE_BLOCK - 1)) * (1.0 / MOE_BLOCK)) * MOE_BLOCK
        l1 = lax.broadcasted_iota(jnp.int32, cnt.shape, 1)
        incl = padded
        for s in (1, 2, 4, 8, 16, 32, 64):
            incl = incl + jnp.where(l1 >= s, pltpu.roll(incl, s, 1), 0.0)
        start_ref[...] = incl - padded
        run_ref[...] = jnp.zeros(run_ref.shape, F32)

    @pl.when(ph == 1)
    def _():
        r = lax.broadcasted_iota(jnp.int32, (tm, tm), 0)
        c = lax.broadcasted_iota(jnp.int32, (tm, tm), 1)
        earlier = (c < r).astype(BF16)
        before = jnp.dot(earlier, member.astype(BF16), preferred_element_type=F32)
        base = before + run_ref[0:1] + start_ref[0:1]
        out = jnp.zeros(idx.shape, F32)
        for k in range(TOP_K):
            d = jnp.sum(onehots[k] * base, axis=1, keepdims=True)
            out = jnp.where(lane == k, d, out)
        dest_ref[...] = out.astype(jnp.int32)
        run_ref[...] = run_ref[...] + tile_count


def _route(rows, idx):
    tm = rows.tm
    dest, counts = pl.pallas_call(
        _route_kernel,
        out_shape=(jax.ShapeDtypeStruct((rows.n_tokens, LANES), jnp.int32),
                   jax.ShapeDtypeStruct((SUBLANES, LANES), F32)),
        grid=(2, rows.n_tiles),
        in_specs=[pl.BlockSpec((tm, LANES), lambda ph, i: (i, 0))],
        out_specs=(pl.BlockSpec((tm, LANES), lambda ph, i: (i * ph, 0)),
                   pl.BlockSpec((SUBLANES, LANES), lambda ph, i: (0, 0))),
        scratch_shapes=[pltpu.VMEM((SUBLANES, LANES), F32), pltpu.VMEM((SUBLANES, LANES), F32)],
        compiler_params=_params("arbitrary", "arbitrary"),
        name="moe_route",
    )(idx)
    return dest, counts


def _dispatch_kernel(dest_ref, h_ref, zero_ref, xs_ref, sem):
    del zero_ref
    i = pl.program_id(0)
    tm = h_ref.shape[0] // ROW_CHUNKS

    def body(t, carry):
        for k in range(TOP_K):
            d = dest_ref[(i * tm + t) * TOP_K + k]
            pltpu.make_async_copy(h_ref.at[pl.ds(t * ROW_CHUNKS, ROW_CHUNKS)],
                                  xs_ref.at[pl.ds(d * ROW_CHUNKS, ROW_CHUNKS)], sem).start(priority=k % 2)
        return carry

    lax.fori_loop(0, tm, body, 0, unroll=8)
    n_words = tm * TOP_K * ROW_CHUNKS
    pltpu.make_async_copy(xs_ref.at[pl.ds(0, n_words)], xs_ref.at[pl.ds(0, n_words)], sem).wait()


def _dispatch(rows, dest_flat, h2u, n_rows):
    tm = rows.tm
    zeros = jnp.zeros((n_rows * ROW_CHUNKS, LANES), jnp.uint32)
    grid_spec = pltpu.PrefetchScalarGridSpec(
        num_scalar_prefetch=1, grid=(rows.n_tiles,),
        in_specs=[pl.BlockSpec((tm * ROW_CHUNKS, LANES), lambda i, d: (i, 0)),
                  pl.BlockSpec(memory_space=pl.ANY)],
        out_specs=pl.BlockSpec(memory_space=pl.ANY),
        scratch_shapes=[pltpu.SemaphoreType.DMA(())])
    return pl.pallas_call(
        _dispatch_kernel,
        out_shape=jax.ShapeDtypeStruct((n_rows * ROW_CHUNKS, LANES), jnp.uint32),
        grid_spec=grid_spec,
        input_output_aliases={2: 0},
        compiler_params=_params("arbitrary"),
        name="moe_dispatch",
    )(dest_flat, h2u, zeros)


def _expert_kernel(be_ref, nu_ref, x_ref, wgu_ref, wd_ref, bg_ref, bu_ref, bd_ref, y_ref,
                   wt_s, wg_s, wu_s, wd_s):
    i = pl.program_id(0)
    used = i < nu_ref[0]
    new_expert = (i == 0) | (be_ref[i] != be_ref[jnp.maximum(i - 1, 0)])

    @pl.when(used & new_expert)
    def _():
        n = wgu_ref.shape[1]
        n_lane_tiles = wt_s.shape[0]
        for c in range(n // EXPERT_XPOSE_CHUNK):
            cols = slice(c * EXPERT_XPOSE_CHUNK, (c + 1) * EXPERT_XPOSE_CHUNK)
            wt = wgu_ref[:, cols].T
            for j in range(n_lane_tiles):
                wt_s[j, cols, :] = wt[:, j * LANES:(j + 1) * LANES]
        for j in range(n_lane_tiles):
            lanes = slice(j * LANES, (j + 1) * LANES)
            wg_s[:, lanes] = wt_s[j, pl.ds(0, n // 2, stride=2), :].astype(BF16)
            wu_s[:, lanes] = wt_s[j, pl.ds(1, n // 2, stride=2), :].astype(BF16)
        wd_s[...] = wd_ref[...].astype(BF16)

    @pl.when(used)
    def _():
        x = _unpack_pairs(_load_token_rows(x_ref, MOE_BLOCK)).astype(BF16)
        y = bd_ref[...]
        for c in range(wd_s.shape[0] // EXPERT_FF_CHUNK):
            ff = slice(c * EXPERT_FF_CHUNK, (c + 1) * EXPERT_FF_CHUNK)
            g = lax.dot_general(x, wg_s[ff, :], _NT, preferred_element_type=F32) + bg_ref[:, ff]
            u = lax.dot_general(x, wu_s[ff, :], _NT, preferred_element_type=F32) + bu_ref[:, ff]
            gate = jnp.minimum(g, SWIGLU_LIMIT)
            up = jnp.clip(u, -SWIGLU_LIMIT, SWIGLU_LIMIT)
            glu = gate * jax.nn.sigmoid(SWIGLU_ALPHA * gate)
            a = ((up + 1.0) * glu).astype(BF16)
            y = y + jnp.dot(a, wd_s[ff, :], preferred_element_type=F32)
        _store_token_rows(y_ref, _pack_pairs(y.astype(BF16).astype(F32)))

    @pl.when(i >= nu_ref[0])
    def _():
        y_ref[...] = jnp.zeros(y_ref.shape, y_ref.dtype)


def _experts(x_sorted, block_e, n_used, w_gu, w_dn, b_g, b_u, b_d):
    rows = x_sorted.shape[0] // ROW_CHUNKS
    n_blocks = rows // MOE_BLOCK
    d_gu = w_gu.shape[2]
    d_ff = w_dn.shape[1]
    by_expert = lambda i, be, nu: (be[i], 0, 0)
    xspec = pl.BlockSpec((MOE_BLOCK * ROW_CHUNKS, LANES), lambda i, be, nu: (i, 0))
    grid_spec = pltpu.PrefetchScalarGridSpec(
        num_scalar_prefetch=2, grid=(n_blocks,),
        in_specs=[xspec,
                  pl.BlockSpec((None, D_MODEL, d_gu), by_expert),
                  pl.BlockSpec((None, d_ff, D_MODEL), by_expert),
                  pl.BlockSpec((None, 1, d_ff), by_expert),
                  pl.BlockSpec((None, 1, d_ff), by_expert),
                  pl.BlockSpec((None, 1, D_MODEL), by_expert)],
        out_specs=xspec,
        scratch_shapes=[pltpu.VMEM((D_MODEL // LANES, d_gu, LANES), F32),
                        pltpu.VMEM((d_ff, D_MODEL), BF16), pltpu.VMEM((d_ff, D_MODEL), BF16),
                        pltpu.VMEM((d_ff, D_MODEL), BF16)])
    return pl.pallas_call(
        _expert_kernel,
        out_shape=jax.ShapeDtypeStruct((rows * ROW_CHUNKS, LANES), jnp.uint32),
        grid_spec=grid_spec,
        compiler_params=pltpu.CompilerParams(dimension_semantics=("arbitrary",),
                                             vmem_limit_bytes=EXPERT_VMEM_LIMIT_BYTES),
        name="experts",
    )(block_e, n_used, x_sorted, w_gu, w_dn, b_g, b_u, b_d)


def _combine_kernel(dest_ref, xm_ref, gt_ref, g2p_ref, g2s_ref, yb_ref, yp_ref, ys_ref, rows_ref, sem,
                    *, n_prompt_tiles):
    i = pl.program_id(0)
    tm = xm_ref.shape[0]
    slot = i % 2

    def gather(tile, into):
        def body(t, carry):
            for k in range(TOP_K):
                d = dest_ref[(tile * tm + t) * TOP_K + k]
                pltpu.make_async_copy(yb_ref.at[pl.ds(d * ROW_CHUNKS, ROW_CHUNKS)],
                                      rows_ref.at[into, k, pl.ds(t * ROW_CHUNKS, ROW_CHUNKS)],
                                      sem.at[into]).start(priority=k % 2)
            return carry

        lax.fori_loop(0, tm, body, 0, unroll=8)

    @pl.when(i == 0)
    def _():
        gather(0, 0)

    @pl.when(i + 1 < pl.num_programs(0))
    def _():
        gather(i + 1, 1 - slot)

    pltpu.make_async_copy(rows_ref.at[slot], rows_ref.at[slot], sem.at[slot]).wait()
    gt = gt_ref[...]
    acc = jnp.zeros(xm_ref.shape, F32)
    for k in range(TOP_K):
        acc = acc + gt[:, k:k + 1] * _unpack_pairs(_load_token_rows(rows_ref.at[slot, k], tm))
    is_prompt = i < n_prompt_tiles
    y = xm_ref[...] + jnp.where(is_prompt, g2p_ref[0], g2s_ref[0]) * acc

    @pl.when(is_prompt)
    def _():
        yp_ref[...] = y

    @pl.when(jnp.logical_not(is_prompt))
    def _():
        ys_ref[...] = y


def _combine(tiles, dest_flat, xm, gates, gate2_p, gate2_s, y_buf, n_prompt, rows_per_seq):
    tm = tiles.tm
    n_prompt_tiles = n_prompt // tm
    n_sample = tiles.n_tokens - n_prompt
    assert n_sample == tm and gate2_s.shape[1] == tm, "the sample group is one token tile"
    tiles_per_seq = rows_per_seq // tm
    last_seq = gate2_p.shape[0] - 1
    grid_spec = pltpu.PrefetchScalarGridSpec(
        num_scalar_prefetch=1, grid=(tiles.n_tiles,),
        in_specs=[pl.BlockSpec((tm, D_MODEL), lambda i, d: (i, 0)),
                  pl.BlockSpec((tm, LANES), lambda i, d: (i, 0)),
                  pl.BlockSpec((1, 1, D_MODEL), lambda i, d: (jnp.minimum(i // tiles_per_seq, last_seq), 0, 0)),
                  pl.BlockSpec((1, tm, D_MODEL), lambda i, d: (0, 0, 0)),
                  pl.BlockSpec(memory_space=pl.ANY)],
        out_specs=(pl.BlockSpec((tm, D_MODEL), lambda i, d: (jnp.minimum(i, n_prompt_tiles - 1), 0)),
                   pl.BlockSpec((tm, D_MODEL), lambda i, d: (0, 0))),
        scratch_shapes=[pltpu.VMEM((2, TOP_K, tm * ROW_CHUNKS, LANES), jnp.uint32),
                        pltpu.SemaphoreType.DMA((2,))])
    return pl.pallas_call(
        functools.partial(_combine_kernel, n_prompt_tiles=n_prompt_tiles),
        out_shape=(jax.ShapeDtypeStruct((n_prompt, D_MODEL), F32),
                   jax.ShapeDtypeStruct((n_sample, D_MODEL), F32)),
        grid_spec=grid_spec,
        compiler_params=_params("arbitrary"),
        name="moe_combine",
    )(dest_flat, xm, gates, gate2_p, gate2_s, y_buf)


class _Tiles:
    def __init__(self, n_tokens, tm):
        self.n_tokens, self.tm, self.n_tiles = n_tokens, tm, n_tokens // tm


def _moe(xm, h2u, idx, gates, gate2_p, gate2_s, moe_w, n_prompt, rows_per_seq):
    rows = _Tiles(xm.shape[0], MOE_TOKEN_TILE)
    wide = _Tiles(xm.shape[0], MOE_WIDE_TOKEN_TILE)
    assert wide.n_tiles * wide.tm == rows.n_tokens == rows.n_tiles * rows.tm
    a = rows.n_tokens * TOP_K
    n_blocks = -(-a // MOE_BLOCK) + N_EXPERTS
    dest, counts = _route(wide, idx)
    dest_flat = dest[:, :TOP_K].reshape(a)
    cnt = counts[0, :N_EXPERTS].astype(jnp.int32)
    pad_end = jnp.cumsum((cnt + MOE_BLOCK - 1) // MOE_BLOCK * MOE_BLOCK)
    block_row = jnp.arange(n_blocks, dtype=jnp.int32) * MOE_BLOCK
    block_e = jnp.minimum(jnp.sum((pad_end[None, :] <= block_row[:, None]).astype(jnp.int32), axis=1),
                          N_EXPERTS - 1)
    n_used = (pad_end[-1:] // MOE_BLOCK).astype(jnp.int32)
    x_sorted = _dispatch(wide, dest_flat, h2u, n_blocks * MOE_BLOCK)
    y_buf = _experts(x_sorted, block_e, n_used, *moe_w)
    return _combine(rows, dest_flat, xm, gates, gate2_p, gate2_s, y_buf, n_prompt, rows_per_seq)


def _group(rows, x, ada, state_rows, qk_and_attend, shared, total_tokens, first_row, token_bufs):
    (g_mix, w_vconv, w_conv, w_o_b, g_ffn, wr_hi, wr_lo, b_r, mid_dtype) = shared
    shift1, scale1, gate1, shift2, scale2, gate2 = [rows.mod_array(m) for m in jnp.split(ada, 6, axis=-1)]
    v_f, v_b, sga, cvp, tail = _vconv_proj(rows, x, g_mix, scale1, shift1, w_vconv, w_conv, state_rows, mid_dtype)
    mix, k_out = qk_and_attend(x, scale1, shift1, v_f, v_b, sga, cvp)
    token_bufs = _out_proj(rows, mix, x, w_o_b, gate1, g_ffn, scale2, shift2, wr_hi, wr_lo, b_r,
                           total_tokens, first_row, token_bufs)
    return token_bufs, gate2, k_out, v_f, tail


def kernel(x_prompt, x_sample, c_prompt, c_sample, cache_k, cache_v, state_conv, page_table, w_ada, b_ada, g_norm_mix, w_in, g_q, g_k, lambda_q1, lambda_k1, lambda_q2, lambda_k2, g_subln, w_conv, w_o, g_norm_ffn, w_router, b_router, w_gate_up, b_gate_up, w_down, b_down):
    assert w_in.shape[0] == 1, "single-layer stack"
    batch, seq, _ = x_prompt.shape
    n_seq, n_new, _ = x_sample.shape
    tp, ts = batch * seq, n_seq * n_new
    n_chunks = D_MODEL // HEAD_DIM

    ada = _ada(jnp.concatenate([c_prompt, c_sample], axis=0), w_ada[0], b_ada[0][None])

    w_in_b = w_in[0].astype(BF16)
    w_kt = w_in[0][:, D_MODEL:2 * D_MODEL].T.astype(BF16)
    w_vconv = w_in_b[:, 2 * D_MODEL:]
    w_o_b = w_o[0].astype(BF16)
    gqk = jnp.stack([jnp.tile(g_q[0], n_chunks), jnp.tile(g_k[0], n_chunks)])[:, None, :]
    gk_col = jnp.tile(g_k[0], n_chunks)[:, None]
    blk = jnp.arange(D_MODEL, dtype=jnp.int32) // HEAD_DIM
    pmat = (blk[:, None] == blk[None, :]).astype(BF16)
    wr = jnp.pad(w_router[0], ((0, 0), (0, LANES - N_EXPERTS)))
    wr_hi = wr.astype(BF16)
    wr_lo = (wr - wr_hi.astype(F32)).astype(BF16)
    b_r = jnp.pad(b_router[0], (0, LANES - N_EXPERTS), constant_values=NEG_BIG)[None]
    moe_w = (w_gate_up[0], w_down[0], b_gate_up[0][:, None, 0::2], b_gate_up[0][:, None, 1::2],
             b_down[0][:, None, :])
    g_mix = g_norm_mix[0][None]
    g_ffn = g_norm_ffn[0][None]
    g_sub = g_subln[0][None]
    lams = (lambda_q1[0][None], lambda_k1[0][None], lambda_q2[0][None], lambda_k2[0][None])

    def shared(mid_dtype):
        return (g_mix, w_vconv, w_conv[0], w_o_b, g_ffn, wr_hi, wr_lo, b_r, mid_dtype)

    rows_p = _Rows(tp, seq)

    def attend_p(x, scale1, shift1, v_f, v_b, sga, cvp):
        q_b, kt_f, kt_b = _qkt_proj(rows_p, x, g_mix, scale1, shift1, w_in_b, gqk, pmat, w_kt, gk_col, batch, seq)
        mix = _attn_prompt(q_b, kt_b, v_b, sga, cvp, g_sub, lams, batch, seq)
        return mix, kt_f

    bufs, gate2_p, kt_p, v_p, tail_p = _group(rows_p, x_prompt.reshape(tp, D_MODEL), ada[:batch], None, attend_p,
                                              shared(BF16), tp + ts, 0, None)

    rows_s = _Rows(ts, n_new)
    st = state_conv[0]
    zeros = jnp.zeros((n_seq, n_new - 2, D_MODEL), F32)
    s1 = jnp.concatenate([st[:, 1:2], jnp.zeros((n_seq, n_new - 1, D_MODEL), F32)], axis=1).reshape(ts, D_MODEL)
    s2 = jnp.concatenate([st, zeros], axis=1).reshape(ts, D_MODEL)

    def attend_s(x, scale1, shift1, v_f, v_b, sga, cvp):
        _, qk_f = _qk_proj(rows_s, x, g_mix, scale1, shift1, w_in_b, gqk, pmat)
        mix = _attn_sample(qk_f, v_f, cache_k[0], cache_v[0], page_table, sga, cvp, g_sub, lams, n_seq, n_new)
        return mix, qk_f[1]

    bufs, gate2_s, k_s, v_s, tail_s = _group(rows_s, x_sample.reshape(ts, D_MODEL), ada[batch:], (s1, s2), attend_s,
                                             shared(F32), tp + ts, tp, bufs)

    y_p, y_s = _moe(*bufs, gate2_p, gate2_s, moe_w, tp, seq)

    tail_s = tail_s.reshape(n_seq, n_new, D_MODEL)
    k_p = kt_p.reshape(1, batch, N_HEADS, 2, HEAD_DIM, seq).transpose(0, 1, 5, 2, 3, 4)
    return (y_p.reshape(batch, seq, D_MODEL),
            y_s.reshape(n_seq, n_new, D_MODEL),
            k_p,
            v_p.reshape(1, batch, seq, N_HEADS, V_DIM),
            tail_p[:, SUBLANES - (CONV_WIDTH - 1):][None],
            k_s.reshape(1, n_seq, n_new, N_HEADS, 2, HEAD_DIM),
            v_s.reshape(1, n_seq, n_new, N_HEADS, V_DIM),
            tail_s[:, n_new - (CONV_WIDTH - 1):][None])
```

```python
import functools
import math

import jax
import jax.numpy as jnp
from jax import lax
from jax.experimental import pallas as pl
from jax.experimental.pallas import tpu as pltpu

F32 = jnp.float32
BF16 = jnp.bfloat16

D_MODEL = 1024
HEAD_DIM = 64
V_DIM = 2 * HEAD_DIM
N_HEADS = D_MODEL // V_DIM
ATTN_SCALE = HEAD_DIM ** -0.5
LOG2_E = math.log2(math.e)
CONV_WIDTH = 3
PAGE_SIZE = 128
N_EXPERTS = 32
TOP_K = 4
SWIGLU_LIMIT = 7.0
SWIGLU_ALPHA = 1.702
NORM_EPS = 1e-6
LAMBDA_INIT = 0.8 - 0.6 * math.exp(-0.3 * 0)

VMEM_LIMIT_BYTES = 48 * 1024 * 1024
EXPERT_VMEM_LIMIT_BYTES = 56 * 1024 * 1024
LANES = 128
PACKED = D_MODEL // 2
ROW_CHUNKS = PACKED // LANES
SUBLANES = 8

ROW_TILE_PROMPT = 512
ATTN_BLOCK = 512
MOE_BLOCK = 512
MOE_TOKEN_TILE = 256
MOE_WIDE_TOKEN_TILE = 1280
EXPERT_XPOSE_CHUNK = 512
EXPERT_FF_CHUNK = 512
ADA_COL_TILE = 1536
SAMPLE_PAGES_PER_STEP = 16
NEG_BIG = -1e30

_NT = (((1,), (1,)), ((), ()))


def _params(*sem):
    return pltpu.CompilerParams(dimension_semantics=sem, vmem_limit_bytes=VMEM_LIMIT_BYTES)


def _rms_mod(x, g, scale, shift):
    ms = jnp.mean(x * x, axis=-1, keepdims=True)
    return (x * lax.rsqrt(ms + NORM_EPS) * g) * (1.0 + scale) + shift


def _ada_kernel(c_ref, w_ref, b_ref, o_ref):
    c = c_ref[...]
    s = (c * jax.nn.sigmoid(c)).astype(BF16)
    o_ref[...] = jnp.dot(s, w_ref[...].astype(BF16), preferred_element_type=F32) + b_ref[...]


def _ada(c_all, w_ada, b_ada):
    n = c_all.shape[0]
    width = w_ada.shape[1]
    return pl.pallas_call(
        _ada_kernel,
        out_shape=jax.ShapeDtypeStruct((n, width), F32),
        grid=(width // ADA_COL_TILE,),
        in_specs=[pl.BlockSpec((n, D_MODEL), lambda j: (0, 0)),
                  pl.BlockSpec((D_MODEL, ADA_COL_TILE), lambda j: (0, j)),
                  pl.BlockSpec((1, ADA_COL_TILE), lambda j: (0, j))],
        out_specs=pl.BlockSpec((n, ADA_COL_TILE), lambda j: (0, j)),
        compiler_params=_params("arbitrary"),
        name="ada",
    )(c_all, w_ada, b_ada)


class _Rows:
    def __init__(self, n_tokens, rows_per_batch):
        if rows_per_batch >= ROW_TILE_PROMPT:
            self.tm = ROW_TILE_PROMPT
            self.tiles_per_batch = rows_per_batch // self.tm
            self.mod_rows = 1
        else:
            self.tm = n_tokens
            self.tiles_per_batch = None
            self.mod_rows = n_tokens
        self.n_tokens = n_tokens
        self.rows_per_batch = rows_per_batch
        self.n_tiles = n_tokens // self.tm

    def mod_array(self, m):
        if self.tiles_per_batch is not None:
            return m[:, None, :]
        return jnp.repeat(m, self.rows_per_batch, axis=0)[None]

    def mod_spec(self, clamp=False):
        if self.tiles_per_batch is not None:
            tpb, last = self.tiles_per_batch, self.n_tiles - 1
            if clamp:
                return pl.BlockSpec((1, 1, D_MODEL), lambda i, *_: (jnp.minimum(i, last) // tpb, 0, 0))
            return pl.BlockSpec((1, 1, D_MODEL), lambda i, *_: (i // tpb, 0, 0))
        return pl.BlockSpec((1, self.mod_rows, D_MODEL), lambda i, *_: (0, 0, 0))

    def row_spec(self, width=D_MODEL, clamp=False):
        if clamp:
            last = self.n_tiles - 1
            return pl.BlockSpec((self.tm, width), lambda i, *_: (jnp.minimum(i, last), 0))
        return pl.BlockSpec((self.tm, width), lambda i, *_: (i, 0))


def _qk_kernel(x_ref, g_ref, sc_ref, sh_ref, w_ref, gqk_ref, p_ref, qkb_ref, qkf_ref, h_ref):
    j = pl.program_id(1)

    @pl.when(j == 0)
    def _():
        h_ref[...] = _rms_mod(x_ref[...], g_ref[...], sc_ref[0], sh_ref[0]).astype(BF16)

    z = jnp.dot(h_ref[...], w_ref[...], preferred_element_type=F32)
    ss = jnp.dot((z * z).astype(BF16), p_ref[...], preferred_element_type=F32)
    zn = z * lax.rsqrt(ss * (1.0 / HEAD_DIM) + NORM_EPS) * gqk_ref[0]
    qkf_ref[0] = zn
    scale = jnp.where(j == 0, ATTN_SCALE, 1.0)
    qkb_ref[...] = (zn * scale).astype(BF16)


def _qk_proj(rows, x, g, scale, shift, w_in_b, gqk, pmat):
    t = rows.n_tokens
    return pl.pallas_call(
        _qk_kernel,
        out_shape=(jax.ShapeDtypeStruct((t, 2 * D_MODEL), BF16),
                   jax.ShapeDtypeStruct((2, t, D_MODEL), F32)),
        grid=(rows.n_tiles, 2),
        in_specs=[rows.row_spec(),
                  pl.BlockSpec((1, D_MODEL), lambda i, j: (0, 0)),
                  rows.mod_spec(), rows.mod_spec(),
                  pl.BlockSpec((D_MODEL, D_MODEL), lambda i, j: (0, j)),
                  pl.BlockSpec((1, 1, D_MODEL), lambda i, j: (j, 0, 0)),
                  pl.BlockSpec((D_MODEL, D_MODEL), lambda i, j: (0, 0))],
        out_specs=(pl.BlockSpec((rows.tm, D_MODEL), lambda i, j: (i, j)),
                   pl.BlockSpec((1, rows.tm, D_MODEL), lambda i, j: (j, i, 0))),
        scratch_shapes=[pltpu.VMEM((rows.tm, D_MODEL), BF16)],
        compiler_params=_params("parallel", "arbitrary"),
        name="qk_proj",
    )(x, g, scale, shift, w_in_b, gqk, pmat)


def _qkt_kernel(x_ref, g_ref, sc_ref, sh_ref, wq_ref, gq_ref, p_ref, wt_ref, gk_ref, qb_ref, kf_ref, kb_ref):
    h = _rms_mod(x_ref[...], g_ref[...], sc_ref[0], sh_ref[0]).astype(BF16)
    z = jnp.dot(h, wq_ref[...], preferred_element_type=F32)
    ss = jnp.dot((z * z).astype(BF16), p_ref[...], preferred_element_type=F32)
    zn = z * lax.rsqrt(ss * (1.0 / HEAD_DIM) + NORM_EPS) * gq_ref[0]
    qb_ref[...] = (zn * (ATTN_SCALE * LOG2_E)).astype(BF16)
    zt = lax.dot_general(wt_ref[...], h, _NT, preferred_element_type=F32)
    tm = zt.shape[1]
    z3 = zt.reshape(D_MODEL // HEAD_DIM, HEAD_DIM, tm)
    s3 = jnp.sum(z3 * z3, axis=1, keepdims=True)
    g3 = gk_ref[...].reshape(D_MODEL // HEAD_DIM, HEAD_DIM, 1)
    kn = (z3 * lax.rsqrt(s3 * (1.0 / HEAD_DIM) + NORM_EPS) * g3).reshape(D_MODEL, tm)
    kf_ref[...] = kn
    kb_ref[...] = kn.astype(BF16)


def _qkt_proj(rows, x, g, scale, shift, w_in_b, gqk, pmat, w_kt, gk_col, batch, seq):
    tpb = rows.tiles_per_batch
    once = pl.Buffered(1)
    kt_spec = pl.BlockSpec((None, D_MODEL, rows.tm), lambda i: (i // tpb, 0, i % tpb))
    return pl.pallas_call(
        _qkt_kernel,
        out_shape=(jax.ShapeDtypeStruct((rows.n_tokens, D_MODEL), BF16),
                   jax.ShapeDtypeStruct((batch, D_MODEL, seq), F32),
                   jax.ShapeDtypeStruct((batch, D_MODEL, seq), BF16)),
        grid=(rows.n_tiles,),
        in_specs=[rows.row_spec(),
                  pl.BlockSpec((1, D_MODEL), lambda i: (0, 0)),
                  rows.mod_spec(), rows.mod_spec(),
                  pl.BlockSpec((D_MODEL, D_MODEL), lambda i: (0, 0), pipeline_mode=once),
                  pl.BlockSpec((1, 1, D_MODEL), lambda i: (0, 0, 0)),
                  pl.BlockSpec((D_MODEL, D_MODEL), lambda i: (0, 0), pipeline_mode=once),
                  pl.BlockSpec((D_MODEL, D_MODEL), lambda i: (0, 0), pipeline_mode=once),
                  pl.BlockSpec((D_MODEL, 1), lambda i: (0, 0))],
        out_specs=(rows.row_spec(), kt_spec, kt_spec),
        compiler_params=_params("parallel"),
        name="qkt_proj",
    )(x, g, scale, shift, w_in_b, gqk, pmat, w_kt, gk_col)


def _vconv_kernel(*refs, tiles_per_batch, rows_per_batch, tail_rows):
    if tiles_per_batch is None:
        (x_ref, g_ref, sc_ref, sh_ref, w_ref, wc_ref, s1_ref, s2_ref,
         vf_ref, vb_ref, sga_ref, cvp_ref, tail_ref, carry_ref) = refs
    else:
        (x_ref, g_ref, sc_ref, sh_ref, w_ref, wc_ref,
         vf_ref, vb_ref, sga_ref, cvp_ref, tail_ref, carry_ref) = refs
    i = pl.program_id(0)
    h = _rms_mod(x_ref[...], g_ref[...], sc_ref[0], sh_ref[0]).astype(BF16)

    def proj(k):
        return jnp.dot(h, w_ref[:, k * D_MODEL:(k + 1) * D_MODEL], preferred_element_type=F32)

    v = proj(0)
    vf_ref[...] = v
    vb_ref[...] = v.astype(BF16)
    sga_ref[...] = jax.nn.sigmoid(proj(4)).astype(sga_ref.dtype)
    u = proj(2) * proj(3)
    tm = u.shape[0]
    row = lax.broadcasted_iota(jnp.int32, (tm, 1), 0)
    r1 = pltpu.roll(u, 1, 0)
    r2 = pltpu.roll(u, 2, 0)
    if tiles_per_batch is None:
        t = row & (rows_per_batch - 1)
        u1 = jnp.where(t >= 1, r1, s1_ref[...])
        u2 = jnp.where(t >= 2, r2, s2_ref[...])
    else:
        first = (i % tiles_per_batch) == 0
        c = jnp.where(first, 0.0, carry_ref[...])
        u1 = jnp.where(row == 0, c[7:8], r1)
        u2 = jnp.where(row == 0, c[6:7], jnp.where(row == 1, c[7:8], r2))
        carry_ref[...] = u[tm - SUBLANES:]
    tail_ref[0] = u[tm - tail_rows:]
    wc = wc_ref[...]
    yc = wc[0:1] * u2 + wc[1:2] * u1 + wc[2:3] * u
    cvp_ref[...] = (jax.nn.sigmoid(proj(5)) * (proj(1) * yc)).astype(cvp_ref.dtype)


def _vconv_proj(rows, x, g, scale, shift, w_vconv, w_conv, state_rows, out_dtype):
    t = rows.n_tokens
    sample_mode = rows.tiles_per_batch is None
    tail_rows = rows.tm if sample_mode else SUBLANES
    n_tail_blocks = 1 if sample_mode else t // rows.rows_per_batch
    in_specs = [rows.row_spec(),
                pl.BlockSpec((1, D_MODEL), lambda i: (0, 0)),
                rows.mod_spec(), rows.mod_spec(),
                pl.BlockSpec(w_vconv.shape, lambda i: (0, 0), pipeline_mode=pl.Buffered(1)),
                pl.BlockSpec((CONV_WIDTH, D_MODEL), lambda i: (0, 0))]
    args = [x, g, scale, shift, w_vconv, w_conv]
    if sample_mode:
        in_specs += [rows.row_spec(), rows.row_spec()]
        args += list(state_rows)
        tail_spec = pl.BlockSpec((1, tail_rows, D_MODEL), lambda i: (0, 0, 0))
    else:
        tpb = rows.tiles_per_batch
        tail_spec = pl.BlockSpec((1, tail_rows, D_MODEL), lambda i: (i // tpb, 0, 0))
    kern = functools.partial(_vconv_kernel, tiles_per_batch=rows.tiles_per_batch,
                             rows_per_batch=rows.rows_per_batch, tail_rows=tail_rows)
    return pl.pallas_call(
        kern,
        out_shape=(jax.ShapeDtypeStruct((t, D_MODEL), F32),
                   jax.ShapeDtypeStruct((t, D_MODEL), BF16),
                   jax.ShapeDtypeStruct((t, D_MODEL), out_dtype),
                   jax.ShapeDtypeStruct((t, D_MODEL), out_dtype),
                   jax.ShapeDtypeStruct((n_tail_blocks, tail_rows, D_MODEL), F32)),
        grid=(rows.n_tiles,),
        in_specs=in_specs,
        out_specs=(rows.row_spec(), rows.row_spec(), rows.row_spec(), rows.row_spec(), tail_spec),
        scratch_shapes=[pltpu.VMEM((SUBLANES, D_MODEL), F32)],
        compiler_params=_params("arbitrary"),
        name="vconv_proj",
    )(*args)


def _lambda_value(lq1, lk1, lq2, lk2):
    e1 = jnp.exp(jnp.sum(lq1 * lk1, axis=-1, keepdims=True))
    e2 = jnp.exp(jnp.sum(lq2 * lk2, axis=-1, keepdims=True))
    return e1 - e2 + LAMBDA_INIT


def _subln_mix(o, g_sub, sga, cvp):
    ms = jnp.mean(o * o, axis=-1, keepdims=True)
    attn = (o * lax.rsqrt(ms + NORM_EPS) * g_sub) * (1.0 - LAMBDA_INIT)
    return sga * attn + cvp


def _flash_update(s, v_ones, m_ref, l_ref, acc_ref):
    m_prev = m_ref[...]
    m_next = jnp.maximum(m_prev, jnp.max(s, axis=1, keepdims=True))
    alpha = jnp.exp2(m_prev - m_next)
    p = jnp.exp2(s - jnp.concatenate([m_next] * (s.shape[1] // LANES), axis=1)).astype(BF16)
    pv = jnp.dot(p, v_ones, preferred_element_type=F32)
    acc_ref[...] = alpha * acc_ref[...] + pv[:, :V_DIM]
    l_ref[...] = alpha * l_ref[...] + pv[:, V_DIM:]
    m_ref[...] = m_next


def _attn_prompt_kernel(qi_ref, ki_ref, q_ref, kt_ref, v_ref, sga_ref, cvp_ref, gs_ref,
                        lq1_ref, lk1_ref, lq2_ref, lk2_ref,
                        o_ref, m0_ref, l0_ref, a0_ref, m1_ref, l1_ref, a1_ref):
    qi = qi_ref[pl.program_id(1)]
    ki = ki_ref[pl.program_id(1)]
    states = ((m0_ref, l0_ref, a0_ref), (m1_ref, l1_ref, a1_ref))

    @pl.when(ki == 0)
    def _():
        for m_ref, l_ref, a_ref in states:
            m_ref[...] = jnp.full(m_ref.shape, -jnp.inf, F32)
            l_ref[...] = jnp.zeros(l_ref.shape, F32)
            a_ref[...] = jnp.zeros(a_ref.shape, F32)

    tq, tk = q_ref.shape[0], kt_ref.shape[1]

    def step(masked):
        lane = lax.broadcasted_iota(jnp.int32, (1, V_DIM), 1)
        zero = jnp.zeros((), BF16)
        ones = jnp.ones((tk, V_DIM), BF16)
        if masked:
            r = lax.broadcasted_iota(jnp.int32, (tq, tk), 0)
            c = lax.broadcasted_iota(jnp.int32, (tq, tk), 1)
            keep = c <= r
        for h in range(N_HEADS):
            cols = slice(h * V_DIM, (h + 1) * V_DIM)
            q = q_ref[:, cols]
            kt = kt_ref[cols, :]
            v_ones = jnp.concatenate([v_ref[:, cols], ones], axis=1)
            qs = (jnp.where(lane < HEAD_DIM, q, zero), jnp.where(lane >= HEAD_DIM, q, zero))
            for qm, (m_ref, l_ref, a_ref) in zip(qs, states):
                s = jnp.dot(qm, kt, preferred_element_type=F32)
                if masked:
                    s = jnp.where(keep, s, -jnp.inf)
                _flash_update(s, v_ones, m_ref.at[h], l_ref.at[h], a_ref.at[h])

    @pl.when(ki < qi)
    def _():
        step(False)

    @pl.when(ki == qi)
    def _():
        step(True)
        lam = _lambda_value(lq1_ref[...], lk1_ref[...], lq2_ref[...], lk2_ref[...])
        for h in range(N_HEADS):
            cols = slice(h * V_DIM, (h + 1) * V_DIM)
            o = a0_ref[h] / l0_ref[h] - lam * (a1_ref[h] / l1_ref[h])
            mix = _subln_mix(o, gs_ref[...], sga_ref[:, cols].astype(F32), cvp_ref[:, cols].astype(F32))
            o_ref[:, cols] = mix.astype(o_ref.dtype)


def _attn_prompt(q_b, kt_b, v_b, sga, cvp, g_sub, lams, batch, seq):
    nb = seq // ATTN_BLOCK
    tq = ATTN_BLOCK
    pairs = [(qi, ki) for qi in range(nb) for ki in range(qi + 1)]
    qi_tab = jnp.array([p[0] for p in pairs], jnp.int32)
    ki_tab = jnp.array([p[1] for p in pairs], jnp.int32)
    q_spec = pl.BlockSpec((tq, D_MODEL), lambda b, p, qt, kt: (b * nb + qt[p], 0))
    k_spec = pl.BlockSpec((None, D_MODEL, tq), lambda b, p, qt, kt: (b, 0, kt[p]))
    v_spec = pl.BlockSpec((tq, D_MODEL), lambda b, p, qt, kt: (b * nb + kt[p], 0))
    vec64 = pl.BlockSpec((1, HEAD_DIM), lambda b, p, qt, kt: (0, 0))
    grid_spec = pltpu.PrefetchScalarGridSpec(
        num_scalar_prefetch=2,
        grid=(batch, len(pairs)),
        in_specs=[q_spec, k_spec, v_spec, q_spec, q_spec,
                  pl.BlockSpec((1, V_DIM), lambda b, p, qt, kt: (0, 0)),
                  vec64, vec64, vec64, vec64],
        out_specs=q_spec,
        scratch_shapes=[pltpu.VMEM((N_HEADS, tq, V_DIM), F32)] * 6)
    return pl.pallas_call(
        _attn_prompt_kernel,
        out_shape=jax.ShapeDtypeStruct((batch * seq, D_MODEL), BF16),
        grid_spec=grid_spec,
        compiler_params=_params("parallel", "arbitrary"),
        name="attn_prompt",
    )(qi_tab, ki_tab, q_b, kt_b, v_b, sga, cvp, g_sub, *lams)


def _attn_sample_kernel(pt_ref, q_ref, kn_ref, vn_ref, *rest, n_new, n_group):
    kc_refs = rest[:n_group]
    vc_refs = rest[n_group:2 * n_group]
    (spread_ref, own_ref, sga_ref, cvp_ref, gs_ref, lq1_ref, lk1_ref, lq2_ref, lk2_ref,
     o_ref, qbd_ref, m_ref, l_ref, acc_ref) = rest[2 * n_group:]
    p = pl.program_id(1)
    rows_per_head = 2 * n_new

    @pl.when(p == 0)
    def _():
        q = q_ref[...] * ATTN_SCALE
        qrep = jnp.concatenate([q] * (N_HEADS * 2), axis=0)
        r = lax.broadcasted_iota(jnp.int32, qrep.shape, 0)
        c = lax.broadcasted_iota(jnp.int32, qrep.shape, 1)
        qbd_ref[...] = jnp.where(c // HEAD_DIM == r // n_new, qrep, 0.0).astype(BF16)
        m_ref[...] = jnp.full(m_ref.shape, -jnp.inf, F32)
        l_ref[...] = jnp.zeros(l_ref.shape, F32)
        acc_ref[...] = jnp.zeros(acc_ref.shape, F32)

    def update(s, pv_of):
        m_prev = m_ref[...]
        m_next = jnp.maximum(m_prev, jnp.max(s, axis=1, keepdims=True))
        alpha = jnp.exp(m_prev - m_next)
        width = s.shape[1]
        m_wide = m_next[:, :width] if width <= LANES else jnp.concatenate([m_next] * (width // LANES), axis=1)
        pr = jnp.exp(s - m_wide)
        l_ref[...] = alpha * l_ref[...] + jnp.sum(pr, axis=1, keepdims=True)
        acc_ref[...] = alpha * acc_ref[...] + pv_of(pr.astype(BF16))
        m_ref[...] = m_next

    def page_pv(pb, v_ref):
        pe = jnp.dot(pb, spread_ref[...], preferred_element_type=F32).astype(BF16) * own_ref[...]
        v2 = v_ref[...].reshape(PAGE_SIZE * N_HEADS, V_DIM).astype(BF16)
        return jnp.dot(pe, v2, preferred_element_type=F32)

    qbd = qbd_ref[...]
    s = jnp.concatenate([jnp.dot(qbd, kc_refs[g][...].astype(BF16), preferred_element_type=F32)
                         for g in range(n_group)], axis=1)

    def pages_pv(pb):
        out = page_pv(pb[:, :PAGE_SIZE], vc_refs[0])
        for g in range(1, n_group):
            out = out + page_pv(pb[:, g * PAGE_SIZE:(g + 1) * PAGE_SIZE], vc_refs[g])
        return out

    update(s, pages_pv)

    @pl.when(p == pl.num_programs(1) - 1)
    def _():
        s_new = lax.dot_general(qbd_ref[...], kn_ref[...].astype(BF16), _NT, preferred_element_type=F32)
        r = lax.broadcasted_iota(jnp.int32, s_new.shape, 0)
        c = lax.broadcasted_iota(jnp.int32, s_new.shape, 1)
        s_new = jnp.where(c <= (r & (n_new - 1)), s_new, -jnp.inf)
        def new_pv(pb):
            return jnp.concatenate(
                [jnp.dot(pb[h * rows_per_head:(h + 1) * rows_per_head].astype(F32),
                         vn_ref[:, h * V_DIM:(h + 1) * V_DIM].astype(BF16).astype(F32),
                         preferred_element_type=F32) for h in range(N_HEADS)], axis=0)

        update(s_new, new_pv)
        lam = _lambda_value(lq1_ref[...], lk1_ref[...], lq2_ref[...], lk2_ref[...])
        acc = acc_ref[...] / l_ref[...]
        for h in range(N_HEADS):
            cols = slice(h * V_DIM, (h + 1) * V_DIM)
            r0 = h * rows_per_head
            o = acc[r0:r0 + n_new] - lam * acc[r0 + n_new:r0 + rows_per_head]
            o_ref[:, cols] = _subln_mix(o, gs_ref[...], sga_ref[:, cols], cvp_ref[:, cols])


def _attn_sample(qkf, vf, cache_k, cache_v, page_table, sga, cvp, g_sub, lams, n_seq, n_new):
    n_pages = page_table.shape[1]
    n_pool = cache_k.shape[0]
    width = N_HEADS * V_DIM
    grp = SAMPLE_PAGES_PER_STEP
    kc = jnp.transpose(cache_k, (0, 2, 3, 4, 1)).reshape(n_pool, width, PAGE_SIZE)
    pt = page_table.reshape(-1)
    n_rows = N_HEADS * 2 * n_new
    row = pl.BlockSpec((n_new, width), lambda b, p, pt: (b, 0))

    def page_index(g):
        return lambda b, p, pt: (pt[b * n_pages + p * grp + g], 0, 0)

    def page_index4(g):
        return lambda b, p, pt: (pt[b * n_pages + p * grp + g], 0, 0, 0)

    k_pages = [pl.BlockSpec((None, width, PAGE_SIZE), page_index(g)) for g in range(grp)]
    v_pages = [pl.BlockSpec((None, PAGE_SIZE, N_HEADS, V_DIM), page_index4(g)) for g in range(grp)]
    vec64 = pl.BlockSpec((1, HEAD_DIM), lambda b, p, pt: (0, 0))
    col = jnp.arange(PAGE_SIZE * N_HEADS, dtype=jnp.int32)
    spread = (col[None, :] // N_HEADS == jnp.arange(PAGE_SIZE, dtype=jnp.int32)[:, None]).astype(BF16)
    own = (col[None, :] % N_HEADS == jnp.arange(n_rows, dtype=jnp.int32)[:, None] // (2 * n_new)).astype(BF16)
    const = lambda shape: pl.BlockSpec(shape, lambda b, p, pt: (0, 0))
    grid_spec = pltpu.PrefetchScalarGridSpec(
        num_scalar_prefetch=1,
        grid=(n_seq, n_pages // grp),
        in_specs=[pl.BlockSpec((None, n_new, width), lambda b, p, pt: (0, b, 0)),
                  pl.BlockSpec((None, n_new, width), lambda b, p, pt: (1, b, 0)),
                  row, *k_pages, *v_pages, const(spread.shape), const(own.shape), row, row,
                  pl.BlockSpec((1, V_DIM), lambda b, p, pt: (0, 0)),
                  vec64, vec64, vec64, vec64],
        out_specs=row,
        scratch_shapes=[pltpu.VMEM((n_rows, width), BF16),
                        pltpu.VMEM((n_rows, V_DIM), F32), pltpu.VMEM((n_rows, V_DIM), F32),
                        pltpu.VMEM((n_rows, V_DIM), F32)])
    return pl.pallas_call(
        functools.partial(_attn_sample_kernel, n_new=n_new, n_group=grp),
        out_shape=jax.ShapeDtypeStruct((n_seq * n_new, width), F32),
        grid_spec=grid_spec,
        compiler_params=_params("parallel", "arbitrary"),
        name="attn_sample",
    )(pt, qkf, qkf, vf, *([kc] * grp), *([cache_v] * grp), spread, own, sga, cvp, g_sub, *lams)


def _out_kernel(mix_ref, x_ref, wo_ref, g1_ref, g_ref, sc_ref, sh_ref, wrh_ref, wrl_ref, br_ref, *refs, n_real_tiles):
    outs = refs[-4:]
    i = pl.program_id(0)

    @pl.when(i < n_real_tiles)
    def _():
        _out_tile(mix_ref, x_ref, wo_ref, g1_ref, g_ref, sc_ref, sh_ref, wrh_ref, wrl_ref, br_ref, *outs)

    @pl.when(i >= n_real_tiles)
    def _():
        for r in outs:
            r[...] = jnp.zeros(r.shape, r.dtype)


def _out_tile(mix_ref, x_ref, wo_ref, g1_ref, g_ref, sc_ref, sh_ref, wrh_ref, wrl_ref, br_ref,
              xm_ref, h2_ref, idx_ref, gt_ref):
    y = jnp.dot(mix_ref[...].astype(BF16), wo_ref[...], preferred_element_type=F32)
    xm = x_ref[...] + g1_ref[0] * y
    xm_ref[...] = xm
    h2 = _rms_mod(xm, g_ref[...], sc_ref[0], sh_ref[0])
    hi = h2.astype(BF16)
    _store_token_rows(h2_ref, _pack_pairs(hi.astype(F32)))
    lo = (h2 - hi.astype(F32)).astype(BF16)
    logits = (jnp.dot(hi, wrh_ref[...], preferred_element_type=F32)
              + jnp.dot(lo, wrh_ref[...], preferred_element_type=F32)
              + jnp.dot(hi, wrl_ref[...], preferred_element_type=F32)) + br_ref[...]
    lane = lax.broadcasted_iota(jnp.int32, logits.shape, 1).astype(F32)
    vals, idxs = [], []
    for _ in range(TOP_K):
        m = jnp.max(logits, axis=-1, keepdims=True)
        ix = jnp.min(jnp.where(logits == m, lane, float(LANES)), axis=-1, keepdims=True)
        logits = jnp.where(lane == ix, -jnp.inf, logits)
        vals.append(m)
        idxs.append(ix)
    es = [jnp.exp(v - vals[0]) for v in vals]
    denom = es[0] + es[1] + es[2] + es[3]
    idx_out = jnp.zeros(logits.shape, F32)
    gt_out = jnp.zeros(logits.shape, F32)
    for k in range(TOP_K):
        idx_out = jnp.where(lane == float(k), idxs[k], idx_out)
        gt_out = jnp.where(lane == float(k), es[k] / denom, gt_out)
    idx_ref[...] = idx_out.astype(jnp.int32)
    gt_ref[...] = gt_out


def _out_proj(rows, mix, x, w_o_b, gate1, g, scale, shift, wr_hi, wr_lo, b_r, total_tokens, first_row, shared_bufs):
    full = lambda shape: pl.BlockSpec(shape, lambda i: (0,) * len(shape))
    off = first_row // rows.tm
    creates = shared_bufs is None
    out_row = lambda width: pl.BlockSpec((rows.tm, width), lambda i: (i + off, 0))
    packed_rows = pl.BlockSpec((rows.tm * ROW_CHUNKS, LANES), lambda i: (i + off, 0))
    in_specs = [rows.row_spec(clamp=creates), rows.row_spec(clamp=creates), full((D_MODEL, D_MODEL)),
                rows.mod_spec(clamp=creates), full((1, D_MODEL)), rows.mod_spec(clamp=creates),
                rows.mod_spec(clamp=creates),
                full((D_MODEL, LANES)), full((D_MODEL, LANES)), full((1, LANES))]
    args = [mix, x, w_o_b, gate1, g, scale, shift, wr_hi, wr_lo, b_r]
    aliases = {}
    if not creates:
        aliases = {len(args) + j: j for j in range(len(shared_bufs))}
        in_specs += [pl.BlockSpec(memory_space=pl.ANY)] * len(shared_bufs)
        args += list(shared_bufs)
    extra = 1 if creates and total_tokens > rows.n_tokens else 0
    assert total_tokens - rows.n_tokens <= rows.tm or not creates, "other group must fit the one extra tile"
    return pl.pallas_call(
        functools.partial(_out_kernel, n_real_tiles=rows.n_tiles),
        out_shape=(jax.ShapeDtypeStruct((total_tokens, D_MODEL), F32),
                   jax.ShapeDtypeStruct((total_tokens * ROW_CHUNKS, LANES), jnp.uint32),
                   jax.ShapeDtypeStruct((total_tokens, LANES), jnp.int32),
                   jax.ShapeDtypeStruct((total_tokens, LANES), F32)),
        grid=(rows.n_tiles + extra,),
        in_specs=in_specs,
        out_specs=(out_row(D_MODEL), packed_rows, out_row(LANES), out_row(LANES)),
        input_output_aliases=aliases,
        compiler_params=_params("arbitrary"),
        name="out_proj_router",
    )(*args)


def _pack_pairs(x):
    bits = pltpu.bitcast(x, jnp.uint32)
    n = x.shape[1] // 2
    return (bits[:, :n] >> 16) | (bits[:, n:] & jnp.uint32(0xFFFF0000))


def _unpack_pairs(w):
    lo = pltpu.bitcast(w << 16, F32)
    hi = pltpu.bitcast(w & jnp.uint32(0xFFFF0000), F32)
    return jnp.concatenate([lo, hi], axis=1)


def _store_token_rows(ref, packed):
    n = packed.shape[0]
    for c in range(ROW_CHUNKS):
        ref[pl.ds(c, n, stride=ROW_CHUNKS), :] = packed[:, c * LANES:(c + 1) * LANES]


def _load_token_rows(ref, n):
    return jnp.concatenate([ref[pl.ds(c, n, stride=ROW_CHUNKS), :] for c in range(ROW_CHUNKS)], axis=1)


def _route_kernel(idx_ref, dest_ref, cnt_ref, run_ref, start_ref):
    ph = pl.program_id(0)
    i = pl.program_id(1)
    idx = idx_ref[...]
    tm = idx.shape[0]
    lane = lax.broadcasted_iota(jnp.int32, idx.shape, 1)
    onehots = [(lane == idx[:, k:k + 1]).astype(F32) for k in range(TOP_K)]
    member = onehots[0] + onehots[1] + onehots[2] + onehots[3]
    tile_count = jnp.sum(member, axis=0, keepdims=True)

    @pl.when((ph == 0) & (i == 0))
    def _():
        cnt_ref[...] = jnp.zeros(cnt_ref.shape, F32)

    @pl.when(ph == 0)
    def _():
        cnt_ref[...] = cnt_ref[...] + tile_count

    @pl.when((ph == 1) & (i == 0))
    def _():
        cnt = cnt_ref[...]
        padded = jnp.floor((cnt + (MOE_BLOCK - 1)) * (1.0 / MOE_BLOCK)) * MOE_BLOCK
        l1 = lax.broadcasted_iota(jnp.int32, cnt.shape, 1)
        incl = padded
        for s in (1, 2, 4, 8, 16, 32, 64):
            incl = incl + jnp.where(l1 >= s, pltpu.roll(incl, s, 1), 0.0)
        start_ref[...] = incl - padded
        run_ref[...] = jnp.zeros(run_ref.shape, F32)

    @pl.when(ph == 1)
    def _():
        r = lax.broadcasted_iota(jnp.int32, (tm, tm), 0)
        c = lax.broadcasted_iota(jnp.int32, (tm, tm), 1)
        earlier = (c < r).astype(BF16)
        before = jnp.dot(earlier, member.astype(BF16), preferred_element_type=F32)
        base = before + run_ref[0:1] + start_ref[0:1]
        out = jnp.zeros(idx.shape, F32)
        for k in range(TOP_K):
            d = jnp.sum(onehots[k] * base, axis=1, keepdims=True)
            out = jnp.where(lane == k, d, out)
        dest_ref[...] = out.astype(jnp.int32)
        run_ref[...] = run_ref[...] + tile_count


def _route(rows, idx):
    tm = rows.tm
    dest, counts = pl.pallas_call(
        _route_kernel,
        out_shape=(jax.ShapeDtypeStruct((rows.n_tokens, LANES), jnp.int32),
                   jax.ShapeDtypeStruct((SUBLANES, LANES), F32)),
        grid=(2, rows.n_tiles),
        in_specs=[pl.BlockSpec((tm, LANES), lambda ph, i: (i, 0))],
        out_specs=(pl.BlockSpec((tm, LANES), lambda ph, i: (i * ph, 0)),
                   pl.BlockSpec((SUBLANES, LANES), lambda ph, i: (0, 0))),
        scratch_shapes=[pltpu.VMEM((SUBLANES, LANES), F32), pltpu.VMEM((SUBLANES, LANES), F32)],
        compiler_params=_params("arbitrary", "arbitrary"),
        name="moe_route",
    )(idx)
    return dest, counts


def _dispatch_kernel(dest_ref, h_ref, zero_ref, xs_ref, sem):
    del zero_ref
    i = pl.program_id(0)
    tm = h_ref.shape[0] // ROW_CHUNKS

    def body(t, carry):
        for k in range(TOP_K):
            d = dest_ref[(i * tm + t) * TOP_K + k]
            pltpu.make_async_copy(h_ref.at[pl.ds(t * ROW_CHUNKS, ROW_CHUNKS)],
                                  xs_ref.at[pl.ds(d * ROW_CHUNKS, ROW_CHUNKS)], sem).start(priority=k % 2)
        return carry

    lax.fori_loop(0, tm, body, 0, unroll=8)
    n_words = tm * TOP_K * ROW_CHUNKS
    pltpu.make_async_copy(xs_ref.at[pl.ds(0, n_words)], xs_ref.at[pl.ds(0, n_words)], sem).wait()


def _dispatch(rows, dest_flat, h2u, n_rows):
    tm = rows.tm
    zeros = jnp.zeros((n_rows * ROW_CHUNKS, LANES), jnp.uint32)
    grid_spec = pltpu.PrefetchScalarGridSpec(
        num_scalar_prefetch=1, grid=(rows.n_tiles,),
        in_specs=[pl.BlockSpec((tm * ROW_CHUNKS, LANES), lambda i, d: (i, 0)),
                  pl.BlockSpec(memory_space=pl.ANY)],
        out_specs=pl.BlockSpec(memory_space=pl.ANY),
        scratch_shapes=[pltpu.SemaphoreType.DMA(())])
    return pl.pallas_call(
        _dispatch_kernel,
        out_shape=jax.ShapeDtypeStruct((n_rows * ROW_CHUNKS, LANES), jnp.uint32),
        grid_spec=grid_spec,
        input_output_aliases={2: 0},
        compiler_params=_params("arbitrary"),
        name="moe_dispatch",
    )(dest_flat, h2u, zeros)


def _expert_prepare(wgu_raw, wd_raw, wt_s, wg_s, wu_s, wd_s):
    n = wgu_raw.shape[1]
    n_lane_tiles = wt_s.shape[0]
    for c in range(n // EXPERT_XPOSE_CHUNK):
        cols = slice(c * EXPERT_XPOSE_CHUNK, (c + 1) * EXPERT_XPOSE_CHUNK)
        wt = wgu_raw[:, cols].T
        for j in range(n_lane_tiles):
            wt_s[j, cols, :] = wt[:, j * LANES:(j + 1) * LANES]
    for j in range(n_lane_tiles):
        lanes = slice(j * LANES, (j + 1) * LANES)
        wg_s[:, lanes] = wt_s[j, pl.ds(0, n // 2, stride=2), :].astype(BF16)
        wu_s[:, lanes] = wt_s[j, pl.ds(1, n // 2, stride=2), :].astype(BF16)
    wd_s[...] = wd_raw[...].astype(BF16)


def _expert_block(x_ref, bg_ref, bu_ref, bd_ref, y_ref, wg_s, wu_s, wd_s):
    x = _unpack_pairs(_load_token_rows(x_ref, MOE_BLOCK)).astype(BF16)
    y = bd_ref[...]
    for c in range(wd_s.shape[0] // EXPERT_FF_CHUNK):
        ff = slice(c * EXPERT_FF_CHUNK, (c + 1) * EXPERT_FF_CHUNK)
        g = lax.dot_general(x, wg_s[ff, :], _NT, preferred_element_type=F32) + bg_ref[:, ff]
        u = lax.dot_general(x, wu_s[ff, :], _NT, preferred_element_type=F32) + bu_ref[:, ff]
        gate = jnp.minimum(g, SWIGLU_LIMIT)
        up = jnp.clip(u, -SWIGLU_LIMIT, SWIGLU_LIMIT)
        glu = gate * jax.nn.sigmoid(SWIGLU_ALPHA * gate)
        a = ((up + 1.0) * glu).astype(BF16)
        y = y + jnp.dot(a, wd_s[ff, :], preferred_element_type=F32)
    _store_token_rows(y_ref, _pack_pairs(y.astype(BF16).astype(F32)))


def _expert_kernel(be_ref, first_ref, last_ref, nxt_ref, slot_ref, nu_ref,
                   x_ref, wgu_hbm, wd_hbm, bg_ref, bu_ref, bd_ref, y_ref,
                   wgu_raw, wd_raw, wt_s, wg_a, wu_a, wd_a, wg_b, wu_b, wd_b, sem):
    i = pl.program_id(0)
    used = i < nu_ref[0]
    first = first_ref[i] == 1
    last = last_ref[i] == 1
    nxt = nxt_ref[i]
    in_a = slot_ref[i] == 0
    sets = ((wg_a, wu_a, wd_a), (wg_b, wu_b, wd_b))

    def fetch(e):
        return (pltpu.make_async_copy(wgu_hbm.at[e], wgu_raw, sem.at[0]),
                pltpu.make_async_copy(wd_hbm.at[e], wd_raw, sem.at[1]))

    @pl.when(i == 0)
    def _():
        for cp in fetch(be_ref[0]):
            cp.start()
        for cp in fetch(be_ref[0]):
            cp.wait()
        _expert_prepare(wgu_raw, wd_raw, wt_s, *sets[0])

    @pl.when(used & first & (nxt >= 0))
    def _():
        for cp in fetch(nxt):
            cp.start()

    hand_over = used & last & (nxt >= 0)
    for cur, flag in ((0, in_a), (1, jnp.logical_not(in_a))):
        @pl.when(hand_over & flag)
        def _(cur=cur):
            for cp in fetch(nxt):
                cp.wait()
            _expert_block(x_ref, bg_ref, bu_ref, bd_ref, y_ref, *sets[cur])
            _expert_prepare(wgu_raw, wd_raw, wt_s, *sets[1 - cur])

        @pl.when(used & jnp.logical_not(last & (nxt >= 0)) & flag)
        def _(cur=cur):
            _expert_block(x_ref, bg_ref, bu_ref, bd_ref, y_ref, *sets[cur])

    @pl.when(i >= nu_ref[0])
    def _():
        y_ref[...] = jnp.zeros(y_ref.shape, y_ref.dtype)


def _experts(x_sorted, block_e, n_used, w_gu, w_dn, b_g, b_u, b_d):
    rows = x_sorted.shape[0] // ROW_CHUNKS
    n_blocks = rows // MOE_BLOCK
    d_gu = w_gu.shape[2]
    d_ff = w_dn.shape[1]
    blk = jnp.arange(n_blocks, dtype=jnp.int32)
    used = blk < n_used[0]
    prev_e = jnp.concatenate([block_e[:1] - 1, block_e[:-1]])
    next_e = jnp.concatenate([block_e[1:], block_e[-1:] + 1])
    first = (used & (block_e != prev_e)).astype(jnp.int32)
    last = (used & ((block_e != next_e) | (blk + 1 >= n_used[0]))).astype(jnp.int32)
    after = jnp.sum((block_e[None, :] <= block_e[:, None]).astype(jnp.int32), axis=1)
    nxt = jnp.where(after < n_used[0], block_e[jnp.minimum(after, n_blocks - 1)], -1).astype(jnp.int32)
    slot = ((jnp.cumsum(first) - 1) % 2).astype(jnp.int32)
    n_prefetch = 6
    by_expert = lambda i, be, *_: (be[i], 0, 0)
    xspec = pl.BlockSpec((MOE_BLOCK * ROW_CHUNKS, LANES), lambda i, *_: (i, 0))
    weights = lambda: (pltpu.VMEM((d_ff, D_MODEL), BF16), pltpu.VMEM((d_ff, D_MODEL), BF16),
                       pltpu.VMEM((d_ff, D_MODEL), BF16))
    grid_spec = pltpu.PrefetchScalarGridSpec(
        num_scalar_prefetch=n_prefetch, grid=(n_blocks,),
        in_specs=[xspec,
                  pl.BlockSpec(memory_space=pl.ANY),
                  pl.BlockSpec(memory_space=pl.ANY),
                  pl.BlockSpec((None, 1, d_ff), by_expert),
                  pl.BlockSpec((None, 1, d_ff), by_expert),
                  pl.BlockSpec((None, 1, D_MODEL), by_expert)],
        out_specs=xspec,
        scratch_shapes=[pltpu.VMEM((D_MODEL, d_gu), F32), pltpu.VMEM((d_ff, D_MODEL), F32),
                        pltpu.VMEM((D_MODEL // LANES, d_gu, LANES), F32),
                        *weights(), *weights(),
                        pltpu.SemaphoreType.DMA((2,))])
    return pl.pallas_call(
        _expert_kernel,
        out_shape=jax.ShapeDtypeStruct((rows * ROW_CHUNKS, LANES), jnp.uint32),
        grid_spec=grid_spec,
        compiler_params=pltpu.CompilerParams(dimension_semantics=("arbitrary",),
                                             vmem_limit_bytes=EXPERT_VMEM_LIMIT_BYTES),
        name="experts",
    )(block_e, first, last, nxt, slot, n_used, x_sorted, w_gu, w_dn, b_g, b_u, b_d)


def _combine_kernel(dest_ref, xm_ref, gt_ref, g2p_ref, g2s_ref, yb_ref, yp_ref, ys_ref, rows_ref, sem,
                    *, n_prompt_tiles):
    i = pl.program_id(0)
    tm = xm_ref.shape[0]
    slot = i % 2

    def gather(tile, into):
        def body(t, carry):
            for k in range(TOP_K):
                d = dest_ref[(tile * tm + t) * TOP_K + k]
                pltpu.make_async_copy(yb_ref.at[pl.ds(d * ROW_CHUNKS, ROW_CHUNKS)],
                                      rows_ref.at[into, k, pl.ds(t * ROW_CHUNKS, ROW_CHUNKS)],
                                      sem.at[into]).start(priority=k % 2)
            return carry

        lax.fori_loop(0, tm, body, 0, unroll=8)

    @pl.when(i == 0)
    def _():
        gather(0, 0)

    @pl.when(i + 1 < pl.num_programs(0))
    def _():
        gather(i + 1, 1 - slot)

    pltpu.make_async_copy(rows_ref.at[slot], rows_ref.at[slot], sem.at[slot]).wait()
    gt = gt_ref[...]
    acc = jnp.zeros(xm_ref.shape, F32)
    for k in range(TOP_K):
        acc = acc + gt[:, k:k + 1] * _unpack_pairs(_load_token_rows(rows_ref.at[slot, k], tm))
    is_prompt = i < n_prompt_tiles
    y = xm_ref[...] + jnp.where(is_prompt, g2p_ref[0], g2s_ref[0]) * acc

    @pl.when(is_prompt)
    def _():
        yp_ref[...] = y

    @pl.when(jnp.logical_not(is_prompt))
    def _():
        ys_ref[...] = y


def _combine(tiles, dest_flat, xm, gates, gate2_p, gate2_s, y_buf, n_prompt, rows_per_seq):
    tm = tiles.tm
    n_prompt_tiles = n_prompt // tm
    n_sample = tiles.n_tokens - n_prompt
    assert n_sample == tm and gate2_s.shape[1] == tm, "the sample group is one token tile"
    tiles_per_seq = rows_per_seq // tm
    last_seq = gate2_p.shape[0] - 1
    grid_spec = pltpu.PrefetchScalarGridSpec(
        num_scalar_prefetch=1, grid=(tiles.n_tiles,),
        in_specs=[pl.BlockSpec((tm, D_MODEL), lambda i, d: (i, 0)),
                  pl.BlockSpec((tm, LANES), lambda i, d: (i, 0)),
                  pl.BlockSpec((1, 1, D_MODEL), lambda i, d: (jnp.minimum(i // tiles_per_seq, last_seq), 0, 0)),
                  pl.BlockSpec((1, tm, D_MODEL), lambda i, d: (0, 0, 0)),
                  pl.BlockSpec(memory_space=pl.ANY)],
        out_specs=(pl.BlockSpec((tm, D_MODEL), lambda i, d: (jnp.minimum(i, n_prompt_tiles - 1), 0)),
                   pl.BlockSpec((tm, D_MODEL), lambda i, d: (0, 0))),
        scratch_shapes=[pltpu.VMEM((2, TOP_K, tm * ROW_CHUNKS, LANES), jnp.uint32),
                        pltpu.SemaphoreType.DMA((2,))])
    return pl.pallas_call(
        functools.partial(_combine_kernel, n_prompt_tiles=n_prompt_tiles),
        out_shape=(jax.ShapeDtypeStruct((n_prompt, D_MODEL), F32),
                   jax.ShapeDtypeStruct((n_sample, D_MODEL), F32)),
        grid_spec=grid_spec,
        compiler_params=_params("arbitrary"),
        name="moe_combine",
    )(dest_flat, xm, gates, gate2_p, gate2_s, y_buf)


class _Tiles:
    def __init__(self, n_tokens, tm):
        self.n_tokens, self.tm, self.n_tiles = n_tokens, tm, n_tokens // tm


def _moe(xm, h2u, idx, gates, gate2_p, gate2_s, moe_w, n_prompt, rows_per_seq):
    rows = _Tiles(xm.shape[0], MOE_TOKEN_TILE)
    wide = _Tiles(xm.shape[0], MOE_WIDE_TOKEN_TILE)
    assert wide.n_tiles * wide.tm == rows.n_tokens == rows.n_tiles * rows.tm
    a = rows.n_tokens * TOP_K
    n_blocks = -(-a // MOE_BLOCK) + N_EXPERTS
    dest, counts = _route(wide, idx)
    dest_flat = dest[:, :TOP_K].reshape(a)
    cnt = counts[0, :N_EXPERTS].astype(jnp.int32)
    pad_end = jnp.cumsum((cnt + MOE_BLOCK - 1) // MOE_BLOCK * MOE_BLOCK)
    block_row = jnp.arange(n_blocks, dtype=jnp.int32) * MOE_BLOCK
    block_e = jnp.minimum(jnp.sum((pad_end[None, :] <= block_row[:, None]).astype(jnp.int32), axis=1),
                          N_EXPERTS - 1)
    n_used = (pad_end[-1:] // MOE_BLOCK).astype(jnp.int32)
    x_sorted = _dispatch(wide, dest_flat, h2u, n_blocks * MOE_BLOCK)
    y_buf = _experts(x_sorted, block_e, n_used, *moe_w)
    return _combine(rows, dest_flat, xm, gates, gate2_p, gate2_s, y_buf, n_prompt, rows_per_seq)


def _group(rows, x, ada, state_rows, qk_and_attend, shared, total_tokens, first_row, token_bufs):
    (g_mix, w_vconv, w_conv, w_o_b, g_ffn, wr_hi, wr_lo, b_r, mid_dtype) = shared
    shift1, scale1, gate1, shift2, scale2, gate2 = [rows.mod_array(m) for m in jnp.split(ada, 6, axis=-1)]
    v_f, v_b, sga, cvp, tail = _vconv_proj(rows, x, g_mix, scale1, shift1, w_vconv, w_conv, state_rows, mid_dtype)
    mix, k_out = qk_and_attend(x, scale1, shift1, v_f, v_b, sga, cvp)
    token_bufs = _out_proj(rows, mix, x, w_o_b, gate1, g_ffn, scale2, shift2, wr_hi, wr_lo, b_r,
                           total_tokens, first_row, token_bufs)
    return token_bufs, gate2, k_out, v_f, tail


def kernel(x_prompt, x_sample, c_prompt, c_sample, cache_k, cache_v, state_conv, page_table, w_ada, b_ada, g_norm_mix, w_in, g_q, g_k, lambda_q1, lambda_k1, lambda_q2, lambda_k2, g_subln, w_conv, w_o, g_norm_ffn, w_router, b_router, w_gate_up, b_gate_up, w_down, b_down):
    assert w_in.shape[0] == 1, "single-layer stack"
    batch, seq, _ = x_prompt.shape
    n_seq, n_new, _ = x_sample.shape
    tp, ts = batch * seq, n_seq * n_new
    n_chunks = D_MODEL // HEAD_DIM

    ada = _ada(jnp.concatenate([c_prompt, c_sample], axis=0), w_ada[0], b_ada[0][None])

    w_in_b = w_in[0].astype(BF16)
    w_kt = w_in[0][:, D_MODEL:2 * D_MODEL].T.astype(BF16)
    w_vconv = w_in_b[:, 2 * D_MODEL:]
    w_o_b = w_o[0].astype(BF16)
    gqk = jnp.stack([jnp.tile(g_q[0], n_chunks), jnp.tile(g_k[0], n_chunks)])[:, None, :]
    gk_col = jnp.tile(g_k[0], n_chunks)[:, None]
    blk = jnp.arange(D_MODEL, dtype=jnp.int32) // HEAD_DIM
    pmat = (blk[:, None] == blk[None, :]).astype(BF16)
    wr = jnp.pad(w_router[0], ((0, 0), (0, LANES - N_EXPERTS)))
    wr_hi = wr.astype(BF16)
    wr_lo = (wr - wr_hi.astype(F32)).astype(BF16)
    b_r = jnp.pad(b_router[0], (0, LANES - N_EXPERTS), constant_values=NEG_BIG)[None]
    moe_w = (w_gate_up[0], w_down[0], b_gate_up[0][:, None, 0::2], b_gate_up[0][:, None, 1::2],
             b_down[0][:, None, :])
    g_mix = g_norm_mix[0][None]
    g_ffn = g_norm_ffn[0][None]
    g_sub = g_subln[0][None]
    lams = (lambda_q1[0][None], lambda_k1[0][None], lambda_q2[0][None], lambda_k2[0][None])

    def shared(mid_dtype):
        return (g_mix, w_vconv, w_conv[0], w_o_b, g_ffn, wr_hi, wr_lo, b_r, mid_dtype)

    rows_p = _Rows(tp, seq)

    def attend_p(x, scale1, shift1, v_f, v_b, sga, cvp):
        q_b, kt_f, kt_b = _qkt_proj(rows_p, x, g_mix, scale1, shift1, w_in_b, gqk, pmat, w_kt, gk_col, batch, seq)
        mix = _attn_prompt(q_b, kt_b, v_b, sga, cvp, g_sub, lams, batch, seq)
        return mix, kt_f

    bufs, gate2_p, kt_p, v_p, tail_p = _group(rows_p, x_prompt.reshape(tp, D_MODEL), ada[:batch], None, attend_p,
                                              shared(BF16), tp + ts, 0, None)

    rows_s = _Rows(ts, n_new)
    st = state_conv[0]
    zeros = jnp.zeros((n_seq, n_new - 2, D_MODEL), F32)
    s1 = jnp.concatenate([st[:, 1:2], jnp.zeros((n_seq, n_new - 1, D_MODEL), F32)], axis=1).reshape(ts, D_MODEL)
    s2 = jnp.concatenate([st, zeros], axis=1).reshape(ts, D_MODEL)

    def attend_s(x, scale1, shift1, v_f, v_b, sga, cvp):
        _, qk_f = _qk_proj(rows_s, x, g_mix, scale1, shift1, w_in_b, gqk, pmat)
        mix = _attn_sample(qk_f, v_f, cache_k[0], cache_v[0], page_table, sga, cvp, g_sub, lams, n_seq, n_new)
        return mix, qk_f[1]

    bufs, gate2_s, k_s, v_s, tail_s = _group(rows_s, x_sample.reshape(ts, D_MODEL), ada[batch:], (s1, s2), attend_s,
                                             shared(F32), tp + ts, tp, bufs)

    y_p, y_s = _moe(*bufs, gate2_p, gate2_s, moe_w, tp, seq)

    tail_s = tail_s.reshape(n_seq, n_new, D_MODEL)
    k_p = kt_p.reshape(1, batch, N_HEADS, 2, HEAD_DIM, seq).transpose(0, 1, 5, 2, 3, 4)
    return (y_p.reshape(batch, seq, D_MODEL),
            y_s.reshape(n_seq, n_new, D_MODEL),
            k_p,
            v_p.reshape(1, batch, seq, N_HEADS, V_DIM),
            tail_p[:, SUBLANES - (CONV_WIDTH - 1):][None],
            k_s.reshape(1, n_seq, n_new, N_HEADS, 2, HEAD_DIM),
            v_s.reshape(1, n_seq, n_new, N_HEADS, V_DIM),
            tail_s[:, n_new - (CONV_WIDTH - 1):][None])
```

```python
import functools
import math

import jax
import jax.numpy as jnp
from jax import lax
from jax.experimental import pallas as pl
from jax.experimental.pallas import tpu as pltpu

F32 = jnp.float32
BF16 = jnp.bfloat16

D_MODEL = 1024
HEAD_DIM = 64
V_DIM = 2 * HEAD_DIM
N_HEADS = D_MODEL // V_DIM
ATTN_SCALE = HEAD_DIM ** -0.5
LOG2_E = math.log2(math.e)
CONV_WIDTH = 3
PAGE_SIZE = 128
N_EXPERTS = 32
TOP_K = 4
SWIGLU_LIMIT = 7.0
SWIGLU_ALPHA = 1.702
NORM_EPS = 1e-6
LAMBDA_INIT = 0.8 - 0.6 * math.exp(-0.3 * 0)

VMEM_LIMIT_BYTES = 48 * 1024 * 1024
EXPERT_VMEM_LIMIT_BYTES = 56 * 1024 * 1024
LANES = 128
PACKED = D_MODEL // 2
ROW_CHUNKS = PACKED // LANES
SUBLANES = 8

ROW_TILE_PROMPT = 512
ATTN_BLOCK = 512
MOE_BLOCK = 512
MOE_TOKEN_TILE = 256
MOE_WIDE_TOKEN_TILE = 1280
EXPERT_XPOSE_CHUNK = 512
EXPERT_FF_CHUNK = 512
ADA_COL_TILE = 1536
SAMPLE_PAGES_PER_STEP = 16
NEG_BIG = -1e30

_NT = (((1,), (1,)), ((), ()))


def _params(*sem):
    return pltpu.CompilerParams(dimension_semantics=sem, vmem_limit_bytes=VMEM_LIMIT_BYTES)


def _rms_mod(x, g, scale, shift):
    ms = jnp.mean(x * x, axis=-1, keepdims=True)
    return (x * lax.rsqrt(ms + NORM_EPS) * g) * (1.0 + scale) + shift


def _ada_kernel(c_ref, w_ref, b_ref, o_ref):
    c = c_ref[...]
    s = (c * jax.nn.sigmoid(c)).astype(BF16)
    o_ref[...] = jnp.dot(s, w_ref[...].astype(BF16), preferred_element_type=F32) + b_ref[...]


def _ada(c_all, w_ada, b_ada):
    n = c_all.shape[0]
    width = w_ada.shape[1]
    return pl.pallas_call(
        _ada_kernel,
        out_shape=jax.ShapeDtypeStruct((n, width), F32),
        grid=(width // ADA_COL_TILE,),
        in_specs=[pl.BlockSpec((n, D_MODEL), lambda j: (0, 0)),
                  pl.BlockSpec((D_MODEL, ADA_COL_TILE), lambda j: (0, j)),
                  pl.BlockSpec((1, ADA_COL_TILE), lambda j: (0, j))],
        out_specs=pl.BlockSpec((n, ADA_COL_TILE), lambda j: (0, j)),
        compiler_params=_params("arbitrary"),
        name="ada",
    )(c_all, w_ada, b_ada)


class _Rows:
    def __init__(self, n_tokens, rows_per_batch):
        if rows_per_batch >= ROW_TILE_PROMPT:
            self.tm = ROW_TILE_PROMPT
            self.tiles_per_batch = rows_per_batch // self.tm
            self.mod_rows = 1
        else:
            self.tm = n_tokens
            self.tiles_per_batch = None
            self.mod_rows = n_tokens
        self.n_tokens = n_tokens
        self.rows_per_batch = rows_per_batch
        self.n_tiles = n_tokens // self.tm

    def mod_array(self, m):
        if self.tiles_per_batch is not None:
            return m[:, None, :]
        return jnp.repeat(m, self.rows_per_batch, axis=0)[None]

    def mod_spec(self, clamp=False):
        if self.tiles_per_batch is not None:
            tpb, last = self.tiles_per_batch, self.n_tiles - 1
            if clamp:
                return pl.BlockSpec((1, 1, D_MODEL), lambda i, *_: (jnp.minimum(i, last) // tpb, 0, 0))
            return pl.BlockSpec((1, 1, D_MODEL), lambda i, *_: (i // tpb, 0, 0))
        return pl.BlockSpec((1, self.mod_rows, D_MODEL), lambda i, *_: (0, 0, 0))

    def row_spec(self, width=D_MODEL, clamp=False):
        if clamp:
            last = self.n_tiles - 1
            return pl.BlockSpec((self.tm, width), lambda i, *_: (jnp.minimum(i, last), 0))
        return pl.BlockSpec((self.tm, width), lambda i, *_: (i, 0))


def _qk_kernel(x_ref, g_ref, sc_ref, sh_ref, w_ref, gqk_ref, p_ref, qkb_ref, qkf_ref, h_ref):
    j = pl.program_id(1)

    @pl.when(j == 0)
    def _():
        h_ref[...] = _rms_mod(x_ref[...], g_ref[...], sc_ref[0], sh_ref[0]).astype(BF16)

    z = jnp.dot(h_ref[...], w_ref[...], preferred_element_type=F32)
    ss = jnp.dot((z * z).astype(BF16), p_ref[...], preferred_element_type=F32)
    zn = z * lax.rsqrt(ss * (1.0 / HEAD_DIM) + NORM_EPS) * gqk_ref[0]
    qkf_ref[0] = zn
    scale = jnp.where(j == 0, ATTN_SCALE, 1.0)
    qkb_ref[...] = (zn * scale).astype(BF16)


def _qk_proj(rows, x, g, scale, shift, w_in_b, gqk, pmat):
    t = rows.n_tokens
    return pl.pallas_call(
        _qk_kernel,
        out_shape=(jax.ShapeDtypeStruct((t, 2 * D_MODEL), BF16),
                   jax.ShapeDtypeStruct((2, t, D_MODEL), F32)),
        grid=(rows.n_tiles, 2),
        in_specs=[rows.row_spec(),
                  pl.BlockSpec((1, D_MODEL), lambda i, j: (0, 0)),
                  rows.mod_spec(), rows.mod_spec(),
                  pl.BlockSpec((D_MODEL, D_MODEL), lambda i, j: (0, j)),
                  pl.BlockSpec((1, 1, D_MODEL), lambda i, j: (j, 0, 0)),
                  pl.BlockSpec((D_MODEL, D_MODEL), lambda i, j: (0, 0))],
        out_specs=(pl.BlockSpec((rows.tm, D_MODEL), lambda i, j: (i, j)),
                   pl.BlockSpec((1, rows.tm, D_MODEL), lambda i, j: (j, i, 0))),
        scratch_shapes=[pltpu.VMEM((rows.tm, D_MODEL), BF16)],
        compiler_params=_params("parallel", "arbitrary"),
        name="qk_proj",
    )(x, g, scale, shift, w_in_b, gqk, pmat)


def _qkt_kernel(x_ref, g_ref, sc_ref, sh_ref, wq_ref, gq_ref, p_ref, wt_ref, gk_ref, qb_ref, kf_ref, kb_ref):
    h = _rms_mod(x_ref[...], g_ref[...], sc_ref[0], sh_ref[0]).astype(BF16)
    z = jnp.dot(h, wq_ref[...], preferred_element_type=F32)
    ss = jnp.dot((z * z).astype(BF16), p_ref[...], preferred_element_type=F32)
    zn = z * lax.rsqrt(ss * (1.0 / HEAD_DIM) + NORM_EPS) * gq_ref[0]
    qb_ref[...] = (zn * (ATTN_SCALE * LOG2_E)).astype(BF16)
    zt = lax.dot_general(wt_ref[...], h, _NT, preferred_element_type=F32)
    tm = zt.shape[1]
    z3 = zt.reshape(D_MODEL // HEAD_DIM, HEAD_DIM, tm)
    s3 = jnp.sum(z3 * z3, axis=1, keepdims=True)
    g3 = gk_ref[...].reshape(D_MODEL // HEAD_DIM, HEAD_DIM, 1)
    kn = (z3 * lax.rsqrt(s3 * (1.0 / HEAD_DIM) + NORM_EPS) * g3).reshape(D_MODEL, tm)
    kf_ref[...] = kn
    kb_ref[...] = kn.astype(BF16)


def _qkt_proj(rows, x, g, scale, shift, w_in_b, gqk, pmat, w_kt, gk_col, batch, seq):
    tpb = rows.tiles_per_batch
    once = pl.Buffered(1)
    kt_spec = pl.BlockSpec((None, D_MODEL, rows.tm), lambda i: (i // tpb, 0, i % tpb))
    return pl.pallas_call(
        _qkt_kernel,
        out_shape=(jax.ShapeDtypeStruct((rows.n_tokens, D_MODEL), BF16),
                   jax.ShapeDtypeStruct((batch, D_MODEL, seq), F32),
                   jax.ShapeDtypeStruct((batch, D_MODEL, seq), BF16)),
        grid=(rows.n_tiles,),
        in_specs=[rows.row_spec(),
                  pl.BlockSpec((1, D_MODEL), lambda i: (0, 0)),
                  rows.mod_spec(), rows.mod_spec(),
                  pl.BlockSpec((D_MODEL, D_MODEL), lambda i: (0, 0), pipeline_mode=once),
                  pl.BlockSpec((1, 1, D_MODEL), lambda i: (0, 0, 0)),
                  pl.BlockSpec((D_MODEL, D_MODEL), lambda i: (0, 0), pipeline_mode=once),
                  pl.BlockSpec((D_MODEL, D_MODEL), lambda i: (0, 0), pipeline_mode=once),
                  pl.BlockSpec((D_MODEL, 1), lambda i: (0, 0))],
        out_specs=(rows.row_spec(), kt_spec, kt_spec),
        compiler_params=_params("parallel"),
        name="qkt_proj",
    )(x, g, scale, shift, w_in_b, gqk, pmat, w_kt, gk_col)


def _vconv_kernel(*refs, tiles_per_batch, rows_per_batch, tail_rows):
    if tiles_per_batch is None:
        (x_ref, g_ref, sc_ref, sh_ref, w_ref, wc_ref, s1_ref, s2_ref,
         vf_ref, vb_ref, sga_ref, cvp_ref, tail_ref, carry_ref) = refs
    else:
        (x_ref, g_ref, sc_ref, sh_ref, w_ref, wc_ref,
         vf_ref, vb_ref, sga_ref, cvp_ref, tail_ref, carry_ref) = refs
    i = pl.program_id(0)
    h = _rms_mod(x_ref[...], g_ref[...], sc_ref[0], sh_ref[0]).astype(BF16)

    def proj(k):
        return jnp.dot(h, w_ref[:, k * D_MODEL:(k + 1) * D_MODEL], preferred_element_type=F32)

    v = proj(0)
    vf_ref[...] = v
    vb_ref[...] = v.astype(BF16)
    sga_ref[...] = jax.nn.sigmoid(proj(4)).astype(sga_ref.dtype)
    u = proj(2) * proj(3)
    tm = u.shape[0]
    row = lax.broadcasted_iota(jnp.int32, (tm, 1), 0)
    r1 = pltpu.roll(u, 1, 0)
    r2 = pltpu.roll(u, 2, 0)
    if tiles_per_batch is None:
        t = row & (rows_per_batch - 1)
        u1 = jnp.where(t >= 1, r1, s1_ref[...])
        u2 = jnp.where(t >= 2, r2, s2_ref[...])
    else:
        first = (i % tiles_per_batch) == 0
        c = jnp.where(first, 0.0, carry_ref[...])
        u1 = jnp.where(row == 0, c[7:8], r1)
        u2 = jnp.where(row == 0, c[6:7], jnp.where(row == 1, c[7:8], r2))
        carry_ref[...] = u[tm - SUBLANES:]
    tail_ref[0] = u[tm - tail_rows:]
    wc = wc_ref[...]
    yc = wc[0:1] * u2 + wc[1:2] * u1 + wc[2:3] * u
    cvp_ref[...] = (jax.nn.sigmoid(proj(5)) * (proj(1) * yc)).astype(cvp_ref.dtype)


def _vconv_proj(rows, x, g, scale, shift, w_vconv, w_conv, state_rows, out_dtype):
    t = rows.n_tokens
    sample_mode = rows.tiles_per_batch is None
    tail_rows = rows.tm if sample_mode else SUBLANES
    n_tail_blocks = 1 if sample_mode else t // rows.rows_per_batch
    in_specs = [rows.row_spec(),
                pl.BlockSpec((1, D_MODEL), lambda i: (0, 0)),
                rows.mod_spec(), rows.mod_spec(),
                pl.BlockSpec(w_vconv.shape, lambda i: (0, 0), pipeline_mode=pl.Buffered(1)),
                pl.BlockSpec((CONV_WIDTH, D_MODEL), lambda i: (0, 0))]
    args = [x, g, scale, shift, w_vconv, w_conv]
    if sample_mode:
        in_specs += [rows.row_spec(), rows.row_spec()]
        args += list(state_rows)
        tail_spec = pl.BlockSpec((1, tail_rows, D_MODEL), lambda i: (0, 0, 0))
    else:
        tpb = rows.tiles_per_batch
        tail_spec = pl.BlockSpec((1, tail_rows, D_MODEL), lambda i: (i // tpb, 0, 0))
    kern = functools.partial(_vconv_kernel, tiles_per_batch=rows.tiles_per_batch,
                             rows_per_batch=rows.rows_per_batch, tail_rows=tail_rows)
    return pl.pallas_call(
        kern,
        out_shape=(jax.ShapeDtypeStruct((t, D_MODEL), F32),
                   jax.ShapeDtypeStruct((t, D_MODEL), BF16),
                   jax.ShapeDtypeStruct((t, D_MODEL), out_dtype),
                   jax.ShapeDtypeStruct((t, D_MODEL), out_dtype),
                   jax.ShapeDtypeStruct((n_tail_blocks, tail_rows, D_MODEL), F32)),
        grid=(rows.n_tiles,),
        in_specs=in_specs,
        out_specs=(rows.row_spec(), rows.row_spec(), rows.row_spec(), rows.row_spec(), tail_spec),
        scratch_shapes=[pltpu.VMEM((SUBLANES, D_MODEL), F32)],
        compiler_params=_params("arbitrary"),
        name="vconv_proj",
    )(*args)


def _lambda_value(lq1, lk1, lq2, lk2):
    e1 = jnp.exp(jnp.sum(lq1 * lk1, axis=-1, keepdims=True))
    e2 = jnp.exp(jnp.sum(lq2 * lk2, axis=-1, keepdims=True))
    return e1 - e2 + LAMBDA_INIT


def _subln_mix(o, g_sub, sga, cvp):
    ms = jnp.mean(o * o, axis=-1, keepdims=True)
    attn = (o * lax.rsqrt(ms + NORM_EPS) * g_sub) * (1.0 - LAMBDA_INIT)
    return sga * attn + cvp


def _flash_update(s, v_ones, m_ref, l_ref, acc_ref):
    m_prev = m_ref[...]
    m_next = jnp.maximum(m_prev, jnp.max(s, axis=1, keepdims=True))
    alpha = jnp.exp2(m_prev - m_next)
    p = jnp.exp2(s - jnp.concatenate([m_next] * (s.shape[1] // LANES), axis=1)).astype(BF16)
    pv = jnp.dot(p, v_ones, preferred_element_type=F32)
    acc_ref[...] = alpha * acc_ref[...] + pv[:, :V_DIM]
    l_ref[...] = alpha * l_ref[...] + pv[:, V_DIM:]
    m_ref[...] = m_next


def _attn_prompt_kernel(qi_ref, ki_ref, q_ref, kt_ref, v_ref, sga_ref, cvp_ref, gs_ref,
                        lq1_ref, lk1_ref, lq2_ref, lk2_ref,
                        o_ref, m0_ref, l0_ref, a0_ref, m1_ref, l1_ref, a1_ref):
    qi = qi_ref[pl.program_id(1)]
    ki = ki_ref[pl.program_id(1)]
    states = ((m0_ref, l0_ref, a0_ref), (m1_ref, l1_ref, a1_ref))

    @pl.when(ki == 0)
    def _():
        for m_ref, l_ref, a_ref in states:
            m_ref[...] = jnp.full(m_ref.shape, -jnp.inf, F32)
            l_ref[...] = jnp.zeros(l_ref.shape, F32)
            a_ref[...] = jnp.zeros(a_ref.shape, F32)

    tq, tk = q_ref.shape[0], kt_ref.shape[1]

    def step(masked):
        lane = lax.broadcasted_iota(jnp.int32, (1, V_DIM), 1)
        zero = jnp.zeros((), BF16)
        ones = jnp.ones((tk, V_DIM), BF16)
        if masked:
            r = lax.broadcasted_iota(jnp.int32, (tq, tk), 0)
            c = lax.broadcasted_iota(jnp.int32, (tq, tk), 1)
            keep = c <= r
        for h in range(N_HEADS):
            cols = slice(h * V_DIM, (h + 1) * V_DIM)
            q = q_ref[:, cols]
            kt = kt_ref[cols, :]
            v_ones = jnp.concatenate([v_ref[:, cols], ones], axis=1)
            qs = (jnp.where(lane < HEAD_DIM, q, zero), jnp.where(lane >= HEAD_DIM, q, zero))
            for qm, (m_ref, l_ref, a_ref) in zip(qs, states):
                s = jnp.dot(qm, kt, preferred_element_type=F32)
                if masked:
                    s = jnp.where(keep, s, -jnp.inf)
                _flash_update(s, v_ones, m_ref.at[h], l_ref.at[h], a_ref.at[h])

    @pl.when(ki < qi)
    def _():
        step(False)

    @pl.when(ki == qi)
    def _():
        step(True)
        lam = _lambda_value(lq1_ref[...], lk1_ref[...], lq2_ref[...], lk2_ref[...])
        for h in range(N_HEADS):
            cols = slice(h * V_DIM, (h + 1) * V_DIM)
            o = a0_ref[h] / l0_ref[h] - lam * (a1_ref[h] / l1_ref[h])
            mix = _subln_mix(o, gs_ref[...], sga_ref[:, cols].astype(F32), cvp_ref[:, cols].astype(F32))
            o_ref[:, cols] = mix.astype(o_ref.dtype)


def _attn_prompt(q_b, kt_b, v_b, sga, cvp, g_sub, lams, batch, seq):
    nb = seq // ATTN_BLOCK
    tq = ATTN_BLOCK
    pairs = [(qi, ki) for qi in range(nb) for ki in range(qi + 1)]
    qi_tab = jnp.array([p[0] for p in pairs], jnp.int32)
    ki_tab = jnp.array([p[1] for p in pairs], jnp.int32)
    q_spec = pl.BlockSpec((tq, D_MODEL), lambda b, p, qt, kt: (b * nb + qt[p], 0))
    k_spec = pl.BlockSpec((None, D_MODEL, tq), lambda b, p, qt, kt: (b, 0, kt[p]))
    v_spec = pl.BlockSpec((tq, D_MODEL), lambda b, p, qt, kt: (b * nb + kt[p], 0))
    vec64 = pl.BlockSpec((1, HEAD_DIM), lambda b, p, qt, kt: (0, 0))
    grid_spec = pltpu.PrefetchScalarGridSpec(
        num_scalar_prefetch=2,
        grid=(batch, len(pairs)),
        in_specs=[q_spec, k_spec, v_spec, q_spec, q_spec,
                  pl.BlockSpec((1, V_DIM), lambda b, p, qt, kt: (0, 0)),
                  vec64, vec64, vec64, vec64],
        out_specs=q_spec,
        scratch_shapes=[pltpu.VMEM((N_HEADS, tq, V_DIM), F32)] * 6)
    return pl.pallas_call(
        _attn_prompt_kernel,
        out_shape=jax.ShapeDtypeStruct((batch * seq, D_MODEL), BF16),
        grid_spec=grid_spec,
        compiler_params=_params("parallel", "arbitrary"),
        name="attn_prompt",
    )(qi_tab, ki_tab, q_b, kt_b, v_b, sga, cvp, g_sub, *lams)


def _attn_sample_kernel(pt_ref, q_ref, kn_ref, vn_ref, *rest, n_new, n_group):
    kc_refs = rest[:n_group]
    vc_refs = rest[n_group:2 * n_group]
    (sga_ref, cvp_ref, gs_ref, lq1_ref, lk1_ref, lq2_ref, lk2_ref,
     o_ref, qbd_ref, m_ref, l_ref, acc_ref) = rest[2 * n_group:]
    p = pl.program_id(1)
    rows_per_head = 2 * n_new

    @pl.when(p == 0)
    def _():
        q = q_ref[...] * ATTN_SCALE
        qrep = jnp.concatenate([q] * (N_HEADS * 2), axis=0)
        r = lax.broadcasted_iota(jnp.int32, qrep.shape, 0)
        c = lax.broadcasted_iota(jnp.int32, qrep.shape, 1)
        qbd_ref[...] = jnp.where(c // HEAD_DIM == r // n_new, qrep, 0.0).astype(BF16)
        m_ref[...] = jnp.full(m_ref.shape, -jnp.inf, F32)
        l_ref[...] = jnp.zeros(l_ref.shape, F32)
        acc_ref[...] = jnp.zeros(acc_ref.shape, F32)

    def update(s, pv_of):
        m_prev = m_ref[...]
        m_next = jnp.maximum(m_prev, jnp.max(s, axis=1, keepdims=True))
        alpha = jnp.exp(m_prev - m_next)
        width = s.shape[1]
        m_wide = m_next[:, :width] if width <= LANES else jnp.concatenate([m_next] * (width // LANES), axis=1)
        pr = jnp.exp(s - m_wide)
        l_ref[...] = alpha * l_ref[...] + jnp.sum(pr, axis=1, keepdims=True)
        acc_ref[...] = alpha * acc_ref[...] + pv_of(pr.astype(BF16))
        m_ref[...] = m_next

    def page_pv(pb, v_ref):
        return jnp.concatenate(
            [jnp.dot(pb[h * rows_per_head:(h + 1) * rows_per_head],
                     v_ref[pl.ds(h, PAGE_SIZE, stride=N_HEADS), :].astype(BF16), preferred_element_type=F32)
             for h in range(N_HEADS)], axis=0)

    qbd = qbd_ref[...]
    s = jnp.concatenate([jnp.dot(qbd, kc_refs[g][...].astype(BF16), preferred_element_type=F32)
                         for g in range(n_group)], axis=1)

    def pages_pv(pb):
        out = page_pv(pb[:, :PAGE_SIZE], vc_refs[0])
        for g in range(1, n_group):
            out = out + page_pv(pb[:, g * PAGE_SIZE:(g + 1) * PAGE_SIZE], vc_refs[g])
        return out

    update(s, pages_pv)

    @pl.when(p == pl.num_programs(1) - 1)
    def _():
        s_new = lax.dot_general(qbd_ref[...], kn_ref[...].astype(BF16), _NT, preferred_element_type=F32)
        r = lax.broadcasted_iota(jnp.int32, s_new.shape, 0)
        c = lax.broadcasted_iota(jnp.int32, s_new.shape, 1)
        s_new = jnp.where(c <= (r & (n_new - 1)), s_new, -jnp.inf)
        def new_pv(pb):
            return jnp.concatenate(
                [jnp.dot(pb[h * rows_per_head:(h + 1) * rows_per_head].astype(F32),
                         vn_ref[:, h * V_DIM:(h + 1) * V_DIM].astype(BF16).astype(F32),
                         preferred_element_type=F32) for h in range(N_HEADS)], axis=0)

        update(s_new, new_pv)
        lam = _lambda_value(lq1_ref[...], lk1_ref[...], lq2_ref[...], lk2_ref[...])
        acc = acc_ref[...] / l_ref[...]
        for h in range(N_HEADS):
            cols = slice(h * V_DIM, (h + 1) * V_DIM)
            r0 = h * rows_per_head
            o = acc[r0:r0 + n_new] - lam * acc[r0 + n_new:r0 + rows_per_head]
            o_ref[:, cols] = _subln_mix(o, gs_ref[...], sga_ref[:, cols], cvp_ref[:, cols])


def _attn_sample(qkf, vf, cache_k, cache_v, page_table, sga, cvp, g_sub, lams, n_seq, n_new):
    n_pages = page_table.shape[1]
    n_pool = cache_k.shape[0]
    width = N_HEADS * V_DIM
    grp = SAMPLE_PAGES_PER_STEP
    kc = jnp.transpose(cache_k, (0, 2, 3, 4, 1)).reshape(n_pool, width, PAGE_SIZE)
    pt = page_table.reshape(-1)
    n_rows = N_HEADS * 2 * n_new
    row = pl.BlockSpec((n_new, width), lambda b, p, pt: (b, 0))

    def page_index(g):
        return lambda b, p, pt: (pt[b * n_pages + p * grp + g], 0, 0)

    k_pages = [pl.BlockSpec((None, width, PAGE_SIZE), page_index(g)) for g in range(grp)]
    v_pages = [pl.BlockSpec((None, PAGE_SIZE * N_HEADS, V_DIM), page_index(g)) for g in range(grp)]
    vec64 = pl.BlockSpec((1, HEAD_DIM), lambda b, p, pt: (0, 0))
    vc = cache_v.reshape(n_pool, PAGE_SIZE * N_HEADS, V_DIM)
    grid_spec = pltpu.PrefetchScalarGridSpec(
        num_scalar_prefetch=1,
        grid=(n_seq, n_pages // grp),
        in_specs=[pl.BlockSpec((None, n_new, width), lambda b, p, pt: (0, b, 0)),
                  pl.BlockSpec((None, n_new, width), lambda b, p, pt: (1, b, 0)),
                  row, *k_pages, *v_pages, row, row,
                  pl.BlockSpec((1, V_DIM), lambda b, p, pt: (0, 0)),
                  vec64, vec64, vec64, vec64],
        out_specs=row,
        scratch_shapes=[pltpu.VMEM((n_rows, width), BF16),
                        pltpu.VMEM((n_rows, V_DIM), F32), pltpu.VMEM((n_rows, V_DIM), F32),
                        pltpu.VMEM((n_rows, V_DIM), F32)])
    return pl.pallas_call(
        functools.partial(_attn_sample_kernel, n_new=n_new, n_group=grp),
        out_shape=jax.ShapeDtypeStruct((n_seq * n_new, width), F32),
        grid_spec=grid_spec,
        compiler_params=_params("parallel", "arbitrary"),
        name="attn_sample",
    )(pt, qkf, qkf, vf, *([kc] * grp), *([vc] * grp), sga, cvp, g_sub, *lams)


def _out_kernel(mix_ref, x_ref, wo_ref, g1_ref, g_ref, sc_ref, sh_ref, wrh_ref, wrl_ref, br_ref, *refs, n_real_tiles):
    outs = refs[-4:]
    i = pl.program_id(0)

    @pl.when(i < n_real_tiles)
    def _():
        _out_tile(mix_ref, x_ref, wo_ref, g1_ref, g_ref, sc_ref, sh_ref, wrh_ref, wrl_ref, br_ref, *outs)

    @pl.when(i >= n_real_tiles)
    def _():
        for r in outs:
            r[...] = jnp.zeros(r.shape, r.dtype)


def _out_tile(mix_ref, x_ref, wo_ref, g1_ref, g_ref, sc_ref, sh_ref, wrh_ref, wrl_ref, br_ref,
              xm_ref, h2_ref, idx_ref, gt_ref):
    y = jnp.dot(mix_ref[...].astype(BF16), wo_ref[...], preferred_element_type=F32)
    xm = x_ref[...] + g1_ref[0] * y
    xm_ref[...] = xm
    h2 = _rms_mod(xm, g_ref[...], sc_ref[0], sh_ref[0])
    hi = h2.astype(BF16)
    _store_token_rows(h2_ref, _pack_pairs(hi.astype(F32)))
    lo = (h2 - hi.astype(F32)).astype(BF16)
    logits = (jnp.dot(hi, wrh_ref[...], preferred_element_type=F32)
              + jnp.dot(lo, wrh_ref[...], preferred_element_type=F32)
              + jnp.dot(hi, wrl_ref[...], preferred_element_type=F32)) + br_ref[...]
    lane = lax.broadcasted_iota(jnp.int32, logits.shape, 1).astype(F32)
    vals, idxs = [], []
    for _ in range(TOP_K):
        m = jnp.max(logits, axis=-1, keepdims=True)
        ix = jnp.min(jnp.where(logits == m, lane, float(LANES)), axis=-1, keepdims=True)
        logits = jnp.where(lane == ix, -jnp.inf, logits)
        vals.append(m)
        idxs.append(ix)
    es = [jnp.exp(v - vals[0]) for v in vals]
    denom = es[0] + es[1] + es[2] + es[3]
    idx_out = jnp.zeros(logits.shape, F32)
    gt_out = jnp.zeros(logits.shape, F32)
    for k in range(TOP_K):
        idx_out = jnp.where(lane == float(k), idxs[k], idx_out)
        gt_out = jnp.where(lane == float(k), es[k] / denom, gt_out)
    idx_ref[...] = idx_out.astype(jnp.int32)
    gt_ref[...] = gt_out


def _out_proj(rows, mix, x, w_o_b, gate1, g, scale, shift, wr_hi, wr_lo, b_r, total_tokens, first_row, shared_bufs):
    full = lambda shape: pl.BlockSpec(shape, lambda i: (0,) * len(shape))
    off = first_row // rows.tm
    creates = shared_bufs is None
    out_row = lambda width: pl.BlockSpec((rows.tm, width), lambda i: (i + off, 0))
    packed_rows = pl.BlockSpec((rows.tm * ROW_CHUNKS, LANES), lambda i: (i + off, 0))
    in_specs = [rows.row_spec(clamp=creates), rows.row_spec(clamp=creates), full((D_MODEL, D_MODEL)),
                rows.mod_spec(clamp=creates), full((1, D_MODEL)), rows.mod_spec(clamp=creates),
                rows.mod_spec(clamp=creates),
                full((D_MODEL, LANES)), full((D_MODEL, LANES)), full((1, LANES))]
    args = [mix, x, w_o_b, gate1, g, scale, shift, wr_hi, wr_lo, b_r]
    aliases = {}
    if not creates:
        aliases = {len(args) + j: j for j in range(len(shared_bufs))}
        in_specs += [pl.BlockSpec(memory_space=pl.ANY)] * len(shared_bufs)
        args += list(shared_bufs)
    extra = 1 if creates and total_tokens > rows.n_tokens else 0
    assert total_tokens - rows.n_tokens <= rows.tm or not creates, "other group must fit the one extra tile"
    return pl.pallas_call(
        functools.partial(_out_kernel, n_real_tiles=rows.n_tiles),
        out_shape=(jax.ShapeDtypeStruct((total_tokens, D_MODEL), F32),
                   jax.ShapeDtypeStruct((total_tokens * ROW_CHUNKS, LANES), jnp.uint32),
                   jax.ShapeDtypeStruct((total_tokens, LANES), jnp.int32),
                   jax.ShapeDtypeStruct((total_tokens, LANES), F32)),
        grid=(rows.n_tiles + extra,),
        in_specs=in_specs,
        out_specs=(out_row(D_MODEL), packed_rows, out_row(LANES), out_row(LANES)),
        input_output_aliases=aliases,
        compiler_params=_params("arbitrary"),
        name="out_proj_router",
    )(*args)


def _pack_pairs(x):
    bits = pltpu.bitcast(x, jnp.uint32)
    n = x.shape[1] // 2
    return (bits[:, :n] >> 16) | (bits[:, n:] & jnp.uint32(0xFFFF0000))


def _unpack_pairs(w):
    lo = pltpu.bitcast(w << 16, F32)
    hi = pltpu.bitcast(w & jnp.uint32(0xFFFF0000), F32)
    return jnp.concatenate([lo, hi], axis=1)


def _store_token_rows(ref, packed):
    n = packed.shape[0]
    for c in range(ROW_CHUNKS):
        ref[pl.ds(c, n, stride=ROW_CHUNKS), :] = packed[:, c * LANES:(c + 1) * LANES]


def _load_token_rows(ref, n):
    return jnp.concatenate([ref[pl.ds(c, n, stride=ROW_CHUNKS), :] for c in range(ROW_CHUNKS)], axis=1)


def _route_kernel(idx_ref, dest_ref, cnt_ref, run_ref, start_ref):
    ph = pl.program_id(0)
    i = pl.program_id(1)
    idx = idx_ref[...]
    tm = idx.shape[0]
    lane = lax.broadcasted_iota(jnp.int32, idx.shape, 1)
    onehots = [(lane == idx[:, k:k + 1]).astype(F32) for k in range(TOP_K)]
    member = onehots[0] + onehots[1] + onehots[2] + onehots[3]
    tile_count = jnp.sum(member, axis=0, keepdims=True)

    @pl.when((ph == 0) & (i == 0))
    def _():
        cnt_ref[...] = jnp.zeros(cnt_ref.shape, F32)

    @pl.when(ph == 0)
    def _():
        cnt_ref[...] = cnt_ref[...] + tile_count

    @pl.when((ph == 1) & (i == 0))
    def _():
        cnt = cnt_ref[...]
        padded = jnp.floor((cnt + (MOE_BLOCK - 1)) * (1.0 / MOE_BLOCK)) * MOE_BLOCK
        l1 = lax.broadcasted_iota(jnp.int32, cnt.shape, 1)
        incl = padded
        for s in (1, 2, 4, 8, 16, 32, 64):
            incl = incl + jnp.where(l1 >= s, pltpu.roll(incl, s, 1), 0.0)
        start_ref[...] = incl - padded
        run_ref[...] = jnp.zeros(run_ref.shape, F32)

    @pl.when(ph == 1)
    def _():
        r = lax.broadcasted_iota(jnp.int32, (tm, tm), 0)
        c = lax.broadcasted_iota(jnp.int32, (tm, tm), 1)
        earlier = (c < r).astype(BF16)
        before = jnp.dot(earlier, member.astype(BF16), preferred_element_type=F32)
        base = before + run_ref[0:1] + start_ref[0:1]
        out = jnp.zeros(idx.shape, F32)
        for k in range(TOP_K):
            d = jnp.sum(onehots[k] * base, axis=1, keepdims=True)
            out = jnp.where(lane == k, d, out)
        dest_ref[...] = out.astype(jnp.int32)
        run_ref[...] = run_ref[...] + tile_count


def _route(rows, idx):
    tm = rows.tm
    dest, counts = pl.pallas_call(
        _route_kernel,
        out_shape=(jax.ShapeDtypeStruct((rows.n_tokens, LANES), jnp.int32),
                   jax.ShapeDtypeStruct((SUBLANES, LANES), F32)),
        grid=(2, rows.n_tiles),
        in_specs=[pl.BlockSpec((tm, LANES), lambda ph, i: (i, 0))],
        out_specs=(pl.BlockSpec((tm, LANES), lambda ph, i: (i * ph, 0)),
                   pl.BlockSpec((SUBLANES, LANES), lambda ph, i: (0, 0))),
        scratch_shapes=[pltpu.VMEM((SUBLANES, LANES), F32), pltpu.VMEM((SUBLANES, LANES), F32)],
        compiler_params=_params("arbitrary", "arbitrary"),
        name="moe_route",
    )(idx)
    return dest, counts


def _dispatch_kernel(dest_ref, h_ref, zero_ref, xs_ref, sem):
    del zero_ref
    i = pl.program_id(0)
    tm = h_ref.shape[0] // ROW_CHUNKS

    def body(t, carry):
        for k in range(TOP_K):
            d = dest_ref[(i * tm + t) * TOP_K + k]
            pltpu.make_async_copy(h_ref.at[pl.ds(t * ROW_CHUNKS, ROW_CHUNKS)],
                                  xs_ref.at[pl.ds(d * ROW_CHUNKS, ROW_CHUNKS)], sem).start(priority=k % 2)
        return carry

    lax.fori_loop(0, tm, body, 0, unroll=8)
    n_words = tm * TOP_K * ROW_CHUNKS
    pltpu.make_async_copy(xs_ref.at[pl.ds(0, n_words)], xs_ref.at[pl.ds(0, n_words)], sem).wait()


def _dispatch(rows, dest_flat, h2u, n_rows):
    tm = rows.tm
    zeros = jnp.zeros((n_rows * ROW_CHUNKS, LANES), jnp.uint32)
    grid_spec = pltpu.PrefetchScalarGridSpec(
        num_scalar_prefetch=1, grid=(rows.n_tiles,),
        in_specs=[pl.BlockSpec((tm * ROW_CHUNKS, LANES), lambda i, d: (i, 0)),
                  pl.BlockSpec(memory_space=pl.ANY)],
        out_specs=pl.BlockSpec(memory_space=pl.ANY),
        scratch_shapes=[pltpu.SemaphoreType.DMA(())])
    return pl.pallas_call(
        _dispatch_kernel,
        out_shape=jax.ShapeDtypeStruct((n_rows * ROW_CHUNKS, LANES), jnp.uint32),
        grid_spec=grid_spec,
        input_output_aliases={2: 0},
        compiler_params=_params("arbitrary"),
        name="moe_dispatch",
    )(dest_flat, h2u, zeros)


def _expert_prepare(wgu_raw, wd_raw, wt_s, wg_s, wu_s, wd_s):
    n = wgu_raw.shape[1]
    n_lane_tiles = wt_s.shape[0]
    for c in range(n // EXPERT_XPOSE_CHUNK):
        cols = slice(c * EXPERT_XPOSE_CHUNK, (c + 1) * EXPERT_XPOSE_CHUNK)
        wt = wgu_raw[:, cols].T
        for j in range(n_lane_tiles):
            wt_s[j, cols, :] = wt[:, j * LANES:(j + 1) * LANES]
    for j in range(n_lane_tiles):
        lanes = slice(j * LANES, (j + 1) * LANES)
        wg_s[:, lanes] = wt_s[j, pl.ds(0, n // 2, stride=2), :].astype(BF16)
        wu_s[:, lanes] = wt_s[j, pl.ds(1, n // 2, stride=2), :].astype(BF16)
    wd_s[...] = wd_raw[...].astype(BF16)


def _expert_block(x_ref, bg_ref, bu_ref, bd_ref, y_ref, wg_s, wu_s, wd_s):
    x = _unpack_pairs(_load_token_rows(x_ref, MOE_BLOCK)).astype(BF16)
    y = bd_ref[...]
    for c in range(wd_s.shape[0] // EXPERT_FF_CHUNK):
        ff = slice(c * EXPERT_FF_CHUNK, (c + 1) * EXPERT_FF_CHUNK)
        g = lax.dot_general(x, wg_s[ff, :], _NT, preferred_element_type=F32) + bg_ref[:, ff]
        u = lax.dot_general(x, wu_s[ff, :], _NT, preferred_element_type=F32) + bu_ref[:, ff]
        gate = jnp.minimum(g, SWIGLU_LIMIT)
        up = jnp.clip(u, -SWIGLU_LIMIT, SWIGLU_LIMIT)
        glu = gate * jax.nn.sigmoid(SWIGLU_ALPHA * gate)
        a = ((up + 1.0) * glu).astype(BF16)
        y = y + jnp.dot(a, wd_s[ff, :], preferred_element_type=F32)
    _store_token_rows(y_ref, _pack_pairs(y.astype(BF16).astype(F32)))


def _expert_kernel(be_ref, first_ref, last_ref, nxt_ref, slot_ref, nu_ref,
                   x_ref, wgu_hbm, wd_hbm, bg_ref, bu_ref, bd_ref, y_ref,
                   wgu_raw, wd_raw, wt_s, wg_a, wu_a, wd_a, wg_b, wu_b, wd_b, sem):
    i = pl.program_id(0)
    used = i < nu_ref[0]
    first = first_ref[i] == 1
    last = last_ref[i] == 1
    nxt = nxt_ref[i]
    in_a = slot_ref[i] == 0
    sets = ((wg_a, wu_a, wd_a), (wg_b, wu_b, wd_b))

    def fetch(e):
        return (pltpu.make_async_copy(wgu_hbm.at[e], wgu_raw, sem.at[0]),
                pltpu.make_async_copy(wd_hbm.at[e], wd_raw, sem.at[1]))

    @pl.when(i == 0)
    def _():
        for cp in fetch(be_ref[0]):
            cp.start()
        for cp in fetch(be_ref[0]):
            cp.wait()
        _expert_prepare(wgu_raw, wd_raw, wt_s, *sets[0])

    @pl.when(used & first & (nxt >= 0))
    def _():
        for cp in fetch(nxt):
            cp.start()

    hand_over = used & last & (nxt >= 0)
    for cur, flag in ((0, in_a), (1, jnp.logical_not(in_a))):
        @pl.when(hand_over & flag)
        def _(cur=cur):
            for cp in fetch(nxt):
                cp.wait()
            _expert_block(x_ref, bg_ref, bu_ref, bd_ref, y_ref, *sets[cur])
            _expert_prepare(wgu_raw, wd_raw, wt_s, *sets[1 - cur])

        @pl.when(used & jnp.logical_not(last & (nxt >= 0)) & flag)
        def _(cur=cur):
            _expert_block(x_ref, bg_ref, bu_ref, bd_ref, y_ref, *sets[cur])

    @pl.when(i >= nu_ref[0])
    def _():
        y_ref[...] = jnp.zeros(y_ref.shape, y_ref.dtype)


def _experts(x_sorted, block_e, n_used, w_gu, w_dn, b_g, b_u, b_d):
    rows = x_sorted.shape[0] // ROW_CHUNKS
    n_blocks = rows // MOE_BLOCK
    d_gu = w_gu.shape[2]
    d_ff = w_dn.shape[1]
    blk = jnp.arange(n_blocks, dtype=jnp.int32)
    used = blk < n_used[0]
    prev_e = jnp.concatenate([block_e[:1] - 1, block_e[:-1]])
    next_e = jnp.concatenate([block_e[1:], block_e[-1:] + 1])
    first = (used & (block_e != prev_e)).astype(jnp.int32)
    last = (used & ((block_e != next_e) | (blk + 1 >= n_used[0]))).astype(jnp.int32)
    after = jnp.sum((block_e[None, :] <= block_e[:, None]).astype(jnp.int32), axis=1)
    nxt = jnp.where(after < n_used[0], block_e[jnp.minimum(after, n_blocks - 1)], -1).astype(jnp.int32)
    slot = ((jnp.cumsum(first) - 1) % 2).astype(jnp.int32)
    n_prefetch = 6
    by_expert = lambda i, be, *_: (be[i], 0, 0)
    xspec = pl.BlockSpec((MOE_BLOCK * ROW_CHUNKS, LANES), lambda i, *_: (i, 0))
    weights = lambda: (pltpu.VMEM((d_ff, D_MODEL), BF16), pltpu.VMEM((d_ff, D_MODEL), BF16),
                       pltpu.VMEM((d_ff, D_MODEL), BF16))
    grid_spec = pltpu.PrefetchScalarGridSpec(
        num_scalar_prefetch=n_prefetch, grid=(n_blocks,),
        in_specs=[xspec,
                  pl.BlockSpec(memory_space=pl.ANY),
                  pl.BlockSpec(memory_space=pl.ANY),
                  pl.BlockSpec((None, 1, d_ff), by_expert),
                  pl.BlockSpec((None, 1, d_ff), by_expert),
                  pl.BlockSpec((None, 1, D_MODEL), by_expert)],
        out_specs=xspec,
        scratch_shapes=[pltpu.VMEM((D_MODEL, d_gu), F32), pltpu.VMEM((d_ff, D_MODEL), F32),
                        pltpu.VMEM((D_MODEL // LANES, d_gu, LANES), F32),
                        *weights(), *weights(),
                        pltpu.SemaphoreType.DMA((2,))])
    return pl.pallas_call(
        _expert_kernel,
        out_shape=jax.ShapeDtypeStruct((rows * ROW_CHUNKS, LANES), jnp.uint32),
        grid_spec=grid_spec,
        compiler_params=pltpu.CompilerParams(dimension_semantics=("arbitrary",),
                                             vmem_limit_bytes=EXPERT_VMEM_LIMIT_BYTES),
        name="experts",
    )(block_e, first, last, nxt, slot, n_used, x_sorted, w_gu, w_dn, b_g, b_u, b_d)


def _combine_kernel(dest_ref, xm_ref, gt_ref, g2p_ref, g2s_ref, yb_ref, yp_ref, ys_ref, rows_ref, sem,
                    *, n_prompt_tiles):
    i = pl.program_id(0)
    tm = xm_ref.shape[0]
    slot = i % 2

    def gather(tile, into):
        def body(t, carry):
            for k in range(TOP_K):
                d = dest_ref[(tile * tm + t) * TOP_K + k]
                pltpu.make_async_copy(yb_ref.at[pl.ds(d * ROW_CHUNKS, ROW_CHUNKS)],
                                      rows_ref.at[into, k, pl.ds(t * ROW_CHUNKS, ROW_CHUNKS)],
                                      sem.at[into]).start(priority=k % 2)
            return carry

        lax.fori_loop(0, tm, body, 0, unroll=8)

    @pl.when(i == 0)
    def _():
        gather(0, 0)

    @pl.when(i + 1 < pl.num_programs(0))
    def _():
        gather(i + 1, 1 - slot)

    pltpu.make_async_copy(rows_ref.at[slot], rows_ref.at[slot], sem.at[slot]).wait()
    gt = gt_ref[...]
    acc = jnp.zeros(xm_ref.shape, F32)
    for k in range(TOP_K):
        acc = acc + gt[:, k:k + 1] * _unpack_pairs(_load_token_rows(rows_ref.at[slot, k], tm))
    is_prompt = i < n_prompt_tiles
    y = xm_ref[...] + jnp.where(is_prompt, g2p_ref[0], g2s_ref[0]) * acc

    @pl.when(is_prompt)
    def _():
        yp_ref[...] = y

    @pl.when(jnp.logical_not(is_prompt))
    def _():
        ys_ref[...] = y


def _combine(tiles, dest_flat, xm, gates, gate2_p, gate2_s, y_buf, n_prompt, rows_per_seq):
    tm = tiles.tm
    n_prompt_tiles = n_prompt // tm
    n_sample = tiles.n_tokens - n_prompt
    assert n_sample == tm and gate2_s.shape[1] == tm, "the sample group is one token tile"
    tiles_per_seq = rows_per_seq // tm
    last_seq = gate2_p.shape[0] - 1
    grid_spec = pltpu.PrefetchScalarGridSpec(
        num_scalar_prefetch=1, grid=(tiles.n_tiles,),
        in_specs=[pl.BlockSpec((tm, D_MODEL), lambda i, d: (i, 0)),
                  pl.BlockSpec((tm, LANES), lambda i, d: (i, 0)),
                  pl.BlockSpec((1, 1, D_MODEL), lambda i, d: (jnp.minimum(i // tiles_per_seq, last_seq), 0, 0)),
                  pl.BlockSpec((1, tm, D_MODEL), lambda i, d: (0, 0, 0)),
                  pl.BlockSpec(memory_space=pl.ANY)],
        out_specs=(pl.BlockSpec((tm, D_MODEL), lambda i, d: (jnp.minimum(i, n_prompt_tiles - 1), 0)),
                   pl.BlockSpec((tm, D_MODEL), lambda i, d: (0, 0))),
        scratch_shapes=[pltpu.VMEM((2, TOP_K, tm * ROW_CHUNKS, LANES), jnp.uint32),
                        pltpu.SemaphoreType.DMA((2,))])
    return pl.pallas_call(
        functools.partial(_combine_kernel, n_prompt_tiles=n_prompt_tiles),
        out_shape=(jax.ShapeDtypeStruct((n_prompt, D_MODEL), F32),
                   jax.ShapeDtypeStruct((n_sample, D_MODEL), F32)),
        grid_spec=grid_spec,
        compiler_params=_params("arbitrary"),
        name="moe_combine",
    )(dest_flat, xm, gates, gate2_p, gate2_s, y_buf)


class _Tiles:
    def __init__(self, n_tokens, tm):
        self.n_tokens, self.tm, self.n_tiles = n_tokens, tm, n_tokens // tm


def _moe(xm, h2u, idx, gates, gate2_p, gate2_s, moe_w, n_prompt, rows_per_seq):
    rows = _Tiles(xm.shape[0], MOE_TOKEN_TILE)
    wide = _Tiles(xm.shape[0], MOE_WIDE_TOKEN_TILE)
    assert wide.n_tiles * wide.tm == rows.n_tokens == rows.n_tiles * rows.tm
    a = rows.n_tokens * TOP_K
    n_blocks = -(-a // MOE_BLOCK) + N_EXPERTS
    dest, counts = _route(wide, idx)
    dest_flat = dest[:, :TOP_K].reshape(a)
    cnt = counts[0, :N_EXPERTS].astype(jnp.int32)
    pad_end = jnp.cumsum((cnt + MOE_BLOCK - 1) // MOE_BLOCK * MOE_BLOCK)
    block_row = jnp.arange(n_blocks, dtype=jnp.int32) * MOE_BLOCK
    block_e = jnp.minimum(jnp.sum((pad_end[None, :] <= block_row[:, None]).astype(jnp.int32), axis=1),
                          N_EXPERTS - 1)
    n_used = (pad_end[-1:] // MOE_BLOCK).astype(jnp.int32)
    x_sorted = _dispatch(wide, dest_flat, h2u, n_blocks * MOE_BLOCK)
    y_buf = _experts(x_sorted, block_e, n_used, *moe_w)
    return _combine(rows, dest_flat, xm, gates, gate2_p, gate2_s, y_buf, n_prompt, rows_per_seq)


def _group(rows, x, ada, state_rows, qk_and_attend, shared, total_tokens, first_row, token_bufs):
    (g_mix, w_vconv, w_conv, w_o_b, g_ffn, wr_hi, wr_lo, b_r, mid_dtype) = shared
    shift1, scale1, gate1, shift2, scale2, gate2 = [rows.mod_array(m) for m in jnp.split(ada, 6, axis=-1)]
    v_f, v_b, sga, cvp, tail = _vconv_proj(rows, x, g_mix, scale1, shift1, w_vconv, w_conv, state_rows, mid_dtype)
    mix, k_out = qk_and_attend(x, scale1, shift1, v_f, v_b, sga, cvp)
    token_bufs = _out_proj(rows, mix, x, w_o_b, gate1, g_ffn, scale2, shift2, wr_hi, wr_lo, b_r,
                           total_tokens, first_row, token_bufs)
    return token_bufs, gate2, k_out, v_f, tail


def kernel(x_prompt, x_sample, c_prompt, c_sample, cache_k, cache_v, state_conv, page_table, w_ada, b_ada, g_norm_mix, w_in, g_q, g_k, lambda_q1, lambda_k1, lambda_q2, lambda_k2, g_subln, w_conv, w_o, g_norm_ffn, w_router, b_router, w_gate_up, b_gate_up, w_down, b_down):
    assert w_in.shape[0] == 1, "single-layer stack"
    batch, seq, _ = x_prompt.shape
    n_seq, n_new, _ = x_sample.shape
    tp, ts = batch * seq, n_seq * n_new
    n_chunks = D_MODEL // HEAD_DIM

    ada = _ada(jnp.concatenate([c_prompt, c_sample], axis=0), w_ada[0], b_ada[0][None])

    w_in_b = w_in[0].astype(BF16)
    w_kt = w_in[0][:, D_MODEL:2 * D_MODEL].T.astype(BF16)
    w_vconv = w_in_b[:, 2 * D_MODEL:]
    w_o_b = w_o[0].astype(BF16)
    gqk = jnp.stack([jnp.tile(g_q[0], n_chunks), jnp.tile(g_k[0], n_chunks)])[:, None, :]
    gk_col = jnp.tile(g_k[0], n_chunks)[:, None]
    blk = jnp.arange(D_MODEL, dtype=jnp.int32) // HEAD_DIM
    pmat = (blk[:, None] == blk[None, :]).astype(BF16)
    wr = jnp.pad(w_router[0], ((0, 0), (0, LANES - N_EXPERTS)))
    wr_hi = wr.astype(BF16)
    wr_lo = (wr - wr_hi.astype(F32)).astype(BF16)
    b_r = jnp.pad(b_router[0], (0, LANES - N_EXPERTS), constant_values=NEG_BIG)[None]
    moe_w = (w_gate_up[0], w_down[0], b_gate_up[0][:, None, 0::2], b_gate_up[0][:, None, 1::2],
             b_down[0][:, None, :])
    g_mix = g_norm_mix[0][None]
    g_ffn = g_norm_ffn[0][None]
    g_sub = g_subln[0][None]
    lams = (lambda_q1[0][None], lambda_k1[0][None], lambda_q2[0][None], lambda_k2[0][None])

    def shared(mid_dtype):
        return (g_mix, w_vconv, w_conv[0], w_o_b, g_ffn, wr_hi, wr_lo, b_r, mid_dtype)

    rows_p = _Rows(tp, seq)

    def attend_p(x, scale1, shift1, v_f, v_b, sga, cvp):
        q_b, kt_f, kt_b = _qkt_proj(rows_p, x, g_mix, scale1, shift1, w_in_b, gqk, pmat, w_kt, gk_col, batch, seq)
        mix = _attn_prompt(q_b, kt_b, v_b, sga, cvp, g_sub, lams, batch, seq)
        return mix, kt_f

    bufs, gate2_p, kt_p, v_p, tail_p = _group(rows_p, x_prompt.reshape(tp, D_MODEL), ada[:batch], None, attend_p,
                                              shared(BF16), tp + ts, 0, None)

    rows_s = _Rows(ts, n_new)
    st = state_conv[0]
    zeros = jnp.zeros((n_seq, n_new - 2, D_MODEL), F32)
    s1 = jnp.concatenate([st[:, 1:2], jnp.zeros((n_seq, n_new - 1, D_MODEL), F32)], axis=1).reshape(ts, D_MODEL)
    s2 = jnp.concatenate([st, zeros], axis=1).reshape(ts, D_MODEL)

    def attend_s(x, scale1, shift1, v_f, v_b, sga, cvp):
        _, qk_f = _qk_proj(rows_s, x, g_mix, scale1, shift1, w_in_b, gqk, pmat)
        mix = _attn_sample(qk_f, v_f, cache_k[0], cache_v[0], page_table, sga, cvp, g_sub, lams, n_seq, n_new)
        return mix, qk_f[1]

    bufs, gate2_s, k_s, v_s, tail_s = _group(rows_s, x_sample.reshape(ts, D_MODEL), ada[batch:], (s1, s2), attend_s,
                                             shared(F32), tp + ts, tp, bufs)

    y_p, y_s = _moe(*bufs, gate2_p, gate2_s, moe_w, tp, seq)

    tail_s = tail_s.reshape(n_seq, n_new, D_MODEL)
    k_p = kt_p.reshape(1, batch, N_HEADS, 2, HEAD_DIM, seq).transpose(0, 1, 5, 2, 3, 4)
    return (y_p.reshape(batch, seq, D_MODEL),
            y_s.reshape(n_seq, n_new, D_MODEL),
            k_p,
            v_p.reshape(1, batch, seq, N_HEADS, V_DIM),
            tail_p[:, SUBLANES - (CONV_WIDTH - 1):][None],
            k_s.reshape(1, n_seq, n_new, N_HEADS, 2, HEAD_DIM),
            v_s.reshape(1, n_seq, n_new, N_HEADS, V_DIM),
            tail_s[:, n_new - (CONV_WIDTH - 1):][None])
```

```python
import functools
import math

import jax
import jax.numpy as jnp
from jax import lax
from jax.experimental import pallas as pl
from jax.experimental.pallas import tpu as pltpu

F32 = jnp.float32
BF16 = jnp.bfloat16

D_MODEL = 1024
HEAD_DIM = 64
V_DIM = 2 * HEAD_DIM
N_HEADS = D_MODEL // V_DIM
ATTN_SCALE = HEAD_DIM ** -0.5
LOG2_E = math.log2(math.e)
CONV_WIDTH = 3
PAGE_SIZE = 128
N_EXPERTS = 32
TOP_K = 4
SWIGLU_LIMIT = 7.0
SWIGLU_ALPHA = 1.702
NORM_EPS = 1e-6
LAMBDA_INIT = 0.8 - 0.6 * math.exp(-0.3 * 0)

VMEM_LIMIT_BYTES = 48 * 1024 * 1024
EXPERT_VMEM_LIMIT_BYTES = 56 * 1024 * 1024
LANES = 128
PACKED = D_MODEL // 2
ROW_CHUNKS = PACKED // LANES
SUBLANES = 8

ROW_TILE_PROMPT = 512
ATTN_BLOCK = 512
MOE_BLOCK = 512
MOE_TOKEN_TILE = 256
MOE_WIDE_TOKEN_TILE = 1280
ZERO_FILL_ROWS = 2048
EXPERT_XPOSE_CHUNK = 512
EXPERT_FF_CHUNK = 512
ADA_COL_TILE = 1536
SAMPLE_PAGES_PER_STEP = 16
NEG_BIG = -1e30

_NT = (((1,), (1,)), ((), ()))


def _params(*sem):
    return pltpu.CompilerParams(dimension_semantics=sem, vmem_limit_bytes=VMEM_LIMIT_BYTES)


def _rms_mod(x, g, scale, shift):
    ms = jnp.mean(x * x, axis=-1, keepdims=True)
    return (x * lax.rsqrt(ms + NORM_EPS) * g) * (1.0 + scale) + shift


def _ada_kernel(c_ref, w_ref, b_ref, o_ref):
    c = c_ref[...]
    s = (c * jax.nn.sigmoid(c)).astype(BF16)
    o_ref[...] = jnp.dot(s, w_ref[...].astype(BF16), preferred_element_type=F32) + b_ref[...]


def _ada(c_all, w_ada, b_ada):
    n = c_all.shape[0]
    width = w_ada.shape[1]
    return pl.pallas_call(
        _ada_kernel,
        out_shape=jax.ShapeDtypeStruct((n, width), F32),
        grid=(width // ADA_COL_TILE,),
        in_specs=[pl.BlockSpec((n, D_MODEL), lambda j: (0, 0)),
                  pl.BlockSpec((D_MODEL, ADA_COL_TILE), lambda j: (0, j)),
                  pl.BlockSpec((1, ADA_COL_TILE), lambda j: (0, j))],
        out_specs=pl.BlockSpec((n, ADA_COL_TILE), lambda j: (0, j)),
        compiler_params=_params("arbitrary"),
        name="ada",
    )(c_all, w_ada, b_ada)


class _Rows:
    def __init__(self, n_tokens, rows_per_batch):
        if rows_per_batch >= ROW_TILE_PROMPT:
            self.tm = ROW_TILE_PROMPT
            self.tiles_per_batch = rows_per_batch // self.tm
            self.mod_rows = 1
        else:
            self.tm = n_tokens
            self.tiles_per_batch = None
            self.mod_rows = n_tokens
        self.n_tokens = n_tokens
        self.rows_per_batch = rows_per_batch
        self.n_tiles = n_tokens // self.tm

    def mod_array(self, m):
        if self.tiles_per_batch is not None:
            return m[:, None, :]
        return jnp.repeat(m, self.rows_per_batch, axis=0)[None]

    def mod_spec(self, clamp=False):
        if self.tiles_per_batch is not None:
            tpb, last = self.tiles_per_batch, self.n_tiles - 1
            if clamp:
                return pl.BlockSpec((1, 1, D_MODEL), lambda i, *_: (jnp.minimum(i, last) // tpb, 0, 0))
            return pl.BlockSpec((1, 1, D_MODEL), lambda i, *_: (i // tpb, 0, 0))
        return pl.BlockSpec((1, self.mod_rows, D_MODEL), lambda i, *_: (0, 0, 0))

    def row_spec(self, width=D_MODEL, clamp=False):
        if clamp:
            last = self.n_tiles - 1
            return pl.BlockSpec((self.tm, width), lambda i, *_: (jnp.minimum(i, last), 0))
        return pl.BlockSpec((self.tm, width), lambda i, *_: (i, 0))


def _qk_kernel(x_ref, g_ref, sc_ref, sh_ref, w_ref, gqk_ref, p_ref, qkb_ref, qkf_ref, h_ref):
    j = pl.program_id(1)

    @pl.when(j == 0)
    def _():
        h_ref[...] = _rms_mod(x_ref[...], g_ref[...], sc_ref[0], sh_ref[0]).astype(BF16)

    z = jnp.dot(h_ref[...], w_ref[...], preferred_element_type=F32)
    ss = jnp.dot((z * z).astype(BF16), p_ref[...], preferred_element_type=F32)
    zn = z * lax.rsqrt(ss * (1.0 / HEAD_DIM) + NORM_EPS) * gqk_ref[0]
    qkf_ref[0] = zn
    scale = jnp.where(j == 0, ATTN_SCALE, 1.0)
    qkb_ref[...] = (zn * scale).astype(BF16)


def _qk_proj(rows, x, g, scale, shift, w_in_b, gqk, pmat):
    t = rows.n_tokens
    return pl.pallas_call(
        _qk_kernel,
        out_shape=(jax.ShapeDtypeStruct((t, 2 * D_MODEL), BF16),
                   jax.ShapeDtypeStruct((2, t, D_MODEL), F32)),
        grid=(rows.n_tiles, 2),
        in_specs=[rows.row_spec(),
                  pl.BlockSpec((1, D_MODEL), lambda i, j: (0, 0)),
                  rows.mod_spec(), rows.mod_spec(),
                  pl.BlockSpec((D_MODEL, D_MODEL), lambda i, j: (0, j)),
                  pl.BlockSpec((1, 1, D_MODEL), lambda i, j: (j, 0, 0)),
                  pl.BlockSpec((D_MODEL, D_MODEL), lambda i, j: (0, 0))],
        out_specs=(pl.BlockSpec((rows.tm, D_MODEL), lambda i, j: (i, j)),
                   pl.BlockSpec((1, rows.tm, D_MODEL), lambda i, j: (j, i, 0))),
        scratch_shapes=[pltpu.VMEM((rows.tm, D_MODEL), BF16)],
        compiler_params=_params("parallel", "arbitrary"),
        name="qk_proj",
    )(x, g, scale, shift, w_in_b, gqk, pmat)


def _qkt_kernel(x_ref, g_ref, sc_ref, sh_ref, wq_ref, gq_ref, p_ref, wt_ref, gk_ref, qb_ref, kf_ref, kb_ref):
    h = _rms_mod(x_ref[...], g_ref[...], sc_ref[0], sh_ref[0]).astype(BF16)
    z = jnp.dot(h, wq_ref[...], preferred_element_type=F32)
    ss = jnp.dot((z * z).astype(BF16), p_ref[...], preferred_element_type=F32)
    zn = z * lax.rsqrt(ss * (1.0 / HEAD_DIM) + NORM_EPS) * gq_ref[0]
    qb_ref[...] = (zn * (ATTN_SCALE * LOG2_E)).astype(BF16)
    zt = lax.dot_general(wt_ref[...], h, _NT, preferred_element_type=F32)
    tm = zt.shape[1]
    z3 = zt.reshape(D_MODEL // HEAD_DIM, HEAD_DIM, tm)
    s3 = jnp.sum(z3 * z3, axis=1, keepdims=True)
    g3 = gk_ref[...].reshape(D_MODEL // HEAD_DIM, HEAD_DIM, 1)
    kn = (z3 * lax.rsqrt(s3 * (1.0 / HEAD_DIM) + NORM_EPS) * g3).reshape(D_MODEL, tm)
    kf_ref[...] = kn
    kb_ref[...] = kn.astype(BF16)


def _qkt_proj(rows, x, g, scale, shift, w_in_b, gqk, pmat, w_kt, gk_col, batch, seq):
    tpb = rows.tiles_per_batch
    once = pl.Buffered(1)
    kt_spec = pl.BlockSpec((None, D_MODEL, rows.tm), lambda i: (i // tpb, 0, i % tpb))
    return pl.pallas_call(
        _qkt_kernel,
        out_shape=(jax.ShapeDtypeStruct((rows.n_tokens, D_MODEL), BF16),
                   jax.ShapeDtypeStruct((batch, D_MODEL, seq), F32),
                   jax.ShapeDtypeStruct((batch, D_MODEL, seq), BF16)),
        grid=(rows.n_tiles,),
        in_specs=[rows.row_spec(),
                  pl.BlockSpec((1, D_MODEL), lambda i: (0, 0)),
                  rows.mod_spec(), rows.mod_spec(),
                  pl.BlockSpec((D_MODEL, D_MODEL), lambda i: (0, 0), pipeline_mode=once),
                  pl.BlockSpec((1, 1, D_MODEL), lambda i: (0, 0, 0)),
                  pl.BlockSpec((D_MODEL, D_MODEL), lambda i: (0, 0), pipeline_mode=once),
                  pl.BlockSpec((D_MODEL, D_MODEL), lambda i: (0, 0), pipeline_mode=once),
                  pl.BlockSpec((D_MODEL, 1), lambda i: (0, 0))],
        out_specs=(rows.row_spec(), kt_spec, kt_spec),
        compiler_params=_params("parallel"),
        name="qkt_proj",
    )(x, g, scale, shift, w_in_b, gqk, pmat, w_kt, gk_col)


def _vconv_kernel(*refs, tiles_per_batch, rows_per_batch, tail_rows):
    if tiles_per_batch is None:
        (x_ref, g_ref, sc_ref, sh_ref, w_ref, wc_ref, s1_ref, s2_ref,
         vf_ref, vb_ref, sga_ref, cvp_ref, tail_ref, carry_ref) = refs
    else:
        (x_ref, g_ref, sc_ref, sh_ref, w_ref, wc_ref,
         vf_ref, vb_ref, sga_ref, cvp_ref, tail_ref, carry_ref) = refs
    i = pl.program_id(0)
    h = _rms_mod(x_ref[...], g_ref[...], sc_ref[0], sh_ref[0]).astype(BF16)

    def proj(k):
        return jnp.dot(h, w_ref[:, k * D_MODEL:(k + 1) * D_MODEL], preferred_element_type=F32)

    v = proj(0)
    vf_ref[...] = v
    vb_ref[...] = v.astype(BF16)
    sga_ref[...] = jax.nn.sigmoid(proj(4)).astype(sga_ref.dtype)
    u = proj(2) * proj(3)
    tm = u.shape[0]
    row = lax.broadcasted_iota(jnp.int32, (tm, 1), 0)
    r1 = pltpu.roll(u, 1, 0)
    r2 = pltpu.roll(u, 2, 0)
    if tiles_per_batch is None:
        t = row & (rows_per_batch - 1)
        u1 = jnp.where(t >= 1, r1, s1_ref[...])
        u2 = jnp.where(t >= 2, r2, s2_ref[...])
    else:
        first = (i % tiles_per_batch) == 0
        c = jnp.where(first, 0.0, carry_ref[...])
        u1 = jnp.where(row == 0, c[7:8], r1)
        u2 = jnp.where(row == 0, c[6:7], jnp.where(row == 1, c[7:8], r2))
        carry_ref[...] = u[tm - SUBLANES:]
    tail_ref[0] = u[tm - tail_rows:]
    wc = wc_ref[...]
    yc = wc[0:1] * u2 + wc[1:2] * u1 + wc[2:3] * u
    cvp_ref[...] = (jax.nn.sigmoid(proj(5)) * (proj(1) * yc)).astype(cvp_ref.dtype)


def _vconv_proj(rows, x, g, scale, shift, w_vconv, w_conv, state_rows, out_dtype):
    t = rows.n_tokens
    sample_mode = rows.tiles_per_batch is None
    tail_rows = rows.tm if sample_mode else SUBLANES
    n_tail_blocks = 1 if sample_mode else t // rows.rows_per_batch
    in_specs = [rows.row_spec(),
                pl.BlockSpec((1, D_MODEL), lambda i: (0, 0)),
                rows.mod_spec(), rows.mod_spec(),
                pl.BlockSpec(w_vconv.shape, lambda i: (0, 0), pipeline_mode=pl.Buffered(1)),
                pl.BlockSpec((CONV_WIDTH, D_MODEL), lambda i: (0, 0))]
    args = [x, g, scale, shift, w_vconv, w_conv]
    if sample_mode:
        in_specs += [rows.row_spec(), rows.row_spec()]
        args += list(state_rows)
        tail_spec = pl.BlockSpec((1, tail_rows, D_MODEL), lambda i: (0, 0, 0))
    else:
        tpb = rows.tiles_per_batch
        tail_spec = pl.BlockSpec((1, tail_rows, D_MODEL), lambda i: (i // tpb, 0, 0))
    kern = functools.partial(_vconv_kernel, tiles_per_batch=rows.tiles_per_batch,
                             rows_per_batch=rows.rows_per_batch, tail_rows=tail_rows)
    return pl.pallas_call(
        kern,
        out_shape=(jax.ShapeDtypeStruct((t, D_MODEL), F32),
                   jax.ShapeDtypeStruct((t, D_MODEL), BF16),
                   jax.ShapeDtypeStruct((t, D_MODEL), out_dtype),
                   jax.ShapeDtypeStruct((t, D_MODEL), out_dtype),
                   jax.ShapeDtypeStruct((n_tail_blocks, tail_rows, D_MODEL), F32)),
        grid=(rows.n_tiles,),
        in_specs=in_specs,
        out_specs=(rows.row_spec(), rows.row_spec(), rows.row_spec(), rows.row_spec(), tail_spec),
        scratch_shapes=[pltpu.VMEM((SUBLANES, D_MODEL), F32)],
        compiler_params=_params("arbitrary"),
        name="vconv_proj",
    )(*args)


def _lambda_value(lq1, lk1, lq2, lk2):
    e1 = jnp.exp(jnp.sum(lq1 * lk1, axis=-1, keepdims=True))
    e2 = jnp.exp(jnp.sum(lq2 * lk2, axis=-1, keepdims=True))
    return e1 - e2 + LAMBDA_INIT


def _subln_mix(o, g_sub, sga, cvp):
    ms = jnp.mean(o * o, axis=-1, keepdims=True)
    attn = (o * lax.rsqrt(ms + NORM_EPS) * g_sub) * (1.0 - LAMBDA_INIT)
    return sga * attn + cvp


def _flash_update(s, v_ones, m_ref, l_ref, acc_ref):
    m_prev = m_ref[...]
    m_next = jnp.maximum(m_prev, jnp.max(s, axis=1, keepdims=True))
    alpha = jnp.exp2(m_prev - m_next)
    p = jnp.exp2(s - jnp.concatenate([m_next] * (s.shape[1] // LANES), axis=1)).astype(BF16)
    pv = jnp.dot(p, v_ones, preferred_element_type=F32)
    acc_ref[...] = alpha * acc_ref[...] + pv[:, :V_DIM]
    l_ref[...] = alpha * l_ref[...] + pv[:, V_DIM:]
    m_ref[...] = m_next


def _attn_prompt_kernel(qi_ref, ki_ref, q_ref, kt_ref, v_ref, sga_ref, cvp_ref, gs_ref,
                        lq1_ref, lk1_ref, lq2_ref, lk2_ref,
                        o_ref, m0_ref, l0_ref, a0_ref, m1_ref, l1_ref, a1_ref):
    qi = qi_ref[pl.program_id(1)]
    ki = ki_ref[pl.program_id(1)]
    states = ((m0_ref, l0_ref, a0_ref), (m1_ref, l1_ref, a1_ref))

    @pl.when(ki == 0)
    def _():
        for m_ref, l_ref, a_ref in states:
            m_ref[...] = jnp.full(m_ref.shape, -jnp.inf, F32)
            l_ref[...] = jnp.zeros(l_ref.shape, F32)
            a_ref[...] = jnp.zeros(a_ref.shape, F32)

    tq, tk = q_ref.shape[0], kt_ref.shape[1]

    def step(masked):
        lane = lax.broadcasted_iota(jnp.int32, (1, V_DIM), 1)
        zero = jnp.zeros((), BF16)
        ones = jnp.ones((tk, V_DIM), BF16)
        if masked:
            r = lax.broadcasted_iota(jnp.int32, (tq, tk), 0)
            c = lax.broadcasted_iota(jnp.int32, (tq, tk), 1)
            keep = c <= r
        for h in range(N_HEADS):
            cols = slice(h * V_DIM, (h + 1) * V_DIM)
            q = q_ref[:, cols]
            kt = kt_ref[cols, :]
            v_ones = jnp.concatenate([v_ref[:, cols], ones], axis=1)
            qs = (jnp.where(lane < HEAD_DIM, q, zero), jnp.where(lane >= HEAD_DIM, q, zero))
            for qm, (m_ref, l_ref, a_ref) in zip(qs, states):
                s = jnp.dot(qm, kt, preferred_element_type=F32)
                if masked:
                    s = jnp.where(keep, s, -jnp.inf)
                _flash_update(s, v_ones, m_ref.at[h], l_ref.at[h], a_ref.at[h])

    @pl.when(ki < qi)
    def _():
        step(False)

    @pl.when(ki == qi)
    def _():
        step(True)
        lam = _lambda_value(lq1_ref[...], lk1_ref[...], lq2_ref[...], lk2_ref[...])
        for h in range(N_HEADS):
            cols = slice(h * V_DIM, (h + 1) * V_DIM)
            o = a0_ref[h] / l0_ref[h] - lam * (a1_ref[h] / l1_ref[h])
            mix = _subln_mix(o, gs_ref[...], sga_ref[:, cols].astype(F32), cvp_ref[:, cols].astype(F32))
            o_ref[:, cols] = mix.astype(o_ref.dtype)


def _attn_prompt(q_b, kt_b, v_b, sga, cvp, g_sub, lams, batch, seq):
    nb = seq // ATTN_BLOCK
    tq = ATTN_BLOCK
    pairs = [(qi, ki) for qi in range(nb) for ki in range(qi + 1)]
    qi_tab = jnp.array([p[0] for p in pairs], jnp.int32)
    ki_tab = jnp.array([p[1] for p in pairs], jnp.int32)
    q_spec = pl.BlockSpec((tq, D_MODEL), lambda b, p, qt, kt: (b * nb + qt[p], 0))
    k_spec = pl.BlockSpec((None, D_MODEL, tq), lambda b, p, qt, kt: (b, 0, kt[p]))
    v_spec = pl.BlockSpec((tq, D_MODEL), lambda b, p, qt, kt: (b * nb + kt[p], 0))
    vec64 = pl.BlockSpec((1, HEAD_DIM), lambda b, p, qt, kt: (0, 0))
    grid_spec = pltpu.PrefetchScalarGridSpec(
        num_scalar_prefetch=2,
        grid=(batch, len(pairs)),
        in_specs=[q_spec, k_spec, v_spec, q_spec, q_spec,
                  pl.BlockSpec((1, V_DIM), lambda b, p, qt, kt: (0, 0)),
                  vec64, vec64, vec64, vec64],
        out_specs=q_spec,
        scratch_shapes=[pltpu.VMEM((N_HEADS, tq, V_DIM), F32)] * 6)
    return pl.pallas_call(
        _attn_prompt_kernel,
        out_shape=jax.ShapeDtypeStruct((batch * seq, D_MODEL), BF16),
        grid_spec=grid_spec,
        compiler_params=_params("parallel", "arbitrary"),
        name="attn_prompt",
    )(qi_tab, ki_tab, q_b, kt_b, v_b, sga, cvp, g_sub, *lams)


def _attn_sample_kernel(pt_ref, q_ref, kn_ref, vn_ref, *rest, n_new, n_group):
    kc_refs = rest[:n_group]
    vc_refs = rest[n_group:2 * n_group]
    (sga_ref, cvp_ref, gs_ref, lq1_ref, lk1_ref, lq2_ref, lk2_ref,
     o_ref, qbd_ref, m_ref, l_ref, acc_ref) = rest[2 * n_group:]
    p = pl.program_id(1)
    rows_per_head = 2 * n_new

    @pl.when(p == 0)
    def _():
        q = q_ref[...] * ATTN_SCALE
        qrep = jnp.concatenate([q] * (N_HEADS * 2), axis=0)
        r = lax.broadcasted_iota(jnp.int32, qrep.shape, 0)
        c = lax.broadcasted_iota(jnp.int32, qrep.shape, 1)
        qbd_ref[...] = jnp.where(c // HEAD_DIM == r // n_new, qrep, 0.0).astype(BF16)
        m_ref[...] = jnp.full(m_ref.shape, -jnp.inf, F32)
        l_ref[...] = jnp.zeros(l_ref.shape, F32)
        acc_ref[...] = jnp.zeros(acc_ref.shape, F32)

    def update(s, pv_of):
        m_prev = m_ref[...]
        m_next = jnp.maximum(m_prev, jnp.max(s, axis=1, keepdims=True))
        alpha = jnp.exp(m_prev - m_next)
        width = s.shape[1]
        m_wide = m_next[:, :width] if width <= LANES else jnp.concatenate([m_next] * (width // LANES), axis=1)
        pr = jnp.exp(s - m_wide)
        l_ref[...] = alpha * l_ref[...] + jnp.sum(pr, axis=1, keepdims=True)
        acc_ref[...] = alpha * acc_ref[...] + pv_of(pr.astype(BF16))
        m_ref[...] = m_next

    def page_pv(pb, v_ref):
        return jnp.concatenate(
            [jnp.dot(pb[h * rows_per_head:(h + 1) * rows_per_head],
                     v_ref[pl.ds(h, PAGE_SIZE, stride=N_HEADS), :].astype(BF16), preferred_element_type=F32)
             for h in range(N_HEADS)], axis=0)

    qbd = qbd_ref[...]
    s = jnp.concatenate([jnp.dot(qbd, kc_refs[g][...].astype(BF16), preferred_element_type=F32)
                         for g in range(n_group)], axis=1)

    def pages_pv(pb):
        out = page_pv(pb[:, :PAGE_SIZE], vc_refs[0])
        for g in range(1, n_group):
            out = out + page_pv(pb[:, g * PAGE_SIZE:(g + 1) * PAGE_SIZE], vc_refs[g])
        return out

    update(s, pages_pv)

    @pl.when(p == pl.num_programs(1) - 1)
    def _():
        s_new = lax.dot_general(qbd_ref[...], kn_ref[...].astype(BF16), _NT, preferred_element_type=F32)
        r = lax.broadcasted_iota(jnp.int32, s_new.shape, 0)
        c = lax.broadcasted_iota(jnp.int32, s_new.shape, 1)
        s_new = jnp.where(c <= (r & (n_new - 1)), s_new, -jnp.inf)
        def new_pv(pb):
            return jnp.concatenate(
                [jnp.dot(pb[h * rows_per_head:(h + 1) * rows_per_head].astype(F32),
                         vn_ref[:, h * V_DIM:(h + 1) * V_DIM].astype(BF16).astype(F32),
                         preferred_element_type=F32) for h in range(N_HEADS)], axis=0)

        update(s_new, new_pv)
        lam = _lambda_value(lq1_ref[...], lk1_ref[...], lq2_ref[...], lk2_ref[...])
        acc = acc_ref[...] / l_ref[...]
        for h in range(N_HEADS):
            cols = slice(h * V_DIM, (h + 1) * V_DIM)
            r0 = h * rows_per_head
            o = acc[r0:r0 + n_new] - lam * acc[r0 + n_new:r0 + rows_per_head]
            o_ref[:, cols] = _subln_mix(o, gs_ref[...], sga_ref[:, cols], cvp_ref[:, cols])


def _attn_sample(qkf, vf, cache_k, cache_v, page_table, sga, cvp, g_sub, lams, n_seq, n_new):
    n_pages = page_table.shape[1]
    n_pool = cache_k.shape[0]
    width = N_HEADS * V_DIM
    grp = SAMPLE_PAGES_PER_STEP
    kc = jnp.transpose(cache_k, (0, 2, 3, 4, 1)).reshape(n_pool, width, PAGE_SIZE)
    pt = page_table.reshape(-1)
    n_rows = N_HEADS * 2 * n_new
    row = pl.BlockSpec((n_new, width), lambda b, p, pt: (b, 0))

    def page_index(g):
        return lambda b, p, pt: (pt[b * n_pages + p * grp + g], 0, 0)

    k_pages = [pl.BlockSpec((None, width, PAGE_SIZE), page_index(g)) for g in range(grp)]
    v_pages = [pl.BlockSpec((None, PAGE_SIZE * N_HEADS, V_DIM), page_index(g)) for g in range(grp)]
    vec64 = pl.BlockSpec((1, HEAD_DIM), lambda b, p, pt: (0, 0))
    vc = cache_v.reshape(n_pool, PAGE_SIZE * N_HEADS, V_DIM)
    grid_spec = pltpu.PrefetchScalarGridSpec(
        num_scalar_prefetch=1,
        grid=(n_seq, n_pages // grp),
        in_specs=[pl.BlockSpec((None, n_new, width), lambda b, p, pt: (0, b, 0)),
                  pl.BlockSpec((None, n_new, width), lambda b, p, pt: (1, b, 0)),
                  row, *k_pages, *v_pages, row, row,
                  pl.BlockSpec((1, V_DIM), lambda b, p, pt: (0, 0)),
                  vec64, vec64, vec64, vec64],
        out_specs=row,
        scratch_shapes=[pltpu.VMEM((n_rows, width), BF16),
                        pltpu.VMEM((n_rows, V_DIM), F32), pltpu.VMEM((n_rows, V_DIM), F32),
                        pltpu.VMEM((n_rows, V_DIM), F32)])
    return pl.pallas_call(
        functools.partial(_attn_sample_kernel, n_new=n_new, n_group=grp),
        out_shape=jax.ShapeDtypeStruct((n_seq * n_new, width), F32),
        grid_spec=grid_spec,
        compiler_params=_params("parallel", "arbitrary"),
        name="attn_sample",
    )(pt, qkf, qkf, vf, *([kc] * grp), *([vc] * grp), sga, cvp, g_sub, *lams)


def _out_kernel(mix_ref, x_ref, wo_ref, g1_ref, g_ref, sc_ref, sh_ref, wrh_ref, wrl_ref, br_ref, *refs, n_real_tiles):
    outs = refs[-4:]
    i = pl.program_id(0)

    @pl.when(i < n_real_tiles)
    def _():
        _out_tile(mix_ref, x_ref, wo_ref, g1_ref, g_ref, sc_ref, sh_ref, wrh_ref, wrl_ref, br_ref, *outs)

    @pl.when(i >= n_real_tiles)
    def _():
        for r in outs:
            r[...] = jnp.zeros(r.shape, r.dtype)


def _out_tile(mix_ref, x_ref, wo_ref, g1_ref, g_ref, sc_ref, sh_ref, wrh_ref, wrl_ref, br_ref,
              xm_ref, h2_ref, idx_ref, gt_ref):
    y = jnp.dot(mix_ref[...].astype(BF16), wo_ref[...], preferred_element_type=F32)
    xm = x_ref[...] + g1_ref[0] * y
    xm_ref[...] = xm
    h2 = _rms_mod(xm, g_ref[...], sc_ref[0], sh_ref[0])
    hi = h2.astype(BF16)
    _store_token_rows(h2_ref, _pack_pairs(hi.astype(F32)))
    lo = (h2 - hi.astype(F32)).astype(BF16)
    logits = (jnp.dot(hi, wrh_ref[...], preferred_element_type=F32)
              + jnp.dot(lo, wrh_ref[...], preferred_element_type=F32)
              + jnp.dot(hi, wrl_ref[...], preferred_element_type=F32)) + br_ref[...]
    lane = lax.broadcasted_iota(jnp.int32, logits.shape, 1).astype(F32)
    vals, idxs = [], []
    for _ in range(TOP_K):
        m = jnp.max(logits, axis=-1, keepdims=True)
        ix = jnp.min(jnp.where(logits == m, lane, float(LANES)), axis=-1, keepdims=True)
        logits = jnp.where(lane == ix, -jnp.inf, logits)
        vals.append(m)
        idxs.append(ix)
    es = [jnp.exp(v - vals[0]) for v in vals]
    denom = es[0] + es[1] + es[2] + es[3]
    idx_out = jnp.zeros(logits.shape, F32)
    gt_out = jnp.zeros(logits.shape, F32)
    for k in range(TOP_K):
        idx_out = jnp.where(lane == float(k), idxs[k], idx_out)
        gt_out = jnp.where(lane == float(k), es[k] / denom, gt_out)
    idx_ref[...] = idx_out.astype(jnp.int32)
    gt_ref[...] = gt_out


def _out_proj(rows, mix, x, w_o_b, gate1, g, scale, shift, wr_hi, wr_lo, b_r, total_tokens, first_row, shared_bufs):
    full = lambda shape: pl.BlockSpec(shape, lambda i: (0,) * len(shape))
    off = first_row // rows.tm
    creates = shared_bufs is None
    out_row = lambda width: pl.BlockSpec((rows.tm, width), lambda i: (i + off, 0))
    packed_rows = pl.BlockSpec((rows.tm * ROW_CHUNKS, LANES), lambda i: (i + off, 0))
    in_specs = [rows.row_spec(clamp=creates), rows.row_spec(clamp=creates), full((D_MODEL, D_MODEL)),
                rows.mod_spec(clamp=creates), full((1, D_MODEL)), rows.mod_spec(clamp=creates),
                rows.mod_spec(clamp=creates),
                full((D_MODEL, LANES)), full((D_MODEL, LANES)), full((1, LANES))]
    args = [mix, x, w_o_b, gate1, g, scale, shift, wr_hi, wr_lo, b_r]
    aliases = {}
    if not creates:
        aliases = {len(args) + j: j for j in range(len(shared_bufs))}
        in_specs += [pl.BlockSpec(memory_space=pl.ANY)] * len(shared_bufs)
        args += list(shared_bufs)
    extra = 1 if creates and total_tokens > rows.n_tokens else 0
    assert total_tokens - rows.n_tokens <= rows.tm or not creates, "other group must fit the one extra tile"
    return pl.pallas_call(
        functools.partial(_out_kernel, n_real_tiles=rows.n_tiles),
        out_shape=(jax.ShapeDtypeStruct((total_tokens, D_MODEL), F32),
                   jax.ShapeDtypeStruct((total_tokens * ROW_CHUNKS, LANES), jnp.uint32),
                   jax.ShapeDtypeStruct((total_tokens, LANES), jnp.int32),
                   jax.ShapeDtypeStruct((total_tokens, LANES), F32)),
        grid=(rows.n_tiles + extra,),
        in_specs=in_specs,
        out_specs=(out_row(D_MODEL), packed_rows, out_row(LANES), out_row(LANES)),
        input_output_aliases=aliases,
        compiler_params=_params("arbitrary"),
        name="out_proj_router",
    )(*args)


def _pack_pairs(x):
    bits = pltpu.bitcast(x, jnp.uint32)
    n = x.shape[1] // 2
    return (bits[:, :n] >> 16) | (bits[:, n:] & jnp.uint32(0xFFFF0000))


def _unpack_pairs(w):
    lo = pltpu.bitcast(w << 16, F32)
    hi = pltpu.bitcast(w & jnp.uint32(0xFFFF0000), F32)
    return jnp.concatenate([lo, hi], axis=1)


def _store_token_rows(ref, packed):
    n = packed.shape[0]
    for c in range(ROW_CHUNKS):
        ref[pl.ds(c, n, stride=ROW_CHUNKS), :] = packed[:, c * LANES:(c + 1) * LANES]


def _load_token_rows(ref, n):
    return jnp.concatenate([ref[pl.ds(c, n, stride=ROW_CHUNKS), :] for c in range(ROW_CHUNKS)], axis=1)


def _route_kernel(idx_ref, dest_ref, cnt_ref, xs_ref, run_ref, start_ref, zero_ref, sem,
                  *, n_fill_chunks, chunks_per_step):
    ph = pl.program_id(0)
    i = pl.program_id(1)
    step = ph * pl.num_programs(1) + i
    fill_rows = zero_ref.shape[0]

    def fill(c):
        start = pl.multiple_of(c * fill_rows, fill_rows)
        return pltpu.make_async_copy(zero_ref, xs_ref.at[pl.ds(start, fill_rows)], sem)

    @pl.when(step == 0)
    def _():
        zero_ref[...] = jnp.zeros(zero_ref.shape, zero_ref.dtype)

    for j in range(chunks_per_step):
        @pl.when((step > 0) & ((step - 1) * chunks_per_step + j < n_fill_chunks))
        def _(j=j):
            fill((step - 1) * chunks_per_step + j).wait()

    for j in range(chunks_per_step):
        @pl.when(step * chunks_per_step + j < n_fill_chunks)
        def _(j=j):
            fill(step * chunks_per_step + j).start()

    for j in range(chunks_per_step):
        @pl.when((step == 2 * pl.num_programs(1) - 1) & (step * chunks_per_step + j < n_fill_chunks))
        def _(j=j):
            fill(step * chunks_per_step + j).wait()

    idx = idx_ref[...]
    tm = idx.shape[0]
    lane = lax.broadcasted_iota(jnp.int32, idx.shape, 1)
    onehots = [(lane == idx[:, k:k + 1]).astype(F32) for k in range(TOP_K)]
    member = onehots[0] + onehots[1] + onehots[2] + onehots[3]
    tile_count = jnp.sum(member, axis=0, keepdims=True)

    @pl.when((ph == 0) & (i == 0))
    def _():
        cnt_ref[...] = jnp.zeros(cnt_ref.shape, F32)

    @pl.when(ph == 0)
    def _():
        cnt_ref[...] = cnt_ref[...] + tile_count

    @pl.when((ph == 1) & (i == 0))
    def _():
        cnt = cnt_ref[...]
        padded = jnp.floor((cnt + (MOE_BLOCK - 1)) * (1.0 / MOE_BLOCK)) * MOE_BLOCK
        l1 = lax.broadcasted_iota(jnp.int32, cnt.shape, 1)
        incl = padded
        for s in (1, 2, 4, 8, 16, 32, 64):
            incl = incl + jnp.where(l1 >= s, pltpu.roll(incl, s, 1), 0.0)
        start_ref[...] = incl - padded
        run_ref[...] = jnp.zeros(run_ref.shape, F32)

    @pl.when(ph == 1)
    def _():
        r = lax.broadcasted_iota(jnp.int32, (tm, tm), 0)
        c = lax.broadcasted_iota(jnp.int32, (tm, tm), 1)
        earlier = (c < r).astype(BF16)
        before = jnp.dot(earlier, member.astype(BF16), preferred_element_type=F32)
        base = before + run_ref[0:1] + start_ref[0:1]
        out = jnp.zeros(idx.shape, F32)
        for k in range(TOP_K):
            d = jnp.sum(onehots[k] * base, axis=1, keepdims=True)
            out = jnp.where(lane == k, d, out)
        dest_ref[...] = out.astype(jnp.int32)
        run_ref[...] = run_ref[...] + tile_count


def _route(rows, idx, n_sorted_rows):
    tm = rows.tm
    n_fill_chunks = n_sorted_rows * ROW_CHUNKS // ZERO_FILL_ROWS
    assert n_fill_chunks * ZERO_FILL_ROWS == n_sorted_rows * ROW_CHUNKS
    chunks_per_step = -(-n_fill_chunks // (2 * rows.n_tiles))
    kern = functools.partial(_route_kernel, n_fill_chunks=n_fill_chunks, chunks_per_step=chunks_per_step)
    return pl.pallas_call(
        kern,
        out_shape=(jax.ShapeDtypeStruct((rows.n_tokens, LANES), jnp.int32),
                   jax.ShapeDtypeStruct((SUBLANES, LANES), F32),
                   jax.ShapeDtypeStruct((n_sorted_rows * ROW_CHUNKS, LANES), jnp.uint32)),
        grid=(2, rows.n_tiles),
        in_specs=[pl.BlockSpec((tm, LANES), lambda ph, i: (i, 0))],
        out_specs=(pl.BlockSpec((tm, LANES), lambda ph, i: (i * ph, 0)),
                   pl.BlockSpec((SUBLANES, LANES), lambda ph, i: (0, 0)),
                   pl.BlockSpec(memory_space=pl.ANY)),
        scratch_shapes=[pltpu.VMEM((SUBLANES, LANES), F32), pltpu.VMEM((SUBLANES, LANES), F32),
                        pltpu.VMEM((ZERO_FILL_ROWS, LANES), jnp.uint32), pltpu.SemaphoreType.DMA(())],
        compiler_params=_params("arbitrary", "arbitrary"),
        name="moe_route",
    )(idx)


def _dispatch_kernel(dest_ref, h_ref, zero_ref, xs_ref, sem):
    del zero_ref
    i = pl.program_id(0)
    tm = h_ref.shape[0] // ROW_CHUNKS

    def body(t, carry):
        for k in range(TOP_K):
            d = dest_ref[(i * tm + t) * TOP_K + k]
            pltpu.make_async_copy(h_ref.at[pl.ds(t * ROW_CHUNKS, ROW_CHUNKS)],
                                  xs_ref.at[pl.ds(d * ROW_CHUNKS, ROW_CHUNKS)], sem).start(priority=k % 2)
        return carry

    lax.fori_loop(0, tm, body, 0, unroll=8)
    n_words = tm * TOP_K * ROW_CHUNKS
    pltpu.make_async_copy(xs_ref.at[pl.ds(0, n_words)], xs_ref.at[pl.ds(0, n_words)], sem).wait()


def _dispatch(rows, dest_flat, h2u, zeroed_rows):
    tm = rows.tm
    grid_spec = pltpu.PrefetchScalarGridSpec(
        num_scalar_prefetch=1, grid=(rows.n_tiles,),
        in_specs=[pl.BlockSpec((tm * ROW_CHUNKS, LANES), lambda i, d: (i, 0)),
                  pl.BlockSpec(memory_space=pl.ANY)],
        out_specs=pl.BlockSpec(memory_space=pl.ANY),
        scratch_shapes=[pltpu.SemaphoreType.DMA(())])
    return pl.pallas_call(
        _dispatch_kernel,
        out_shape=jax.ShapeDtypeStruct(zeroed_rows.shape, jnp.uint32),
        grid_spec=grid_spec,
        input_output_aliases={2: 0},
        compiler_params=_params("arbitrary"),
        name="moe_dispatch",
    )(dest_flat, h2u, zeroed_rows)


def _expert_prepare(wgu_raw, wd_raw, wt_s, wg_s, wu_s, wd_s):
    n = wgu_raw.shape[1]
    n_lane_tiles = wt_s.shape[0]
    for c in range(n // EXPERT_XPOSE_CHUNK):
        cols = slice(c * EXPERT_XPOSE_CHUNK, (c + 1) * EXPERT_XPOSE_CHUNK)
        wt = wgu_raw[:, cols].T
        for j in range(n_lane_tiles):
            wt_s[j, cols, :] = wt[:, j * LANES:(j + 1) * LANES]
    for j in range(n_lane_tiles):
        lanes = slice(j * LANES, (j + 1) * LANES)
        wg_s[:, lanes] = wt_s[j, pl.ds(0, n // 2, stride=2), :].astype(BF16)
        wu_s[:, lanes] = wt_s[j, pl.ds(1, n // 2, stride=2), :].astype(BF16)
    wd_s[...] = wd_raw[...].astype(BF16)


def _expert_block(x_ref, bg_ref, bu_ref, bd_ref, y_ref, wg_s, wu_s, wd_s):
    x = _unpack_pairs(_load_token_rows(x_ref, MOE_BLOCK)).astype(BF16)
    y = bd_ref[...]
    for c in range(wd_s.shape[0] // EXPERT_FF_CHUNK):
        ff = slice(c * EXPERT_FF_CHUNK, (c + 1) * EXPERT_FF_CHUNK)
        g = lax.dot_general(x, wg_s[ff, :], _NT, preferred_element_type=F32) + bg_ref[:, ff]
        u = lax.dot_general(x, wu_s[ff, :], _NT, preferred_element_type=F32) + bu_ref[:, ff]
        gate = jnp.minimum(g, SWIGLU_LIMIT)
        up = jnp.clip(u, -SWIGLU_LIMIT, SWIGLU_LIMIT)
        glu = gate * jax.nn.sigmoid(SWIGLU_ALPHA * gate)
        a = ((up + 1.0) * glu).astype(BF16)
        y = y + jnp.dot(a, wd_s[ff, :], preferred_element_type=F32)
    _store_token_rows(y_ref, _pack_pairs(y.astype(BF16).astype(F32)))


def _expert_kernel(be_ref, first_ref, last_ref, nxt_ref, slot_ref, nu_ref,
                   x_ref, wgu_hbm, wd_hbm, bg_ref, bu_ref, bd_ref, y_ref,
                   wgu_raw, wd_raw, wt_s, wg_a, wu_a, wd_a, wg_b, wu_b, wd_b, sem):
    i = pl.program_id(0)
    used = i < nu_ref[0]
    first = first_ref[i] == 1
    last = last_ref[i] == 1
    nxt = nxt_ref[i]
    in_a = slot_ref[i] == 0
    sets = ((wg_a, wu_a, wd_a), (wg_b, wu_b, wd_b))

    def fetch(e):
        return (pltpu.make_async_copy(wgu_hbm.at[e], wgu_raw, sem.at[0]),
                pltpu.make_async_copy(wd_hbm.at[e], wd_raw, sem.at[1]))

    @pl.when(i == 0)
    def _():
        for cp in fetch(be_ref[0]):
            cp.start()
        for cp in fetch(be_ref[0]):
            cp.wait()
        _expert_prepare(wgu_raw, wd_raw, wt_s, *sets[0])

    @pl.when(used & first & (nxt >= 0))
    def _():
        for cp in fetch(nxt):
            cp.start()

    hand_over = used & last & (nxt >= 0)
    for cur, flag in ((0, in_a), (1, jnp.logical_not(in_a))):
        @pl.when(hand_over & flag)
        def _(cur=cur):
            for cp in fetch(nxt):
                cp.wait()
            _expert_block(x_ref, bg_ref, bu_ref, bd_ref, y_ref, *sets[cur])
            _expert_prepare(wgu_raw, wd_raw, wt_s, *sets[1 - cur])

        @pl.when(used & jnp.logical_not(last & (nxt >= 0)) & flag)
        def _(cur=cur):
            _expert_block(x_ref, bg_ref, bu_ref, bd_ref, y_ref, *sets[cur])

    @pl.when(i >= nu_ref[0])
    def _():
        y_ref[...] = jnp.zeros(y_ref.shape, y_ref.dtype)


def _experts(x_sorted, block_e, n_used, w_gu, w_dn, b_g, b_u, b_d):
    rows = x_sorted.shape[0] // ROW_CHUNKS
    n_blocks = rows // MOE_BLOCK
    d_gu = w_gu.shape[2]
    d_ff = w_dn.shape[1]
    blk = jnp.arange(n_blocks, dtype=jnp.int32)
    used = blk < n_used[0]
    prev_e = jnp.concatenate([block_e[:1] - 1, block_e[:-1]])
    next_e = jnp.concatenate([block_e[1:], block_e[-1:] + 1])
    first = (used & (block_e != prev_e)).astype(jnp.int32)
    last = (used & ((block_e != next_e) | (blk + 1 >= n_used[0]))).astype(jnp.int32)
    after = jnp.sum((block_e[None, :] <= block_e[:, None]).astype(jnp.int32), axis=1)
    nxt = jnp.where(after < n_used[0], block_e[jnp.minimum(after, n_blocks - 1)], -1).astype(jnp.int32)
    slot = ((jnp.cumsum(first) - 1) % 2).astype(jnp.int32)
    n_prefetch = 6
    by_expert = lambda i, be, *_: (be[i], 0, 0)
    xspec = pl.BlockSpec((MOE_BLOCK * ROW_CHUNKS, LANES), lambda i, *_: (i, 0))
    weights = lambda: (pltpu.VMEM((d_ff, D_MODEL), BF16), pltpu.VMEM((d_ff, D_MODEL), BF16),
                       pltpu.VMEM((d_ff, D_MODEL), BF16))
    grid_spec = pltpu.PrefetchScalarGridSpec(
        num_scalar_prefetch=n_prefetch, grid=(n_blocks,),
        in_specs=[xspec,
                  pl.BlockSpec(memory_space=pl.ANY),
                  pl.BlockSpec(memory_space=pl.ANY),
                  pl.BlockSpec((None, 1, d_ff), by_expert),
                  pl.BlockSpec((None, 1, d_ff), by_expert),
                  pl.BlockSpec((None, 1, D_MODEL), by_expert)],
        out_specs=xspec,
        scratch_shapes=[pltpu.VMEM((D_MODEL, d_gu), F32), pltpu.VMEM((d_ff, D_MODEL), F32),
                        pltpu.VMEM((D_MODEL // LANES, d_gu, LANES), F32),
                        *weights(), *weights(),
                        pltpu.SemaphoreType.DMA((2,))])
    return pl.pallas_call(
        _expert_kernel,
        out_shape=jax.ShapeDtypeStruct((rows * ROW_CHUNKS, LANES), jnp.uint32),
        grid_spec=grid_spec,
        compiler_params=pltpu.CompilerParams(dimension_semantics=("arbitrary",),
                                             vmem_limit_bytes=EXPERT_VMEM_LIMIT_BYTES),
        name="experts",
    )(block_e, first, last, nxt, slot, n_used, x_sorted, w_gu, w_dn, b_g, b_u, b_d)


def _combine_kernel(dest_ref, xm_ref, gt_ref, g2p_ref, g2s_ref, yb_ref, yp_ref, ys_ref, rows_ref, sem,
                    *, n_prompt_tiles):
    i = pl.program_id(0)
    tm = xm_ref.shape[0]
    slot = i % 2

    def gather(tile, into):
        def body(t, carry):
            for k in range(TOP_K):
                d = dest_ref[(tile * tm + t) * TOP_K + k]
                pltpu.make_async_copy(yb_ref.at[pl.ds(d * ROW_CHUNKS, ROW_CHUNKS)],
                                      rows_ref.at[into, k, pl.ds(t * ROW_CHUNKS, ROW_CHUNKS)],
                                      sem.at[into]).start(priority=k % 2)
            return carry

        lax.fori_loop(0, tm, body, 0, unroll=8)

    @pl.when(i == 0)
    def _():
        gather(0, 0)

    @pl.when(i + 1 < pl.num_programs(0))
    def _():
        gather(i + 1, 1 - slot)

    pltpu.make_async_copy(rows_ref.at[slot], rows_ref.at[slot], sem.at[slot]).wait()
    gt = gt_ref[...]
    acc = jnp.zeros(xm_ref.shape, F32)
    for k in range(TOP_K):
        acc = acc + gt[:, k:k + 1] * _unpack_pairs(_load_token_rows(rows_ref.at[slot, k], tm))
    is_prompt = i < n_prompt_tiles
    y = xm_ref[...] + jnp.where(is_prompt, g2p_ref[0], g2s_ref[0]) * acc

    @pl.when(is_prompt)
    def _():
        yp_ref[...] = y

    @pl.when(jnp.logical_not(is_prompt))
    def _():
        ys_ref[...] = y


def _combine(tiles, dest_flat, xm, gates, gate2_p, gate2_s, y_buf, n_prompt, rows_per_seq):
    tm = tiles.tm
    n_prompt_tiles = n_prompt // tm
    n_sample = tiles.n_tokens - n_prompt
    assert n_sample == tm and gate2_s.shape[1] == tm, "the sample group is one token tile"
    tiles_per_seq = rows_per_seq // tm
    last_seq = gate2_p.shape[0] - 1
    grid_spec = pltpu.PrefetchScalarGridSpec(
        num_scalar_prefetch=1, grid=(tiles.n_tiles,),
        in_specs=[pl.BlockSpec((tm, D_MODEL), lambda i, d: (i, 0)),
                  pl.BlockSpec((tm, LANES), lambda i, d: (i, 0)),
                  pl.BlockSpec((1, 1, D_MODEL), lambda i, d: (jnp.minimum(i // tiles_per_seq, last_seq), 0, 0)),
                  pl.BlockSpec((1, tm, D_MODEL), lambda i, d: (0, 0, 0)),
                  pl.BlockSpec(memory_space=pl.ANY)],
        out_specs=(pl.BlockSpec((tm, D_MODEL), lambda i, d: (jnp.minimum(i, n_prompt_tiles - 1), 0)),
                   pl.BlockSpec((tm, D_MODEL), lambda i, d: (0, 0))),
        scratch_shapes=[pltpu.VMEM((2, TOP_K, tm * ROW_CHUNKS, LANES), jnp.uint32),
                        pltpu.SemaphoreType.DMA((2,))])
    return pl.pallas_call(
        functools.partial(_combine_kernel, n_prompt_tiles=n_prompt_tiles),
        out_shape=(jax.ShapeDtypeStruct((n_prompt, D_MODEL), F32),
                   jax.ShapeDtypeStruct((n_sample, D_MODEL), F32)),
        grid_spec=grid_spec,
        compiler_params=_params("arbitrary"),
        name="moe_combine",
    )(dest_flat, xm, gates, gate2_p, gate2_s, y_buf)


class _Tiles:
    def __init__(self, n_tokens, tm):
        self.n_tokens, self.tm, self.n_tiles = n_tokens, tm, n_tokens // tm


def _moe(xm, h2u, idx, gates, gate2_p, gate2_s, moe_w, n_prompt, rows_per_seq):
    rows = _Tiles(xm.shape[0], MOE_TOKEN_TILE)
    wide = _Tiles(xm.shape[0], MOE_WIDE_TOKEN_TILE)
    assert wide.n_tiles * wide.tm == rows.n_tokens == rows.n_tiles * rows.tm
    a = rows.n_tokens * TOP_K
    n_blocks = -(-a // MOE_BLOCK) + N_EXPERTS
    dest, counts, zeroed_rows = _route(wide, idx, n_blocks * MOE_BLOCK)
    dest_flat = dest[:, :TOP_K].reshape(a)
    cnt = counts[0, :N_EXPERTS].astype(jnp.int32)
    pad_end = jnp.cumsum((cnt + MOE_BLOCK - 1) // MOE_BLOCK * MOE_BLOCK)
    block_row = jnp.arange(n_blocks, dtype=jnp.int32) * MOE_BLOCK
    block_e = jnp.minimum(jnp.sum((pad_end[None, :] <= block_row[:, None]).astype(jnp.int32), axis=1),
                          N_EXPERTS - 1)
    n_used = (pad_end[-1:] // MOE_BLOCK).astype(jnp.int32)
    x_sorted = _dispatch(wide, dest_flat, h2u, zeroed_rows)
    y_buf = _experts(x_sorted, block_e, n_used, *moe_w)
    return _combine(rows, dest_flat, xm, gates, gate2_p, gate2_s, y_buf, n_prompt, rows_per_seq)


def _group(rows, x, ada, state_rows, qk_and_attend, shared, total_tokens, first_row, token_bufs):
    (g_mix, w_vconv, w_conv, w_o_b, g_ffn, wr_hi, wr_lo, b_r, mid_dtype) = shared
    shift1, scale1, gate1, shift2, scale2, gate2 = [rows.mod_array(m) for m in jnp.split(ada, 6, axis=-1)]
    v_f, v_b, sga, cvp, tail = _vconv_proj(rows, x, g_mix, scale1, shift1, w_vconv, w_conv, state_rows, mid_dtype)
    mix, k_out = qk_and_attend(x, scale1, shift1, v_f, v_b, sga, cvp)
    token_bufs = _out_proj(rows, mix, x, w_o_b, gate1, g_ffn, scale2, shift2, wr_hi, wr_lo, b_r,
                           total_tokens, first_row, token_bufs)
    return token_bufs, gate2, k_out, v_f, tail


def kernel(x_prompt, x_sample, c_prompt, c_sample, cache_k, cache_v, state_conv, page_table, w_ada, b_ada, g_norm_mix, w_in, g_q, g_k, lambda_q1, lambda_k1, lambda_q2, lambda_k2, g_subln, w_conv, w_o, g_norm_ffn, w_router, b_router, w_gate_up, b_gate_up, w_down, b_down):
    assert w_in.shape[0] == 1, "single-layer stack"
    batch, seq, _ = x_prompt.shape
    n_seq, n_new, _ = x_sample.shape
    tp, ts = batch * seq, n_seq * n_new
    n_chunks = D_MODEL // HEAD_DIM

    ada = _ada(jnp.concatenate([c_prompt, c_sample], axis=0), w_ada[0], b_ada[0][None])

    w_in_b = w_in[0].astype(BF16)
    w_kt = w_in[0][:, D_MODEL:2 * D_MODEL].T.astype(BF16)
    w_vconv = w_in_b[:, 2 * D_MODEL:]
    w_o_b = w_o[0].astype(BF16)
    gqk = jnp.stack([jnp.tile(g_q[0], n_chunks), jnp.tile(g_k[0], n_chunks)])[:, None, :]
    gk_col = jnp.tile(g_k[0], n_chunks)[:, None]
    blk = jnp.arange(D_MODEL, dtype=jnp.int32) // HEAD_DIM
    pmat = (blk[:, None] == blk[None, :]).astype(BF16)
    wr = jnp.pad(w_router[0], ((0, 0), (0, LANES - N_EXPERTS)))
    wr_hi = wr.astype(BF16)
    wr_lo = (wr - wr_hi.astype(F32)).astype(BF16)
    b_r = jnp.pad(b_router[0], (0, LANES - N_EXPERTS), constant_values=NEG_BIG)[None]
    moe_w = (w_gate_up[0], w_down[0], b_gate_up[0][:, None, 0::2], b_gate_up[0][:, None, 1::2],
             b_down[0][:, None, :])
    g_mix = g_norm_mix[0][None]
    g_ffn = g_norm_ffn[0][None]
    g_sub = g_subln[0][None]
    lams = (lambda_q1[0][None], lambda_k1[0][None], lambda_q2[0][None], lambda_k2[0][None])

    def shared(mid_dtype):
        return (g_mix, w_vconv, w_conv[0], w_o_b, g_ffn, wr_hi, wr_lo, b_r, mid_dtype)

    rows_p = _Rows(tp, seq)

    def attend_p(x, scale1, shift1, v_f, v_b, sga, cvp):
        q_b, kt_f, kt_b = _qkt_proj(rows_p, x, g_mix, scale1, shift1, w_in_b, gqk, pmat, w_kt, gk_col, batch, seq)
        mix = _attn_prompt(q_b, kt_b, v_b, sga, cvp, g_sub, lams, batch, seq)
        return mix, kt_f

    bufs, gate2_p, kt_p, v_p, tail_p = _group(rows_p, x_prompt.reshape(tp, D_MODEL), ada[:batch], None, attend_p,
                                              shared(BF16), tp + ts, 0, None)

    rows_s = _Rows(ts, n_new)
    st = state_conv[0]
    zeros = jnp.zeros((n_seq, n_new - 2, D_MODEL), F32)
    s1 = jnp.concatenate([st[:, 1:2], jnp.zeros((n_seq, n_new - 1, D_MODEL), F32)], axis=1).reshape(ts, D_MODEL)
    s2 = jnp.concatenate([st, zeros], axis=1).reshape(ts, D_MODEL)

    def attend_s(x, scale1, shift1, v_f, v_b, sga, cvp):
        _, qk_f = _qk_proj(rows_s, x, g_mix, scale1, shift1, w_in_b, gqk, pmat)
        mix = _attn_sample(qk_f, v_f, cache_k[0], cache_v[0], page_table, sga, cvp, g_sub, lams, n_seq, n_new)
        return mix, qk_f[1]

    bufs, gate2_s, k_s, v_s, tail_s = _group(rows_s, x_sample.reshape(ts, D_MODEL), ada[batch:], (s1, s2), attend_s,
                                             shared(F32), tp + ts, tp, bufs)

    y_p, y_s = _moe(*bufs, gate2_p, gate2_s, moe_w, tp, seq)

    tail_s = tail_s.reshape(n_seq, n_new, D_MODEL)
    k_p = kt_p.reshape(1, batch, N_HEADS, 2, HEAD_DIM, seq).transpose(0, 1, 5, 2, 3, 4)
    return (y_p.reshape(batch, seq, D_MODEL),
            y_s.reshape(n_seq, n_new, D_MODEL),
            k_p,
            v_p.reshape(1, batch, seq, N_HEADS, V_DIM),
            tail_p[:, SUBLANES - (CONV_WIDTH - 1):][None],
            k_s.reshape(1, n_seq, n_new, N_HEADS, 2, HEAD_DIM),
            v_s.reshape(1, n_seq, n_new, N_HEADS, V_DIM),
            tail_s[:, n_new - (CONV_WIDTH - 1):][None])
```
